```python
import jax
import jax.numpy as jnp
from jax import lax
import numpy as np

D_MODEL = 1024
BATCH = 8
SEQ = 2048
DEPTH = 1

ATT_GROUPS = ((128, 1), (512, 4), (2048, 16))
ATT_HEADS = 8
ATT_HEAD_DIM = 64
ATT_GROUP_WIDTH = ATT_HEADS * ATT_HEAD_DIM
ATT_WIDTH = len(ATT_GROUPS) * ATT_GROUP_WIDTH
ATT_BLOCK = 128

HGRN_HEAD_DIM = 128
HGRN_HEADS = D_MODEL // HGRN_HEAD_DIM
HGRN_WIDTH = HGRN_HEADS * HGRN_HEAD_DIM
HGRN_CHUNK = 32

IN_WIDTHS = (ATT_WIDTH, ATT_WIDTH, ATT_WIDTH,
             HGRN_WIDTH, HGRN_WIDTH, HGRN_WIDTH, HGRN_WIDTH,
             D_MODEL, D_MODEL)
IN_WIDTH = sum(IN_WIDTHS)

N_EXPERTS = 256
TOP_K = 8
N_EXPERT_GROUPS = 8
TOPK_GROUPS = 4
EXPERT_DIM = 256
SHARED_DIM = 256
ROUTED_SCALE = 2.5
MOE_BLOCK = 128

LN_EPS = 1e-5
RMS_EPS = 1e-6
DEEPNORM_ALPHA = (2 * DEPTH) ** 0.25
DEEPNORM_BETA = (8 * DEPTH) ** -0.25

kernel_name = 'hybrid_dilated_attn_hgrn2_moe'


def layer_norm(x, g, b):
    xf = x.astype(jnp.float32)
    mu = jnp.mean(xf, -1, keepdims=True)
    xc = xf - mu
    var = jnp.mean(xc * xc, -1, keepdims=True)
    y = xc * lax.rsqrt(var + LN_EPS) * g.astype(jnp.float32) + b.astype(jnp.float32)
    return y.astype(x.dtype)


def dilated_window_attention(q, k, v, window, dilation):
    B, S, H, Dh = q.shape
    steps = window // dilation
    L = S // dilation
    n_prev = -(-steps // ATT_BLOCK)
    nb = -(-L // ATT_BLOCK)
    Lp = nb * ATT_BLOCK
    BD = B * dilation

    def to_blocks(t):
        t = t.reshape(B, L, dilation, H, Dh).transpose(0, 2, 1, 3, 4).reshape(BD, L, H, Dh)
        t = jnp.pad(t, ((0, 0), (0, Lp - L), (0, 0), (0, 0)))
        return t.reshape(BD, nb, ATT_BLOCK, H, Dh)

    def with_context(t):
        prev = [jnp.pad(t, ((0, 0), (p, 0), (0, 0), (0, 0), (0, 0)))[:, :nb] for p in range(n_prev, 0, -1)]
        return jnp.concatenate(prev + [t], axis=2)

    qb, kb, vb = to_blocks(q), to_blocks(k), to_blocks(v)
    kc, vc = with_context(kb), with_context(vb)
    ctx = (n_prev + 1) * ATT_BLOCK
    s = jnp.einsum('bnqhd,bnkhd->bnhqk', qb, kc).astype(jnp.float32) * (Dh ** -0.5)
    qi = jnp.arange(ATT_BLOCK)[:, None] + n_prev * ATT_BLOCK
    kj = jnp.arange(ctx)[None, :]
    dist = qi - kj
    key_pos = jnp.arange(nb)[:, None] * ATT_BLOCK + kj - n_prev * ATT_BLOCK
    mask = ((dist >= 0) & (dist <= steps))[None] & (key_pos >= 0)[:, None, :]
    s = jnp.where(mask[None, :, None], s, -jnp.inf)
    m = jnp.max(s, -1)
    p = jnp.exp(s - m[..., None])
    l = jnp.sum(p, -1)
    o = jnp.einsum('bnhqk,bnkhd->bnqhd', p, vc.astype(jnp.float32))

    o = o.reshape(BD, Lp, H, Dh)[:, :L].reshape(B, dilation, L, H, Dh)
    o = o.transpose(0, 2, 1, 3, 4).reshape(B, S, H, Dh)

    def stat_back(t):
        t = t.transpose(0, 1, 3, 2).reshape(BD, Lp, H)[:, :L].reshape(B, dilation, L, H)
        return t.transpose(0, 2, 1, 3).reshape(B, S, H)

    return o, stat_back(m), stat_back(l)


def dilated_attention_mixer(q_all, k_all, v_all):
    B, S, _ = q_all.shape
    shp = (B, S, len(ATT_GROUPS), ATT_HEADS, ATT_HEAD_DIM)
    q_all, k_all, v_all = q_all.reshape(shp), k_all.reshape(shp), v_all.reshape(shp)
    outs, ms, ls = [], [], []
    for g, (window, dilation) in enumerate(ATT_GROUPS):
        o, m, l = dilated_window_attention(q_all[:, :, g], k_all[:, :, g], v_all[:, :, g], window, dilation)
        outs.append(o)
        ms.append(m)
        ls.append(l)
    o = jnp.stack(outs)
    m = jnp.stack(ms)
    l = jnp.stack(ls)
    w = jnp.exp(m - jnp.max(m, 0, keepdims=True))
    out = jnp.sum(w[..., None] * o, 0) / jnp.sum(w * l, 0)[..., None]
    return out.reshape(B, S, ATT_GROUP_WIDTH).astype(q_all.dtype)


def hgrn2_mixer(q, f_pre, i, g, lower_bound, norm_w):
    B, S, _ = q.shape
    H, Dk, C = HGRN_HEADS, HGRN_HEAD_DIM, HGRN_CHUNK
    nc = S // C
    f32 = jnp.float32
    lb = lower_bound.astype(f32).reshape(H, Dk)
    z = f_pre.astype(f32).reshape(B, S, H, Dk)
    log_f = jnp.log(lb + (1.0 - lb) * jax.nn.sigmoid(z))
    key = (1.0 - lb) * jax.nn.sigmoid(-z)

    def chunks(t):
        return t.astype(f32).reshape(B, nc, C, H, Dk).transpose(1, 0, 3, 2, 4)

    qc = chunks(q.reshape(B, S, H, Dk)) * (Dk ** -0.5)
    kc = chunks(key)
    vc = chunks(i.reshape(B, S, H, Dk))
    bc = jnp.cumsum(chunks(log_f), axis=3)
    causal = jnp.tril(jnp.ones((C, C), dtype=bool))[None, None, :, :, None]

    def step(state, inp):
        qt, kt, vt, bt = inp
        diff = bt[:, :, :, None, :] - bt[:, :, None, :, :]
        decay = jnp.exp(jnp.where(causal, diff, -jnp.inf))
        attn = jnp.sum(qt[:, :, :, None, :] * kt[:, :, None, :, :] * decay, -1)
        o = (jnp.einsum('bhts,bhsv->bhtv', attn, vt)
             + jnp.einsum('bhtk,bhkv->bhtv', qt * jnp.exp(bt), state))
        b_last = bt[:, :, -1]
        state = (jnp.exp(b_last)[..., None] * state
                 + jnp.einsum('bhsk,bhsv->bhkv', kt * jnp.exp(b_last[:, :, None] - bt), vt))
        return state, o

    state0 = jnp.zeros((B, H, Dk, Dk), f32)
    _, o = lax.scan(step, state0, (qc, kc, vc, bc))
    o = o.transpose(1, 0, 3, 2, 4).reshape(B, S, H, Dk)
    o = o * lax.rsqrt(jnp.mean(o * o, -1, keepdims=True) + RMS_EPS) * norm_w.astype(f32)
    gate = g.astype(f32).reshape(B, S, H, Dk)
    o = o * jax.nn.silu(gate)
    return o.reshape(B, S, HGRN_WIDTH).astype(q.dtype)


def moe_ffn(x, router_w, router_bias, w_in_e, w_out_e, w_in_sh, w_out_sh):
    B, S, D = x.shape
    N = B * S
    E = N_EXPERTS
    xf = x.reshape(N, D)
    scores = jax.nn.sigmoid(jnp.einsum('nd,de->ne', xf, router_w).astype(jnp.float32))
    biased = scores + router_bias.astype(jnp.float32)
    grp_score = jnp.sum(lax.top_k(biased.reshape(N, N_EXPERT_GROUPS, E // N_EXPERT_GROUPS), 2)[0], -1)
    _, grp_idx = lax.top_k(grp_score, TOPK_GROUPS)
    grp_mask = jnp.any(grp_idx[:, :, None] == jnp.arange(N_EXPERT_GROUPS)[None, None, :], axis=1)
    expert_mask = jnp.repeat(grp_mask, E // N_EXPERT_GROUPS, axis=1)
    _, idx = lax.top_k(jnp.where(expert_mask, biased, -jnp.inf), TOP_K)
    gate_w = jnp.take_along_axis(scores, idx, axis=1)
    gate_w = gate_w / jnp.sum(gate_w, -1, keepdims=True) * ROUTED_SCALE

    NK = N * TOP_K
    e_flat = idx.reshape(NK)
    tok_flat = jnp.repeat(jnp.arange(N, dtype=jnp.int32), TOP_K)
    w_flat = gate_w.reshape(NK)
    order = jnp.argsort(e_flat)
    e_s, tok_s, w_s = e_flat[order], tok_flat[order], w_flat[order]
    counts = jax.ops.segment_sum(jnp.ones((NK,), jnp.int32), e_flat, num_segments=E)
    padded = (counts + MOE_BLOCK - 1) // MOE_BLOCK * MOE_BLOCK
    start = jnp.cumsum(counts) - counts
    pend = jnp.cumsum(padded)
    pstart = pend - padded
    dest = pstart[e_s] + jnp.arange(NK, dtype=jnp.int32) - start[e_s]
    P = NK + E * MOE_BLOCK
    nb = P // MOE_BLOCK
    tok_buf = jnp.full((P,), N, jnp.int32).at[dest].set(tok_s)
    w_buf = jnp.zeros((P,), jnp.float32).at[dest].set(w_s)
    e_blk = jnp.minimum(jnp.searchsorted(pend, jnp.arange(nb, dtype=jnp.int32) * MOE_BLOCK, side='right'), E - 1)
    x_pad = jnp.concatenate([xf, jnp.zeros((1, D), xf.dtype)], axis=0)

    def step(y, inp):
        tok, wt, e = inp
        hu = x_pad[tok] @ w_in_e[e]
        hg, hv = jnp.split(hu, 2, axis=-1)
        out = (jax.nn.silu(hg) * hv) @ w_out_e[e]
        return y.at[tok].add(out * wt[:, None].astype(out.dtype)), None

    y0 = jnp.zeros((N + 1, D), x.dtype)
    y, _ = lax.scan(step, y0, (tok_buf.reshape(nb, MOE_BLOCK), w_buf.reshape(nb, MOE_BLOCK), e_blk))
    routed = y[:N]
    hs = xf @ w_in_sh
    sg, sv = jnp.split(hs, 2, axis=-1)
    shared = (jax.nn.silu(sg) * sv) @ w_out_sh
    return (routed + shared).reshape(B, S, D)


def setup_inputs(seed: int = 0) -> dict:
    key = jax.random.key(seed)
    ks = jax.random.split(key, 20)
    f32 = jnp.float32

    def normal(k, shape, scale):
        return jax.random.normal(k, shape, f32) * scale

    in_scales = (1.0, 1.0, DEEPNORM_BETA, 1.0, 1.0, DEEPNORM_BETA, 1.0, 1.0, 1.0)
    col_scale = jnp.concatenate([jnp.full((w,), s, f32) for w, s in zip(IN_WIDTHS, in_scales)])
    return {
        'x': normal(ks[0], (BATCH, SEQ, D_MODEL), 1.0),
        'ln_in_g': 1.0 + normal(ks[1], (D_MODEL,), 0.02),
        'ln_in_b': normal(ks[2], (D_MODEL,), 0.02),
        'w_in': normal(ks[3], (DEPTH, D_MODEL, IN_WIDTH), D_MODEL ** -0.5) * col_scale,
        'hgrn_lb_logits': normal(ks[4], (DEPTH + 1, HGRN_WIDTH), 0.5),
        'hgrn_norm_w': 1.0 + normal(ks[5], (DEPTH, HGRN_HEAD_DIM), 0.02),
        'w_branch_att': normal(ks[6], (DEPTH, ATT_GROUP_WIDTH, D_MODEL), ATT_GROUP_WIDTH ** -0.5),
        'w_branch_hgrn': normal(ks[7], (DEPTH, HGRN_WIDTH, D_MODEL), HGRN_WIDTH ** -0.5),
        'w_out': normal(ks[8], (DEPTH, D_MODEL, D_MODEL), D_MODEL ** -0.5 * DEEPNORM_BETA),
        'ln1_g': 1.0 + normal(ks[9], (DEPTH, D_MODEL), 0.02),
        'ln1_b': normal(ks[10], (DEPTH, D_MODEL), 0.02),
        'router_w': normal(ks[11], (DEPTH, D_MODEL, N_EXPERTS), D_MODEL ** -0.5),
        'router_bias': normal(ks[12], (DEPTH, N_EXPERTS), 0.01),
        'expert_w_in': normal(ks[13], (DEPTH, N_EXPERTS, D_MODEL, 2 * EXPERT_DIM), D_MODEL ** -0.5),
        'expert_w_out': normal(ks[14], (DEPTH, N_EXPERTS, EXPERT_DIM, D_MODEL), EXPERT_DIM ** -0.5 * DEEPNORM_BETA),
        'shared_w_in': normal(ks[15], (DEPTH, D_MODEL, 2 * SHARED_DIM), D_MODEL ** -0.5),
        'shared_w_out': normal(ks[16], (DEPTH, SHARED_DIM, D_MODEL), SHARED_DIM ** -0.5 * DEEPNORM_BETA),
        'ln2_g': 1.0 + normal(ks[17], (DEPTH, D_MODEL), 0.02),
        'ln2_b': normal(ks[18], (DEPTH, D_MODEL), 0.02),
    }


def reference(x, ln_in_g, ln_in_b, w_in, hgrn_lb_logits, hgrn_norm_w, w_branch_att,
              w_branch_hgrn, w_out, ln1_g, ln1_b, router_w, router_bias, expert_w_in,
              expert_w_out, shared_w_in, shared_w_out, ln2_g, ln2_b):
    offsets = [int(o) for o in np.cumsum(IN_WIDTHS)[:-1]]
    lower_bounds = jnp.cumsum(jax.nn.softmax(hgrn_lb_logits.astype(jnp.float32), axis=0), axis=0)
    h = layer_norm(x, ln_in_g, ln_in_b)
    for l in range(DEPTH):
        w_parts = jnp.split(w_in[l], offsets, axis=1)
        q_a, k_a, v_a, q_b, f_b, i_b, g_b, gate_a, gate_b = [
            jnp.einsum('bsd,de->bse', h, w) for w in w_parts]
        o_a = dilated_attention_mixer(q_a, k_a, v_a)
        o_b = hgrn2_mixer(q_b, f_b, i_b, g_b, lower_bounds[l], hgrn_norm_w[l])
        y_a = jnp.einsum('bsc,cd->bsd', o_a, w_branch_att[l])
        y_b = jnp.einsum('bsc,cd->bsd', o_b, w_branch_hgrn[l])
        merged = jax.nn.sigmoid(gate_a) * y_a + jax.nn.sigmoid(gate_b) * y_b
        mix = jnp.einsum('bsd,de->bse', merged, w_out[l])
        h = layer_norm(DEEPNORM_ALPHA * h + mix, ln1_g[l], ln1_b[l])
        ffn = moe_ffn(h, router_w[l], router_bias[l], expert_w_in[l], expert_w_out[l],
                      shared_w_in[l], shared_w_out[l])
        h = layer_norm(DEEPNORM_ALPHA * h + ffn, ln2_g[l], ln2_b[l])
    return h
```

```python
import functools

import jax
import jax.numpy as jnp
from jax import lax
from jax.experimental import pallas as pl
from jax.experimental.pallas import tpu as pltpu

F32 = jnp.float32
BF16 = jnp.bfloat16
U32 = jnp.uint32
I32 = jnp.int32

D_MODEL = 1024
ATT_GROUPS = ((128, 1), (512, 4), (2048, 16))
ATT_HEADS = 8
ATT_HEAD_DIM = 64
ATT_GROUP_WIDTH = ATT_HEADS * ATT_HEAD_DIM
ATT_WIDTH = len(ATT_GROUPS) * ATT_GROUP_WIDTH
ATT_BLOCK = 128
HGRN_HEAD_DIM = 128
HGRN_HEADS = D_MODEL // HGRN_HEAD_DIM
HGRN_WIDTH = HGRN_HEADS * HGRN_HEAD_DIM
HGRN_CHUNK = 32
N_EXPERTS = 256
TOP_K = 8
N_EXPERT_GROUPS = 8
TOPK_GROUPS = 4
EXPERT_DIM = 256
SHARED_DIM = 256
ROUTED_SCALE = 2.5
MOE_BLOCK = 128
LN_EPS = 1e-5
RMS_EPS = 1e-6
DEPTH = 1
DEEPNORM_ALPHA = (2 * DEPTH) ** 0.25

VMEM_LIMIT = 56 * 1024 * 1024
LANES = 128

ROW_TILE = 1024
MAIN_COL_TILE = 1536
MIX_ROW_TILE = 512
OUT_ROW_TILE = 512


def _params(sem, vmem=VMEM_LIMIT):
    return pltpu.CompilerParams(dimension_semantics=sem, vmem_limit_bytes=vmem)


def _layer_norm(x, g, b):
    mu = jnp.mean(x, -1, keepdims=True)
    xc = x - mu
    var = jnp.mean(xc * xc, -1, keepdims=True)
    return xc * lax.rsqrt(var + LN_EPS) * g + b


def _sigmoid(x):
    return 1.0 / (1.0 + jnp.exp(-x))


def _dot(a, b):
    return jnp.dot(a, b, preferred_element_type=F32)


def _dot_nt(a, b):
    return lax.dot_general(a, b, (((1,), (1,)), ((), ())), preferred_element_type=F32)


def _dot_tn(a, b):
    return lax.dot_general(a, b, (((0,), (0,)), ((), ())), preferred_element_type=F32)


def _pack_bf16_pair(lo, hi):
    lo_bits = pltpu.bitcast(lo.astype(BF16).astype(F32), U32) >> 16
    hi_bits = pltpu.bitcast(hi.astype(BF16).astype(F32), U32) & jnp.uint32(0xFFFF0000)
    return hi_bits | lo_bits


def _unpack_bf16_pair(w):
    lo = pltpu.bitcast(w << 16, F32)
    hi = pltpu.bitcast(w & jnp.uint32(0xFFFF0000), F32)
    return lo, hi


def _inproj_main_kernel(x_ref, g_ref, b_ref, w_ref, o_ref, h_scr):
    @pl.when(pl.program_id(1) == 0)
    def _():
        h_scr[...] = _layer_norm(x_ref[...], g_ref[...], b_ref[...]).astype(BF16)

    o_ref[...] = _dot(h_scr[...], w_ref[...]).astype(BF16)


def _inproj_main(x2, g, b, w_main):
    n = x2.shape[0]
    width = w_main.shape[1]
    return pl.pallas_call(
        _inproj_main_kernel,
        out_shape=jax.ShapeDtypeStruct((n, width), BF16),
        grid=(n // ROW_TILE, width // MAIN_COL_TILE),
        in_specs=[
            pl.BlockSpec((ROW_TILE, D_MODEL), lambda i, j: (i, 0)),
            pl.BlockSpec((1, D_MODEL), lambda i, j: (0, 0)),
            pl.BlockSpec((1, D_MODEL), lambda i, j: (0, 0)),
            pl.BlockSpec((D_MODEL, MAIN_COL_TILE), lambda i, j: (0, j)),
        ],
        out_specs=pl.BlockSpec((ROW_TILE, MAIN_COL_TILE), lambda i, j: (i, j)),
        scratch_shapes=[pltpu.VMEM((ROW_TILE, D_MODEL), BF16)],
        compiler_params=_params(("parallel", "arbitrary")),
        name="inproj_main",
    )(x2, g, b, w_main)


def _inproj_qkv_kernel(x_ref, g_ref, b_ref, w_ref, o0_ref, o1_ref, o2_ref,
                       hf_scr, h0_scr, h1_scr, h2_scr):
    @pl.when(pl.program_id(1) == 0)
    def _():
        hf = _layer_norm(x_ref[...], g_ref[...], b_ref[...])
        h0_scr[...] = hf.astype(BF16)
        for c in range(D_MODEL // LANES):
            hf_scr[c] = hf[:, c * LANES:(c + 1) * LANES]
        for h_scr, (_, dil) in ((h1_scr, ATT_GROUPS[1]), (h2_scr, ATT_GROUPS[2])):
            rows = ROW_TILE // dil
            for r in range(dil):
                for c in range(D_MODEL // LANES):
                    h_scr[r * rows:(r + 1) * rows, c * LANES:(c + 1) * LANES] = (
                        hf_scr[c, pl.ds(r, rows, stride=dil), :].astype(BF16))

    gw = ATT_GROUP_WIDTH
    o0_ref[0] = _dot(h0_scr[...], w_ref[:, 0:gw]).astype(BF16)
    d1 = ATT_GROUPS[1][1]
    o1_ref[0, 0] = _dot(h1_scr[...], w_ref[:, gw:2 * gw]).astype(BF16).reshape(d1, ROW_TILE // d1, gw)
    d2 = ATT_GROUPS[2][1]
    o2_ref[0, 0] = _dot(h2_scr[...], w_ref[:, 2 * gw:3 * gw]).astype(BF16).reshape(d2, ROW_TILE // d2, gw)


def _inproj_qkv(x2, g, b, w_qkv, batch, seq):
    n = x2.shape[0]
    gw = ATT_GROUP_WIDTH
    tiles_per_seq = seq // ROW_TILE
    d1, d2 = ATT_GROUPS[1][1], ATT_GROUPS[2][1]
    out_shape = (
        jax.ShapeDtypeStruct((3, n, gw), BF16),
        jax.ShapeDtypeStruct((3, batch, d1, seq // d1, gw), BF16),
        jax.ShapeDtypeStruct((3, batch, d2, seq // d2, gw), BF16),
    )
    return pl.pallas_call(
        _inproj_qkv_kernel,
        out_shape=out_shape,
        grid=(n // ROW_TILE, 3),
        in_specs=[
            pl.BlockSpec((ROW_TILE, D_MODEL), lambda i, t: (i, 0)),
            pl.BlockSpec((1, D_MODEL), lambda i, t: (0, 0)),
            pl.BlockSpec((1, D_MODEL), lambda i, t: (0, 0)),
            pl.BlockSpec((D_MODEL, ATT_WIDTH), lambda i, t: (0, t)),
        ],
        out_specs=(
            pl.BlockSpec((1, ROW_TILE, gw), lambda i, t: (t, i, 0)),
            pl.BlockSpec((1, 1, d1, ROW_TILE // d1, gw),
                         lambda i, t: (t, i // tiles_per_seq, 0, i % tiles_per_seq, 0)),
            pl.BlockSpec((1, 1, d2, ROW_TILE // d2, gw),
                         lambda i, t: (t, i // tiles_per_seq, 0, i % tiles_per_seq, 0)),
        ),
        scratch_shapes=[
            pltpu.VMEM((D_MODEL // LANES, ROW_TILE, LANES), F32),
            pltpu.VMEM((ROW_TILE, D_MODEL), BF16),
            pltpu.VMEM((ROW_TILE, D_MODEL), BF16),
            pltpu.VMEM((ROW_TILE, D_MODEL), BF16),
        ],
        compiler_params=_params(("parallel", "arbitrary")),
        name="inproj_qkv",
    )(x2, g, b, w_qkv)


def _attn_kernel(q_ref, kc_ref, kp_ref, vc_ref, vp_ref, o_ref, lse_ref):
    blk = ATT_BLOCK
    has_prev = pl.program_id(2) > 0
    row = lax.broadcasted_iota(I32, (blk, blk), 0)
    col = lax.broadcasted_iota(I32, (blk, blk), 1)
    mask_c = col <= row
    mask_p = (col >= row) & has_prev
    scale = ATT_HEAD_DIM ** -0.5
    neg = -jnp.inf
    for h in range(ATT_HEADS):
        sl = slice(h * ATT_HEAD_DIM, (h + 1) * ATT_HEAD_DIM)
        q = q_ref[:, sl]
        s_c = jnp.where(mask_c, _dot_nt(q, kc_ref[:, sl]) * scale, neg)
        s_p = jnp.where(mask_p, _dot_nt(q, kp_ref[:, sl]) * scale, neg)
        m = jnp.maximum(jnp.max(s_c, -1, keepdims=True), jnp.max(s_p, -1, keepdims=True))
        p_c = jnp.exp(s_c - m)
        p_p = jnp.exp(s_p - m)
        l = jnp.sum(p_c, -1, keepdims=True) + jnp.sum(p_p, -1, keepdims=True)
        o = _dot(p_c.astype(BF16), vc_ref[:, sl]) + _dot(p_p.astype(BF16), vp_ref[:, sl])
        o_ref[:, sl] = (o / l).astype(BF16)
        lse_ref[:, sl] = jnp.broadcast_to(m + jnp.log(l), (blk, ATT_HEAD_DIM))


def _attention_group(qkv):
    _, batch, dil, sub_len, gw = qkv.shape
    blk = ATT_BLOCK

    def spec(t, prev):
        def index(b, r, i):
            return (t, b, r, jnp.maximum(i - 1, 0) if prev else i, 0)
        return pl.BlockSpec((None, None, None, blk, gw), index)

    out_spec = pl.BlockSpec((None, None, blk, gw), lambda b, r, i: (b, r, i, 0))
    return pl.pallas_call(
        _attn_kernel,
        out_shape=(jax.ShapeDtypeStruct((batch, dil, sub_len, gw), BF16),
                   jax.ShapeDtypeStruct((batch, dil, sub_len, gw), F32)),
        grid=(batch, dil, sub_len // blk),
        in_specs=[spec(0, False), spec(1, False), spec(1, True), spec(2, False), spec(2, True)],
        out_specs=(out_spec, out_spec),
        compiler_params=_params(("parallel", "parallel", "arbitrary")),
        name=f"dilated_attention_d{dil}",
    )(qkv, qkv, qkv, qkv, qkv)


HGRN_ROWS = 256


def _hgrn_kernel(q_ref, f_ref, i_ref, g_ref, lb_ref, nw_ref, o_ref, state_scr):
    seq = q_ref.shape[0]
    c = HGRN_CHUNK
    rows = HGRN_ROWS
    nchunk = rows // c
    dk = HGRN_HEAD_DIM
    lb = lb_ref[...]
    one_m_lb = 1.0 - lb
    nw = nw_ref[...]

    r_i = lax.broadcasted_iota(I32, (rows, rows), 0)
    c_i = lax.broadcasted_iota(I32, (rows, rows), 1)
    tri = jnp.where((r_i // c == c_i // c) & (c_i <= r_i), 1.0, 0.0).astype(BF16)
    t_i = lax.broadcasted_iota(I32, (c, c), 0)
    s_i = lax.broadcasted_iota(I32, (c, c), 1)
    causal = s_i <= t_i

    state_scr[...] = jnp.zeros_like(state_scr)

    def body(gi, carry):
        r0 = pl.multiple_of(gi * rows, rows)
        z = f_ref[pl.ds(r0, rows), :].astype(F32)
        log_f = jnp.log(lb + one_m_lb * _sigmoid(z))
        key = one_m_lb * _sigmoid(-z)
        p0 = log_f.astype(BF16)
        rem = log_f - p0.astype(F32)
        p1 = rem.astype(BF16)
        p2 = (rem - p1.astype(F32)).astype(BF16)
        bcum = _dot(tri, p0) + _dot(tri, p1) + _dot(tri, p2)

        q = q_ref[pl.ds(r0, rows), :].astype(F32) * (dk ** -0.5)
        v = i_ref[pl.ds(r0, rows), :]
        st = state_scr[...]
        outs = []
        for j in range(nchunk):
            sl = slice(j * c, (j + 1) * c)
            b_j = bcum[sl]
            b_mid = b_j[c // 2:c // 2 + 1]
            b_last = b_j[c - 1:c]
            q_t = q[sl] * jnp.exp(b_j - b_mid)
            k_t = key[sl] * jnp.exp(b_mid - b_j)
            v_j = v[sl]
            att = jnp.where(causal, _dot_nt(q_t.astype(BF16), k_t.astype(BF16)), 0.0)
            o_j = _dot(att.astype(BF16), v_j)
            q_in = (q_t * jnp.exp(b_mid)).astype(BF16)
            o_j = o_j + _dot_nt(q_in, st.astype(BF16))
            k_st = (k_t * jnp.exp(b_last - b_mid)).astype(BF16)
            st = st * jnp.exp(b_last) + _dot_tn(v_j, k_st)
            outs.append(o_j)
        state_scr[...] = st
        o = jnp.concatenate(outs, axis=0)
        o = o * lax.rsqrt(jnp.mean(o * o, -1, keepdims=True) + RMS_EPS) * nw
        gate = g_ref[pl.ds(r0, rows), :].astype(F32)
        o_ref[pl.ds(r0, rows), :] = (o * gate * _sigmoid(gate)).astype(BF16)
        return carry

    lax.fori_loop(0, seq // rows, body, 0)


def _hgrn(proj, lower_bound, norm_w, batch, seq):
    n = proj.shape[0]
    dk = HGRN_HEAD_DIM
    heads = HGRN_HEADS

    def seg(k):
        return pl.BlockSpec((seq, dk), lambda b, h: (b, k * heads + h))

    return pl.pallas_call(
        _hgrn_kernel,
        out_shape=jax.ShapeDtypeStruct((n, HGRN_WIDTH), BF16),
        grid=(batch, heads),
        in_specs=[seg(0), seg(1), seg(2), seg(3),
                  pl.BlockSpec((1, dk), lambda b, h: (0, h)),
                  pl.BlockSpec((1, dk), lambda b, h: (0, 0))],
        out_specs=pl.BlockSpec((seq, dk), lambda b, h: (b, h)),
        scratch_shapes=[pltpu.VMEM((dk, dk), F32)],
        compiler_params=_params(("parallel", "parallel")),
        name="hgrn2",
    )(proj, proj, proj, proj, lower_bound, norm_w)


def _mix_kernel(x_ref, gin_ref, bin_ref, o0_ref, l0_ref, o1_ref, l1_ref, o2_ref, l2_ref,
                ob_ref, ga_ref, gb_ref, wa_ref, wb_ref, wo_ref, g1_ref, b1_ref,
                rwh_ref, rwl_ref,
                h_ref, hp_ref, lg_ref,
                so1_scr, sl1_scr, so2_scr, sl2_scr):
    tm = x_ref.shape[0]
    for o_ref, l_ref, so_scr, sl_scr, (_, dil) in (
            (o1_ref, l1_ref, so1_scr, sl1_scr, ATT_GROUPS[1]),
            (o2_ref, l2_ref, so2_scr, sl2_scr, ATT_GROUPS[2])):
        for r in range(dil):
            o_r = o_ref[r].astype(F32)
            l_r = l_ref[r]
            for c in range(ATT_GROUP_WIDTH // LANES):
                cs = slice(c * LANES, (c + 1) * LANES)
                so_scr[c, pl.ds(r, tm // dil, stride=dil), :] = o_r[:, cs]
                sl_scr[c, pl.ds(r, tm // dil, stride=dil), :] = l_r[:, cs]

    def natural(scr):
        return jnp.concatenate([scr[c] for c in range(ATT_GROUP_WIDTH // LANES)], axis=-1)

    l0 = l0_ref[...]
    l1 = natural(sl1_scr)
    l2 = natural(sl2_scr)
    m = jnp.maximum(jnp.maximum(l0, l1), l2)
    e0 = jnp.exp(l0 - m)
    e1 = jnp.exp(l1 - m)
    e2 = jnp.exp(l2 - m)
    o_att = (e0 * o0_ref[...].astype(F32) + e1 * natural(so1_scr) + e2 * natural(so2_scr)) / (e0 + e1 + e2)

    y_a = _dot(o_att.astype(BF16), wa_ref[...])
    y_b = _dot(ob_ref[...], wb_ref[...])
    merged = _sigmoid(ga_ref[...].astype(F32)) * y_a + _sigmoid(gb_ref[...].astype(F32)) * y_b
    mix = _dot(merged.astype(BF16), wo_ref[...])
    h_in = _layer_norm(x_ref[...], gin_ref[...], bin_ref[...])
    h1 = _layer_norm(DEEPNORM_ALPHA * h_in + mix, g1_ref[...], b1_ref[...])
    h_ref[...] = h1

    half = D_MODEL // 2
    hp_ref[...] = _pack_bf16_pair(h1[:, :half], h1[:, half:])

    h_hi = h1.astype(BF16)
    h_lo = (h1 - h_hi.astype(F32)).astype(BF16)
    rwh = rwh_ref[...]
    lg_ref[...] = _dot_nt(rwh, h_hi) + _dot_nt(rwh, h_lo) + _dot_nt(rwl_ref[...], h_hi)


def _mix(x2, gin, bin_, att, ob, proj, wa, wb, wo, g1, b1, rwh, rwl, batch, seq):
    n = x2.shape[0]
    tm = MIX_ROW_TILE
    gw = ATT_GROUP_WIDTH
    tiles_per_seq = seq // tm
    (o0, l0), (o1, l1), (o2, l2) = att
    d1, d2 = ATT_GROUPS[1][1], ATT_GROUPS[2][1]

    def full(shape):
        return pl.BlockSpec(shape, lambda i: (0,) * len(shape))

    def dil_spec(dil):
        return pl.BlockSpec((None, dil, tm // dil, gw),
                            lambda i: (i // tiles_per_seq, 0, i % tiles_per_seq, 0))

    nat_spec = pl.BlockSpec((tm, gw), lambda i: (i, 0))
    return pl.pallas_call(
        _mix_kernel,
        out_shape=(jax.ShapeDtypeStruct((n, D_MODEL), F32),
                   jax.ShapeDtypeStruct((n, D_MODEL // 2), U32),
                   jax.ShapeDtypeStruct((N_EXPERTS, n), F32)),
        grid=(n // tm,),
        in_specs=[
            pl.BlockSpec((tm, D_MODEL), lambda i: (i, 0)),
            full((1, D_MODEL)), full((1, D_MODEL)),
            nat_spec, nat_spec,
            dil_spec(d1), dil_spec(d1), dil_spec(d2), dil_spec(d2),
            pl.BlockSpec((tm, HGRN_WIDTH), lambda i: (i, 0)),
            pl.BlockSpec((tm, D_MODEL), lambda i: (i, 4)),
            pl.BlockSpec((tm, D_MODEL), lambda i: (i, 5)),
            full((gw, D_MODEL)), full((HGRN_WIDTH, D_MODEL)), full((D_MODEL, D_MODEL)),
            full((1, D_MODEL)), full((1, D_MODEL)),
            full((N_EXPERTS, D_MODEL)), full((N_EXPERTS, D_MODEL)),
        ],
        out_specs=(pl.BlockSpec((tm, D_MODEL), lambda i: (i, 0)),
                   pl.BlockSpec((tm, D_MODEL // 2), lambda i: (i, 0)),
                   pl.BlockSpec((N_EXPERTS, tm), lambda i: (0, i))),
        scratch_shapes=[pltpu.VMEM((gw // LANES, tm, LANES), F32)] * 4,
        compiler_params=_params(("parallel",)),
        name="branch_mix",
    )(x2, gin, bin_, o0, l0, o1, l1, o2, l2, ob, proj, proj, wa, wb, wo, g1, b1, rwh, rwl)


def _expert_kernel(be_ref, bfirst_ref, bvalid_ref, xs_ref, ws_ref, wi_ref, wo_ref, ys_ref,
                   wi_scr, wo_scr):
    i = pl.program_id(0)

    @pl.when(bvalid_ref[i] == 0)
    def _():
        ys_ref[...] = jnp.zeros_like(ys_ref)

    @pl.when(bvalid_ref[i] > 0)
    def _():
        @pl.when(bfirst_ref[i] == 1)
        def _():
            wi_scr[...] = wi_ref[0].astype(BF16)
            wo_scr[...] = wo_ref[0].astype(BF16)

        half = D_MODEL // 2
        x_lo, x_hi = _unpack_bf16_pair(xs_ref[...])
        hu = (_dot(x_lo.astype(BF16), wi_scr[:half, :]) + _dot(x_hi.astype(BF16), wi_scr[half:, :]))
        hg = hu[:, :EXPERT_DIM]
        hv = hu[:, EXPERT_DIM:]
        act = (hg * _sigmoid(hg) * hv).astype(BF16)
        out = _dot(act, wo_scr[...]) * ws_ref[...]
        ys_ref[...] = _pack_bf16_pair(out[:, :half], out[:, half:])


def _experts(blk_expert, blk_first, blk_valid, xs, ws, w_in_e, w_out_e):
    p = xs.shape[0]
    nb = p // MOE_BLOCK
    half = D_MODEL // 2
    grid_spec = pltpu.PrefetchScalarGridSpec(
        num_scalar_prefetch=3,
        grid=(nb,),
        in_specs=[
            pl.BlockSpec((MOE_BLOCK, half), lambda i, be, bf, bv: (i, 0)),
            pl.BlockSpec((MOE_BLOCK, 1), lambda i, be, bf, bv: (i, 0)),
            pl.BlockSpec((1, D_MODEL, 2 * EXPERT_DIM), lambda i, be, bf, bv: (be[i], 0, 0)),
            pl.BlockSpec((1, EXPERT_DIM, D_MODEL), lambda i, be, bf, bv: (be[i], 0, 0)),
        ],
        out_specs=pl.BlockSpec((MOE_BLOCK, half), lambda i, be, bf, bv: (i, 0)),
        scratch_shapes=[pltpu.VMEM((D_MODEL, 2 * EXPERT_DIM), BF16),
                        pltpu.VMEM((EXPERT_DIM, D_MODEL), BF16)],
    )
    return pl.pallas_call(
        _expert_kernel,
        out_shape=jax.ShapeDtypeStruct((p, half), U32),
        grid_spec=grid_spec,
        compiler_params=_params(("arbitrary",)),
        name="routed_experts",
    )(blk_expert, blk_first, blk_valid, xs, ws, w_in_e, w_out_e)


def _combine_kernel(h_ref, yg_ref, wsi_ref, wso_ref, g2_ref, b2_ref, o_ref):
    h1 = h_ref[...]
    lo, hi = _unpack_bf16_pair(yg_ref[0])
    for k in range(1, TOP_K):
        lo_k, hi_k = _unpack_bf16_pair(yg_ref[k])
        lo = lo + lo_k
        hi = hi + hi_k
    routed = jnp.concatenate([lo, hi], axis=-1)
    hs = _dot(h1.astype(BF16), wsi_ref[...])
    sg = hs[:, :SHARED_DIM]
    sv = hs[:, SHARED_DIM:]
    shared = _dot((sg * _sigmoid(sg) * sv).astype(BF16), wso_ref[...])
    o_ref[...] = _layer_norm(DEEPNORM_ALPHA * h1 + routed + shared, g2_ref[...], b2_ref[...])


def _combine(h1, yg, wsi, wso, g2, b2):
    n = h1.shape[0]
    tm = OUT_ROW_TILE
    half = D_MODEL // 2

    def full(shape):
        return pl.BlockSpec(shape, lambda i: (0,) * len(shape))

    return pl.pallas_call(
        _combine_kernel,
        out_shape=jax.ShapeDtypeStruct((n, D_MODEL), F32),
        grid=(n // tm,),
        in_specs=[
            pl.BlockSpec((tm, D_MODEL), lambda i: (i, 0)),
            pl.BlockSpec((TOP_K, tm, half), lambda i: (0, i, 0)),
            full((D_MODEL, 2 * SHARED_DIM)), full((SHARED_DIM, D_MODEL)),
            full((1, D_MODEL)), full((1, D_MODEL)),
        ],
        out_specs=pl.BlockSpec((tm, D_MODEL), lambda i: (i, 0)),
        compiler_params=_params(("parallel",)),
        name="shared_combine",
    )(h1, yg, wsi, wso, g2, b2)


def _route(logits_t, router_bias):
    e = N_EXPERTS
    scores = jax.nn.sigmoid(logits_t.T)
    n = scores.shape[0]
    biased = scores + router_bias.astype(F32)
    grp = lax.top_k(biased.reshape(n, N_EXPERT_GROUPS, e // N_EXPERT_GROUPS), 2)[0].sum(-1)
    _, grp_idx = lax.top_k(grp, TOPK_GROUPS)
    grp_mask = jnp.any(grp_idx[:, :, None] == jnp.arange(N_EXPERT_GROUPS)[None, None, :], axis=1)
    expert_mask = jnp.repeat(grp_mask, e // N_EXPERT_GROUPS, axis=1)
    _, idx = lax.top_k(jnp.where(expert_mask, biased, -jnp.inf), TOP_K)
    gate_w = jnp.take_along_axis(scores, idx, axis=1)
    gate_w = gate_w / jnp.sum(gate_w, -1, keepdims=True) * ROUTED_SCALE
    return idx.T.astype(I32), gate_w.T


def _dispatch_plan(idx_t):
    k, n = idx_t.shape
    e = N_EXPERTS
    onehot = (idx_t[:, :, None] == jnp.arange(e, dtype=I32)[None, None, :]).any(0).astype(I32)
    before = jnp.cumsum(onehot, axis=0) - onehot
    counts = onehot.sum(0)
    rank = before[jnp.arange(n)[None, :], idx_t]
    padded = (counts + MOE_BLOCK - 1) // MOE_BLOCK * MOE_BLOCK
    pend = jnp.cumsum(padded)
    pstart = pend - padded
    dest = pstart[idx_t] + rank
    p_total = n * k + e * MOE_BLOCK
    nb = p_total // MOE_BLOCK
    blk_start = jnp.arange(nb, dtype=I32) * MOE_BLOCK
    blk_expert_raw = jnp.searchsorted(pend, blk_start, side='right').astype(I32)
    blk_valid = (blk_start < pend[-1]).astype(I32)
    last_expert = jnp.max(jnp.where(counts > 0, jnp.arange(e), 0)).astype(I32)
    blk_expert = jnp.where(blk_valid == 1, jnp.minimum(blk_expert_raw, e - 1), last_expert)
    prev = jnp.concatenate([jnp.full((1,), -1, I32), blk_expert[:-1]])
    blk_first = ((blk_expert != prev) & (blk_valid == 1)).astype(I32)
    return dest.astype(I32), blk_expert, blk_first, blk_valid, p_total


def kernel(x, ln_in_g, ln_in_b, w_in, hgrn_lb_logits, hgrn_norm_w, w_branch_att, w_branch_hgrn,
           w_out, ln1_g, ln1_b, router_w, router_bias, expert_w_in, expert_w_out, shared_w_in,
           shared_w_out, ln2_g, ln2_b):
    batch, seq, d = x.shape
    n = batch * seq
    x2 = x.reshape(n, d)
    row = lambda v: v.reshape(1, -1).astype(F32)

    lower_bounds = jnp.cumsum(jax.nn.softmax(hgrn_lb_logits.astype(F32), axis=0), axis=0)
    l = 0
    w_l = w_in[l]
    w_qkv = w_l[:, :3 * ATT_WIDTH].astype(BF16)
    w_main = w_l[:, 3 * ATT_WIDTH:].astype(BF16)
    gin, bin_ = row(ln_in_g), row(ln_in_b)

    proj = _inproj_main(x2, gin, bin_, w_main)
    qkv0, qkv1, qkv2 = _inproj_qkv(x2, gin, bin_, w_qkv, batch, seq)
    qkv0 = qkv0.reshape(3, batch, 1, seq, ATT_GROUP_WIDTH)
    att = [_attention_group(qkv) for qkv in (qkv0, qkv1, qkv2)]
    o0, l0 = att[0]
    att[0] = (o0.reshape(n, ATT_GROUP_WIDTH), l0.reshape(n, ATT_GROUP_WIDTH))

    ob = _hgrn(proj, row(lower_bounds[l]), row(hgrn_norm_w[l]), batch, seq)

    rw_t = router_w[l].T.astype(F32)
    rwh = rw_t.astype(BF16)
    rwl = (rw_t - rwh.astype(F32)).astype(BF16)
    h1, h1p, logits_t = _mix(
        x2, gin, bin_, att, ob, proj,
        w_branch_att[l].astype(BF16), w_branch_hgrn[l].astype(BF16), w_out[l].astype(BF16),
        row(ln1_g[l]), row(ln1_b[l]), rwh, rwl, batch, seq)

    idx_t, gate_t = _route(logits_t, router_bias[l])
    dest, blk_expert, blk_first, blk_valid, p_total = _dispatch_plan(idx_t)
    tok = jnp.broadcast_to(jnp.arange(n, dtype=I32)[None, :], dest.shape)
    tok_buf = jnp.zeros((p_total,), I32).at[dest.reshape(-1)].set(tok.reshape(-1))
    w_buf = jnp.zeros((p_total,), F32).at[dest.reshape(-1)].set(gate_t.reshape(-1))
    xs = h1p[tok_buf]
    ys = _experts(blk_expert, blk_first, blk_valid, xs, w_buf.reshape(p_total, 1),
                  expert_w_in[l], expert_w_out[l])
    yg = ys[dest]
    out = _combine(h1, yg, shared_w_in[l].astype(BF16), shared_w_out[l].astype(BF16),
                   row(ln2_g[l]), row(ln2_b[l]))
    return out.reshape(batch, seq, d)
```

```python
import functools

import jax
import jax.numpy as jnp
from jax import lax
from jax.experimental import pallas as pl
from jax.experimental.pallas import tpu as pltpu
from jax.experimental.pallas import tpu_sc as plsc

F32 = jnp.float32
BF16 = jnp.bfloat16
U32 = jnp.uint32
I32 = jnp.int32

D_MODEL = 1024
ATT_GROUPS = ((128, 1), (512, 4), (2048, 16))
ATT_HEADS = 8
ATT_HEAD_DIM = 64
ATT_GROUP_WIDTH = ATT_HEADS * ATT_HEAD_DIM
ATT_WIDTH = len(ATT_GROUPS) * ATT_GROUP_WIDTH
ATT_BLOCK = 128
HGRN_HEAD_DIM = 128
HGRN_HEADS = D_MODEL // HGRN_HEAD_DIM
HGRN_WIDTH = HGRN_HEADS * HGRN_HEAD_DIM
HGRN_CHUNK = 32
N_EXPERTS = 256
TOP_K = 8
N_EXPERT_GROUPS = 8
TOPK_GROUPS = 4
EXPERT_DIM = 256
SHARED_DIM = 256
ROUTED_SCALE = 2.5
MOE_BLOCK = 128
LN_EPS = 1e-5
RMS_EPS = 1e-6
DEPTH = 1
DEEPNORM_ALPHA = (2 * DEPTH) ** 0.25

VMEM_LIMIT = 56 * 1024 * 1024
LANES = 128

ROW_TILE = 1024
MAIN_COL_TILE = 1536
MIX_ROW_TILE = 512
OUT_ROW_TILE = 512


def _params(sem, vmem=VMEM_LIMIT):
    return pltpu.CompilerParams(dimension_semantics=sem, vmem_limit_bytes=vmem)


def _layer_norm(x, g, b):
    mu = jnp.mean(x, -1, keepdims=True)
    xc = x - mu
    var = jnp.mean(xc * xc, -1, keepdims=True)
    return xc * lax.rsqrt(var + LN_EPS) * g + b


def _sigmoid(x):
    return 1.0 / (1.0 + jnp.exp(-x))


def _dot(a, b):
    return jnp.dot(a, b, preferred_element_type=F32)


def _dot_nt(a, b):
    return lax.dot_general(a, b, (((1,), (1,)), ((), ())), preferred_element_type=F32)


def _dot_tn(a, b):
    return lax.dot_general(a, b, (((0,), (0,)), ((), ())), preferred_element_type=F32)


def _pack_bf16_pair(lo, hi):
    lo_bits = pltpu.bitcast(lo.astype(BF16).astype(F32), U32) >> 16
    hi_bits = pltpu.bitcast(hi.astype(BF16).astype(F32), U32) & jnp.uint32(0xFFFF0000)
    return hi_bits | lo_bits


def _unpack_bf16_pair(w):
    lo = pltpu.bitcast(w << 16, F32)
    hi = pltpu.bitcast(w & jnp.uint32(0xFFFF0000), F32)
    return lo, hi


PACK_PARTS = 2
PACK_WIDTH = D_MODEL // 2 // PACK_PARTS


def _pack_rows(v):
    half = D_MODEL // 2
    parts = []
    for j in range(PACK_PARTS):
        lo = v[:, j * PACK_WIDTH:(j + 1) * PACK_WIDTH]
        hi = v[:, half + j * PACK_WIDTH:half + (j + 1) * PACK_WIDTH]
        parts.append(pltpu.bitcast(_pack_bf16_pair(lo, hi), I32))
    return parts


def _unpack_rows(parts):
    pairs = [_unpack_bf16_pair(pltpu.bitcast(p, U32)) for p in parts]
    return jnp.concatenate([lo for lo, _ in pairs] + [hi for _, hi in pairs], axis=-1)


def _inproj_main_kernel(x_ref, g_ref, b_ref, w_ref, o_ref, h_scr):
    @pl.when(pl.program_id(1) == 0)
    def _():
        h_scr[...] = _layer_norm(x_ref[...], g_ref[...], b_ref[...]).astype(BF16)

    o_ref[...] = _dot(h_scr[...], w_ref[...]).astype(BF16)


def _inproj_main(x2, g, b, w_main):
    n = x2.shape[0]
    width = w_main.shape[1]
    return pl.pallas_call(
        _inproj_main_kernel,
        out_shape=jax.ShapeDtypeStruct((n, width), BF16),
        grid=(n // ROW_TILE, width // MAIN_COL_TILE),
        in_specs=[
            pl.BlockSpec((ROW_TILE, D_MODEL), lambda i, j: (i, 0)),
            pl.BlockSpec((1, D_MODEL), lambda i, j: (0, 0)),
            pl.BlockSpec((1, D_MODEL), lambda i, j: (0, 0)),
            pl.BlockSpec((D_MODEL, MAIN_COL_TILE), lambda i, j: (0, j)),
        ],
        out_specs=pl.BlockSpec((ROW_TILE, MAIN_COL_TILE), lambda i, j: (i, j)),
        scratch_shapes=[pltpu.VMEM((ROW_TILE, D_MODEL), BF16)],
        compiler_params=_params(("parallel", "arbitrary")),
        name="inproj_main",
    )(x2, g, b, w_main)


def _inproj_qkv_kernel(x_ref, g_ref, b_ref, w_ref, o0_ref, o1_ref, o2_ref,
                       hf_scr, h0_scr, h1_scr, h2_scr):
    @pl.when(pl.program_id(1) == 0)
    def _():
        hf = _layer_norm(x_ref[...], g_ref[...], b_ref[...])
        h0_scr[...] = hf.astype(BF16)
        for c in range(D_MODEL // LANES):
            hf_scr[c] = hf[:, c * LANES:(c + 1) * LANES]
        for h_scr, (_, dil) in ((h1_scr, ATT_GROUPS[1]), (h2_scr, ATT_GROUPS[2])):
            rows = ROW_TILE // dil
            for r in range(dil):
                for c in range(D_MODEL // LANES):
                    h_scr[r * rows:(r + 1) * rows, c * LANES:(c + 1) * LANES] = (
                        hf_scr[c, pl.ds(r, rows, stride=dil), :].astype(BF16))

    gw = ATT_GROUP_WIDTH
    o0_ref[0] = _dot(h0_scr[...], w_ref[:, 0:gw]).astype(BF16)
    d1 = ATT_GROUPS[1][1]
    o1_ref[0, 0] = _dot(h1_scr[...], w_ref[:, gw:2 * gw]).astype(BF16).reshape(d1, ROW_TILE // d1, gw)
    d2 = ATT_GROUPS[2][1]
    o2_ref[0, 0] = _dot(h2_scr[...], w_ref[:, 2 * gw:3 * gw]).astype(BF16).reshape(d2, ROW_TILE // d2, gw)


def _inproj_qkv(x2, g, b, w_qkv, batch, seq):
    n = x2.shape[0]
    gw = ATT_GROUP_WIDTH
    tiles_per_seq = seq // ROW_TILE
    d1, d2 = ATT_GROUPS[1][1], ATT_GROUPS[2][1]
    out_shape = (
        jax.ShapeDtypeStruct((3, n, gw), BF16),
        jax.ShapeDtypeStruct((3, batch, d1, seq // d1, gw), BF16),
        jax.ShapeDtypeStruct((3, batch, d2, seq // d2, gw), BF16),
    )
    return pl.pallas_call(
        _inproj_qkv_kernel,
        out_shape=out_shape,
        grid=(n // ROW_TILE, 3),
        in_specs=[
            pl.BlockSpec((ROW_TILE, D_MODEL), lambda i, t: (i, 0)),
            pl.BlockSpec((1, D_MODEL), lambda i, t: (0, 0)),
            pl.BlockSpec((1, D_MODEL), lambda i, t: (0, 0)),
            pl.BlockSpec((D_MODEL, ATT_WIDTH), lambda i, t: (0, t)),
        ],
        out_specs=(
            pl.BlockSpec((1, ROW_TILE, gw), lambda i, t: (t, i, 0)),
            pl.BlockSpec((1, 1, d1, ROW_TILE // d1, gw),
                         lambda i, t: (t, i // tiles_per_seq, 0, i % tiles_per_seq, 0)),
            pl.BlockSpec((1, 1, d2, ROW_TILE // d2, gw),
                         lambda i, t: (t, i // tiles_per_seq, 0, i % tiles_per_seq, 0)),
        ),
        scratch_shapes=[
            pltpu.VMEM((D_MODEL // LANES, ROW_TILE, LANES), F32),
            pltpu.VMEM((ROW_TILE, D_MODEL), BF16),
            pltpu.VMEM((ROW_TILE, D_MODEL), BF16),
            pltpu.VMEM((ROW_TILE, D_MODEL), BF16),
        ],
        compiler_params=_params(("parallel", "arbitrary")),
        name="inproj_qkv",
    )(x2, g, b, w_qkv)


def _attn_kernel(q_ref, kc_ref, kp_ref, vc_ref, vp_ref, o_ref, lse_ref):
    blk = ATT_BLOCK
    has_prev = pl.program_id(2) > 0
    row = lax.broadcasted_iota(I32, (blk, blk), 0)
    col = lax.broadcasted_iota(I32, (blk, blk), 1)
    mask_c = col <= row
    mask_p = (col >= row) & has_prev
    scale = ATT_HEAD_DIM ** -0.5
    neg = -jnp.inf
    for h in range(ATT_HEADS):
        sl = slice(h * ATT_HEAD_DIM, (h + 1) * ATT_HEAD_DIM)
        q = q_ref[:, sl]
        s_c = jnp.where(mask_c, _dot_nt(q, kc_ref[:, sl]) * scale, neg)
        s_p = jnp.where(mask_p, _dot_nt(q, kp_ref[:, sl]) * scale, neg)
        m = jnp.maximum(jnp.max(s_c, -1, keepdims=True), jnp.max(s_p, -1, keepdims=True))
        p_c = jnp.exp(s_c - m)
        p_p = jnp.exp(s_p - m)
        l = jnp.sum(p_c, -1, keepdims=True) + jnp.sum(p_p, -1, keepdims=True)
        o = _dot(p_c.astype(BF16), vc_ref[:, sl]) + _dot(p_p.astype(BF16), vp_ref[:, sl])
        o_ref[:, sl] = (o / l).astype(BF16)
        lse_ref[:, sl] = jnp.broadcast_to(m + jnp.log(l), (blk, ATT_HEAD_DIM))


def _attention_group(qkv):
    _, batch, dil, sub_len, gw = qkv.shape
    blk = ATT_BLOCK

    def spec(t, prev):
        def index(b, r, i):
            return (t, b, r, jnp.maximum(i - 1, 0) if prev else i, 0)
        return pl.BlockSpec((None, None, None, blk, gw), index)

    out_spec = pl.BlockSpec((None, None, blk, gw), lambda b, r, i: (b, r, i, 0))
    return pl.pallas_call(
        _attn_kernel,
        out_shape=(jax.ShapeDtypeStruct((batch, dil, sub_len, gw), BF16),
                   jax.ShapeDtypeStruct((batch, dil, sub_len, gw), F32)),
        grid=(batch, dil, sub_len // blk),
        in_specs=[spec(0, False), spec(1, False), spec(1, True), spec(2, False), spec(2, True)],
        out_specs=(out_spec, out_spec),
        compiler_params=_params(("parallel", "parallel", "arbitrary")),
        name=f"dilated_attention_d{dil}",
    )(qkv, qkv, qkv, qkv, qkv)


HGRN_ROWS = 256


def _hgrn_kernel(q_ref, f_ref, i_ref, g_ref, lb_ref, nw_ref, o_ref, state_scr):
    seq = q_ref.shape[0]
    c = HGRN_CHUNK
    rows = HGRN_ROWS
    nchunk = rows // c
    dk = HGRN_HEAD_DIM
    lb = lb_ref[...]
    one_m_lb = 1.0 - lb
    nw = nw_ref[...]

    r_i = lax.broadcasted_iota(I32, (rows, rows), 0)
    c_i = lax.broadcasted_iota(I32, (rows, rows), 1)
    tri = jnp.where((r_i // c == c_i // c) & (c_i <= r_i), 1.0, 0.0).astype(BF16)
    t_i = lax.broadcasted_iota(I32, (c, c), 0)
    s_i = lax.broadcasted_iota(I32, (c, c), 1)
    causal = s_i <= t_i

    state_scr[...] = jnp.zeros_like(state_scr)

    def body(gi, carry):
        r0 = pl.multiple_of(gi * rows, rows)
        z = f_ref[pl.ds(r0, rows), :].astype(F32)
        log_f = jnp.log(lb + one_m_lb * _sigmoid(z))
        key = one_m_lb * _sigmoid(-z)
        p0 = log_f.astype(BF16)
        rem = log_f - p0.astype(F32)
        p1 = rem.astype(BF16)
        p2 = (rem - p1.astype(F32)).astype(BF16)
        bcum = _dot(tri, p0) + _dot(tri, p1) + _dot(tri, p2)

        q = q_ref[pl.ds(r0, rows), :].astype(F32) * (dk ** -0.5)
        v = i_ref[pl.ds(r0, rows), :]
        st = state_scr[...]
        outs = []
        for j in range(nchunk):
            sl = slice(j * c, (j + 1) * c)
            b_j = bcum[sl]
            b_mid = b_j[c // 2:c // 2 + 1]
            b_last = b_j[c - 1:c]
            q_t = q[sl] * jnp.exp(b_j - b_mid)
            k_t = key[sl] * jnp.exp(b_mid - b_j)
            v_j = v[sl]
            att = jnp.where(causal, _dot_nt(q_t.astype(BF16), k_t.astype(BF16)), 0.0)
            o_j = _dot(att.astype(BF16), v_j)
            q_in = (q_t * jnp.exp(b_mid)).astype(BF16)
            o_j = o_j + _dot_nt(q_in, st.astype(BF16))
            k_st = (k_t * jnp.exp(b_last - b_mid)).astype(BF16)
            st = st * jnp.exp(b_last) + _dot_tn(v_j, k_st)
            outs.append(o_j)
        state_scr[...] = st
        o = jnp.concatenate(outs, axis=0)
        o = o * lax.rsqrt(jnp.mean(o * o, -1, keepdims=True) + RMS_EPS) * nw
        gate = g_ref[pl.ds(r0, rows), :].astype(F32)
        o_ref[pl.ds(r0, rows), :] = (o * gate * _sigmoid(gate)).astype(BF16)
        return carry

    lax.fori_loop(0, seq // rows, body, 0)


def _hgrn(proj, lower_bound, norm_w, batch, seq):
    n = proj.shape[0]
    dk = HGRN_HEAD_DIM
    heads = HGRN_HEADS

    def seg(k):
        return pl.BlockSpec((seq, dk), lambda b, h: (b, k * heads + h))

    return pl.pallas_call(
        _hgrn_kernel,
        out_shape=jax.ShapeDtypeStruct((n, HGRN_WIDTH), BF16),
        grid=(batch, heads),
        in_specs=[seg(0), seg(1), seg(2), seg(3),
                  pl.BlockSpec((1, dk), lambda b, h: (0, h)),
                  pl.BlockSpec((1, dk), lambda b, h: (0, 0))],
        out_specs=pl.BlockSpec((seq, dk), lambda b, h: (b, h)),
        scratch_shapes=[pltpu.VMEM((dk, dk), F32)],
        compiler_params=_params(("parallel", "parallel")),
        name="hgrn2",
    )(proj, proj, proj, proj, lower_bound, norm_w)


def _mix_kernel(x_ref, gin_ref, bin_ref, o0_ref, l0_ref, o1_ref, l1_ref, o2_ref, l2_ref,
                ob_ref, ga_ref, gb_ref, wa_ref, wb_ref, wo_ref, g1_ref, b1_ref,
                rwh_ref, rwl_ref,
                h_ref, hp0_ref, hp1_ref, lg_ref,
                so1_scr, sl1_scr, so2_scr, sl2_scr):
    tm = x_ref.shape[0]
    for o_ref, l_ref, so_scr, sl_scr, (_, dil) in (
            (o1_ref, l1_ref, so1_scr, sl1_scr, ATT_GROUPS[1]),
            (o2_ref, l2_ref, so2_scr, sl2_scr, ATT_GROUPS[2])):
        for r in range(dil):
            o_r = o_ref[r].astype(F32)
            l_r = l_ref[r]
            for c in range(ATT_GROUP_WIDTH // LANES):
                cs = slice(c * LANES, (c + 1) * LANES)
                so_scr[c, pl.ds(r, tm // dil, stride=dil), :] = o_r[:, cs]
                sl_scr[c, pl.ds(r, tm // dil, stride=dil), :] = l_r[:, cs]

    def natural(scr):
        return jnp.concatenate([scr[c] for c in range(ATT_GROUP_WIDTH // LANES)], axis=-1)

    l0 = l0_ref[...]
    l1 = natural(sl1_scr)
    l2 = natural(sl2_scr)
    m = jnp.maximum(jnp.maximum(l0, l1), l2)
    e0 = jnp.exp(l0 - m)
    e1 = jnp.exp(l1 - m)
    e2 = jnp.exp(l2 - m)
    o_att = (e0 * o0_ref[...].astype(F32) + e1 * natural(so1_scr) + e2 * natural(so2_scr)) / (e0 + e1 + e2)

    y_a = _dot(o_att.astype(BF16), wa_ref[...])
    y_b = _dot(ob_ref[...], wb_ref[...])
    merged = _sigmoid(ga_ref[...].astype(F32)) * y_a + _sigmoid(gb_ref[...].astype(F32)) * y_b
    mix = _dot(merged.astype(BF16), wo_ref[...])
    h_in = _layer_norm(x_ref[...], gin_ref[...], bin_ref[...])
    h1 = _layer_norm(DEEPNORM_ALPHA * h_in + mix, g1_ref[...], b1_ref[...])
    h_ref[...] = h1

    hp0_ref[...], hp1_ref[...] = _pack_rows(h1)

    h_hi = h1.astype(BF16)
    h_lo = (h1 - h_hi.astype(F32)).astype(BF16)
    rwh = rwh_ref[...]
    lg_ref[...] = _dot_nt(rwh, h_hi) + _dot_nt(rwh, h_lo) + _dot_nt(rwl_ref[...], h_hi)


def _mix(x2, gin, bin_, att, ob, proj, wa, wb, wo, g1, b1, rwh, rwl, batch, seq):
    n = x2.shape[0]
    tm = MIX_ROW_TILE
    gw = ATT_GROUP_WIDTH
    tiles_per_seq = seq // tm
    (o0, l0), (o1, l1), (o2, l2) = att
    d1, d2 = ATT_GROUPS[1][1], ATT_GROUPS[2][1]

    def full(shape):
        return pl.BlockSpec(shape, lambda i: (0,) * len(shape))

    def dil_spec(dil):
        return pl.BlockSpec((None, dil, tm // dil, gw),
                            lambda i: (i // tiles_per_seq, 0, i % tiles_per_seq, 0))

    nat_spec = pl.BlockSpec((tm, gw), lambda i: (i, 0))
    return pl.pallas_call(
        _mix_kernel,
        out_shape=(jax.ShapeDtypeStruct((n, D_MODEL), F32),
                   jax.ShapeDtypeStruct((n, PACK_WIDTH), I32),
                   jax.ShapeDtypeStruct((n, PACK_WIDTH), I32),
                   jax.ShapeDtypeStruct((N_EXPERTS, n), F32)),
        grid=(n // tm,),
        in_specs=[
            pl.BlockSpec((tm, D_MODEL), lambda i: (i, 0)),
            full((1, D_MODEL)), full((1, D_MODEL)),
            nat_spec, nat_spec,
            dil_spec(d1), dil_spec(d1), dil_spec(d2), dil_spec(d2),
            pl.BlockSpec((tm, HGRN_WIDTH), lambda i: (i, 0)),
            pl.BlockSpec((tm, D_MODEL), lambda i: (i, 4)),
            pl.BlockSpec((tm, D_MODEL), lambda i: (i, 5)),
            full((gw, D_MODEL)), full((HGRN_WIDTH, D_MODEL)), full((D_MODEL, D_MODEL)),
            full((1, D_MODEL)), full((1, D_MODEL)),
            full((N_EXPERTS, D_MODEL)), full((N_EXPERTS, D_MODEL)),
        ],
        out_specs=(pl.BlockSpec((tm, D_MODEL), lambda i: (i, 0)),
                   pl.BlockSpec((tm, PACK_WIDTH), lambda i: (i, 0)),
                   pl.BlockSpec((tm, PACK_WIDTH), lambda i: (i, 0)),
                   pl.BlockSpec((N_EXPERTS, tm), lambda i: (0, i))),
        scratch_shapes=[pltpu.VMEM((gw // LANES, tm, LANES), F32)] * 4,
        compiler_params=_params(("parallel",)),
        name="branch_mix",
    )(x2, gin, bin_, o0, l0, o1, l1, o2, l2, ob, proj, proj, wa, wb, wo, g1, b1, rwh, rwl)


def _expert_kernel(be_ref, bfirst_ref, brows_ref, xs0_ref, xs1_ref, wi_ref, wo_ref,
                   ys0_ref, ys1_ref, wi_scr, wo_scr):
    i = pl.program_id(0)
    rows = brows_ref[i]

    @pl.when(rows == 0)
    def _():
        ys0_ref[...] = jnp.zeros_like(ys0_ref)
        ys1_ref[...] = jnp.zeros_like(ys1_ref)

    @pl.when(rows > 0)
    def _():
        @pl.when(bfirst_ref[i] == 1)
        def _():
            wi_scr[...] = wi_ref[0].astype(BF16)
            wo_scr[...] = wo_ref[0].astype(BF16)

        live = lax.broadcasted_iota(I32, (MOE_BLOCK, PACK_WIDTH), 0) < rows
        x = _unpack_rows([jnp.where(live, xs0_ref[...], 0), jnp.where(live, xs1_ref[...], 0)])
        hu = _dot(x.astype(BF16), wi_scr[...])
        hg = hu[:, :EXPERT_DIM]
        hv = hu[:, EXPERT_DIM:]
        act = (hg * _sigmoid(hg) * hv).astype(BF16)
        ys0_ref[...], ys1_ref[...] = _pack_rows(_dot(act, wo_scr[...]))


def _experts(blk_expert, blk_first, blk_rows, xs, w_in_e, w_out_e):
    p = xs[0].shape[0]
    nb = p // MOE_BLOCK
    row_spec = pl.BlockSpec((MOE_BLOCK, PACK_WIDTH), lambda i, be, bf, bv: (i, 0))
    grid_spec = pltpu.PrefetchScalarGridSpec(
        num_scalar_prefetch=3,
        grid=(nb,),
        in_specs=[
            row_spec, row_spec,
            pl.BlockSpec((1, D_MODEL, 2 * EXPERT_DIM), lambda i, be, bf, bv: (be[i], 0, 0)),
            pl.BlockSpec((1, EXPERT_DIM, D_MODEL), lambda i, be, bf, bv: (be[i], 0, 0)),
        ],
        out_specs=(row_spec, row_spec),
        scratch_shapes=[pltpu.VMEM((D_MODEL, 2 * EXPERT_DIM), BF16),
                        pltpu.VMEM((EXPERT_DIM, D_MODEL), BF16)],
    )
    return pl.pallas_call(
        _expert_kernel,
        out_shape=(jax.ShapeDtypeStruct((p, PACK_WIDTH), I32),) * PACK_PARTS,
        grid_spec=grid_spec,
        compiler_params=_params(("arbitrary",)),
        name="routed_experts",
    )(blk_expert, blk_first, blk_rows, xs[0], xs[1], w_in_e, w_out_e)


def _combine_kernel(h_ref, yg0_ref, yg1_ref, gate_ref, wsi_ref, wso_ref, g2_ref, b2_ref, o_ref):
    h1 = h_ref[...]
    gate = gate_ref[...]
    routed = None
    for k in range(TOP_K):
        y_k = _unpack_rows([yg0_ref[k], yg1_ref[k]]) * gate[:, k:k + 1]
        routed = y_k if routed is None else routed + y_k
    hs = _dot(h1.astype(BF16), wsi_ref[...])
    sg = hs[:, :SHARED_DIM]
    sv = hs[:, SHARED_DIM:]
    shared = _dot((sg * _sigmoid(sg) * sv).astype(BF16), wso_ref[...])
    o_ref[...] = _layer_norm(DEEPNORM_ALPHA * h1 + routed + shared, g2_ref[...], b2_ref[...])


def _combine(h1, yg, gate_nk, wsi, wso, g2, b2):
    n = h1.shape[0]
    tm = OUT_ROW_TILE
    yg_spec = pl.BlockSpec((TOP_K, tm, PACK_WIDTH), lambda i: (0, i, 0))

    def full(shape):
        return pl.BlockSpec(shape, lambda i: (0,) * len(shape))

    return pl.pallas_call(
        _combine_kernel,
        out_shape=jax.ShapeDtypeStruct((n, D_MODEL), F32),
        grid=(n // tm,),
        in_specs=[
            pl.BlockSpec((tm, D_MODEL), lambda i: (i, 0)),
            yg_spec, yg_spec,
            pl.BlockSpec((tm, TOP_K), lambda i: (i, 0)),
            full((D_MODEL, 2 * SHARED_DIM)), full((SHARED_DIM, D_MODEL)),
            full((1, D_MODEL)), full((1, D_MODEL)),
        ],
        out_specs=pl.BlockSpec((tm, D_MODEL), lambda i: (i, 0)),
        compiler_params=_params(("parallel",)),
        name="shared_combine",
    )(h1, yg[0], yg[1], gate_nk, wsi, wso, g2, b2)


ROUTE_TILE = 512


def _pick_first_max(vals, iota, axis, size):
    m = jnp.max(vals, axis=axis, keepdims=True)
    idx = jnp.min(jnp.where(vals == m, iota, size), axis=axis, keepdims=True)
    return m, idx


def _route_kernel(lg_ref, bias_ref, idx_ref, gate_ref, rank_ref, cnt_ref, carry_scr):
    e, tn = lg_ref.shape
    groups = N_EXPERT_GROUPS
    gsz = e // groups
    neg = -jnp.inf

    @pl.when(pl.program_id(0) == 0)
    def _():
        carry_scr[...] = jnp.zeros_like(carry_scr)

    scores = _sigmoid(lg_ref[...])
    biased = scores + bias_ref[...]

    b3 = biased.reshape(groups, gsz, tn)
    io3 = lax.broadcasted_iota(I32, b3.shape, 1)
    m1, i1 = _pick_first_max(b3, io3, 1, gsz)
    m2 = jnp.max(jnp.where(io3 == i1, neg, b3), axis=1, keepdims=True)
    grp = m1 + m2
    iog = lax.broadcasted_iota(I32, grp.shape, 0)
    keep = jnp.zeros(grp.shape, F32)
    for _ in range(TOPK_GROUPS):
        _, gi = _pick_first_max(grp, iog, 0, groups)
        hit = iog == gi
        keep = jnp.where(hit, 1.0, keep)
        grp = jnp.where(hit, neg, grp)
    masked = jnp.where(keep > 0.0, b3, neg).reshape(e, tn)

    ioe = lax.broadcasted_iota(I32, (e, tn), 0)
    onehot = jnp.zeros((e, tn), F32)
    idxs, gates = [], []
    for _ in range(TOP_K):
        _, ei = _pick_first_max(masked, ioe, 0, e)
        hit = ioe == ei
        gates.append(jnp.sum(jnp.where(hit, scores, 0.0), axis=0, keepdims=True))
        onehot = jnp.where(hit, 1.0, onehot)
        masked = jnp.where(hit, neg, masked)
        idxs.append(ei)
    gsum = gates[0]
    for g in gates[1:]:
        gsum = gsum + g
    idx_ref[...] = jnp.concatenate(idxs, axis=0)
    gate_ref[...] = jnp.concatenate(gates, axis=0) / gsum * ROUTED_SCALE

    t_r = lax.broadcasted_iota(I32, (tn, tn), 0)
    t_c = lax.broadcasted_iota(I32, (tn, tn), 1)
    earlier = jnp.where(t_r < t_c, 1.0, 0.0).astype(BF16)
    oh = onehot.astype(BF16)
    base = carry_scr[...]
    before = _dot(oh, earlier) + jnp.concatenate([base] * (tn // LANES), axis=1)
    ranks = [jnp.sum(jnp.where(ioe == ei, before, 0.0), axis=0, keepdims=True) for ei in idxs]
    rank_ref[...] = jnp.concatenate(ranks, axis=0).astype(I32)
    total = base + _dot(oh, jnp.ones((tn, LANES), BF16))
    carry_scr[...] = total
    cnt_ref[...] = total


def _route(logits_t, bias_col):
    e, n = logits_t.shape
    tn = ROUTE_TILE
    tok_spec = pl.BlockSpec((TOP_K, tn), lambda i: (0, i))
    return pl.pallas_call(
        _route_kernel,
        out_shape=(jax.ShapeDtypeStruct((TOP_K, n), I32),
                   jax.ShapeDtypeStruct((TOP_K, n), F32),
                   jax.ShapeDtypeStruct((TOP_K, n), I32),
                   jax.ShapeDtypeStruct((e, LANES), F32)),
        grid=(n // tn,),
        in_specs=[pl.BlockSpec((e, tn), lambda i: (0, i)),
                  pl.BlockSpec((e, 1), lambda i: (0, 0))],
        out_specs=(tok_spec, tok_spec, tok_spec, pl.BlockSpec((e, LANES), lambda i: (0, 0))),
        scratch_shapes=[pltpu.VMEM((e, LANES), F32)],
        compiler_params=_params(("arbitrary",)),
        name="router_topk",
    )(logits_t, bias_col)


def _plan_kernel(idx_ref, rank_ref, cnt_ref, dest_ref, tab_ref):
    e = cnt_ref.shape[0]
    tn = idx_ref.shape[1]
    nb = tab_ref.shape[1]
    cnt = cnt_ref[...]
    nblk = jnp.floor((cnt + (MOE_BLOCK - 1)) * (1.0 / MOE_BLOCK))
    e_r = lax.broadcasted_iota(I32, (e, e), 0)
    e_c = lax.broadcasted_iota(I32, (e, e), 1)
    lower = jnp.where(e_c < e_r, 1.0, 0.0).astype(BF16)
    start_blk = _dot(lower, nblk.astype(BF16))

    ioe = lax.broadcasted_iota(I32, (e, tn), 0)
    start_row = jnp.concatenate([start_blk * MOE_BLOCK] * (tn // LANES), axis=1)
    idx = idx_ref[...]
    dests = [jnp.sum(jnp.where(ioe == idx[k:k + 1], start_row, 0.0), axis=0, keepdims=True)
             for k in range(TOP_K)]
    dest_ref[...] = jnp.concatenate(dests, axis=0).astype(I32) + rank_ref[...]

    @pl.when(pl.program_id(0) == 0)
    def _():
        sb = start_blk[:, 0:1]
        nbk = nblk[:, 0:1]
        c1 = cnt[:, 0:1]
        j = lax.broadcasted_iota(I32, (e, nb), 1).astype(F32)
        ioe_b = lax.broadcasted_iota(I32, (e, nb), 0)
        inside = (j >= sb) & (j < sb + nbk)
        owner = jnp.sum(jnp.where(inside, ioe_b, 0), axis=0, keepdims=True)
        rows = jnp.sum(jnp.where(inside, jnp.clip(c1 - (j - sb) * MOE_BLOCK, 0.0, MOE_BLOCK), 0.0),
                       axis=0, keepdims=True)
        first = jnp.sum(jnp.where((j == sb) & (nbk > 0.0), 1, 0), axis=0, keepdims=True)
        last_expert = jnp.max(jnp.where(nbk > 0.0, ioe_b, 0), axis=0, keepdims=True)
        used = jnp.sum(jnp.where(inside, 1, 0), axis=0, keepdims=True)
        owner = jnp.where(used > 0, owner, last_expert)
        tab_ref[...] = jnp.concatenate(
            [owner, first, rows.astype(I32), jnp.zeros((5, nb), I32)], axis=0)


def _plan(idx_t, rank_t, counts, nb):
    k, n = idx_t.shape
    e = counts.shape[0]
    tn = ROUTE_TILE
    tok_spec = pl.BlockSpec((k, tn), lambda i: (0, i))
    return pl.pallas_call(
        _plan_kernel,
        out_shape=(jax.ShapeDtypeStruct((k, n), I32), jax.ShapeDtypeStruct((8, nb), I32)),
        grid=(n // tn,),
        in_specs=[tok_spec, tok_spec, pl.BlockSpec((e, LANES), lambda i: (0, 0))],
        out_specs=(tok_spec, pl.BlockSpec((8, nb), lambda i: (0, 0))),
        compiler_params=_params(("arbitrary",)),
        name="dispatch_plan",
    )(idx_t, rank_t, counts)


SC_WINDOW = 128


def _sc_mesh():
    return plsc.VectorSubcoreMesh(core_axis_name="core", subcore_axis_name="subcore")


def _dispatch_rows(parts, dest, p_total):
    n, width = parts[0].shape
    top_k = dest.shape[0]
    out_type = (jax.ShapeDtypeStruct((p_total, width), parts[0].dtype),) * len(parts)

    @functools.partial(pl.kernel, mesh=_sc_mesh(), scratch_types=[], out_type=out_type,
                       name="dispatch_rows")
    def scatter(*refs):
        x_hbms = refs[:len(parts)]
        i_hbm = refs[len(parts)]
        o_hbms = refs[len(parts) + 1:]
        for x_hbm, o_hbm in zip(x_hbms, o_hbms):
            def body(x_vmem, i_vmem, o_hbm=o_hbm):
                for k in range(top_k):
                    pltpu.sync_copy(x_vmem, o_hbm.at[i_vmem.at[k]])

            pltpu.emit_pipeline(
                body,
                grid=(n // SC_WINDOW,),
                in_specs=[pl.BlockSpec((SC_WINDOW, width), lambda i: (i, 0)),
                          pl.BlockSpec((top_k, SC_WINDOW), lambda i: (0, i))],
                out_specs=[],
                core_axis_name=("core", "subcore"),
                dimension_semantics=(pltpu.PARALLEL,),
            )(x_hbm, i_hbm)

    return scatter(*parts, dest)


def _gather_rows(parts, idx_flat):
    count = idx_flat.shape[1]
    width = parts[0].shape[1]
    out_type = (jax.ShapeDtypeStruct((count, width), parts[0].dtype),) * len(parts)

    @functools.partial(pl.kernel, mesh=_sc_mesh(), scratch_types=[], out_type=out_type,
                       name="combine_rows")
    def gather(*refs):
        y_hbms = refs[:len(parts)]
        i_hbm = refs[len(parts)]
        o_hbms = refs[len(parts) + 1:]
        for y_hbm, o_hbm in zip(y_hbms, o_hbms):
            def body(i_vmem, o_vmem, y_hbm=y_hbm):
                pltpu.sync_copy(y_hbm.at[i_vmem.at[0]], o_vmem)

            pltpu.emit_pipeline(
                body,
                grid=(count // SC_WINDOW,),
                in_specs=[pl.BlockSpec((1, SC_WINDOW), lambda i: (0, i))],
                out_specs=[pl.BlockSpec((SC_WINDOW, width), lambda i: (i, 0))],
                core_axis_name=("core", "subcore"),
                dimension_semantics=(pltpu.PARALLEL,),
            )(i_hbm, o_hbm)

    return gather(*parts, idx_flat)


def kernel(x, ln_in_g, ln_in_b, w_in, hgrn_lb_logits, hgrn_norm_w, w_branch_att, w_branch_hgrn,
           w_out, ln1_g, ln1_b, router_w, router_bias, expert_w_in, expert_w_out, shared_w_in,
           shared_w_out, ln2_g, ln2_b):
    batch, seq, d = x.shape
    n = batch * seq
    x2 = x.reshape(n, d)
    row = lambda v: v.reshape(1, -1).astype(F32)

    lower_bounds = jnp.cumsum(jax.nn.softmax(hgrn_lb_logits.astype(F32), axis=0), axis=0)
    l = 0
    w_l = w_in[l]
    w_qkv = w_l[:, :3 * ATT_WIDTH].astype(BF16)
    w_main = w_l[:, 3 * ATT_WIDTH:].astype(BF16)
    gin, bin_ = row(ln_in_g), row(ln_in_b)

    proj = _inproj_main(x2, gin, bin_, w_main)
    qkv0, qkv1, qkv2 = _inproj_qkv(x2, gin, bin_, w_qkv, batch, seq)
    qkv0 = qkv0.reshape(3, batch, 1, seq, ATT_GROUP_WIDTH)
    att = [_attention_group(qkv) for qkv in (qkv0, qkv1, qkv2)]
    o0, l0 = att[0]
    att[0] = (o0.reshape(n, ATT_GROUP_WIDTH), l0.reshape(n, ATT_GROUP_WIDTH))

    ob = _hgrn(proj, row(lower_bounds[l]), row(hgrn_norm_w[l]), batch, seq)

    rw_t = router_w[l].T.astype(F32)
    rwh = rw_t.astype(BF16)
    rwl = (rw_t - rwh.astype(F32)).astype(BF16)
    h1, hp0, hp1, logits_t = _mix(
        x2, gin, bin_, att, ob, proj,
        w_branch_att[l].astype(BF16), w_branch_hgrn[l].astype(BF16), w_out[l].astype(BF16),
        row(ln1_g[l]), row(ln1_b[l]), rwh, rwl, batch, seq)

    idx_t, gate_t, rank_t, counts = _route(logits_t, router_bias[l].reshape(-1, 1).astype(F32))
    p_total = n * TOP_K + N_EXPERTS * MOE_BLOCK
    dest, tables = _plan(idx_t, rank_t, counts, p_total // MOE_BLOCK)
    blk_expert, blk_first, blk_rows = tables[0], tables[1], tables[2]
    xs = _dispatch_rows((hp0, hp1), dest, p_total)
    ys = _experts(blk_expert, blk_first, blk_rows, xs, expert_w_in[l], expert_w_out[l])
    yg = [g.reshape(TOP_K, n, PACK_WIDTH)
          for g in _gather_rows(ys, dest.reshape(1, TOP_K * n))]
    out = _combine(h1, yg, gate_t.T, shared_w_in[l].astype(BF16), shared_w_out[l].astype(BF16),
                   row(ln2_g[l]), row(ln2_b[l]))
    return out.reshape(batch, seq, d)
```

```python
import functools

import jax
import jax.numpy as jnp
from jax import lax
from jax.experimental import pallas as pl
from jax.experimental.pallas import tpu as pltpu
from jax.experimental.pallas import tpu_sc as plsc

F32 = jnp.float32
BF16 = jnp.bfloat16
U32 = jnp.uint32
I32 = jnp.int32

D_MODEL = 1024
ATT_GROUPS = ((128, 1), (512, 4), (2048, 16))
ATT_HEADS = 8
ATT_HEAD_DIM = 64
ATT_GROUP_WIDTH = ATT_HEADS * ATT_HEAD_DIM
ATT_WIDTH = len(ATT_GROUPS) * ATT_GROUP_WIDTH
ATT_BLOCK = 128
HGRN_HEAD_DIM = 128
HGRN_HEADS = D_MODEL // HGRN_HEAD_DIM
HGRN_WIDTH = HGRN_HEADS * HGRN_HEAD_DIM
HGRN_CHUNK = 32
N_EXPERTS = 256
TOP_K = 8
N_EXPERT_GROUPS = 8
TOPK_GROUPS = 4
EXPERT_DIM = 256
SHARED_DIM = 256
ROUTED_SCALE = 2.5
MOE_BLOCK = 128
LN_EPS = 1e-5
RMS_EPS = 1e-6
DEPTH = 1
DEEPNORM_ALPHA = (2 * DEPTH) ** 0.25

VMEM_LIMIT = 56 * 1024 * 1024
LANES = 128

ROW_TILE = 1024
MAIN_COL_TILE = 1536
MIX_ROW_TILE = 512
OUT_ROW_TILE = 512


def _params(sem, vmem=VMEM_LIMIT):
    return pltpu.CompilerParams(dimension_semantics=sem, vmem_limit_bytes=vmem)


def _layer_norm(x, g, b):
    mu = jnp.mean(x, -1, keepdims=True)
    xc = x - mu
    var = jnp.mean(xc * xc, -1, keepdims=True)
    return xc * lax.rsqrt(var + LN_EPS) * g + b


def _sigmoid(x):
    return 1.0 / (1.0 + jnp.exp(-x))


def _dot(a, b):
    return jnp.dot(a, b, preferred_element_type=F32)


def _dot_nt(a, b):
    return lax.dot_general(a, b, (((1,), (1,)), ((), ())), preferred_element_type=F32)


def _dot_tn(a, b):
    return lax.dot_general(a, b, (((0,), (0,)), ((), ())), preferred_element_type=F32)


def _pack_bf16_pair(lo, hi):
    lo_bits = pltpu.bitcast(lo.astype(BF16).astype(F32), U32) >> 16
    hi_bits = pltpu.bitcast(hi.astype(BF16).astype(F32), U32) & jnp.uint32(0xFFFF0000)
    return hi_bits | lo_bits


def _unpack_bf16_pair(w):
    lo = pltpu.bitcast(w << 16, F32)
    hi = pltpu.bitcast(w & jnp.uint32(0xFFFF0000), F32)
    return lo, hi


PACK_PARTS = 2
PACK_WIDTH = D_MODEL // 2 // PACK_PARTS


def _pack_rows(v):
    half = D_MODEL // 2
    parts = []
    for j in range(PACK_PARTS):
        lo = v[:, j * PACK_WIDTH:(j + 1) * PACK_WIDTH]
        hi = v[:, half + j * PACK_WIDTH:half + (j + 1) * PACK_WIDTH]
        parts.append(pltpu.bitcast(_pack_bf16_pair(lo, hi), I32))
    return parts


def _unpack_rows(parts):
    pairs = [_unpack_bf16_pair(pltpu.bitcast(p, U32)) for p in parts]
    return jnp.concatenate([lo for lo, _ in pairs] + [hi for _, hi in pairs], axis=-1)


def _inproj_main_kernel(x_ref, g_ref, b_ref, w_ref, o_ref, h_scr):
    @pl.when(pl.program_id(1) == 0)
    def _():
        h_scr[...] = _layer_norm(x_ref[...], g_ref[...], b_ref[...]).astype(BF16)

    o_ref[...] = _dot(h_scr[...], w_ref[...]).astype(BF16)


def _inproj_main(x2, g, b, w_main):
    n = x2.shape[0]
    width = w_main.shape[1]
    return pl.pallas_call(
        _inproj_main_kernel,
        out_shape=jax.ShapeDtypeStruct((n, width), BF16),
        grid=(n // ROW_TILE, width // MAIN_COL_TILE),
        in_specs=[
            pl.BlockSpec((ROW_TILE, D_MODEL), lambda i, j: (i, 0)),
            pl.BlockSpec((1, D_MODEL), lambda i, j: (0, 0)),
            pl.BlockSpec((1, D_MODEL), lambda i, j: (0, 0)),
            pl.BlockSpec((D_MODEL, MAIN_COL_TILE), lambda i, j: (0, j)),
        ],
        out_specs=pl.BlockSpec((ROW_TILE, MAIN_COL_TILE), lambda i, j: (i, j)),
        scratch_shapes=[pltpu.VMEM((ROW_TILE, D_MODEL), BF16)],
        compiler_params=_params(("parallel", "arbitrary")),
        name="inproj_main",
    )(x2, g, b, w_main)


def _inproj_qkv_kernel(x_ref, g_ref, b_ref, w_ref, o0_ref, o1_ref, o2_ref,
                       hf_scr, h0_scr, h1_scr, h2_scr):
    @pl.when(pl.program_id(1) == 0)
    def _():
        hf = _layer_norm(x_ref[...], g_ref[...], b_ref[...])
        h0_scr[...] = hf.astype(BF16)
        for c in range(D_MODEL // LANES):
            hf_scr[c] = hf[:, c * LANES:(c + 1) * LANES]
        for h_scr, (_, dil) in ((h1_scr, ATT_GROUPS[1]), (h2_scr, ATT_GROUPS[2])):
            rows = ROW_TILE // dil
            for r in range(dil):
                for c in range(D_MODEL // LANES):
                    h_scr[r * rows:(r + 1) * rows, c * LANES:(c + 1) * LANES] = (
                        hf_scr[c, pl.ds(r, rows, stride=dil), :].astype(BF16))

    gw = ATT_GROUP_WIDTH
    o0_ref[0] = _dot(h0_scr[...], w_ref[:, 0:gw]).astype(BF16)
    d1 = ATT_GROUPS[1][1]
    o1_ref[0, 0] = _dot(h1_scr[...], w_ref[:, gw:2 * gw]).astype(BF16).reshape(d1, ROW_TILE // d1, gw)
    d2 = ATT_GROUPS[2][1]
    o2_ref[0, 0] = _dot(h2_scr[...], w_ref[:, 2 * gw:3 * gw]).astype(BF16).reshape(d2, ROW_TILE // d2, gw)


def _inproj_qkv(x2, g, b, w_qkv, batch, seq):
    n = x2.shape[0]
    gw = ATT_GROUP_WIDTH
    tiles_per_seq = seq // ROW_TILE
    d1, d2 = ATT_GROUPS[1][1], ATT_GROUPS[2][1]
    out_shape = (
        jax.ShapeDtypeStruct((3, n, gw), BF16),
        jax.ShapeDtypeStruct((3, batch, d1, seq // d1, gw), BF16),
        jax.ShapeDtypeStruct((3, batch, d2, seq // d2, gw), BF16),
    )
    return pl.pallas_call(
        _inproj_qkv_kernel,
        out_shape=out_shape,
        grid=(n // ROW_TILE, 3),
        in_specs=[
            pl.BlockSpec((ROW_TILE, D_MODEL), lambda i, t: (i, 0)),
            pl.BlockSpec((1, D_MODEL), lambda i, t: (0, 0)),
            pl.BlockSpec((1, D_MODEL), lambda i, t: (0, 0)),
            pl.BlockSpec((D_MODEL, ATT_WIDTH), lambda i, t: (0, t)),
        ],
        out_specs=(
            pl.BlockSpec((1, ROW_TILE, gw), lambda i, t: (t, i, 0)),
            pl.BlockSpec((1, 1, d1, ROW_TILE // d1, gw),
                         lambda i, t: (t, i // tiles_per_seq, 0, i % tiles_per_seq, 0)),
            pl.BlockSpec((1, 1, d2, ROW_TILE // d2, gw),
                         lambda i, t: (t, i // tiles_per_seq, 0, i % tiles_per_seq, 0)),
        ),
        scratch_shapes=[
            pltpu.VMEM((D_MODEL // LANES, ROW_TILE, LANES), F32),
            pltpu.VMEM((ROW_TILE, D_MODEL), BF16),
            pltpu.VMEM((ROW_TILE, D_MODEL), BF16),
            pltpu.VMEM((ROW_TILE, D_MODEL), BF16),
        ],
        compiler_params=_params(("parallel", "arbitrary")),
        name="inproj_qkv",
    )(x2, g, b, w_qkv)


def _attn_kernel(q_ref, kc_ref, kp_ref, vc_ref, vp_ref, o_ref, lse_ref):
    blk = ATT_BLOCK
    has_prev = pl.program_id(2) > 0
    row = lax.broadcasted_iota(I32, (blk, blk), 0)
    col = lax.broadcasted_iota(I32, (blk, blk), 1)
    mask_c = col <= row
    mask_p = (col >= row) & has_prev
    scale = ATT_HEAD_DIM ** -0.5
    neg = -jnp.inf
    for h in range(ATT_HEADS):
        sl = slice(h * ATT_HEAD_DIM, (h + 1) * ATT_HEAD_DIM)
        q = q_ref[:, sl]
        s_c = jnp.where(mask_c, _dot_nt(q, kc_ref[:, sl]) * scale, neg)
        s_p = jnp.where(mask_p, _dot_nt(q, kp_ref[:, sl]) * scale, neg)
        m = jnp.maximum(jnp.max(s_c, -1, keepdims=True), jnp.max(s_p, -1, keepdims=True))
        p_c = jnp.exp(s_c - m)
        p_p = jnp.exp(s_p - m)
        l = jnp.sum(p_c, -1, keepdims=True) + jnp.sum(p_p, -1, keepdims=True)
        o = _dot(p_c.astype(BF16), vc_ref[:, sl]) + _dot(p_p.astype(BF16), vp_ref[:, sl])
        o_ref[:, sl] = (o / l).astype(BF16)
        lse_ref[:, sl] = jnp.broadcast_to(m + jnp.log(l), (blk, ATT_HEAD_DIM))


def _attention_group(qkv):
    _, batch, dil, sub_len, gw = qkv.shape
    blk = ATT_BLOCK

    def spec(t, prev):
        def index(b, r, i):
            return (t, b, r, jnp.maximum(i - 1, 0) if prev else i, 0)
        return pl.BlockSpec((None, None, None, blk, gw), index)

    out_spec = pl.BlockSpec((None, None, blk, gw), lambda b, r, i: (b, r, i, 0))
    return pl.pallas_call(
        _attn_kernel,
        out_shape=(jax.ShapeDtypeStruct((batch, dil, sub_len, gw), BF16),
                   jax.ShapeDtypeStruct((batch, dil, sub_len, gw), F32)),
        grid=(batch, dil, sub_len // blk),
        in_specs=[spec(0, False), spec(1, False), spec(1, True), spec(2, False), spec(2, True)],
        out_specs=(out_spec, out_spec),
        compiler_params=_params(("parallel", "parallel", "arbitrary")),
        name=f"dilated_attention_d{dil}",
    )(qkv, qkv, qkv, qkv, qkv)


HGRN_ROWS = 256


def _hgrn_kernel(q_ref, f_ref, i_ref, g_ref, lb_ref, nw_ref, o_ref, state_scr):
    seq = q_ref.shape[0]
    c = HGRN_CHUNK
    rows = HGRN_ROWS
    nchunk = rows // c
    dk = HGRN_HEAD_DIM
    lb = lb_ref[...]
    one_m_lb = 1.0 - lb
    nw = nw_ref[...]

    r_i = lax.broadcasted_iota(I32, (rows, rows), 0)
    c_i = lax.broadcasted_iota(I32, (rows, rows), 1)
    tri = jnp.where((r_i // c == c_i // c) & (c_i <= r_i), 1.0, 0.0).astype(BF16)
    t_i = lax.broadcasted_iota(I32, (c, c), 0)
    s_i = lax.broadcasted_iota(I32, (c, c), 1)
    causal = s_i <= t_i

    state_scr[...] = jnp.zeros_like(state_scr)

    def body(gi, carry):
        r0 = pl.multiple_of(gi * rows, rows)
        z = f_ref[pl.ds(r0, rows), :].astype(F32)
        log_f = jnp.log(lb + one_m_lb * _sigmoid(z))
        key = one_m_lb * _sigmoid(-z)
        p0 = log_f.astype(BF16)
        rem = log_f - p0.astype(F32)
        p1 = rem.astype(BF16)
        p2 = (rem - p1.astype(F32)).astype(BF16)
        bcum = _dot(tri, p0) + _dot(tri, p1) + _dot(tri, p2)

        q = q_ref[pl.ds(r0, rows), :].astype(F32) * (dk ** -0.5)
        v = i_ref[pl.ds(r0, rows), :]
        st = state_scr[...]
        outs = []
        for j in range(nchunk):
            sl = slice(j * c, (j + 1) * c)
            b_j = bcum[sl]
            b_mid = b_j[c // 2:c // 2 + 1]
            b_last = b_j[c - 1:c]
            q_t = q[sl] * jnp.exp(b_j - b_mid)
            k_t = key[sl] * jnp.exp(b_mid - b_j)
            v_j = v[sl]
            att = jnp.where(causal, _dot_nt(q_t.astype(BF16), k_t.astype(BF16)), 0.0)
            o_j = _dot(att.astype(BF16), v_j)
            q_in = (q_t * jnp.exp(b_mid)).astype(BF16)
            o_j = o_j + _dot_nt(q_in, st.astype(BF16))
            k_st = (k_t * jnp.exp(b_last - b_mid)).astype(BF16)
            st = st * jnp.exp(b_last) + _dot_tn(v_j, k_st)
            outs.append(o_j)
        state_scr[...] = st
        o = jnp.concatenate(outs, axis=0)
        o = o * lax.rsqrt(jnp.mean(o * o, -1, keepdims=True) + RMS_EPS) * nw
        gate = g_ref[pl.ds(r0, rows), :].astype(F32)
        o_ref[pl.ds(r0, rows), :] = (o * gate * _sigmoid(gate)).astype(BF16)
        return carry

    lax.fori_loop(0, seq // rows, body, 0)


def _hgrn(proj, lower_bound, norm_w, batch, seq):
    n = proj.shape[0]
    dk = HGRN_HEAD_DIM
    heads = HGRN_HEADS

    def seg(k):
        return pl.BlockSpec((seq, dk), lambda b, h: (b, k * heads + h))

    return pl.pallas_call(
        _hgrn_kernel,
        out_shape=jax.ShapeDtypeStruct((n, HGRN_WIDTH), BF16),
        grid=(batch, heads),
        in_specs=[seg(0), seg(1), seg(2), seg(3),
                  pl.BlockSpec((1, dk), lambda b, h: (0, h)),
                  pl.BlockSpec((1, dk), lambda b, h: (0, 0))],
        out_specs=pl.BlockSpec((seq, dk), lambda b, h: (b, h)),
        scratch_shapes=[pltpu.VMEM((dk, dk), F32)],
        compiler_params=_params(("parallel", "parallel")),
        name="hgrn2",
    )(proj, proj, proj, proj, lower_bound, norm_w)


def _mix_kernel(x_ref, gin_ref, bin_ref, o0_ref, l0_ref, o1_ref, l1_ref, o2_ref, l2_ref,
                ob_ref, ga_ref, gb_ref, wa_ref, wb_ref, wo_ref, g1_ref, b1_ref,
                rwh_ref, rwl_ref,
                h_ref, hp0_ref, hp1_ref, lg_ref,
                so1_scr, sl1_scr, so2_scr, sl2_scr):
    tm = x_ref.shape[0]
    for o_ref, l_ref, so_scr, sl_scr, (_, dil) in (
            (o1_ref, l1_ref, so1_scr, sl1_scr, ATT_GROUPS[1]),
            (o2_ref, l2_ref, so2_scr, sl2_scr, ATT_GROUPS[2])):
        for r in range(dil):
            o_r = o_ref[r].astype(F32)
            l_r = l_ref[r]
            for c in range(ATT_GROUP_WIDTH // LANES):
                cs = slice(c * LANES, (c + 1) * LANES)
                so_scr[c, pl.ds(r, tm // dil, stride=dil), :] = o_r[:, cs]
                sl_scr[c, pl.ds(r, tm // dil, stride=dil), :] = l_r[:, cs]

    def natural(scr):
        return jnp.concatenate([scr[c] for c in range(ATT_GROUP_WIDTH // LANES)], axis=-1)

    l0 = l0_ref[...]
    l1 = natural(sl1_scr)
    l2 = natural(sl2_scr)
    m = jnp.maximum(jnp.maximum(l0, l1), l2)
    e0 = jnp.exp(l0 - m)
    e1 = jnp.exp(l1 - m)
    e2 = jnp.exp(l2 - m)
    o_att = (e0 * o0_ref[...].astype(F32) + e1 * natural(so1_scr) + e2 * natural(so2_scr)) / (e0 + e1 + e2)

    y_a = _dot(o_att.astype(BF16), wa_ref[...])
    y_b = _dot(ob_ref[...], wb_ref[...])
    merged = _sigmoid(ga_ref[...].astype(F32)) * y_a + _sigmoid(gb_ref[...].astype(F32)) * y_b
    mix = _dot(merged.astype(BF16), wo_ref[...])
    h_in = _layer_norm(x_ref[...], gin_ref[...], bin_ref[...])
    h1 = _layer_norm(DEEPNORM_ALPHA * h_in + mix, g1_ref[...], b1_ref[...])
    h_ref[...] = h1

    hp0_ref[...], hp1_ref[...] = _pack_rows(h1)

    h_hi = h1.astype(BF16)
    h_lo = (h1 - h_hi.astype(F32)).astype(BF16)
    rwh = rwh_ref[...]
    lg_ref[...] = _dot_nt(rwh, h_hi) + _dot_nt(rwh, h_lo) + _dot_nt(rwl_ref[...], h_hi)


def _mix(x2, gin, bin_, att, ob, proj, wa, wb, wo, g1, b1, rwh, rwl, batch, seq):
    n = x2.shape[0]
    tm = MIX_ROW_TILE
    gw = ATT_GROUP_WIDTH
    tiles_per_seq = seq // tm
    (o0, l0), (o1, l1), (o2, l2) = att
    d1, d2 = ATT_GROUPS[1][1], ATT_GROUPS[2][1]

    def full(shape):
        return pl.BlockSpec(shape, lambda i: (0,) * len(shape))

    def dil_spec(dil):
        return pl.BlockSpec((None, dil, tm // dil, gw),
                            lambda i: (i // tiles_per_seq, 0, i % tiles_per_seq, 0))

    nat_spec = pl.BlockSpec((tm, gw), lambda i: (i, 0))
    return pl.pallas_call(
        _mix_kernel,
        out_shape=(jax.ShapeDtypeStruct((n, D_MODEL), F32),
                   jax.ShapeDtypeStruct((n, PACK_WIDTH), I32),
                   jax.ShapeDtypeStruct((n, PACK_WIDTH), I32),
                   jax.ShapeDtypeStruct((N_EXPERTS, n), F32)),
        grid=(n // tm,),
        in_specs=[
            pl.BlockSpec((tm, D_MODEL), lambda i: (i, 0)),
            full((1, D_MODEL)), full((1, D_MODEL)),
            nat_spec, nat_spec,
            dil_spec(d1), dil_spec(d1), dil_spec(d2), dil_spec(d2),
            pl.BlockSpec((tm, HGRN_WIDTH), lambda i: (i, 0)),
            pl.BlockSpec((tm, D_MODEL), lambda i: (i, 4)),
            pl.BlockSpec((tm, D_MODEL), lambda i: (i, 5)),
            full((gw, D_MODEL)), full((HGRN_WIDTH, D_MODEL)), full((D_MODEL, D_MODEL)),
            full((1, D_MODEL)), full((1, D_MODEL)),
            full((N_EXPERTS, D_MODEL)), full((N_EXPERTS, D_MODEL)),
        ],
        out_specs=(pl.BlockSpec((tm, D_MODEL), lambda i: (i, 0)),
                   pl.BlockSpec((tm, PACK_WIDTH), lambda i: (i, 0)),
                   pl.BlockSpec((tm, PACK_WIDTH), lambda i: (i, 0)),
                   pl.BlockSpec((N_EXPERTS, tm), lambda i: (0, i))),
        scratch_shapes=[pltpu.VMEM((gw // LANES, tm, LANES), F32)] * 4,
        compiler_params=_params(("parallel",)),
        name="branch_mix",
    )(x2, gin, bin_, o0, l0, o1, l1, o2, l2, ob, proj, proj, wa, wb, wo, g1, b1, rwh, rwl)


def _expert_kernel(sblk_ref, nblk_ref, cnt_ref, xs0_hbm, xs1_hbm, wi_ref, wo_ref,
                   ys0_hbm, ys1_hbm, xbuf0, xbuf1, ybuf0, ybuf1, in_sem, out_sem, wi_scr, wo_scr):
    e = pl.program_id(0)
    last = pl.num_programs(0) - 1
    first_blk = sblk_ref[e]
    n_blk = nblk_ref[e]
    n_rows = cnt_ref[e]
    total = sblk_ref[last] + nblk_ref[last]
    xs_hbm, ys_hbm = (xs0_hbm, xs1_hbm), (ys0_hbm, ys1_hbm)
    xbuf, ybuf = (xbuf0, xbuf1), (ybuf0, ybuf1)

    def rows_of(b):
        return pl.ds(pl.multiple_of(b * MOE_BLOCK, MOE_BLOCK), MOE_BLOCK)

    def x_copy(b, slot, part):
        return pltpu.make_async_copy(xs_hbm[part].at[rows_of(b)], xbuf[part].at[slot],
                                     in_sem.at[part, slot])

    def y_copy(b, slot, part):
        return pltpu.make_async_copy(ybuf[part].at[slot], ys_hbm[part].at[rows_of(b)],
                                     out_sem.at[part, slot])

    @pl.when((e == 0) & (total > 0))
    def _():
        for part in range(PACK_PARTS):
            x_copy(0, 0, part).start()

    @pl.when(n_blk > 0)
    def _():
        wi_scr[...] = wi_ref[0].astype(BF16)
        wo_scr[...] = wo_ref[0].astype(BF16)

        def block(j, carry):
            b = first_blk + j
            slot = b % 2
            for part in range(PACK_PARTS):
                x_copy(b, slot, part).wait()

            @pl.when(b + 1 < total)
            def _():
                for part in range(PACK_PARTS):
                    x_copy(b + 1, 1 - slot, part).start()

            @pl.when(b >= 2)
            def _():
                for part in range(PACK_PARTS):
                    y_copy(b - 2, slot, part).wait()

            live = (lax.broadcasted_iota(I32, (MOE_BLOCK, PACK_WIDTH), 0)
                    < n_rows - j * MOE_BLOCK)
            x = _unpack_rows([jnp.where(live, xbuf[part][slot], 0) for part in range(PACK_PARTS)])
            hu = _dot(x.astype(BF16), wi_scr[...])
            hg = hu[:, :EXPERT_DIM]
            hv = hu[:, EXPERT_DIM:]
            act = (hg * _sigmoid(hg) * hv).astype(BF16)
            packed = _pack_rows(_dot(act, wo_scr[...]))
            for part in range(PACK_PARTS):
                ybuf[part][slot] = packed[part]
                y_copy(b, slot, part).start()
            return carry

        lax.fori_loop(0, n_blk, block, 0)

    @pl.when(e == last)
    def _():
        for back in (1, 2):
            @pl.when(total >= back)
            def _():
                for part in range(PACK_PARTS):
                    y_copy(total - back, (total - back) % 2, part).wait()


def _experts(start_blk, n_blk, n_rows, xs, w_in_e, w_out_e):
    p = xs[0].shape[0]
    n_exp = w_in_e.shape[0]
    any_spec = pl.BlockSpec(memory_space=pl.ANY)
    row_buf = pltpu.VMEM((2, MOE_BLOCK, PACK_WIDTH), I32)
    grid_spec = pltpu.PrefetchScalarGridSpec(
        num_scalar_prefetch=3,
        grid=(n_exp,),
        in_specs=[
            any_spec, any_spec,
            pl.BlockSpec((1, D_MODEL, 2 * EXPERT_DIM), lambda e, sb, nb, nr: (e, 0, 0)),
            pl.BlockSpec((1, EXPERT_DIM, D_MODEL), lambda e, sb, nb, nr: (e, 0, 0)),
        ],
        out_specs=(any_spec, any_spec),
        scratch_shapes=[row_buf, row_buf, row_buf, row_buf,
                        pltpu.SemaphoreType.DMA((PACK_PARTS, 2)),
                        pltpu.SemaphoreType.DMA((PACK_PARTS, 2)),
                        pltpu.VMEM((D_MODEL, 2 * EXPERT_DIM), BF16),
                        pltpu.VMEM((EXPERT_DIM, D_MODEL), BF16)],
    )
    return pl.pallas_call(
        _expert_kernel,
        out_shape=(jax.ShapeDtypeStruct((p, PACK_WIDTH), I32),) * PACK_PARTS,
        grid_spec=grid_spec,
        compiler_params=_params(("arbitrary",)),
        name="routed_experts",
    )(start_blk, n_blk, n_rows, xs[0], xs[1], w_in_e, w_out_e)


def _combine_kernel(h_ref, yg0_ref, yg1_ref, gate_ref, wsi_ref, wso_ref, g2_ref, b2_ref, o_ref):
    h1 = h_ref[...]
    gate = gate_ref[...]
    routed = None
    for k in range(TOP_K):
        y_k = _unpack_rows([yg0_ref[k], yg1_ref[k]]) * gate[:, k:k + 1]
        routed = y_k if routed is None else routed + y_k
    hs = _dot(h1.astype(BF16), wsi_ref[...])
    sg = hs[:, :SHARED_DIM]
    sv = hs[:, SHARED_DIM:]
    shared = _dot((sg * _sigmoid(sg) * sv).astype(BF16), wso_ref[...])
    o_ref[...] = _layer_norm(DEEPNORM_ALPHA * h1 + routed + shared, g2_ref[...], b2_ref[...])


def _combine(h1, yg, gate_nk, wsi, wso, g2, b2):
    n = h1.shape[0]
    tm = OUT_ROW_TILE
    yg_spec = pl.BlockSpec((TOP_K, tm, PACK_WIDTH), lambda i: (0, i, 0))

    def full(shape):
        return pl.BlockSpec(shape, lambda i: (0,) * len(shape))

    return pl.pallas_call(
        _combine_kernel,
        out_shape=jax.ShapeDtypeStruct((n, D_MODEL), F32),
        grid=(n // tm,),
        in_specs=[
            pl.BlockSpec((tm, D_MODEL), lambda i: (i, 0)),
            yg_spec, yg_spec,
            pl.BlockSpec((tm, TOP_K), lambda i: (i, 0)),
            full((D_MODEL, 2 * SHARED_DIM)), full((SHARED_DIM, D_MODEL)),
            full((1, D_MODEL)), full((1, D_MODEL)),
        ],
        out_specs=pl.BlockSpec((tm, D_MODEL), lambda i: (i, 0)),
        compiler_params=_params(("parallel",)),
        name="shared_combine",
    )(h1, yg[0], yg[1], gate_nk, wsi, wso, g2, b2)


ROUTE_TILE = 512


def _pick_first_max(vals, iota, axis, size):
    m = jnp.max(vals, axis=axis, keepdims=True)
    idx = jnp.min(jnp.where(vals == m, iota, size), axis=axis, keepdims=True)
    return m, idx


def _route_kernel(lg_ref, bias_ref, idx_ref, gate_ref, rank_ref, cnt_ref, carry_scr):
    e, tn = lg_ref.shape
    groups = N_EXPERT_GROUPS
    gsz = e // groups
    neg = -jnp.inf

    @pl.when(pl.program_id(0) == 0)
    def _():
        carry_scr[...] = jnp.zeros_like(carry_scr)

    scores = _sigmoid(lg_ref[...])
    biased = scores + bias_ref[...]

    b3 = biased.reshape(groups, gsz, tn)
    io3 = lax.broadcasted_iota(I32, b3.shape, 1)
    m1, i1 = _pick_first_max(b3, io3, 1, gsz)
    m2 = jnp.max(jnp.where(io3 == i1, neg, b3), axis=1, keepdims=True)
    grp = m1 + m2
    iog = lax.broadcasted_iota(I32, grp.shape, 0)
    keep = jnp.zeros(grp.shape, F32)
    for _ in range(TOPK_GROUPS):
        _, gi = _pick_first_max(grp, iog, 0, groups)
        hit = iog == gi
        keep = jnp.where(hit, 1.0, keep)
        grp = jnp.where(hit, neg, grp)
    masked = jnp.where(keep > 0.0, b3, neg).reshape(e, tn)

    ioe = lax.broadcasted_iota(I32, (e, tn), 0)
    onehot = jnp.zeros((e, tn), F32)
    idxs, gates = [], []
    for _ in range(TOP_K):
        _, ei = _pick_first_max(masked, ioe, 0, e)
        hit = ioe == ei
        gates.append(jnp.sum(jnp.where(hit, scores, 0.0), axis=0, keepdims=True))
        onehot = jnp.where(hit, 1.0, onehot)
        masked = jnp.where(hit, neg, masked)
        idxs.append(ei)
    gsum = gates[0]
    for g in gates[1:]:
        gsum = gsum + g
    idx_ref[...] = jnp.concatenate(idxs, axis=0)
    gate_ref[...] = jnp.concatenate(gates, axis=0) / gsum * ROUTED_SCALE

    t_r = lax.broadcasted_iota(I32, (tn, tn), 0)
    t_c = lax.broadcasted_iota(I32, (tn, tn), 1)
    earlier = jnp.where(t_r < t_c, 1.0, 0.0).astype(BF16)
    oh = onehot.astype(BF16)
    base = carry_scr[...]
    before = _dot(oh, earlier) + jnp.concatenate([base] * (tn // LANES), axis=1)
    ranks = [jnp.sum(jnp.where(ioe == ei, before, 0.0), axis=0, keepdims=True) for ei in idxs]
    rank_ref[...] = jnp.concatenate(ranks, axis=0).astype(I32)
    total = base + _dot(oh, jnp.ones((tn, LANES), BF16))
    carry_scr[...] = total
    cnt_ref[...] = total


def _route(logits_t, bias_col):
    e, n = logits_t.shape
    tn = ROUTE_TILE
    tok_spec = pl.BlockSpec((TOP_K, tn), lambda i: (0, i))
    return pl.pallas_call(
        _route_kernel,
        out_shape=(jax.ShapeDtypeStruct((TOP_K, n), I32),
                   jax.ShapeDtypeStruct((TOP_K, n), F32),
                   jax.ShapeDtypeStruct((TOP_K, n), I32),
                   jax.ShapeDtypeStruct((e, LANES), F32)),
        grid=(n // tn,),
        in_specs=[pl.BlockSpec((e, tn), lambda i: (0, i)),
                  pl.BlockSpec((e, 1), lambda i: (0, 0))],
        out_specs=(tok_spec, tok_spec, tok_spec, pl.BlockSpec((e, LANES), lambda i: (0, 0))),
        scratch_shapes=[pltpu.VMEM((e, LANES), F32)],
        compiler_params=_params(("arbitrary",)),
        name="router_topk",
    )(logits_t, bias_col)


def _plan_kernel(idx_ref, rank_ref, cnt_ref, dest_ref, sblk_ref):
    e = cnt_ref.shape[0]
    tn = idx_ref.shape[1]
    cnt = cnt_ref[...]
    nblk = jnp.floor((cnt + (MOE_BLOCK - 1)) * (1.0 / MOE_BLOCK))
    e_r = lax.broadcasted_iota(I32, (e, e), 0)
    e_c = lax.broadcasted_iota(I32, (e, e), 1)
    lower = jnp.where(e_c < e_r, 1.0, 0.0).astype(BF16)
    start_blk = _dot(lower, nblk.astype(BF16))

    ioe = lax.broadcasted_iota(I32, (e, tn), 0)
    start_row = jnp.concatenate([start_blk * MOE_BLOCK] * (tn // LANES), axis=1)
    idx = idx_ref[...]
    dests = [jnp.sum(jnp.where(ioe == idx[k:k + 1], start_row, 0.0), axis=0, keepdims=True)
             for k in range(TOP_K)]
    dest_ref[...] = jnp.concatenate(dests, axis=0).astype(I32) + rank_ref[...]
    sblk_ref[...] = start_blk.astype(I32)


def _plan(idx_t, rank_t, counts):
    k, n = idx_t.shape
    e = counts.shape[0]
    tn = ROUTE_TILE
    tok_spec = pl.BlockSpec((k, tn), lambda i: (0, i))
    exp_spec = pl.BlockSpec((e, LANES), lambda i: (0, 0))
    return pl.pallas_call(
        _plan_kernel,
        out_shape=(jax.ShapeDtypeStruct((k, n), I32), jax.ShapeDtypeStruct((e, LANES), I32)),
        grid=(n // tn,),
        in_specs=[tok_spec, tok_spec, exp_spec],
        out_specs=(tok_spec, exp_spec),
        compiler_params=_params(("arbitrary",)),
        name="dispatch_plan",
    )(idx_t, rank_t, counts)


SC_WINDOW = 128


def _sc_mesh():
    return plsc.VectorSubcoreMesh(core_axis_name="core", subcore_axis_name="subcore")


def _dispatch_rows(parts, dest, p_total):
    n, width = parts[0].shape
    top_k = dest.shape[0]
    out_type = (jax.ShapeDtypeStruct((p_total, width), parts[0].dtype),) * len(parts)

    @functools.partial(pl.kernel, mesh=_sc_mesh(), scratch_types=[], out_type=out_type,
                       name="dispatch_rows")
    def scatter(*refs):
        x_hbms = refs[:len(parts)]
        i_hbm = refs[len(parts)]
        o_hbms = refs[len(parts) + 1:]
        for x_hbm, o_hbm in zip(x_hbms, o_hbms):
            def body(x_vmem, i_vmem, o_hbm=o_hbm):
                for k in range(top_k):
                    pltpu.sync_copy(x_vmem, o_hbm.at[i_vmem.at[k]])

            pltpu.emit_pipeline(
                body,
                grid=(n // SC_WINDOW,),
                in_specs=[pl.BlockSpec((SC_WINDOW, width), lambda i: (i, 0)),
                          pl.BlockSpec((top_k, SC_WINDOW), lambda i: (0, i))],
                out_specs=[],
                core_axis_name=("core", "subcore"),
                dimension_semantics=(pltpu.PARALLEL,),
            )(x_hbm, i_hbm)

    return scatter(*parts, dest)


def _gather_rows(parts, idx_flat):
    count = idx_flat.shape[1]
    width = parts[0].shape[1]
    out_type = (jax.ShapeDtypeStruct((count, width), parts[0].dtype),) * len(parts)

    @functools.partial(pl.kernel, mesh=_sc_mesh(), scratch_types=[], out_type=out_type,
                       name="combine_rows")
    def gather(*refs):
        y_hbms = refs[:len(parts)]
        i_hbm = refs[len(parts)]
        o_hbms = refs[len(parts) + 1:]
        for y_hbm, o_hbm in zip(y_hbms, o_hbms):
            def body(i_vmem, o_vmem, y_hbm=y_hbm):
                pltpu.sync_copy(y_hbm.at[i_vmem.at[0]], o_vmem)

            pltpu.emit_pipeline(
                body,
                grid=(count // SC_WINDOW,),
                in_specs=[pl.BlockSpec((1, SC_WINDOW), lambda i: (0, i))],
                out_specs=[pl.BlockSpec((SC_WINDOW, width), lambda i: (i, 0))],
                core_axis_name=("core", "subcore"),
                dimension_semantics=(pltpu.PARALLEL,),
            )(i_hbm, o_hbm)

    return gather(*parts, idx_flat)


def kernel(x, ln_in_g, ln_in_b, w_in, hgrn_lb_logits, hgrn_norm_w, w_branch_att, w_branch_hgrn,
           w_out, ln1_g, ln1_b, router_w, router_bias, expert_w_in, expert_w_out, shared_w_in,
           shared_w_out, ln2_g, ln2_b):
    batch, seq, d = x.shape
    n = batch * seq
    x2 = x.reshape(n, d)
    row = lambda v: v.reshape(1, -1).astype(F32)

    lower_bounds = jnp.cumsum(jax.nn.softmax(hgrn_lb_logits.astype(F32), axis=0), axis=0)
    l = 0
    w_l = w_in[l]
    w_qkv = w_l[:, :3 * ATT_WIDTH].astype(BF16)
    w_main = w_l[:, 3 * ATT_WIDTH:].astype(BF16)
    gin, bin_ = row(ln_in_g), row(ln_in_b)

    proj = _inproj_main(x2, gin, bin_, w_main)
    qkv0, qkv1, qkv2 = _inproj_qkv(x2, gin, bin_, w_qkv, batch, seq)
    qkv0 = qkv0.reshape(3, batch, 1, seq, ATT_GROUP_WIDTH)
    att = [_attention_group(qkv) for qkv in (qkv0, qkv1, qkv2)]
    o0, l0 = att[0]
    att[0] = (o0.reshape(n, ATT_GROUP_WIDTH), l0.reshape(n, ATT_GROUP_WIDTH))

    ob = _hgrn(proj, row(lower_bounds[l]), row(hgrn_norm_w[l]), batch, seq)

    rw_t = router_w[l].T.astype(F32)
    rwh = rw_t.astype(BF16)
    rwl = (rw_t - rwh.astype(F32)).astype(BF16)
    h1, hp0, hp1, logits_t = _mix(
        x2, gin, bin_, att, ob, proj,
        w_branch_att[l].astype(BF16), w_branch_hgrn[l].astype(BF16), w_out[l].astype(BF16),
        row(ln1_g[l]), row(ln1_b[l]), rwh, rwl, batch, seq)

    idx_t, gate_t, rank_t, counts = _route(logits_t, router_bias[l].reshape(-1, 1).astype(F32))
    p_total = n * TOP_K + N_EXPERTS * MOE_BLOCK
    dest, start_blk = _plan(idx_t, rank_t, counts)
    n_rows = counts[:, 0].astype(I32)
    n_blk = (n_rows + (MOE_BLOCK - 1)) // MOE_BLOCK
    xs = _dispatch_rows((hp0, hp1), dest, p_total)
    ys = _experts(start_blk[:, 0], n_blk, n_rows, xs, expert_w_in[l], expert_w_out[l])
    yg = [g.reshape(TOP_K, n, PACK_WIDTH)
          for g in _gather_rows(ys, dest.reshape(1, TOP_K * n))]
    out = _combine(h1, yg, gate_t.T, shared_w_in[l].astype(BF16), shared_w_out[l].astype(BF16),
                   row(ln2_g[l]), row(ln2_b[l]))
    return out.reshape(batch, seq, d)
```

```python
import functools

import jax
import jax.numpy as jnp
from jax import lax
from jax.experimental import pallas as pl
from jax.experimental.pallas import tpu as pltpu
from jax.experimental.pallas import tpu_sc as plsc

F32 = jnp.float32
BF16 = jnp.bfloat16
U32 = jnp.uint32
I32 = jnp.int32

D_MODEL = 1024
ATT_GROUPS = ((128, 1), (512, 4), (2048, 16))
ATT_HEADS = 8
ATT_HEAD_DIM = 64
ATT_GROUP_WIDTH = ATT_HEADS * ATT_HEAD_DIM
ATT_WIDTH = len(ATT_GROUPS) * ATT_GROUP_WIDTH
ATT_BLOCK = 128
HGRN_HEAD_DIM = 128
HGRN_HEADS = D_MODEL // HGRN_HEAD_DIM
HGRN_WIDTH = HGRN_HEADS * HGRN_HEAD_DIM
HGRN_CHUNK = 32
N_EXPERTS = 256
TOP_K = 8
N_EXPERT_GROUPS = 8
TOPK_GROUPS = 4
EXPERT_DIM = 256
SHARED_DIM = 256
ROUTED_SCALE = 2.5
MOE_BLOCK = 128
LN_EPS = 1e-5
RMS_EPS = 1e-6
DEPTH = 1
DEEPNORM_ALPHA = (2 * DEPTH) ** 0.25

VMEM_LIMIT = 56 * 1024 * 1024
LANES = 128

ROW_TILE = 1024
MAIN_COL_TILE = 1536
MIX_ROW_TILE = 512
OUT_ROW_TILE = 512


def _params(sem, vmem=VMEM_LIMIT):
    return pltpu.CompilerParams(dimension_semantics=sem, vmem_limit_bytes=vmem)


def _layer_norm(x, g, b):
    mu = jnp.mean(x, -1, keepdims=True)
    xc = x - mu
    var = jnp.mean(xc * xc, -1, keepdims=True)
    return xc * lax.rsqrt(var + LN_EPS) * g + b


def _sigmoid(x):
    return 1.0 / (1.0 + jnp.exp(-x))


def _dot(a, b):
    return jnp.dot(a, b, preferred_element_type=F32)


def _dot_nt(a, b):
    return lax.dot_general(a, b, (((1,), (1,)), ((), ())), preferred_element_type=F32)


def _dot_tn(a, b):
    return lax.dot_general(a, b, (((0,), (0,)), ((), ())), preferred_element_type=F32)


def _pack_bf16_pair(lo, hi):
    lo_bits = pltpu.bitcast(lo.astype(BF16).astype(F32), U32) >> 16
    hi_bits = pltpu.bitcast(hi.astype(BF16).astype(F32), U32) & jnp.uint32(0xFFFF0000)
    return hi_bits | lo_bits


def _unpack_bf16_pair(w):
    lo = pltpu.bitcast(w << 16, F32)
    hi = pltpu.bitcast(w & jnp.uint32(0xFFFF0000), F32)
    return lo, hi


PACK_PARTS = 2
PACK_WIDTH = D_MODEL // 2 // PACK_PARTS


def _pack_rows(v):
    half = D_MODEL // 2
    parts = []
    for j in range(PACK_PARTS):
        lo = v[:, j * PACK_WIDTH:(j + 1) * PACK_WIDTH]
        hi = v[:, half + j * PACK_WIDTH:half + (j + 1) * PACK_WIDTH]
        parts.append(pltpu.bitcast(_pack_bf16_pair(lo, hi), I32))
    return parts


def _unpack_rows(parts):
    pairs = [_unpack_bf16_pair(pltpu.bitcast(p, U32)) for p in parts]
    return jnp.concatenate([lo for lo, _ in pairs] + [hi for _, hi in pairs], axis=-1)


def _inproj_main_kernel(x_ref, g_ref, b_ref, w_ref, o_ref, h_scr):
    @pl.when(pl.program_id(1) == 0)
    def _():
        h_scr[...] = _layer_norm(x_ref[...], g_ref[...], b_ref[...]).astype(BF16)

    o_ref[...] = _dot(h_scr[...], w_ref[...]).astype(BF16)


def _inproj_main(x2, g, b, w_main):
    n = x2.shape[0]
    width = w_main.shape[1]
    return pl.pallas_call(
        _inproj_main_kernel,
        out_shape=jax.ShapeDtypeStruct((n, width), BF16),
        grid=(n // ROW_TILE, width // MAIN_COL_TILE),
        in_specs=[
            pl.BlockSpec((ROW_TILE, D_MODEL), lambda i, j: (i, 0)),
            pl.BlockSpec((1, D_MODEL), lambda i, j: (0, 0)),
            pl.BlockSpec((1, D_MODEL), lambda i, j: (0, 0)),
            pl.BlockSpec((D_MODEL, MAIN_COL_TILE), lambda i, j: (0, j)),
        ],
        out_specs=pl.BlockSpec((ROW_TILE, MAIN_COL_TILE), lambda i, j: (i, j)),
        scratch_shapes=[pltpu.VMEM((ROW_TILE, D_MODEL), BF16)],
        compiler_params=_params(("parallel", "arbitrary")),
        name="inproj_main",
    )(x2, g, b, w_main)


def _inproj_qkv_kernel(x_ref, g_ref, b_ref, w_ref, o0_ref, o1_ref, o2_ref,
                       hf_scr, h0_scr, h1_scr, h2_scr):
    @pl.when(pl.program_id(1) == 0)
    def _():
        hf = _layer_norm(x_ref[...], g_ref[...], b_ref[...])
        h0_scr[...] = hf.astype(BF16)
        for c in range(D_MODEL // LANES):
            hf_scr[c] = hf[:, c * LANES:(c + 1) * LANES]
        for h_scr, (_, dil) in ((h1_scr, ATT_GROUPS[1]), (h2_scr, ATT_GROUPS[2])):
            rows = ROW_TILE // dil
            for r in range(dil):
                for c in range(D_MODEL // LANES):
                    h_scr[r * rows:(r + 1) * rows, c * LANES:(c + 1) * LANES] = (
                        hf_scr[c, pl.ds(r, rows, stride=dil), :].astype(BF16))

    gw = ATT_GROUP_WIDTH
    o0_ref[0] = _dot(h0_scr[...], w_ref[:, 0:gw]).astype(BF16)
    d1 = ATT_GROUPS[1][1]
    o1_ref[0, 0] = _dot(h1_scr[...], w_ref[:, gw:2 * gw]).astype(BF16).reshape(d1, ROW_TILE // d1, gw)
    d2 = ATT_GROUPS[2][1]
    o2_ref[0, 0] = _dot(h2_scr[...], w_ref[:, 2 * gw:3 * gw]).astype(BF16).reshape(d2, ROW_TILE // d2, gw)


def _inproj_qkv(x2, g, b, w_qkv, batch, seq):
    n = x2.shape[0]
    gw = ATT_GROUP_WIDTH
    tiles_per_seq = seq // ROW_TILE
    d1, d2 = ATT_GROUPS[1][1], ATT_GROUPS[2][1]
    out_shape = (
        jax.ShapeDtypeStruct((3, n, gw), BF16),
        jax.ShapeDtypeStruct((3, batch, d1, seq // d1, gw), BF16),
        jax.ShapeDtypeStruct((3, batch, d2, seq // d2, gw), BF16),
    )
    return pl.pallas_call(
        _inproj_qkv_kernel,
        out_shape=out_shape,
        grid=(n // ROW_TILE, 3),
        in_specs=[
            pl.BlockSpec((ROW_TILE, D_MODEL), lambda i, t: (i, 0)),
            pl.BlockSpec((1, D_MODEL), lambda i, t: (0, 0)),
            pl.BlockSpec((1, D_MODEL), lambda i, t: (0, 0)),
            pl.BlockSpec((D_MODEL, ATT_WIDTH), lambda i, t: (0, t)),
        ],
        out_specs=(
            pl.BlockSpec((1, ROW_TILE, gw), lambda i, t: (t, i, 0)),
            pl.BlockSpec((1, 1, d1, ROW_TILE // d1, gw),
                         lambda i, t: (t, i // tiles_per_seq, 0, i % tiles_per_seq, 0)),
            pl.BlockSpec((1, 1, d2, ROW_TILE // d2, gw),
                         lambda i, t: (t, i // tiles_per_seq, 0, i % tiles_per_seq, 0)),
        ),
        scratch_shapes=[
            pltpu.VMEM((D_MODEL // LANES, ROW_TILE, LANES), F32),
            pltpu.VMEM((ROW_TILE, D_MODEL), BF16),
            pltpu.VMEM((ROW_TILE, D_MODEL), BF16),
            pltpu.VMEM((ROW_TILE, D_MODEL), BF16),
        ],
        compiler_params=_params(("parallel", "arbitrary")),
        name="inproj_qkv",
    )(x2, g, b, w_qkv)


def _attn_kernel(q_ref, kc_ref, kp_ref, vc_ref, vp_ref, o_ref, lse_ref):
    blk = ATT_BLOCK
    has_prev = pl.program_id(2) > 0
    row = lax.broadcasted_iota(I32, (blk, blk), 0)
    col = lax.broadcasted_iota(I32, (blk, blk), 1)
    mask_c = col <= row
    mask_p = (col >= row) & has_prev
    scale = ATT_HEAD_DIM ** -0.5
    neg = -jnp.inf
    for h in range(ATT_HEADS):
        sl = slice(h * ATT_HEAD_DIM, (h + 1) * ATT_HEAD_DIM)
        q = q_ref[:, sl]
        s_c = jnp.where(mask_c, _dot_nt(q, kc_ref[:, sl]) * scale, neg)
        s_p = jnp.where(mask_p, _dot_nt(q, kp_ref[:, sl]) * scale, neg)
        m = jnp.maximum(jnp.max(s_c, -1, keepdims=True), jnp.max(s_p, -1, keepdims=True))
        p_c = jnp.exp(s_c - m)
        p_p = jnp.exp(s_p - m)
        l = jnp.sum(p_c, -1, keepdims=True) + jnp.sum(p_p, -1, keepdims=True)
        o = _dot(p_c.astype(BF16), vc_ref[:, sl]) + _dot(p_p.astype(BF16), vp_ref[:, sl])
        o_ref[:, sl] = (o / l).astype(BF16)
        lse_ref[:, sl] = jnp.broadcast_to(m + jnp.log(l), (blk, ATT_HEAD_DIM))


def _attention_group(qkv):
    _, batch, dil, sub_len, gw = qkv.shape
    blk = ATT_BLOCK

    def spec(t, prev):
        def index(b, r, i):
            return (t, b, r, jnp.maximum(i - 1, 0) if prev else i, 0)
        return pl.BlockSpec((None, None, None, blk, gw), index)

    out_spec = pl.BlockSpec((None, None, blk, gw), lambda b, r, i: (b, r, i, 0))
    return pl.pallas_call(
        _attn_kernel,
        out_shape=(jax.ShapeDtypeStruct((batch, dil, sub_len, gw), BF16),
                   jax.ShapeDtypeStruct((batch, dil, sub_len, gw), F32)),
        grid=(batch, dil, sub_len // blk),
        in_specs=[spec(0, False), spec(1, False), spec(1, True), spec(2, False), spec(2, True)],
        out_specs=(out_spec, out_spec),
        compiler_params=_params(("parallel", "parallel", "arbitrary")),
        name=f"dilated_attention_d{dil}",
    )(qkv, qkv, qkv, qkv, qkv)


HGRN_ROWS = 256


def _hgrn_kernel(q_ref, f_ref, i_ref, g_ref, lb_ref, nw_ref, o_ref, state_scr):
    seq = q_ref.shape[0]
    c = HGRN_CHUNK
    rows = HGRN_ROWS
    nchunk = rows // c
    dk = HGRN_HEAD_DIM
    lb = lb_ref[...]
    one_m_lb = 1.0 - lb
    nw = nw_ref[...]

    r_i = lax.broadcasted_iota(I32, (rows, rows), 0)
    c_i = lax.broadcasted_iota(I32, (rows, rows), 1)
    tri = jnp.where((r_i // c == c_i // c) & (c_i <= r_i), 1.0, 0.0).astype(BF16)
    t_i = lax.broadcasted_iota(I32, (c, c), 0)
    s_i = lax.broadcasted_iota(I32, (c, c), 1)
    causal = s_i <= t_i

    state_scr[...] = jnp.zeros_like(state_scr)

    def body(gi, carry):
        r0 = pl.multiple_of(gi * rows, rows)
        z = f_ref[pl.ds(r0, rows), :].astype(F32)
        log_f = jnp.log(lb + one_m_lb * _sigmoid(z))
        key = one_m_lb * _sigmoid(-z)
        p0 = log_f.astype(BF16)
        rem = log_f - p0.astype(F32)
        p1 = rem.astype(BF16)
        p2 = (rem - p1.astype(F32)).astype(BF16)
        bcum = _dot(tri, p0) + _dot(tri, p1) + _dot(tri, p2)

        q = q_ref[pl.ds(r0, rows), :].astype(F32) * (dk ** -0.5)
        v = i_ref[pl.ds(r0, rows), :]
        st = state_scr[...]
        outs = []
        for j in range(nchunk):
            sl = slice(j * c, (j + 1) * c)
            b_j = bcum[sl]
            b_mid = b_j[c // 2:c // 2 + 1]
            b_last = b_j[c - 1:c]
            q_t = q[sl] * jnp.exp(b_j - b_mid)
            k_t = key[sl] * jnp.exp(b_mid - b_j)
            v_j = v[sl]
            att = jnp.where(causal, _dot_nt(q_t.astype(BF16), k_t.astype(BF16)), 0.0)
            o_j = _dot(att.astype(BF16), v_j)
            q_in = (q_t * jnp.exp(b_mid)).astype(BF16)
            o_j = o_j + _dot_nt(q_in, st.astype(BF16))
            k_st = (k_t * jnp.exp(b_last - b_mid)).astype(BF16)
            st = st * jnp.exp(b_last) + _dot_tn(v_j, k_st)
            outs.append(o_j)
        state_scr[...] = st
        o = jnp.concatenate(outs, axis=0)
        o = o * lax.rsqrt(jnp.mean(o * o, -1, keepdims=True) + RMS_EPS) * nw
        gate = g_ref[pl.ds(r0, rows), :].astype(F32)
        o_ref[pl.ds(r0, rows), :] = (o * gate * _sigmoid(gate)).astype(BF16)
        return carry

    lax.fori_loop(0, seq // rows, body, 0)


def _hgrn(proj, lower_bound, norm_w, batch, seq):
    n = proj.shape[0]
    dk = HGRN_HEAD_DIM
    heads = HGRN_HEADS

    def seg(k):
        return pl.BlockSpec((seq, dk), lambda b, h: (b, k * heads + h))

    return pl.pallas_call(
        _hgrn_kernel,
        out_shape=jax.ShapeDtypeStruct((n, HGRN_WIDTH), BF16),
        grid=(batch, heads),
        in_specs=[seg(0), seg(1), seg(2), seg(3),
                  pl.BlockSpec((1, dk), lambda b, h: (0, h)),
                  pl.BlockSpec((1, dk), lambda b, h: (0, 0))],
        out_specs=pl.BlockSpec((seq, dk), lambda b, h: (b, h)),
        scratch_shapes=[pltpu.VMEM((dk, dk), F32)],
        compiler_params=_params(("parallel", "parallel")),
        name="hgrn2",
    )(proj, proj, proj, proj, lower_bound, norm_w)


def _mix_kernel(x_ref, gin_ref, bin_ref, o0_ref, l0_ref, o1_ref, l1_ref, o2_ref, l2_ref,
                ob_ref, ga_ref, gb_ref, wa_ref, wb_ref, wo_ref, g1_ref, b1_ref,
                rwh_ref, rwl_ref,
                h_ref, hp0_ref, hp1_ref, lg_ref,
                so1_scr, sl1_scr, so2_scr, sl2_scr):
    tm = x_ref.shape[0]
    for o_ref, l_ref, so_scr, sl_scr, (_, dil) in (
            (o1_ref, l1_ref, so1_scr, sl1_scr, ATT_GROUPS[1]),
            (o2_ref, l2_ref, so2_scr, sl2_scr, ATT_GROUPS[2])):
        for r in range(dil):
            o_r = o_ref[r].astype(F32)
            l_r = l_ref[r]
            for c in range(ATT_GROUP_WIDTH // LANES):
                cs = slice(c * LANES, (c + 1) * LANES)
                so_scr[c, pl.ds(r, tm // dil, stride=dil), :] = o_r[:, cs]
                sl_scr[c, pl.ds(r, tm // dil, stride=dil), :] = l_r[:, cs]

    def natural(scr):
        return jnp.concatenate([scr[c] for c in range(ATT_GROUP_WIDTH // LANES)], axis=-1)

    l0 = l0_ref[...]
    l1 = natural(sl1_scr)
    l2 = natural(sl2_scr)
    m = jnp.maximum(jnp.maximum(l0, l1), l2)
    e0 = jnp.exp(l0 - m)
    e1 = jnp.exp(l1 - m)
    e2 = jnp.exp(l2 - m)
    o_att = (e0 * o0_ref[...].astype(F32) + e1 * natural(so1_scr) + e2 * natural(so2_scr)) / (e0 + e1 + e2)

    y_a = _dot(o_att.astype(BF16), wa_ref[...])
    y_b = _dot(ob_ref[...], wb_ref[...])
    merged = _sigmoid(ga_ref[...].astype(F32)) * y_a + _sigmoid(gb_ref[...].astype(F32)) * y_b
    mix = _dot(merged.astype(BF16), wo_ref[...])
    h_in = _layer_norm(x_ref[...], gin_ref[...], bin_ref[...])
    h1 = _layer_norm(DEEPNORM_ALPHA * h_in + mix, g1_ref[...], b1_ref[...])
    h_ref[...] = h1

    hp0_ref[...], hp1_ref[...] = _pack_rows(h1)

    h_hi = h1.astype(BF16)
    h_lo = (h1 - h_hi.astype(F32)).astype(BF16)
    rwh = rwh_ref[...]
    lg_ref[...] = _dot_nt(rwh, h_hi) + _dot_nt(rwh, h_lo) + _dot_nt(rwl_ref[...], h_hi)


def _mix(x2, gin, bin_, att, ob, proj, wa, wb, wo, g1, b1, rwh, rwl, batch, seq):
    n = x2.shape[0]
    tm = MIX_ROW_TILE
    gw = ATT_GROUP_WIDTH
    tiles_per_seq = seq // tm
    (o0, l0), (o1, l1), (o2, l2) = att
    d1, d2 = ATT_GROUPS[1][1], ATT_GROUPS[2][1]

    def full(shape):
        return pl.BlockSpec(shape, lambda i: (0,) * len(shape))

    def dil_spec(dil):
        return pl.BlockSpec((None, dil, tm // dil, gw),
                            lambda i: (i // tiles_per_seq, 0, i % tiles_per_seq, 0))

    nat_spec = pl.BlockSpec((tm, gw), lambda i: (i, 0))
    return pl.pallas_call(
        _mix_kernel,
        out_shape=(jax.ShapeDtypeStruct((n, D_MODEL), F32),
                   jax.ShapeDtypeStruct((n, PACK_WIDTH), I32),
                   jax.ShapeDtypeStruct((n, PACK_WIDTH), I32),
                   jax.ShapeDtypeStruct((N_EXPERTS, n), F32)),
        grid=(n // tm,),
        in_specs=[
            pl.BlockSpec((tm, D_MODEL), lambda i: (i, 0)),
            full((1, D_MODEL)), full((1, D_MODEL)),
            nat_spec, nat_spec,
            dil_spec(d1), dil_spec(d1), dil_spec(d2), dil_spec(d2),
            pl.BlockSpec((tm, HGRN_WIDTH), lambda i: (i, 0)),
            pl.BlockSpec((tm, D_MODEL), lambda i: (i, 4)),
            pl.BlockSpec((tm, D_MODEL), lambda i: (i, 5)),
            full((gw, D_MODEL)), full((HGRN_WIDTH, D_MODEL)), full((D_MODEL, D_MODEL)),
            full((1, D_MODEL)), full((1, D_MODEL)),
            full((N_EXPERTS, D_MODEL)), full((N_EXPERTS, D_MODEL)),
        ],
        out_specs=(pl.BlockSpec((tm, D_MODEL), lambda i: (i, 0)),
                   pl.BlockSpec((tm, PACK_WIDTH), lambda i: (i, 0)),
                   pl.BlockSpec((tm, PACK_WIDTH), lambda i: (i, 0)),
                   pl.BlockSpec((N_EXPERTS, tm), lambda i: (0, i))),
        scratch_shapes=[pltpu.VMEM((gw // LANES, tm, LANES), F32)] * 4,
        compiler_params=_params(("parallel",)),
        name="branch_mix",
    )(x2, gin, bin_, o0, l0, o1, l1, o2, l2, ob, proj, proj, wa, wb, wo, g1, b1, rwh, rwl)


ROW_SLOTS = 8


def _expert_kernel(sblk_ref, nblk_ref, cnt_ref, xs0_hbm, xs1_hbm, wi_ref, wo_ref,
                   ys0_hbm, ys1_hbm, xbuf0, xbuf1, ybuf0, ybuf1, in_sem, out_sem, wi_scr, wo_scr):
    e = pl.program_id(0)
    last = pl.num_programs(0) - 1
    first_blk = sblk_ref[e]
    n_blk = nblk_ref[e]
    n_rows = cnt_ref[e]
    total = sblk_ref[last] + nblk_ref[last]
    xs_hbm, ys_hbm = (xs0_hbm, xs1_hbm), (ys0_hbm, ys1_hbm)
    xbuf, ybuf = (xbuf0, xbuf1), (ybuf0, ybuf1)

    def rows_of(b):
        return pl.ds(pl.multiple_of(b * MOE_BLOCK, MOE_BLOCK), MOE_BLOCK)

    def x_copy(b, slot, part):
        return pltpu.make_async_copy(xs_hbm[part].at[rows_of(b)], xbuf[part].at[slot],
                                     in_sem.at[part, slot])

    def y_copy(b, slot, part):
        return pltpu.make_async_copy(ybuf[part].at[slot], ys_hbm[part].at[rows_of(b)],
                                     out_sem.at[part, slot])

    @pl.when(e == 0)
    def _():
        for b in range(ROW_SLOTS - 1):
            @pl.when(b < total)
            def _():
                for part in range(PACK_PARTS):
                    x_copy(b, b, part).start()

    @pl.when(n_blk > 0)
    def _():
        wi_scr[...] = wi_ref[0].astype(BF16)
        wo_scr[...] = wo_ref[0].astype(BF16)

        def block(j, carry):
            b = first_blk + j
            slot = b % ROW_SLOTS
            for part in range(PACK_PARTS):
                x_copy(b, slot, part).wait()

            ahead = b + (ROW_SLOTS - 1)

            @pl.when(ahead < total)
            def _():
                for part in range(PACK_PARTS):
                    x_copy(ahead, ahead % ROW_SLOTS, part).start()

            @pl.when(b >= ROW_SLOTS)
            def _():
                for part in range(PACK_PARTS):
                    y_copy(b - ROW_SLOTS, slot, part).wait()

            live = (lax.broadcasted_iota(I32, (MOE_BLOCK, PACK_WIDTH), 0)
                    < n_rows - j * MOE_BLOCK)
            x = _unpack_rows([jnp.where(live, xbuf[part][slot], 0) for part in range(PACK_PARTS)])
            hu = _dot(x.astype(BF16), wi_scr[...])
            hg = hu[:, :EXPERT_DIM]
            hv = hu[:, EXPERT_DIM:]
            act = (hg * _sigmoid(hg) * hv).astype(BF16)
            packed = _pack_rows(_dot(act, wo_scr[...]))
            for part in range(PACK_PARTS):
                ybuf[part][slot] = packed[part]
                y_copy(b, slot, part).start()
            return carry

        lax.fori_loop(0, n_blk, block, 0)

    @pl.when(e == last)
    def _():
        for back in range(1, ROW_SLOTS + 1):
            @pl.when(total >= back)
            def _():
                for part in range(PACK_PARTS):
                    y_copy(total - back, (total - back) % ROW_SLOTS, part).wait()


def _experts(start_blk, n_blk, n_rows, xs, w_in_e, w_out_e):
    p = xs[0].shape[0]
    n_exp = w_in_e.shape[0]
    any_spec = pl.BlockSpec(memory_space=pl.ANY)
    row_buf = pltpu.VMEM((ROW_SLOTS, MOE_BLOCK, PACK_WIDTH), I32)
    grid_spec = pltpu.PrefetchScalarGridSpec(
        num_scalar_prefetch=3,
        grid=(n_exp,),
        in_specs=[
            any_spec, any_spec,
            pl.BlockSpec((1, D_MODEL, 2 * EXPERT_DIM), lambda e, sb, nb, nr: (e, 0, 0)),
            pl.BlockSpec((1, EXPERT_DIM, D_MODEL), lambda e, sb, nb, nr: (e, 0, 0)),
        ],
        out_specs=(any_spec, any_spec),
        scratch_shapes=[row_buf, row_buf, row_buf, row_buf,
                        pltpu.SemaphoreType.DMA((PACK_PARTS, ROW_SLOTS)),
                        pltpu.SemaphoreType.DMA((PACK_PARTS, ROW_SLOTS)),
                        pltpu.VMEM((D_MODEL, 2 * EXPERT_DIM), BF16),
                        pltpu.VMEM((EXPERT_DIM, D_MODEL), BF16)],
    )
    return pl.pallas_call(
        _expert_kernel,
        out_shape=(jax.ShapeDtypeStruct((p, PACK_WIDTH), I32),) * PACK_PARTS,
        grid_spec=grid_spec,
        compiler_params=_params(("arbitrary",)),
        name="routed_experts",
    )(start_blk, n_blk, n_rows, xs[0], xs[1], w_in_e, w_out_e)


def _combine_kernel(h_ref, yg0_ref, yg1_ref, gate_ref, wsi_ref, wso_ref, g2_ref, b2_ref, o_ref):
    h1 = h_ref[...]
    gate = gate_ref[...]
    routed = None
    for k in range(TOP_K):
        y_k = _unpack_rows([yg0_ref[k], yg1_ref[k]]) * gate[:, k:k + 1]
        routed = y_k if routed is None else routed + y_k
    hs = _dot(h1.astype(BF16), wsi_ref[...])
    sg = hs[:, :SHARED_DIM]
    sv = hs[:, SHARED_DIM:]
    shared = _dot((sg * _sigmoid(sg) * sv).astype(BF16), wso_ref[...])
    o_ref[...] = _layer_norm(DEEPNORM_ALPHA * h1 + routed + shared, g2_ref[...], b2_ref[...])


def _combine(h1, yg, gate_nk, wsi, wso, g2, b2):
    n = h1.shape[0]
    tm = OUT_ROW_TILE
    yg_spec = pl.BlockSpec((TOP_K, tm, PACK_WIDTH), lambda i: (0, i, 0))

    def full(shape):
        return pl.BlockSpec(shape, lambda i: (0,) * len(shape))

    return pl.pallas_call(
        _combine_kernel,
        out_shape=jax.ShapeDtypeStruct((n, D_MODEL), F32),
        grid=(n // tm,),
        in_specs=[
            pl.BlockSpec((tm, D_MODEL), lambda i: (i, 0)),
            yg_spec, yg_spec,
            pl.BlockSpec((tm, TOP_K), lambda i: (i, 0)),
            full((D_MODEL, 2 * SHARED_DIM)), full((SHARED_DIM, D_MODEL)),
            full((1, D_MODEL)), full((1, D_MODEL)),
        ],
        out_specs=pl.BlockSpec((tm, D_MODEL), lambda i: (i, 0)),
        compiler_params=_params(("parallel",)),
        name="shared_combine",
    )(h1, yg[0], yg[1], gate_nk, wsi, wso, g2, b2)


ROUTE_TILE = 512


def _pick_first_max(vals, iota, axis, size):
    m = jnp.max(vals, axis=axis, keepdims=True)
    idx = jnp.min(jnp.where(vals == m, iota, size), axis=axis, keepdims=True)
    return m, idx


def _route_kernel(lg_ref, bias_ref, idx_ref, gate_ref, rank_ref, cnt_ref, carry_scr):
    e, tn = lg_ref.shape
    groups = N_EXPERT_GROUPS
    gsz = e // groups
    neg = -jnp.inf

    @pl.when(pl.program_id(0) == 0)
    def _():
        carry_scr[...] = jnp.zeros_like(carry_scr)

    scores = _sigmoid(lg_ref[...])
    biased = scores + bias_ref[...]

    b3 = biased.reshape(groups, gsz, tn)
    io3 = lax.broadcasted_iota(I32, b3.shape, 1)
    m1, i1 = _pick_first_max(b3, io3, 1, gsz)
    m2 = jnp.max(jnp.where(io3 == i1, neg, b3), axis=1, keepdims=True)
    grp = m1 + m2
    iog = lax.broadcasted_iota(I32, grp.shape, 0)
    keep = jnp.zeros(grp.shape, F32)
    for _ in range(TOPK_GROUPS):
        _, gi = _pick_first_max(grp, iog, 0, groups)
        hit = iog == gi
        keep = jnp.where(hit, 1.0, keep)
        grp = jnp.where(hit, neg, grp)
    masked = jnp.where(keep > 0.0, b3, neg).reshape(e, tn)

    ioe = lax.broadcasted_iota(I32, (e, tn), 0)
    onehot = jnp.zeros((e, tn), F32)
    idxs, gates = [], []
    for _ in range(TOP_K):
        _, ei = _pick_first_max(masked, ioe, 0, e)
        hit = ioe == ei
        gates.append(jnp.sum(jnp.where(hit, scores, 0.0), axis=0, keepdims=True))
        onehot = jnp.where(hit, 1.0, onehot)
        masked = jnp.where(hit, neg, masked)
        idxs.append(ei)
    gsum = gates[0]
    for g in gates[1:]:
        gsum = gsum + g
    idx_ref[...] = jnp.concatenate(idxs, axis=0)
    gate_ref[...] = jnp.concatenate(gates, axis=0) / gsum * ROUTED_SCALE

    t_r = lax.broadcasted_iota(I32, (tn, tn), 0)
    t_c = lax.broadcasted_iota(I32, (tn, tn), 1)
    earlier = jnp.where(t_r < t_c, 1.0, 0.0).astype(BF16)
    oh = onehot.astype(BF16)
    base = carry_scr[...]
    before = _dot(oh, earlier) + jnp.concatenate([base] * (tn // LANES), axis=1)
    ranks = [jnp.sum(jnp.where(ioe == ei, before, 0.0), axis=0, keepdims=True) for ei in idxs]
    rank_ref[...] = jnp.concatenate(ranks, axis=0).astype(I32)
    total = base + _dot(oh, jnp.ones((tn, LANES), BF16))
    carry_scr[...] = total
    cnt_ref[...] = total


def _route(logits_t, bias_col):
    e, n = logits_t.shape
    tn = ROUTE_TILE
    tok_spec = pl.BlockSpec((TOP_K, tn), lambda i: (0, i))
    return pl.pallas_call(
        _route_kernel,
        out_shape=(jax.ShapeDtypeStruct((TOP_K, n), I32),
                   jax.ShapeDtypeStruct((TOP_K, n), F32),
                   jax.ShapeDtypeStruct((TOP_K, n), I32),
                   jax.ShapeDtypeStruct((e, LANES), F32)),
        grid=(n // tn,),
        in_specs=[pl.BlockSpec((e, tn), lambda i: (0, i)),
                  pl.BlockSpec((e, 1), lambda i: (0, 0))],
        out_specs=(tok_spec, tok_spec, tok_spec, pl.BlockSpec((e, LANES), lambda i: (0, 0))),
        scratch_shapes=[pltpu.VMEM((e, LANES), F32)],
        compiler_params=_params(("arbitrary",)),
        name="router_topk",
    )(logits_t, bias_col)


def _plan_kernel(idx_ref, rank_ref, cnt_ref, dest_ref, sblk_ref):
    e = cnt_ref.shape[0]
    tn = idx_ref.shape[1]
    cnt = cnt_ref[...]
    nblk = jnp.floor((cnt + (MOE_BLOCK - 1)) * (1.0 / MOE_BLOCK))
    e_r = lax.broadcasted_iota(I32, (e, e), 0)
    e_c = lax.broadcasted_iota(I32, (e, e), 1)
    lower = jnp.where(e_c < e_r, 1.0, 0.0).astype(BF16)
    start_blk = _dot(lower, nblk.astype(BF16))

    ioe = lax.broadcasted_iota(I32, (e, tn), 0)
    start_row = jnp.concatenate([start_blk * MOE_BLOCK] * (tn // LANES), axis=1)
    idx = idx_ref[...]
    dests = [jnp.sum(jnp.where(ioe == idx[k:k + 1], start_row, 0.0), axis=0, keepdims=True)
             for k in range(TOP_K)]
    dest_ref[...] = jnp.concatenate(dests, axis=0).astype(I32) + rank_ref[...]
    sblk_ref[...] = start_blk.astype(I32)


def _plan(idx_t, rank_t, counts):
    k, n = idx_t.shape
    e = counts.shape[0]
    tn = ROUTE_TILE
    tok_spec = pl.BlockSpec((k, tn), lambda i: (0, i))
    exp_spec = pl.BlockSpec((e, LANES), lambda i: (0, 0))
    return pl.pallas_call(
        _plan_kernel,
        out_shape=(jax.ShapeDtypeStruct((k, n), I32), jax.ShapeDtypeStruct((e, LANES), I32)),
        grid=(n // tn,),
        in_specs=[tok_spec, tok_spec, exp_spec],
        out_specs=(tok_spec, exp_spec),
        compiler_params=_params(("arbitrary",)),
        name="dispatch_plan",
    )(idx_t, rank_t, counts)


SC_WINDOW = 128


def _sc_mesh():
    return plsc.VectorSubcoreMesh(core_axis_name="core", subcore_axis_name="subcore")


def _dispatch_rows(parts, dest, p_total):
    n, width = parts[0].shape
    top_k = dest.shape[0]
    out_type = (jax.ShapeDtypeStruct((p_total, width), parts[0].dtype),) * len(parts)

    @functools.partial(pl.kernel, mesh=_sc_mesh(), scratch_types=[], out_type=out_type,
                       name="dispatch_rows")
    def scatter(*refs):
        x_hbms = refs[:len(parts)]
        i_hbm = refs[len(parts)]
        o_hbms = refs[len(parts) + 1:]
        for x_hbm, o_hbm in zip(x_hbms, o_hbms):
            def body(x_vmem, i_vmem, o_hbm=o_hbm):
                for k in range(top_k):
                    pltpu.sync_copy(x_vmem, o_hbm.at[i_vmem.at[k]])

            pltpu.emit_pipeline(
                body,
                grid=(n // SC_WINDOW,),
                in_specs=[pl.BlockSpec((SC_WINDOW, width), lambda i: (i, 0)),
                          pl.BlockSpec((top_k, SC_WINDOW), lambda i: (0, i))],
                out_specs=[],
                core_axis_name=("core", "subcore"),
                dimension_semantics=(pltpu.PARALLEL,),
            )(x_hbm, i_hbm)

    return scatter(*parts, dest)


def _gather_rows(parts, idx_flat):
    count = idx_flat.shape[1]
    width = parts[0].shape[1]
    out_type = (jax.ShapeDtypeStruct((count, width), parts[0].dtype),) * len(parts)

    @functools.partial(pl.kernel, mesh=_sc_mesh(), scratch_types=[], out_type=out_type,
                       name="combine_rows")
    def gather(*refs):
        y_hbms = refs[:len(parts)]
        i_hbm = refs[len(parts)]
        o_hbms = refs[len(parts) + 1:]
        for y_hbm, o_hbm in zip(y_hbms, o_hbms):
            def body(i_vmem, o_vmem, y_hbm=y_hbm):
                pltpu.sync_copy(y_hbm.at[i_vmem.at[0]], o_vmem)

            pltpu.emit_pipeline(
                body,
                grid=(count // SC_WINDOW,),
                in_specs=[pl.BlockSpec((1, SC_WINDOW), lambda i: (0, i))],
                out_specs=[pl.BlockSpec((SC_WINDOW, width), lambda i: (i, 0))],
                core_axis_name=("core", "subcore"),
                dimension_semantics=(pltpu.PARALLEL,),
            )(i_hbm, o_hbm)

    return gather(*parts, idx_flat)


def kernel(x, ln_in_g, ln_in_b, w_in, hgrn_lb_logits, hgrn_norm_w, w_branch_att, w_branch_hgrn,
           w_out, ln1_g, ln1_b, router_w, router_bias, expert_w_in, expert_w_out, shared_w_in,
           shared_w_out, ln2_g, ln2_b):
    batch, seq, d = x.shape
    n = batch * seq
    x2 = x.reshape(n, d)
    row = lambda v: v.reshape(1, -1).astype(F32)

    lower_bounds = jnp.cumsum(jax.nn.softmax(hgrn_lb_logits.astype(F32), axis=0), axis=0)
    l = 0
    w_l = w_in[l]
    w_qkv = w_l[:, :3 * ATT_WIDTH].astype(BF16)
    w_main = w_l[:, 3 * ATT_WIDTH:].astype(BF16)
    gin, bin_ = row(ln_in_g), row(ln_in_b)

    proj = _inproj_main(x2, gin, bin_, w_main)
    qkv0, qkv1, qkv2 = _inproj_qkv(x2, gin, bin_, w_qkv, batch, seq)
    qkv0 = qkv0.reshape(3, batch, 1, seq, ATT_GROUP_WIDTH)
    att = [_attention_group(qkv) for qkv in (qkv0, qkv1, qkv2)]
    o0, l0 = att[0]
    att[0] = (o0.reshape(n, ATT_GROUP_WIDTH), l0.reshape(n, ATT_GROUP_WIDTH))

    ob = _hgrn(proj, row(lower_bounds[l]), row(hgrn_norm_w[l]), batch, seq)

    rw_t = router_w[l].T.astype(F32)
    rwh = rw_t.astype(BF16)
    rwl = (rw_t - rwh.astype(F32)).astype(BF16)
    h1, hp0, hp1, logits_t = _mix(
        x2, gin, bin_, att, ob, proj,
        w_branch_att[l].astype(BF16), w_branch_hgrn[l].astype(BF16), w_out[l].astype(BF16),
        row(ln1_g[l]), row(ln1_b[l]), rwh, rwl, batch, seq)

    idx_t, gate_t, rank_t, counts = _route(logits_t, router_bias[l].reshape(-1, 1).astype(F32))
    p_total = n * TOP_K + N_EXPERTS * MOE_BLOCK
    dest, start_blk = _plan(idx_t, rank_t, counts)
    n_rows = counts[:, 0].astype(I32)
    n_blk = (n_rows + (MOE_BLOCK - 1)) // MOE_BLOCK
    xs = _dispatch_rows((hp0, hp1), dest, p_total)
    ys = _experts(start_blk[:, 0], n_blk, n_rows, xs, expert_w_in[l], expert_w_out[l])
    yg = [g.reshape(TOP_K, n, PACK_WIDTH)
          for g in _gather_rows(ys, dest.reshape(1, TOP_K * n))]
    out = _combine(h1, yg, gate_t.T, shared_w_in[l].astype(BF16), shared_w_out[l].astype(BF16),
                   row(ln2_g[l]), row(ln2_b[l]))
    return out.reshape(batch, seq, d)
```

```python
import functools

import jax
import jax.numpy as jnp
from jax import lax
from jax.experimental import pallas as pl
from jax.experimental.pallas import tpu as pltpu
from jax.experimental.pallas import tpu_sc as plsc

F32 = jnp.float32
BF16 = jnp.bfloat16
U32 = jnp.uint32
I32 = jnp.int32

D_MODEL = 1024
ATT_GROUPS = ((128, 1), (512, 4), (2048, 16))
ATT_HEADS = 8
ATT_HEAD_DIM = 64
ATT_GROUP_WIDTH = ATT_HEADS * ATT_HEAD_DIM
ATT_WIDTH = len(ATT_GROUPS) * ATT_GROUP_WIDTH
ATT_BLOCK = 128
HGRN_HEAD_DIM = 128
HGRN_HEADS = D_MODEL // HGRN_HEAD_DIM
HGRN_WIDTH = HGRN_HEADS * HGRN_HEAD_DIM
HGRN_CHUNK = 32
N_EXPERTS = 256
TOP_K = 8
N_EXPERT_GROUPS = 8
TOPK_GROUPS = 4
EXPERT_DIM = 256
SHARED_DIM = 256
ROUTED_SCALE = 2.5
MOE_BLOCK = 128
LN_EPS = 1e-5
RMS_EPS = 1e-6
DEPTH = 1
DEEPNORM_ALPHA = (2 * DEPTH) ** 0.25

VMEM_LIMIT = 56 * 1024 * 1024
LANES = 128

ROW_TILE = 1024
MAIN_COL_TILE = 1536
MIX_ROW_TILE = 512
OUT_ROW_TILE = 512


def _params(sem, vmem=VMEM_LIMIT):
    return pltpu.CompilerParams(dimension_semantics=sem, vmem_limit_bytes=vmem)


def _layer_norm(x, g, b):
    mu = jnp.mean(x, -1, keepdims=True)
    xc = x - mu
    var = jnp.mean(xc * xc, -1, keepdims=True)
    return xc * lax.rsqrt(var + LN_EPS) * g + b


def _sigmoid(x):
    return 1.0 / (1.0 + jnp.exp(-x))


def _dot(a, b):
    return jnp.dot(a, b, preferred_element_type=F32)


def _dot_nt(a, b):
    return lax.dot_general(a, b, (((1,), (1,)), ((), ())), preferred_element_type=F32)


def _dot_tn(a, b):
    return lax.dot_general(a, b, (((0,), (0,)), ((), ())), preferred_element_type=F32)


def _pack_bf16_pair(lo, hi):
    lo_bits = pltpu.bitcast(lo.astype(BF16).astype(F32), U32) >> 16
    hi_bits = pltpu.bitcast(hi.astype(BF16).astype(F32), U32) & jnp.uint32(0xFFFF0000)
    return hi_bits | lo_bits


def _unpack_bf16_pair(w):
    lo = pltpu.bitcast(w << 16, F32)
    hi = pltpu.bitcast(w & jnp.uint32(0xFFFF0000), F32)
    return lo, hi


PACK_PARTS = 2
PACK_WIDTH = D_MODEL // 2 // PACK_PARTS


def _pack_rows(v):
    half = D_MODEL // 2
    parts = []
    for j in range(PACK_PARTS):
        lo = v[:, j * PACK_WIDTH:(j + 1) * PACK_WIDTH]
        hi = v[:, half + j * PACK_WIDTH:half + (j + 1) * PACK_WIDTH]
        parts.append(pltpu.bitcast(_pack_bf16_pair(lo, hi), I32))
    return parts


def _unpack_rows(parts):
    pairs = [_unpack_bf16_pair(pltpu.bitcast(p, U32)) for p in parts]
    return jnp.concatenate([lo for lo, _ in pairs] + [hi for _, hi in pairs], axis=-1)


def _inproj_main_kernel(x_ref, g_ref, b_ref, w_ref, o_ref, h_scr):
    @pl.when(pl.program_id(1) == 0)
    def _():
        h_scr[...] = _layer_norm(x_ref[...], g_ref[...], b_ref[...]).astype(BF16)

    o_ref[...] = _dot(h_scr[...], w_ref[...]).astype(BF16)


def _inproj_main(x2, g, b, w_main):
    n = x2.shape[0]
    width = w_main.shape[1]
    return pl.pallas_call(
        _inproj_main_kernel,
        out_shape=jax.ShapeDtypeStruct((n, width), BF16),
        grid=(n // ROW_TILE, width // MAIN_COL_TILE),
        in_specs=[
            pl.BlockSpec((ROW_TILE, D_MODEL), lambda i, j: (i, 0)),
            pl.BlockSpec((1, D_MODEL), lambda i, j: (0, 0)),
            pl.BlockSpec((1, D_MODEL), lambda i, j: (0, 0)),
            pl.BlockSpec((D_MODEL, MAIN_COL_TILE), lambda i, j: (0, j)),
        ],
        out_specs=pl.BlockSpec((ROW_TILE, MAIN_COL_TILE), lambda i, j: (i, j)),
        scratch_shapes=[pltpu.VMEM((ROW_TILE, D_MODEL), BF16)],
        compiler_params=_params(("parallel", "arbitrary")),
        name="inproj_main",
    )(x2, g, b, w_main)


def _inproj_qkv_kernel(x_ref, g_ref, b_ref, w_ref, o0_ref, o1_ref, o2_ref,
                       hf_scr, h0_scr, h1_scr, h2_scr):
    @pl.when(pl.program_id(1) == 0)
    def _():
        hf = _layer_norm(x_ref[...], g_ref[...], b_ref[...])
        h0_scr[...] = hf.astype(BF16)
        for c in range(D_MODEL // LANES):
            hf_scr[c] = hf[:, c * LANES:(c + 1) * LANES]
        for h_scr, (_, dil) in ((h1_scr, ATT_GROUPS[1]), (h2_scr, ATT_GROUPS[2])):
            rows = ROW_TILE // dil
            for r in range(dil):
                for c in range(D_MODEL // LANES):
                    h_scr[r * rows:(r + 1) * rows, c * LANES:(c + 1) * LANES] = (
                        hf_scr[c, pl.ds(r, rows, stride=dil), :].astype(BF16))

    gw = ATT_GROUP_WIDTH
    o0_ref[0] = _dot(h0_scr[...], w_ref[:, 0:gw]).astype(BF16)
    d1 = ATT_GROUPS[1][1]
    o1_ref[0, 0] = _dot(h1_scr[...], w_ref[:, gw:2 * gw]).astype(BF16).reshape(d1, ROW_TILE // d1, gw)
    d2 = ATT_GROUPS[2][1]
    o2_ref[0, 0] = _dot(h2_scr[...], w_ref[:, 2 * gw:3 * gw]).astype(BF16).reshape(d2, ROW_TILE // d2, gw)


def _inproj_qkv(x2, g, b, w_qkv, batch, seq):
    n = x2.shape[0]
    gw = ATT_GROUP_WIDTH
    tiles_per_seq = seq // ROW_TILE
    d1, d2 = ATT_GROUPS[1][1], ATT_GROUPS[2][1]
    out_shape = (
        jax.ShapeDtypeStruct((3, n, gw), BF16),
        jax.ShapeDtypeStruct((3, batch, d1, seq // d1, gw), BF16),
        jax.ShapeDtypeStruct((3, batch, d2, seq // d2, gw), BF16),
    )
    return pl.pallas_call(
        _inproj_qkv_kernel,
        out_shape=out_shape,
        grid=(n // ROW_TILE, 3),
        in_specs=[
            pl.BlockSpec((ROW_TILE, D_MODEL), lambda i, t: (i, 0)),
            pl.BlockSpec((1, D_MODEL), lambda i, t: (0, 0)),
            pl.BlockSpec((1, D_MODEL), lambda i, t: (0, 0)),
            pl.BlockSpec((D_MODEL, ATT_WIDTH), lambda i, t: (0, t)),
        ],
        out_specs=(
            pl.BlockSpec((1, ROW_TILE, gw), lambda i, t: (t, i, 0)),
            pl.BlockSpec((1, 1, d1, ROW_TILE // d1, gw),
                         lambda i, t: (t, i // tiles_per_seq, 0, i % tiles_per_seq, 0)),
            pl.BlockSpec((1, 1, d2, ROW_TILE // d2, gw),
                         lambda i, t: (t, i // tiles_per_seq, 0, i % tiles_per_seq, 0)),
        ),
        scratch_shapes=[
            pltpu.VMEM((D_MODEL // LANES, ROW_TILE, LANES), F32),
            pltpu.VMEM((ROW_TILE, D_MODEL), BF16),
            pltpu.VMEM((ROW_TILE, D_MODEL), BF16),
            pltpu.VMEM((ROW_TILE, D_MODEL), BF16),
        ],
        compiler_params=_params(("parallel", "arbitrary")),
        name="inproj_qkv",
    )(x2, g, b, w_qkv)


def _attn_kernel(q_ref, kc_ref, kp_ref, vc_ref, vp_ref, o_ref, lse_ref):
    blk = ATT_BLOCK
    has_prev = pl.program_id(2) > 0
    row = lax.broadcasted_iota(I32, (blk, blk), 0)
    col = lax.broadcasted_iota(I32, (blk, blk), 1)
    mask_c = col <= row
    mask_p = (col >= row) & has_prev
    scale = ATT_HEAD_DIM ** -0.5
    neg = -jnp.inf
    for h in range(ATT_HEADS):
        sl = slice(h * ATT_HEAD_DIM, (h + 1) * ATT_HEAD_DIM)
        q = q_ref[:, sl]
        s_c = jnp.where(mask_c, _dot_nt(q, kc_ref[:, sl]) * scale, neg)
        s_p = jnp.where(mask_p, _dot_nt(q, kp_ref[:, sl]) * scale, neg)
        m = jnp.maximum(jnp.max(s_c, -1, keepdims=True), jnp.max(s_p, -1, keepdims=True))
        p_c = jnp.exp(s_c - m)
        p_p = jnp.exp(s_p - m)
        l = jnp.sum(p_c, -1, keepdims=True) + jnp.sum(p_p, -1, keepdims=True)
        o = _dot(p_c.astype(BF16), vc_ref[:, sl]) + _dot(p_p.astype(BF16), vp_ref[:, sl])
        o_ref[:, sl] = (o / l).astype(BF16)
        lse_ref[:, sl] = jnp.broadcast_to(m + jnp.log(l), (blk, ATT_HEAD_DIM))


def _attention_group(qkv):
    _, batch, dil, sub_len, gw = qkv.shape
    blk = ATT_BLOCK

    def spec(t, prev):
        def index(b, r, i):
            return (t, b, r, jnp.maximum(i - 1, 0) if prev else i, 0)
        return pl.BlockSpec((None, None, None, blk, gw), index)

    out_spec = pl.BlockSpec((None, None, blk, gw), lambda b, r, i: (b, r, i, 0))
    return pl.pallas_call(
        _attn_kernel,
        out_shape=(jax.ShapeDtypeStruct((batch, dil, sub_len, gw), BF16),
                   jax.ShapeDtypeStruct((batch, dil, sub_len, gw), F32)),
        grid=(batch, dil, sub_len // blk),
        in_specs=[spec(0, False), spec(1, False), spec(1, True), spec(2, False), spec(2, True)],
        out_specs=(out_spec, out_spec),
        compiler_params=_params(("parallel", "parallel", "arbitrary")),
        name=f"dilated_attention_d{dil}",
    )(qkv, qkv, qkv, qkv, qkv)


HGRN_ROWS = 256


def _hgrn_kernel(q_ref, f_ref, i_ref, g_ref, lb_ref, nw_ref, o_ref, state_scr):
    seq = q_ref.shape[0]
    c = HGRN_CHUNK
    rows = HGRN_ROWS
    nchunk = rows // c
    dk = HGRN_HEAD_DIM
    lb = lb_ref[...]
    one_m_lb = 1.0 - lb
    nw = nw_ref[...]

    r_i = lax.broadcasted_iota(I32, (rows, rows), 0)
    c_i = lax.broadcasted_iota(I32, (rows, rows), 1)
    tri = jnp.where((r_i // c == c_i // c) & (c_i <= r_i), 1.0, 0.0).astype(BF16)
    t_i = lax.broadcasted_iota(I32, (c, c), 0)
    s_i = lax.broadcasted_iota(I32, (c, c), 1)
    causal = s_i <= t_i

    state_scr[...] = jnp.zeros_like(state_scr)

    def body(gi, carry):
        r0 = pl.multiple_of(gi * rows, rows)
        z = f_ref[pl.ds(r0, rows), :].astype(F32)
        log_f = jnp.log(lb + one_m_lb * _sigmoid(z))
        key = one_m_lb * _sigmoid(-z)
        p0 = log_f.astype(BF16)
        rem = log_f - p0.astype(F32)
        p1 = rem.astype(BF16)
        p2 = (rem - p1.astype(F32)).astype(BF16)
        bcum = _dot(tri, p0) + _dot(tri, p1) + _dot(tri, p2)

        q = q_ref[pl.ds(r0, rows), :].astype(F32) * (dk ** -0.5)
        v = i_ref[pl.ds(r0, rows), :]
        st = state_scr[...]
        outs = []
        for j in range(nchunk):
            sl = slice(j * c, (j + 1) * c)
            b_j = bcum[sl]
            b_mid = b_j[c // 2:c // 2 + 1]
            b_last = b_j[c - 1:c]
            q_t = q[sl] * jnp.exp(b_j - b_mid)
            k_t = key[sl] * jnp.exp(b_mid - b_j)
            v_j = v[sl]
            att = jnp.where(causal, _dot_nt(q_t.astype(BF16), k_t.astype(BF16)), 0.0)
            o_j = _dot(att.astype(BF16), v_j)
            q_in = (q_t * jnp.exp(b_mid)).astype(BF16)
            o_j = o_j + _dot_nt(q_in, st.astype(BF16))
            k_st = (k_t * jnp.exp(b_last - b_mid)).astype(BF16)
            st = st * jnp.exp(b_last) + _dot_tn(v_j, k_st)
            outs.append(o_j)
        state_scr[...] = st
        o = jnp.concatenate(outs, axis=0)
        o = o * lax.rsqrt(jnp.mean(o * o, -1, keepdims=True) + RMS_EPS) * nw
        gate = g_ref[pl.ds(r0, rows), :].astype(F32)
        o_ref[pl.ds(r0, rows), :] = (o * gate * _sigmoid(gate)).astype(BF16)
        return carry

    lax.fori_loop(0, seq // rows, body, 0)


def _hgrn(proj, lower_bound, norm_w, batch, seq):
    n = proj.shape[0]
    dk = HGRN_HEAD_DIM
    heads = HGRN_HEADS

    def seg(k):
        return pl.BlockSpec((seq, dk), lambda b, h: (b, k * heads + h))

    return pl.pallas_call(
        _hgrn_kernel,
        out_shape=jax.ShapeDtypeStruct((n, HGRN_WIDTH), BF16),
        grid=(batch, heads),
        in_specs=[seg(0), seg(1), seg(2), seg(3),
                  pl.BlockSpec((1, dk), lambda b, h: (0, h)),
                  pl.BlockSpec((1, dk), lambda b, h: (0, 0))],
        out_specs=pl.BlockSpec((seq, dk), lambda b, h: (b, h)),
        scratch_shapes=[pltpu.VMEM((dk, dk), F32)],
        compiler_params=_params(("parallel", "parallel")),
        name="hgrn2",
    )(proj, proj, proj, proj, lower_bound, norm_w)


def _mix_kernel(x_ref, gin_ref, bin_ref, o0_ref, l0_ref, o1_ref, l1_ref, o2_ref, l2_ref,
                ob_ref, ga_ref, gb_ref, wa_ref, wb_ref, wo_ref, g1_ref, b1_ref,
                rwh_ref, rwl_ref,
                h_ref, hp0_ref, hp1_ref, lg_ref,
                so1_scr, sl1_scr, so2_scr, sl2_scr):
    tm = x_ref.shape[0]
    for o_ref, l_ref, so_scr, sl_scr, (_, dil) in (
            (o1_ref, l1_ref, so1_scr, sl1_scr, ATT_GROUPS[1]),
            (o2_ref, l2_ref, so2_scr, sl2_scr, ATT_GROUPS[2])):
        for r in range(dil):
            o_r = o_ref[r].astype(F32)
            l_r = l_ref[r]
            for c in range(ATT_GROUP_WIDTH // LANES):
                cs = slice(c * LANES, (c + 1) * LANES)
                so_scr[c, pl.ds(r, tm // dil, stride=dil), :] = o_r[:, cs]
                sl_scr[c, pl.ds(r, tm // dil, stride=dil), :] = l_r[:, cs]

    def natural(scr):
        return jnp.concatenate([scr[c] for c in range(ATT_GROUP_WIDTH // LANES)], axis=-1)

    l0 = l0_ref[...]
    l1 = natural(sl1_scr)
    l2 = natural(sl2_scr)
    m = jnp.maximum(jnp.maximum(l0, l1), l2)
    e0 = jnp.exp(l0 - m)
    e1 = jnp.exp(l1 - m)
    e2 = jnp.exp(l2 - m)
    o_att = (e0 * o0_ref[...].astype(F32) + e1 * natural(so1_scr) + e2 * natural(so2_scr)) / (e0 + e1 + e2)

    y_a = _dot(o_att.astype(BF16), wa_ref[...])
    y_b = _dot(ob_ref[...], wb_ref[...])
    merged = _sigmoid(ga_ref[...].astype(F32)) * y_a + _sigmoid(gb_ref[...].astype(F32)) * y_b
    mix = _dot(merged.astype(BF16), wo_ref[...])
    h_in = _layer_norm(x_ref[...], gin_ref[...], bin_ref[...])
    h1 = _layer_norm(DEEPNORM_ALPHA * h_in + mix, g1_ref[...], b1_ref[...])
    h_ref[...] = h1

    hp0_ref[...], hp1_ref[...] = _pack_rows(h1)

    h_hi = h1.astype(BF16)
    h_lo = (h1 - h_hi.astype(F32)).astype(BF16)
    rwh = rwh_ref[...]
    lg_ref[...] = _dot_nt(rwh, h_hi) + _dot_nt(rwh, h_lo) + _dot_nt(rwl_ref[...], h_hi)


def _mix(x2, gin, bin_, att, ob, proj, wa, wb, wo, g1, b1, rwh, rwl, batch, seq):
    n = x2.shape[0]
    tm = MIX_ROW_TILE
    gw = ATT_GROUP_WIDTH
    tiles_per_seq = seq // tm
    (o0, l0), (o1, l1), (o2, l2) = att
    d1, d2 = ATT_GROUPS[1][1], ATT_GROUPS[2][1]

    def full(shape):
        return pl.BlockSpec(shape, lambda i: (0,) * len(shape))

    def dil_spec(dil):
        return pl.BlockSpec((None, dil, tm // dil, gw),
                            lambda i: (i // tiles_per_seq, 0, i % tiles_per_seq, 0))

    nat_spec = pl.BlockSpec((tm, gw), lambda i: (i, 0))
    return pl.pallas_call(
        _mix_kernel,
        out_shape=(jax.ShapeDtypeStruct((n, D_MODEL), F32),
                   jax.ShapeDtypeStruct((n, PACK_WIDTH), I32),
                   jax.ShapeDtypeStruct((n, PACK_WIDTH), I32),
                   jax.ShapeDtypeStruct((N_EXPERTS, n), F32)),
        grid=(n // tm,),
        in_specs=[
            pl.BlockSpec((tm, D_MODEL), lambda i: (i, 0)),
            full((1, D_MODEL)), full((1, D_MODEL)),
            nat_spec, nat_spec,
            dil_spec(d1), dil_spec(d1), dil_spec(d2), dil_spec(d2),
            pl.BlockSpec((tm, HGRN_WIDTH), lambda i: (i, 0)),
            pl.BlockSpec((tm, D_MODEL), lambda i: (i, 4)),
            pl.BlockSpec((tm, D_MODEL), lambda i: (i, 5)),
            full((gw, D_MODEL)), full((HGRN_WIDTH, D_MODEL)), full((D_MODEL, D_MODEL)),
            full((1, D_MODEL)), full((1, D_MODEL)),
            full((N_EXPERTS, D_MODEL)), full((N_EXPERTS, D_MODEL)),
        ],
        out_specs=(pl.BlockSpec((tm, D_MODEL), lambda i: (i, 0)),
                   pl.BlockSpec((tm, PACK_WIDTH), lambda i: (i, 0)),
                   pl.BlockSpec((tm, PACK_WIDTH), lambda i: (i, 0)),
                   pl.BlockSpec((N_EXPERTS, tm), lambda i: (0, i))),
        scratch_shapes=[pltpu.VMEM((gw // LANES, tm, LANES), F32)] * 4,
        compiler_params=_params(("parallel",)),
        name="branch_mix",
    )(x2, gin, bin_, o0, l0, o1, l1, o2, l2, ob, proj, proj, wa, wb, wo, g1, b1, rwh, rwl)


ROW_SLOTS = 16
ROW_GROUP = 4
ROW_AHEAD = ROW_SLOTS - ROW_GROUP
W_SLOTS = 3


def _expert_kernel(sblk_ref, nblk_ref, cnt_ref, xs0_hbm, xs1_hbm, wi_hbm, wo_hbm,
                   ys0_hbm, ys1_hbm, xbuf0, xbuf1, ybuf0, ybuf1, wibuf, wobuf,
                   in_sem, out_sem, w_sem, wi_scr, wo_scr):
    e = pl.program_id(0)
    n_exp = pl.num_programs(0)
    first_blk = sblk_ref[e]
    n_blk = nblk_ref[e]
    n_rows = cnt_ref[e]
    total = sblk_ref[n_exp - 1] + nblk_ref[n_exp - 1]
    xs_hbm, ys_hbm = (xs0_hbm, xs1_hbm), (ys0_hbm, ys1_hbm)
    xbuf, ybuf = (xbuf0, xbuf1), (ybuf0, ybuf1)

    def rows_of(b):
        return pl.ds(pl.multiple_of(b * MOE_BLOCK, MOE_BLOCK), MOE_BLOCK)

    def x_copy(b, part):
        slot = b % ROW_SLOTS
        return pltpu.make_async_copy(xs_hbm[part].at[rows_of(b)], xbuf[part].at[slot],
                                     in_sem.at[part, slot])

    def y_copy(b, part):
        slot = b % ROW_SLOTS
        return pltpu.make_async_copy(ybuf[part].at[slot], ys_hbm[part].at[rows_of(b)],
                                     out_sem.at[part, slot])

    def w_copies(ex):
        slot = ex % W_SLOTS
        return (pltpu.make_async_copy(wi_hbm.at[ex], wibuf.at[slot], w_sem.at[0, slot]),
                pltpu.make_async_copy(wo_hbm.at[ex], wobuf.at[slot], w_sem.at[1, slot]))

    def start_weights(ex):
        @pl.when((ex < n_exp) & (nblk_ref[jnp.minimum(ex, n_exp - 1)] > 0))
        def _():
            for cp in w_copies(ex):
                cp.start()

    @pl.when(e == 0)
    def _():
        for b in range(ROW_AHEAD):
            @pl.when(b < total)
            def _():
                for part in range(PACK_PARTS):
                    x_copy(b, part).start()
        for ex in range(W_SLOTS - 1):
            start_weights(ex)

    start_weights(e + (W_SLOTS - 1))

    def step(j, group):
        b0 = first_blk + j
        for i in range(group):
            for part in range(PACK_PARTS):
                x_copy(b0 + i, part).wait()
        for i in range(group):
            @pl.when(b0 + ROW_AHEAD + i < total)
            def _():
                for part in range(PACK_PARTS):
                    x_copy(b0 + ROW_AHEAD + i, part).start()
        for i in range(group):
            @pl.when(b0 + i >= ROW_SLOTS)
            def _():
                for part in range(PACK_PARTS):
                    y_copy(b0 + i - ROW_SLOTS, part).wait()

        rows = group * MOE_BLOCK
        live = (lax.broadcasted_iota(I32, (rows, PACK_WIDTH), 0)
                < n_rows - j * MOE_BLOCK)
        parts = [jnp.concatenate([xbuf[part][(b0 + i) % ROW_SLOTS] for i in range(group)], axis=0)
                 for part in range(PACK_PARTS)]
        x = _unpack_rows([jnp.where(live, p, 0) for p in parts])
        hu = _dot(x.astype(BF16), wi_scr[...])
        hg = hu[:, :EXPERT_DIM]
        hv = hu[:, EXPERT_DIM:]
        act = (hg * _sigmoid(hg) * hv).astype(BF16)
        packed = _pack_rows(_dot(act, wo_scr[...]))
        for i in range(group):
            for part in range(PACK_PARTS):
                ybuf[part][(b0 + i) % ROW_SLOTS] = packed[part][i * MOE_BLOCK:(i + 1) * MOE_BLOCK]
                y_copy(b0 + i, part).start()

    @pl.when(n_blk > 0)
    def _():
        for cp in w_copies(e):
            cp.wait()
        slot = e % W_SLOTS
        wi_scr[...] = wibuf[slot].astype(BF16)
        wo_scr[...] = wobuf[slot].astype(BF16)

        def full_group(g, carry):
            step(g * ROW_GROUP, ROW_GROUP)
            return carry

        lax.fori_loop(0, n_blk // ROW_GROUP, full_group, 0)
        done = n_blk // ROW_GROUP * ROW_GROUP
        group = ROW_GROUP // 2
        while group >= 1:
            @pl.when((n_blk & group) != 0)
            def _(group=group, done=done):
                step(done, group)
            done = done + (n_blk & group)
            group //= 2

    @pl.when(e == n_exp - 1)
    def _():
        for back in range(1, ROW_SLOTS + 1):
            @pl.when(total >= back)
            def _():
                for part in range(PACK_PARTS):
                    y_copy(total - back, part).wait()


def _experts(start_blk, n_blk, n_rows, xs, w_in_e, w_out_e):
    p = xs[0].shape[0]
    n_exp = w_in_e.shape[0]
    any_spec = pl.BlockSpec(memory_space=pl.ANY)
    row_buf = pltpu.VMEM((ROW_SLOTS, MOE_BLOCK, PACK_WIDTH), I32)
    grid_spec = pltpu.PrefetchScalarGridSpec(
        num_scalar_prefetch=3,
        grid=(n_exp,),
        in_specs=[any_spec, any_spec, any_spec, any_spec],
        out_specs=(any_spec, any_spec),
        scratch_shapes=[row_buf, row_buf, row_buf, row_buf,
                        pltpu.VMEM((W_SLOTS, D_MODEL, 2 * EXPERT_DIM), F32),
                        pltpu.VMEM((W_SLOTS, EXPERT_DIM, D_MODEL), F32),
                        pltpu.SemaphoreType.DMA((PACK_PARTS, ROW_SLOTS)),
                        pltpu.SemaphoreType.DMA((PACK_PARTS, ROW_SLOTS)),
                        pltpu.SemaphoreType.DMA((2, W_SLOTS)),
                        pltpu.VMEM((D_MODEL, 2 * EXPERT_DIM), BF16),
                        pltpu.VMEM((EXPERT_DIM, D_MODEL), BF16)],
    )
    return pl.pallas_call(
        _expert_kernel,
        out_shape=(jax.ShapeDtypeStruct((p, PACK_WIDTH), I32),) * PACK_PARTS,
        grid_spec=grid_spec,
        compiler_params=_params(("arbitrary",)),
        name="routed_experts",
    )(start_blk, n_blk, n_rows, xs[0], xs[1], w_in_e, w_out_e)


def _combine_kernel(h_ref, yg0_ref, yg1_ref, gate_ref, wsi_ref, wso_ref, g2_ref, b2_ref, o_ref):
    h1 = h_ref[...]
    gate = gate_ref[...]
    routed = None
    for k in range(TOP_K):
        y_k = _unpack_rows([yg0_ref[k], yg1_ref[k]]) * gate[:, k:k + 1]
        routed = y_k if routed is None else routed + y_k
    hs = _dot(h1.astype(BF16), wsi_ref[...])
    sg = hs[:, :SHARED_DIM]
    sv = hs[:, SHARED_DIM:]
    shared = _dot((sg * _sigmoid(sg) * sv).astype(BF16), wso_ref[...])
    o_ref[...] = _layer_norm(DEEPNORM_ALPHA * h1 + routed + shared, g2_ref[...], b2_ref[...])


def _combine(h1, yg, gate_nk, wsi, wso, g2, b2):
    n = h1.shape[0]
    tm = OUT_ROW_TILE
    yg_spec = pl.BlockSpec((TOP_K, tm, PACK_WIDTH), lambda i: (0, i, 0))

    def full(shape):
        return pl.BlockSpec(shape, lambda i: (0,) * len(shape))

    return pl.pallas_call(
        _combine_kernel,
        out_shape=jax.ShapeDtypeStruct((n, D_MODEL), F32),
        grid=(n // tm,),
        in_specs=[
            pl.BlockSpec((tm, D_MODEL), lambda i: (i, 0)),
            yg_spec, yg_spec,
            pl.BlockSpec((tm, TOP_K), lambda i: (i, 0)),
            full((D_MODEL, 2 * SHARED_DIM)), full((SHARED_DIM, D_MODEL)),
            full((1, D_MODEL)), full((1, D_MODEL)),
        ],
        out_specs=pl.BlockSpec((tm, D_MODEL), lambda i: (i, 0)),
        compiler_params=_params(("parallel",)),
        name="shared_combine",
    )(h1, yg[0], yg[1], gate_nk, wsi, wso, g2, b2)


ROUTE_TILE = 512


def _pick_first_max(vals, iota, axis, size):
    m = jnp.max(vals, axis=axis, keepdims=True)
    idx = jnp.min(jnp.where(vals == m, iota, size), axis=axis, keepdims=True)
    return m, idx


def _route_kernel(lg_ref, bias_ref, idx_ref, gate_ref, rank_ref, cnt_ref, carry_scr):
    e, tn = lg_ref.shape
    groups = N_EXPERT_GROUPS
    gsz = e // groups
    neg = -jnp.inf

    @pl.when(pl.program_id(0) == 0)
    def _():
        carry_scr[...] = jnp.zeros_like(carry_scr)

    scores = _sigmoid(lg_ref[...])
    biased = scores + bias_ref[...]

    b3 = biased.reshape(groups, gsz, tn)
    io3 = lax.broadcasted_iota(I32, b3.shape, 1)
    m1, i1 = _pick_first_max(b3, io3, 1, gsz)
    m2 = jnp.max(jnp.where(io3 == i1, neg, b3), axis=1, keepdims=True)
    grp = m1 + m2
    iog = lax.broadcasted_iota(I32, grp.shape, 0)
    keep = jnp.zeros(grp.shape, F32)
    for _ in range(TOPK_GROUPS):
        _, gi = _pick_first_max(grp, iog, 0, groups)
        hit = iog == gi
        keep = jnp.where(hit, 1.0, keep)
        grp = jnp.where(hit, neg, grp)
    masked = jnp.where(keep > 0.0, b3, neg).reshape(e, tn)

    ioe = lax.broadcasted_iota(I32, (e, tn), 0)
    onehot = jnp.zeros((e, tn), F32)
    idxs, gates = [], []
    for _ in range(TOP_K):
        _, ei = _pick_first_max(masked, ioe, 0, e)
        hit = ioe == ei
        gates.append(jnp.sum(jnp.where(hit, scores, 0.0), axis=0, keepdims=True))
        onehot = jnp.where(hit, 1.0, onehot)
        masked = jnp.where(hit, neg, masked)
        idxs.append(ei)
    gsum = gates[0]
    for g in gates[1:]:
        gsum = gsum + g
    idx_ref[...] = jnp.concatenate(idxs, axis=0)
    gate_ref[...] = jnp.concatenate(gates, axis=0) / gsum * ROUTED_SCALE

    t_r = lax.broadcasted_iota(I32, (tn, tn), 0)
    t_c = lax.broadcasted_iota(I32, (tn, tn), 1)
    earlier = jnp.where(t_r < t_c, 1.0, 0.0).astype(BF16)
    oh = onehot.astype(BF16)
    base = carry_scr[...]
    before = _dot(oh, earlier) + jnp.concatenate([base] * (tn // LANES), axis=1)
    ranks = [jnp.sum(jnp.where(ioe == ei, before, 0.0), axis=0, keepdims=True) for ei in idxs]
    rank_ref[...] = jnp.concatenate(ranks, axis=0).astype(I32)
    total = base + _dot(oh, jnp.ones((tn, LANES), BF16))
    carry_scr[...] = total
    cnt_ref[...] = total


def _route(logits_t, bias_col):
    e, n = logits_t.shape
    tn = ROUTE_TILE
    tok_spec = pl.BlockSpec((TOP_K, tn), lambda i: (0, i))
    return pl.pallas_call(
        _route_kernel,
        out_shape=(jax.ShapeDtypeStruct((TOP_K, n), I32),
                   jax.ShapeDtypeStruct((TOP_K, n), F32),
                   jax.ShapeDtypeStruct((TOP_K, n), I32),
                   jax.ShapeDtypeStruct((e, LANES), F32)),
        grid=(n // tn,),
        in_specs=[pl.BlockSpec((e, tn), lambda i: (0, i)),
                  pl.BlockSpec((e, 1), lambda i: (0, 0))],
        out_specs=(tok_spec, tok_spec, tok_spec, pl.BlockSpec((e, LANES), lambda i: (0, 0))),
        scratch_shapes=[pltpu.VMEM((e, LANES), F32)],
        compiler_params=_params(("arbitrary",)),
        name="router_topk",
    )(logits_t, bias_col)


def _plan_kernel(idx_ref, rank_ref, cnt_ref, dest_ref, sblk_ref):
    e = cnt_ref.shape[0]
    tn = idx_ref.shape[1]
    cnt = cnt_ref[...]
    nblk = jnp.floor((cnt + (MOE_BLOCK - 1)) * (1.0 / MOE_BLOCK))
    e_r = lax.broadcasted_iota(I32, (e, e), 0)
    e_c = lax.broadcasted_iota(I32, (e, e), 1)
    lower = jnp.where(e_c < e_r, 1.0, 0.0).astype(BF16)
    start_blk = _dot(lower, nblk.astype(BF16))

    ioe = lax.broadcasted_iota(I32, (e, tn), 0)
    start_row = jnp.concatenate([start_blk * MOE_BLOCK] * (tn // LANES), axis=1)
    idx = idx_ref[...]
    dests = [jnp.sum(jnp.where(ioe == idx[k:k + 1], start_row, 0.0), axis=0, keepdims=True)
             for k in range(TOP_K)]
    dest_ref[...] = jnp.concatenate(dests, axis=0).astype(I32) + rank_ref[...]
    sblk_ref[...] = start_blk.astype(I32)


def _plan(idx_t, rank_t, counts):
    k, n = idx_t.shape
    e = counts.shape[0]
    tn = ROUTE_TILE
    tok_spec = pl.BlockSpec((k, tn), lambda i: (0, i))
    exp_spec = pl.BlockSpec((e, LANES), lambda i: (0, 0))
    return pl.pallas_call(
        _plan_kernel,
        out_shape=(jax.ShapeDtypeStruct((k, n), I32), jax.ShapeDtypeStruct((e, LANES), I32)),
        grid=(n // tn,),
        in_specs=[tok_spec, tok_spec, exp_spec],
        out_specs=(tok_spec, exp_spec),
        compiler_params=_params(("arbitrary",)),
        name="dispatch_plan",
    )(idx_t, rank_t, counts)


SC_WINDOW = 128


def _sc_mesh():
    return plsc.VectorSubcoreMesh(core_axis_name="core", subcore_axis_name="subcore")


def _dispatch_rows(parts, dest, p_total):
    n, width = parts[0].shape
    top_k = dest.shape[0]
    out_type = (jax.ShapeDtypeStruct((p_total, width), parts[0].dtype),) * len(parts)

    @functools.partial(pl.kernel, mesh=_sc_mesh(), scratch_types=[], out_type=out_type,
                       name="dispatch_rows")
    def scatter(*refs):
        x_hbms = refs[:len(parts)]
        i_hbm = refs[len(parts)]
        o_hbms = refs[len(parts) + 1:]
        for x_hbm, o_hbm in zip(x_hbms, o_hbms):
            def body(x_vmem, i_vmem, o_hbm=o_hbm):
                for k in range(top_k):
                    pltpu.sync_copy(x_vmem, o_hbm.at[i_vmem.at[k]])

            pltpu.emit_pipeline(
                body,
                grid=(n // SC_WINDOW,),
                in_specs=[pl.BlockSpec((SC_WINDOW, width), lambda i: (i, 0)),
                          pl.BlockSpec((top_k, SC_WINDOW), lambda i: (0, i))],
                out_specs=[],
                core_axis_name=("core", "subcore"),
                dimension_semantics=(pltpu.PARALLEL,),
            )(x_hbm, i_hbm)

    return scatter(*parts, dest)


def _gather_rows(parts, idx_flat):
    count = idx_flat.shape[1]
    width = parts[0].shape[1]
    out_type = (jax.ShapeDtypeStruct((count, width), parts[0].dtype),) * len(parts)

    @functools.partial(pl.kernel, mesh=_sc_mesh(), scratch_types=[], out_type=out_type,
                       name="combine_rows")
    def gather(*refs):
        y_hbms = refs[:len(parts)]
        i_hbm = refs[len(parts)]
        o_hbms = refs[len(parts) + 1:]
        for y_hbm, o_hbm in zip(y_hbms, o_hbms):
            def body(i_vmem, o_vmem, y_hbm=y_hbm):
                pltpu.sync_copy(y_hbm.at[i_vmem.at[0]], o_vmem)

            pltpu.emit_pipeline(
                body,
                grid=(count // SC_WINDOW,),
                in_specs=[pl.BlockSpec((1, SC_WINDOW), lambda i: (0, i))],
                out_specs=[pl.BlockSpec((SC_WINDOW, width), lambda i: (i, 0))],
                core_axis_name=("core", "subcore"),
                dimension_semantics=(pltpu.PARALLEL,),
            )(i_hbm, o_hbm)

    return gather(*parts, idx_flat)


def kernel(x, ln_in_g, ln_in_b, w_in, hgrn_lb_logits, hgrn_norm_w, w_branch_att, w_branch_hgrn,
           w_out, ln1_g, ln1_b, router_w, router_bias, expert_w_in, expert_w_out, shared_w_in,
           shared_w_out, ln2_g, ln2_b):
    batch, seq, d = x.shape
    n = batch * seq
    x2 = x.reshape(n, d)
    row = lambda v: v.reshape(1, -1).astype(F32)

    lower_bounds = jnp.cumsum(jax.nn.softmax(hgrn_lb_logits.astype(F32), axis=0), axis=0)
    l = 0
    w_l = w_in[l]
    w_qkv = w_l[:, :3 * ATT_WIDTH].astype(BF16)
    w_main = w_l[:, 3 * ATT_WIDTH:].astype(BF16)
    gin, bin_ = row(ln_in_g), row(ln_in_b)

    proj = _inproj_main(x2, gin, bin_, w_main)
    qkv0, qkv1, qkv2 = _inproj_qkv(x2, gin, bin_, w_qkv, batch, seq)
    qkv0 = qkv0.reshape(3, batch, 1, seq, ATT_GROUP_WIDTH)
    att = [_attention_group(qkv) for qkv in (qkv0, qkv1, qkv2)]
    o0, l0 = att[0]
    att[0] = (o0.reshape(n, ATT_GROUP_WIDTH), l0.reshape(n, ATT_GROUP_WIDTH))

    ob = _hgrn(proj, row(lower_bounds[l]), row(hgrn_norm_w[l]), batch, seq)

    rw_t = router_w[l].T.astype(F32)
    rwh = rw_t.astype(BF16)
    rwl = (rw_t - rwh.astype(F32)).astype(BF16)
    h1, hp0, hp1, logits_t = _mix(
        x2, gin, bin_, att, ob, proj,
        w_branch_att[l].astype(BF16), w_branch_hgrn[l].astype(BF16), w_out[l].astype(BF16),
        row(ln1_g[l]), row(ln1_b[l]), rwh, rwl, batch, seq)

    idx_t, gate_t, rank_t, counts = _route(logits_t, router_bias[l].reshape(-1, 1).astype(F32))
    p_total = n * TOP_K + N_EXPERTS * MOE_BLOCK
    dest, start_blk = _plan(idx_t, rank_t, counts)
    n_rows = counts[:, 0].astype(I32)
    n_blk = (n_rows + (MOE_BLOCK - 1)) // MOE_BLOCK
    xs = _dispatch_rows((hp0, hp1), dest, p_total)
    ys = _experts(start_blk[:, 0], n_blk, n_rows, xs, expert_w_in[l], expert_w_out[l])
    yg = [g.reshape(TOP_K, n, PACK_WIDTH)
          for g in _gather_rows(ys, dest.reshape(1, TOP_K * n))]
    out = _combine(h1, yg, gate_t.T, shared_w_in[l].astype(BF16), shared_w_out[l].astype(BF16),
                   row(ln2_g[l]), row(ln2_b[l]))
    return out.reshape(batch, seq, d)
```

```python
import functools

import jax
import jax.numpy as jnp
from jax import lax
from jax.experimental import pallas as pl
from jax.experimental.pallas import tpu as pltpu
from jax.experimental.pallas import tpu_sc as plsc

F32 = jnp.float32
BF16 = jnp.bfloat16
U32 = jnp.uint32
I32 = jnp.int32

D_MODEL = 1024
ATT_GROUPS = ((128, 1), (512, 4), (2048, 16))
ATT_HEADS = 8
ATT_HEAD_DIM = 64
ATT_GROUP_WIDTH = ATT_HEADS * ATT_HEAD_DIM
ATT_WIDTH = len(ATT_GROUPS) * ATT_GROUP_WIDTH
ATT_BLOCK = 128
HGRN_HEAD_DIM = 128
HGRN_HEADS = D_MODEL // HGRN_HEAD_DIM
HGRN_WIDTH = HGRN_HEADS * HGRN_HEAD_DIM
HGRN_CHUNK = 32
N_EXPERTS = 256
TOP_K = 8
N_EXPERT_GROUPS = 8
TOPK_GROUPS = 4
EXPERT_DIM = 256
SHARED_DIM = 256
ROUTED_SCALE = 2.5
MOE_BLOCK = 128
LN_EPS = 1e-5
RMS_EPS = 1e-6
DEPTH = 1
DEEPNORM_ALPHA = (2 * DEPTH) ** 0.25

VMEM_LIMIT = 56 * 1024 * 1024
LANES = 128

ROW_TILE = 1024
MAIN_COL_TILE = 1536
MIX_ROW_TILE = 512
OUT_ROW_TILE = 512


def _params(sem, vmem=VMEM_LIMIT):
    return pltpu.CompilerParams(dimension_semantics=sem, vmem_limit_bytes=vmem)


def _layer_norm(x, g, b):
    mu = jnp.mean(x, -1, keepdims=True)
    xc = x - mu
    var = jnp.mean(xc * xc, -1, keepdims=True)
    return xc * lax.rsqrt(var + LN_EPS) * g + b


def _sigmoid(x):
    return 1.0 / (1.0 + jnp.exp(-x))


def _dot(a, b):
    return jnp.dot(a, b, preferred_element_type=F32)


def _dot_nt(a, b):
    return lax.dot_general(a, b, (((1,), (1,)), ((), ())), preferred_element_type=F32)


def _dot_tn(a, b):
    return lax.dot_general(a, b, (((0,), (0,)), ((), ())), preferred_element_type=F32)


def _pack_bf16_pair(lo, hi):
    lo_bits = pltpu.bitcast(lo.astype(BF16).astype(F32), U32) >> 16
    hi_bits = pltpu.bitcast(hi.astype(BF16).astype(F32), U32) & jnp.uint32(0xFFFF0000)
    return hi_bits | lo_bits


def _unpack_bf16_pair(w):
    lo = pltpu.bitcast(w << 16, F32)
    hi = pltpu.bitcast(w & jnp.uint32(0xFFFF0000), F32)
    return lo, hi


PACK_PARTS = 2
PACK_WIDTH = D_MODEL // 2 // PACK_PARTS


def _pack_rows(v):
    half = D_MODEL // 2
    parts = []
    for j in range(PACK_PARTS):
        lo = v[:, j * PACK_WIDTH:(j + 1) * PACK_WIDTH]
        hi = v[:, half + j * PACK_WIDTH:half + (j + 1) * PACK_WIDTH]
        parts.append(pltpu.bitcast(_pack_bf16_pair(lo, hi), I32))
    return parts


def _unpack_rows(parts):
    pairs = [_unpack_bf16_pair(pltpu.bitcast(p, U32)) for p in parts]
    return jnp.concatenate([lo for lo, _ in pairs] + [hi for _, hi in pairs], axis=-1)


def _inproj_main_kernel(x_ref, g_ref, b_ref, w_ref, o_ref, h_scr):
    @pl.when(pl.program_id(1) == 0)
    def _():
        h_scr[...] = _layer_norm(x_ref[...], g_ref[...], b_ref[...]).astype(BF16)

    o_ref[...] = _dot(h_scr[...], w_ref[...]).astype(BF16)


def _inproj_main(x2, g, b, w_main):
    n = x2.shape[0]
    width = w_main.shape[1]
    return pl.pallas_call(
        _inproj_main_kernel,
        out_shape=jax.ShapeDtypeStruct((n, width), BF16),
        grid=(n // ROW_TILE, width // MAIN_COL_TILE),
        in_specs=[
            pl.BlockSpec((ROW_TILE, D_MODEL), lambda i, j: (i, 0)),
            pl.BlockSpec((1, D_MODEL), lambda i, j: (0, 0)),
            pl.BlockSpec((1, D_MODEL), lambda i, j: (0, 0)),
            pl.BlockSpec((D_MODEL, MAIN_COL_TILE), lambda i, j: (0, j)),
        ],
        out_specs=pl.BlockSpec((ROW_TILE, MAIN_COL_TILE), lambda i, j: (i, j)),
        scratch_shapes=[pltpu.VMEM((ROW_TILE, D_MODEL), BF16)],
        compiler_params=_params(("parallel", "arbitrary")),
        name="inproj_main",
    )(x2, g, b, w_main)


def _inproj_qkv_kernel(x_ref, g_ref, b_ref, w_ref, o0_ref, o1_ref, o2_ref,
                       hf_scr, h0_scr, h1_scr, h2_scr):
    @pl.when(pl.program_id(1) == 0)
    def _():
        hf = _layer_norm(x_ref[...], g_ref[...], b_ref[...])
        h0_scr[...] = hf.astype(BF16)
        for c in range(D_MODEL // LANES):
            hf_scr[c] = hf[:, c * LANES:(c + 1) * LANES]
        for h_scr, (_, dil) in ((h1_scr, ATT_GROUPS[1]), (h2_scr, ATT_GROUPS[2])):
            rows = ROW_TILE // dil
            for r in range(dil):
                for c in range(D_MODEL // LANES):
                    h_scr[r * rows:(r + 1) * rows, c * LANES:(c + 1) * LANES] = (
                        hf_scr[c, pl.ds(r, rows, stride=dil), :].astype(BF16))

    gw = ATT_GROUP_WIDTH
    o0_ref[0] = _dot(h0_scr[...], w_ref[:, 0:gw]).astype(BF16)
    d1 = ATT_GROUPS[1][1]
    o1_ref[0, 0] = _dot(h1_scr[...], w_ref[:, gw:2 * gw]).astype(BF16).reshape(d1, ROW_TILE // d1, gw)
    d2 = ATT_GROUPS[2][1]
    o2_ref[0, 0] = _dot(h2_scr[...], w_ref[:, 2 * gw:3 * gw]).astype(BF16).reshape(d2, ROW_TILE // d2, gw)


def _inproj_qkv(x2, g, b, w_qkv, batch, seq):
    n = x2.shape[0]
    gw = ATT_GROUP_WIDTH
    tiles_per_seq = seq // ROW_TILE
    d1, d2 = ATT_GROUPS[1][1], ATT_GROUPS[2][1]
    out_shape = (
        jax.ShapeDtypeStruct((3, n, gw), BF16),
        jax.ShapeDtypeStruct((3, batch, d1, seq // d1, gw), BF16),
        jax.ShapeDtypeStruct((3, batch, d2, seq // d2, gw), BF16),
    )
    return pl.pallas_call(
        _inproj_qkv_kernel,
        out_shape=out_shape,
        grid=(n // ROW_TILE, 3),
        in_specs=[
            pl.BlockSpec((ROW_TILE, D_MODEL), lambda i, t: (i, 0)),
            pl.BlockSpec((1, D_MODEL), lambda i, t: (0, 0)),
            pl.BlockSpec((1, D_MODEL), lambda i, t: (0, 0)),
            pl.BlockSpec((D_MODEL, ATT_WIDTH), lambda i, t: (0, t)),
        ],
        out_specs=(
            pl.BlockSpec((1, ROW_TILE, gw), lambda i, t: (t, i, 0)),
            pl.BlockSpec((1, 1, d1, ROW_TILE // d1, gw),
                         lambda i, t: (t, i // tiles_per_seq, 0, i % tiles_per_seq, 0)),
            pl.BlockSpec((1, 1, d2, ROW_TILE // d2, gw),
                         lambda i, t: (t, i // tiles_per_seq, 0, i % tiles_per_seq, 0)),
        ),
        scratch_shapes=[
            pltpu.VMEM((D_MODEL // LANES, ROW_TILE, LANES), F32),
            pltpu.VMEM((ROW_TILE, D_MODEL), BF16),
            pltpu.VMEM((ROW_TILE, D_MODEL), BF16),
            pltpu.VMEM((ROW_TILE, D_MODEL), BF16),
        ],
        compiler_params=_params(("parallel", "arbitrary")),
        name="inproj_qkv",
    )(x2, g, b, w_qkv)


def _attn_kernel(q_ref, kc_ref, kp_ref, vc_ref, vp_ref, o_ref, lse_ref):
    blk = ATT_BLOCK
    has_prev = pl.program_id(2) > 0
    row = lax.broadcasted_iota(I32, (blk, blk), 0)
    col = lax.broadcasted_iota(I32, (blk, blk), 1)
    mask_c = col <= row
    mask_p = (col >= row) & has_prev
    scale = ATT_HEAD_DIM ** -0.5
    neg = -jnp.inf
    for h in range(ATT_HEADS):
        sl = slice(h * ATT_HEAD_DIM, (h + 1) * ATT_HEAD_DIM)
        q = q_ref[:, sl]
        s_c = jnp.where(mask_c, _dot_nt(q, kc_ref[:, sl]) * scale, neg)
        s_p = jnp.where(mask_p, _dot_nt(q, kp_ref[:, sl]) * scale, neg)
        m = jnp.maximum(jnp.max(s_c, -1, keepdims=True), jnp.max(s_p, -1, keepdims=True))
        p_c = jnp.exp(s_c - m)
        p_p = jnp.exp(s_p - m)
        l = jnp.sum(p_c, -1, keepdims=True) + jnp.sum(p_p, -1, keepdims=True)
        o = _dot(p_c.astype(BF16), vc_ref[:, sl]) + _dot(p_p.astype(BF16), vp_ref[:, sl])
        o_ref[:, sl] = (o / l).astype(BF16)
        lse_ref[:, sl] = jnp.broadcast_to(m + jnp.log(l), (blk, ATT_HEAD_DIM))


def _attention_group(qkv):
    _, batch, dil, sub_len, gw = qkv.shape
    blk = ATT_BLOCK

    def spec(t, prev):
        def index(b, r, i):
            return (t, b, r, jnp.maximum(i - 1, 0) if prev else i, 0)
        return pl.BlockSpec((None, None, None, blk, gw), index)

    out_spec = pl.BlockSpec((None, None, blk, gw), lambda b, r, i: (b, r, i, 0))
    return pl.pallas_call(
        _attn_kernel,
        out_shape=(jax.ShapeDtypeStruct((batch, dil, sub_len, gw), BF16),
                   jax.ShapeDtypeStruct((batch, dil, sub_len, gw), F32)),
        grid=(batch, dil, sub_len // blk),
        in_specs=[spec(0, False), spec(1, False), spec(1, True), spec(2, False), spec(2, True)],
        out_specs=(out_spec, out_spec),
        compiler_params=_params(("parallel", "parallel", "arbitrary")),
        name=f"dilated_attention_d{dil}",
    )(qkv, qkv, qkv, qkv, qkv)


HGRN_ROWS = 256
HGRN_HEADS_PER_STEP = 4


def _hgrn_rows(q, z, v, gate, st, lb, nw, tri, same_chunk_causal, chunk_mask):
    c = HGRN_CHUNK
    dk = HGRN_HEAD_DIM
    rows = HGRN_ROWS
    nchunk = rows // c
    one_m_lb = 1.0 - lb
    z = z.astype(F32)
    log_f = jnp.log(lb + one_m_lb * _sigmoid(z))
    key = one_m_lb * _sigmoid(-z)
    p0 = log_f.astype(BF16)
    rem = log_f - p0.astype(F32)
    p1 = rem.astype(BF16)
    p2 = (rem - p1.astype(F32)).astype(BF16)
    bcum = _dot(tri, p0) + _dot(tri, p1) + _dot(tri, p2)

    b3 = bcum.reshape(nchunk, c, dk)
    b_mid = b3[:, c // 2:c // 2 + 1, :]
    b_last = b3[:, c - 1:c, :]
    q3 = q.astype(F32).reshape(nchunk, c, dk) * (dk ** -0.5)
    q_t3 = q3 * jnp.exp(b3 - b_mid)
    k_t3 = key.reshape(nchunk, c, dk) * jnp.exp(b_mid - b3)
    q_t = q_t3.reshape(rows, dk).astype(BF16)
    k_t = k_t3.reshape(rows, dk).astype(BF16)
    q_in = (q_t3 * jnp.exp(b_mid)).reshape(rows, dk).astype(BF16)
    k_st = (k_t3 * jnp.exp(b_last - b_mid)).reshape(rows, dk).astype(BF16)

    att = jnp.where(same_chunk_causal, _dot_nt(q_t, k_t), 0.0)
    o = _dot(att.astype(BF16), v)

    k_spread = jnp.concatenate([k_st * chunk_mask[j] for j in range(nchunk)], axis=1)
    incr = _dot_tn(v, k_spread)
    decay = jnp.exp(b_last.reshape(nchunk, dk))
    before = []
    for j in range(nchunk):
        before.append(st.astype(BF16))
        st = st * decay[j:j + 1] + incr[:, j * dk:(j + 1) * dk]
    q_spread = jnp.concatenate([q_in * chunk_mask[j] for j in range(nchunk)], axis=1)
    o = o + _dot_nt(q_spread, jnp.concatenate(before, axis=1))

    o = o * lax.rsqrt(jnp.mean(o * o, -1, keepdims=True) + RMS_EPS) * nw
    gate = gate.astype(F32)
    return o * gate * _sigmoid(gate), st


def _hgrn_kernel(q_ref, f_ref, i_ref, g_ref, lb_ref, nw_ref, o_ref, state_scr, mask_scr):
    seq = q_ref.shape[0]
    c = HGRN_CHUNK
    rows = HGRN_ROWS
    dk = HGRN_HEAD_DIM
    nw = nw_ref[...]

    r_i = lax.broadcasted_iota(I32, (rows, rows), 0)
    c_i = lax.broadcasted_iota(I32, (rows, rows), 1)
    same_chunk_causal = (r_i // c == c_i // c) & (c_i <= r_i)
    tri = jnp.where(same_chunk_causal, 1.0, 0.0).astype(BF16)
    row_chunk = lax.broadcasted_iota(I32, (rows, dk), 0) // c
    for j in range(rows // c):
        mask_scr[j] = jnp.where(row_chunk == j, 1.0, 0.0).astype(BF16)

    state_scr[...] = jnp.zeros_like(state_scr)

    def body(gi, carry):
        rs = pl.ds(pl.multiple_of(gi * rows, rows), rows)
        for h in range(HGRN_HEADS_PER_STEP):
            cs = slice(h * dk, (h + 1) * dk)
            o, st = _hgrn_rows(q_ref[rs, cs], f_ref[rs, cs], i_ref[rs, cs], g_ref[rs, cs],
                               state_scr[h], lb_ref[:, cs], nw, tri, same_chunk_causal, mask_scr)
            state_scr[h] = st
            o_ref[rs, cs] = o.astype(BF16)
        return carry

    lax.fori_loop(0, seq // rows, body, 0)


def _hgrn(proj, lower_bound, norm_w, batch, seq):
    n = proj.shape[0]
    dk = HGRN_HEAD_DIM
    width = HGRN_HEADS_PER_STEP * dk
    steps = HGRN_HEADS // HGRN_HEADS_PER_STEP

    def seg(k):
        return pl.BlockSpec((seq, width), lambda b, h: (b, k * steps + h))

    return pl.pallas_call(
        _hgrn_kernel,
        out_shape=jax.ShapeDtypeStruct((n, HGRN_WIDTH), BF16),
        grid=(batch, steps),
        in_specs=[seg(0), seg(1), seg(2), seg(3),
                  pl.BlockSpec((1, width), lambda b, h: (0, h)),
                  pl.BlockSpec((1, dk), lambda b, h: (0, 0))],
        out_specs=pl.BlockSpec((seq, width), lambda b, h: (b, h)),
        scratch_shapes=[pltpu.VMEM((HGRN_HEADS_PER_STEP, dk, dk), F32),
                        pltpu.VMEM((HGRN_ROWS // HGRN_CHUNK, HGRN_ROWS, dk), BF16)],
        compiler_params=_params(("parallel", "parallel")),
        name="hgrn2",
    )(proj, proj, proj, proj, lower_bound, norm_w)


def _mix_kernel(x_ref, gin_ref, bin_ref, o0_ref, l0_ref, o1_ref, l1_ref, o2_ref, l2_ref,
                ob_ref, ga_ref, gb_ref, wa_ref, wb_ref, wo_ref, g1_ref, b1_ref,
                rwh_ref, rwl_ref,
                h_ref, hp0_ref, hp1_ref, lg_ref,
                so1_scr, sl1_scr, so2_scr, sl2_scr):
    tm = x_ref.shape[0]
    for o_ref, l_ref, so_scr, sl_scr, (_, dil) in (
            (o1_ref, l1_ref, so1_scr, sl1_scr, ATT_GROUPS[1]),
            (o2_ref, l2_ref, so2_scr, sl2_scr, ATT_GROUPS[2])):
        for r in range(dil):
            o_r = o_ref[r].astype(F32)
            l_r = l_ref[r]
            for c in range(ATT_GROUP_WIDTH // LANES):
                cs = slice(c * LANES, (c + 1) * LANES)
                so_scr[c, pl.ds(r, tm // dil, stride=dil), :] = o_r[:, cs]
                sl_scr[c, pl.ds(r, tm // dil, stride=dil), :] = l_r[:, cs]

    def natural(scr):
        return jnp.concatenate([scr[c] for c in range(ATT_GROUP_WIDTH // LANES)], axis=-1)

    l0 = l0_ref[...]
    l1 = natural(sl1_scr)
    l2 = natural(sl2_scr)
    m = jnp.maximum(jnp.maximum(l0, l1), l2)
    e0 = jnp.exp(l0 - m)
    e1 = jnp.exp(l1 - m)
    e2 = jnp.exp(l2 - m)
    o_att = (e0 * o0_ref[...].astype(F32) + e1 * natural(so1_scr) + e2 * natural(so2_scr)) / (e0 + e1 + e2)

    y_a = _dot(o_att.astype(BF16), wa_ref[...])
    y_b = _dot(ob_ref[...], wb_ref[...])
    merged = _sigmoid(ga_ref[...].astype(F32)) * y_a + _sigmoid(gb_ref[...].astype(F32)) * y_b
    mix = _dot(merged.astype(BF16), wo_ref[...])
    h_in = _layer_norm(x_ref[...], gin_ref[...], bin_ref[...])
    h1 = _layer_norm(DEEPNORM_ALPHA * h_in + mix, g1_ref[...], b1_ref[...])
    h_ref[...] = h1

    hp0_ref[...], hp1_ref[...] = _pack_rows(h1)

    h_hi = h1.astype(BF16)
    h_lo = (h1 - h_hi.astype(F32)).astype(BF16)
    rwh = rwh_ref[...]
    lg_ref[...] = _dot_nt(rwh, h_hi) + _dot_nt(rwh, h_lo) + _dot_nt(rwl_ref[...], h_hi)


def _mix(x2, gin, bin_, att, ob, proj, wa, wb, wo, g1, b1, rwh, rwl, batch, seq):
    n = x2.shape[0]
    tm = MIX_ROW_TILE
    gw = ATT_GROUP_WIDTH
    tiles_per_seq = seq // tm
    (o0, l0), (o1, l1), (o2, l2) = att
    d1, d2 = ATT_GROUPS[1][1], ATT_GROUPS[2][1]

    def full(shape):
        return pl.BlockSpec(shape, lambda i: (0,) * len(shape))

    def dil_spec(dil):
        return pl.BlockSpec((None, dil, tm // dil, gw),
                            lambda i: (i // tiles_per_seq, 0, i % tiles_per_seq, 0))

    nat_spec = pl.BlockSpec((tm, gw), lambda i: (i, 0))
    return pl.pallas_call(
        _mix_kernel,
        out_shape=(jax.ShapeDtypeStruct((n, D_MODEL), F32),
                   jax.ShapeDtypeStruct((n, PACK_WIDTH), I32),
                   jax.ShapeDtypeStruct((n, PACK_WIDTH), I32),
                   jax.ShapeDtypeStruct((N_EXPERTS, n), F32)),
        grid=(n // tm,),
        in_specs=[
            pl.BlockSpec((tm, D_MODEL), lambda i: (i, 0)),
            full((1, D_MODEL)), full((1, D_MODEL)),
            nat_spec, nat_spec,
            dil_spec(d1), dil_spec(d1), dil_spec(d2), dil_spec(d2),
            pl.BlockSpec((tm, HGRN_WIDTH), lambda i: (i, 0)),
            pl.BlockSpec((tm, D_MODEL), lambda i: (i, 4)),
            pl.BlockSpec((tm, D_MODEL), lambda i: (i, 5)),
            full((gw, D_MODEL)), full((HGRN_WIDTH, D_MODEL)), full((D_MODEL, D_MODEL)),
            full((1, D_MODEL)), full((1, D_MODEL)),
            full((N_EXPERTS, D_MODEL)), full((N_EXPERTS, D_MODEL)),
        ],
        out_specs=(pl.BlockSpec((tm, D_MODEL), lambda i: (i, 0)),
                   pl.BlockSpec((tm, PACK_WIDTH), lambda i: (i, 0)),
                   pl.BlockSpec((tm, PACK_WIDTH), lambda i: (i, 0)),
                   pl.BlockSpec((N_EXPERTS, tm), lambda i: (0, i))),
        scratch_shapes=[pltpu.VMEM((gw // LANES, tm, LANES), F32)] * 4,
        compiler_params=_params(("parallel",)),
        name="branch_mix",
    )(x2, gin, bin_, o0, l0, o1, l1, o2, l2, ob, proj, proj, wa, wb, wo, g1, b1, rwh, rwl)


ROW_SLOTS = 16
ROW_GROUP = 4
ROW_AHEAD = ROW_SLOTS - ROW_GROUP
W_SLOTS = 3


def _expert_kernel(sblk_ref, nblk_ref, cnt_ref, xs0_hbm, xs1_hbm, wi_hbm, wo_hbm,
                   ys0_hbm, ys1_hbm, xbuf0, xbuf1, ybuf0, ybuf1, wibuf, wobuf,
                   in_sem, out_sem, w_sem, wi_scr, wo_scr):
    e = pl.program_id(0)
    n_exp = pl.num_programs(0)
    first_blk = sblk_ref[e]
    n_blk = nblk_ref[e]
    n_rows = cnt_ref[e]
    total = sblk_ref[n_exp - 1] + nblk_ref[n_exp - 1]
    xs_hbm, ys_hbm = (xs0_hbm, xs1_hbm), (ys0_hbm, ys1_hbm)
    xbuf, ybuf = (xbuf0, xbuf1), (ybuf0, ybuf1)

    def rows_of(b):
        return pl.ds(pl.multiple_of(b * MOE_BLOCK, MOE_BLOCK), MOE_BLOCK)

    def x_copy(b, part):
        slot = b % ROW_SLOTS
        return pltpu.make_async_copy(xs_hbm[part].at[rows_of(b)], xbuf[part].at[slot],
                                     in_sem.at[part, slot])

    def y_copy(b, part):
        slot = b % ROW_SLOTS
        return pltpu.make_async_copy(ybuf[part].at[slot], ys_hbm[part].at[rows_of(b)],
                                     out_sem.at[part, slot])

    def w_copies(ex):
        slot = ex % W_SLOTS
        return (pltpu.make_async_copy(wi_hbm.at[ex], wibuf.at[slot], w_sem.at[0, slot]),
                pltpu.make_async_copy(wo_hbm.at[ex], wobuf.at[slot], w_sem.at[1, slot]))

    def start_weights(ex):
        @pl.when((ex < n_exp) & (nblk_ref[jnp.minimum(ex, n_exp - 1)] > 0))
        def _():
            for cp in w_copies(ex):
                cp.start()

    @pl.when(e == 0)
    def _():
        for b in range(ROW_AHEAD):
            @pl.when(b < total)
            def _():
                for part in range(PACK_PARTS):
                    x_copy(b, part).start()
        for ex in range(W_SLOTS - 1):
            start_weights(ex)

    start_weights(e + (W_SLOTS - 1))

    def step(j, group):
        b0 = first_blk + j
        for i in range(group):
            for part in range(PACK_PARTS):
                x_copy(b0 + i, part).wait()
        for i in range(group):
            @pl.when(b0 + ROW_AHEAD + i < total)
            def _():
                for part in range(PACK_PARTS):
                    x_copy(b0 + ROW_AHEAD + i, part).start()
        for i in range(group):
            @pl.when(b0 + i >= ROW_SLOTS)
            def _():
                for part in range(PACK_PARTS):
                    y_copy(b0 + i - ROW_SLOTS, part).wait()

        rows = group * MOE_BLOCK
        live = (lax.broadcasted_iota(I32, (rows, PACK_WIDTH), 0)
                < n_rows - j * MOE_BLOCK)
        parts = [jnp.concatenate([xbuf[part][(b0 + i) % ROW_SLOTS] for i in range(group)], axis=0)
                 for part in range(PACK_PARTS)]
        x = _unpack_rows([jnp.where(live, p, 0) for p in parts])
        hu = _dot(x.astype(BF16), wi_scr[...])
        hg = hu[:, :EXPERT_DIM]
        hv = hu[:, EXPERT_DIM:]
        act = (hg * _sigmoid(hg) * hv).astype(BF16)
        packed = _pack_rows(_dot(act, wo_scr[...]))
        for i in range(group):
            for part in range(PACK_PARTS):
                ybuf[part][(b0 + i) % ROW_SLOTS] = packed[part][i * MOE_BLOCK:(i + 1) * MOE_BLOCK]
                y_copy(b0 + i, part).start()

    @pl.when(n_blk > 0)
    def _():
        for cp in w_copies(e):
            cp.wait()
        slot = e % W_SLOTS
        wi_scr[...] = wibuf[slot].astype(BF16)
        wo_scr[...] = wobuf[slot].astype(BF16)

        def full_group(g, carry):
            step(g * ROW_GROUP, ROW_GROUP)
            return carry

        lax.fori_loop(0, n_blk // ROW_GROUP, full_group, 0)
        done = n_blk // ROW_GROUP * ROW_GROUP
        group = ROW_GROUP // 2
        while group >= 1:
            @pl.when((n_blk & group) != 0)
            def _(group=group, done=done):
                step(done, group)
            done = done + (n_blk & group)
            group //= 2

    @pl.when(e == n_exp - 1)
    def _():
        for back in range(1, ROW_SLOTS + 1):
            @pl.when(total >= back)
            def _():
                for part in range(PACK_PARTS):
                    y_copy(total - back, part).wait()


def _experts(start_blk, n_blk, n_rows, xs, w_in_e, w_out_e):
    p = xs[0].shape[0]
    n_exp = w_in_e.shape[0]
    any_spec = pl.BlockSpec(memory_space=pl.ANY)
    row_buf = pltpu.VMEM((ROW_SLOTS, MOE_BLOCK, PACK_WIDTH), I32)
    grid_spec = pltpu.PrefetchScalarGridSpec(
        num_scalar_prefetch=3,
        grid=(n_exp,),
        in_specs=[any_spec, any_spec, any_spec, any_spec],
        out_specs=(any_spec, any_spec),
        scratch_shapes=[row_buf, row_buf, row_buf, row_buf,
                        pltpu.VMEM((W_SLOTS, D_MODEL, 2 * EXPERT_DIM), F32),
                        pltpu.VMEM((W_SLOTS, EXPERT_DIM, D_MODEL), F32),
                        pltpu.SemaphoreType.DMA((PACK_PARTS, ROW_SLOTS)),
                        pltpu.SemaphoreType.DMA((PACK_PARTS, ROW_SLOTS)),
                        pltpu.SemaphoreType.DMA((2, W_SLOTS)),
                        pltpu.VMEM((D_MODEL, 2 * EXPERT_DIM), BF16),
                        pltpu.VMEM((EXPERT_DIM, D_MODEL), BF16)],
    )
    return pl.pallas_call(
        _expert_kernel,
        out_shape=(jax.ShapeDtypeStruct((p, PACK_WIDTH), I32),) * PACK_PARTS,
        grid_spec=grid_spec,
        compiler_params=_params(("arbitrary",)),
        name="routed_experts",
    )(start_blk, n_blk, n_rows, xs[0], xs[1], w_in_e, w_out_e)


def _combine_kernel(h_ref, yg0_ref, yg1_ref, gate_ref, wsi_ref, wso_ref, g2_ref, b2_ref, o_ref):
    h1 = h_ref[...]
    gate = gate_ref[...]
    routed = None
    for k in range(TOP_K):
        y_k = _unpack_rows([yg0_ref[k], yg1_ref[k]]) * gate[:, k:k + 1]
        routed = y_k if routed is None else routed + y_k
    hs = _dot(h1.astype(BF16), wsi_ref[...])
    sg = hs[:, :SHARED_DIM]
    sv = hs[:, SHARED_DIM:]
    shared = _dot((sg * _sigmoid(sg) * sv).astype(BF16), wso_ref[...])
    o_ref[...] = _layer_norm(DEEPNORM_ALPHA * h1 + routed + shared, g2_ref[...], b2_ref[...])


def _combine(h1, yg, gate_nk, wsi, wso, g2, b2):
    n = h1.shape[0]
    tm = OUT_ROW_TILE
    yg_spec = pl.BlockSpec((TOP_K, tm, PACK_WIDTH), lambda i: (0, i, 0))

    def full(shape):
        return pl.BlockSpec(shape, lambda i: (0,) * len(shape))

    return pl.pallas_call(
        _combine_kernel,
        out_shape=jax.ShapeDtypeStruct((n, D_MODEL), F32),
        grid=(n // tm,),
        in_specs=[
            pl.BlockSpec((tm, D_MODEL), lambda i: (i, 0)),
            yg_spec, yg_spec,
            pl.BlockSpec((tm, TOP_K), lambda i: (i, 0)),
            full((D_MODEL, 2 * SHARED_DIM)), full((SHARED_DIM, D_MODEL)),
            full((1, D_MODEL)), full((1, D_MODEL)),
        ],
        out_specs=pl.BlockSpec((tm, D_MODEL), lambda i: (i, 0)),
        compiler_params=_params(("parallel",)),
        name="shared_combine",
    )(h1, yg[0], yg[1], gate_nk, wsi, wso, g2, b2)


ROUTE_TILE = 512


def _pick_first_max(vals, iota, axis, size):
    m = jnp.max(vals, axis=axis, keepdims=True)
    idx = jnp.min(jnp.where(vals == m, iota, size), axis=axis, keepdims=True)
    return m, idx


def _route_kernel(lg_ref, bias_ref, idx_ref, gate_ref, rank_ref, cnt_ref, carry_scr):
    e, tn = lg_ref.shape
    groups = N_EXPERT_GROUPS
    gsz = e // groups
    neg = -jnp.inf

    @pl.when(pl.program_id(0) == 0)
    def _():
        carry_scr[...] = jnp.zeros_like(carry_scr)

    scores = _sigmoid(lg_ref[...])
    biased = scores + bias_ref[...]

    b3 = biased.reshape(groups, gsz, tn)
    io3 = lax.broadcasted_iota(I32, b3.shape, 1)
    m1, i1 = _pick_first_max(b3, io3, 1, gsz)
    m2 = jnp.max(jnp.where(io3 == i1, neg, b3), axis=1, keepdims=True)
    grp = m1 + m2
    iog = lax.broadcasted_iota(I32, grp.shape, 0)
    keep = jnp.zeros(grp.shape, F32)
    for _ in range(TOPK_GROUPS):
        _, gi = _pick_first_max(grp, iog, 0, groups)
        hit = iog == gi
        keep = jnp.where(hit, 1.0, keep)
        grp = jnp.where(hit, neg, grp)
    masked = jnp.where(keep > 0.0, b3, neg).reshape(e, tn)

    ioe = lax.broadcasted_iota(I32, (e, tn), 0)
    onehot = jnp.zeros((e, tn), F32)
    idxs, gates = [], []
    for _ in range(TOP_K):
        _, ei = _pick_first_max(masked, ioe, 0, e)
        hit = ioe == ei
        gates.append(jnp.sum(jnp.where(hit, scores, 0.0), axis=0, keepdims=True))
        onehot = jnp.where(hit, 1.0, onehot)
        masked = jnp.where(hit, neg, masked)
        idxs.append(ei)
    gsum = gates[0]
    for g in gates[1:]:
        gsum = gsum + g
    idx_ref[...] = jnp.concatenate(idxs, axis=0)
    gate_ref[...] = jnp.concatenate(gates, axis=0) / gsum * ROUTED_SCALE

    t_r = lax.broadcasted_iota(I32, (tn, tn), 0)
    t_c = lax.broadcasted_iota(I32, (tn, tn), 1)
    earlier = jnp.where(t_r < t_c, 1.0, 0.0).astype(BF16)
    oh = onehot.astype(BF16)
    base = carry_scr[...]
    before = _dot(oh, earlier) + jnp.concatenate([base] * (tn // LANES), axis=1)
    ranks = [jnp.sum(jnp.where(ioe == ei, before, 0.0), axis=0, keepdims=True) for ei in idxs]
    rank_ref[...] = jnp.concatenate(ranks, axis=0).astype(I32)
    total = base + _dot(oh, jnp.ones((tn, LANES), BF16))
    carry_scr[...] = total
    cnt_ref[...] = total


def _route(logits_t, bias_col):
    e, n = logits_t.shape
    tn = ROUTE_TILE
    tok_spec = pl.BlockSpec((TOP_K, tn), lambda i: (0, i))
    return pl.pallas_call(
        _route_kernel,
        out_shape=(jax.ShapeDtypeStruct((TOP_K, n), I32),
                   jax.ShapeDtypeStruct((TOP_K, n), F32),
                   jax.ShapeDtypeStruct((TOP_K, n), I32),
                   jax.ShapeDtypeStruct((e, LANES), F32)),
        grid=(n // tn,),
        in_specs=[pl.BlockSpec((e, tn), lambda i: (0, i)),
                  pl.BlockSpec((e, 1), lambda i: (0, 0))],
        out_specs=(tok_spec, tok_spec, tok_spec, pl.BlockSpec((e, LANES), lambda i: (0, 0))),
        scratch_shapes=[pltpu.VMEM((e, LANES), F32)],
        compiler_params=_params(("arbitrary",)),
        name="router_topk",
    )(logits_t, bias_col)


def _plan_kernel(idx_ref, rank_ref, cnt_ref, dest_ref, sblk_ref):
    e = cnt_ref.shape[0]
    tn = idx_ref.shape[1]
    cnt = cnt_ref[...]
    nblk = jnp.floor((cnt + (MOE_BLOCK - 1)) * (1.0 / MOE_BLOCK))
    e_r = lax.broadcasted_iota(I32, (e, e), 0)
    e_c = lax.broadcasted_iota(I32, (e, e), 1)
    lower = jnp.where(e_c < e_r, 1.0, 0.0).astype(BF16)
    start_blk = _dot(lower, nblk.astype(BF16))

    ioe = lax.broadcasted_iota(I32, (e, tn), 0)
    start_row = jnp.concatenate([start_blk * MOE_BLOCK] * (tn // LANES), axis=1)
    idx = idx_ref[...]
    dests = [jnp.sum(jnp.where(ioe == idx[k:k + 1], start_row, 0.0), axis=0, keepdims=True)
             for k in range(TOP_K)]
    dest_ref[...] = jnp.concatenate(dests, axis=0).astype(I32) + rank_ref[...]
    sblk_ref[...] = start_blk.astype(I32)


def _plan(idx_t, rank_t, counts):
    k, n = idx_t.shape
    e = counts.shape[0]
    tn = ROUTE_TILE
    tok_spec = pl.BlockSpec((k, tn), lambda i: (0, i))
    exp_spec = pl.BlockSpec((e, LANES), lambda i: (0, 0))
    return pl.pallas_call(
        _plan_kernel,
        out_shape=(jax.ShapeDtypeStruct((k, n), I32), jax.ShapeDtypeStruct((e, LANES), I32)),
        grid=(n // tn,),
        in_specs=[tok_spec, tok_spec, exp_spec],
        out_specs=(tok_spec, exp_spec),
        compiler_params=_params(("arbitrary",)),
        name="dispatch_plan",
    )(idx_t, rank_t, counts)


SC_WINDOW = 128


def _sc_mesh():
    return plsc.VectorSubcoreMesh(core_axis_name="core", subcore_axis_name="subcore")


def _dispatch_rows(parts, dest, p_total):
    n, width = parts[0].shape
    top_k = dest.shape[0]
    out_type = (jax.ShapeDtypeStruct((p_total, width), parts[0].dtype),) * len(parts)

    @functools.partial(pl.kernel, mesh=_sc_mesh(), scratch_types=[], out_type=out_type,
                       name="dispatch_rows")
    def scatter(*refs):
        x_hbms = refs[:len(parts)]
        i_hbm = refs[len(parts)]
        o_hbms = refs[len(parts) + 1:]
        for x_hbm, o_hbm in zip(x_hbms, o_hbms):
            def body(x_vmem, i_vmem, o_hbm=o_hbm):
                for k in range(top_k):
                    pltpu.sync_copy(x_vmem, o_hbm.at[i_vmem.at[k]])

            pltpu.emit_pipeline(
                body,
                grid=(n // SC_WINDOW,),
                in_specs=[pl.BlockSpec((SC_WINDOW, width), lambda i: (i, 0)),
                          pl.BlockSpec((top_k, SC_WINDOW), lambda i: (0, i))],
                out_specs=[],
                core_axis_name=("core", "subcore"),
                dimension_semantics=(pltpu.PARALLEL,),
            )(x_hbm, i_hbm)

    return scatter(*parts, dest)


def _gather_rows(parts, idx_flat):
    count = idx_flat.shape[1]
    width = parts[0].shape[1]
    out_type = (jax.ShapeDtypeStruct((count, width), parts[0].dtype),) * len(parts)

    @functools.partial(pl.kernel, mesh=_sc_mesh(), scratch_types=[], out_type=out_type,
                       name="combine_rows")
    def gather(*refs):
        y_hbms = refs[:len(parts)]
        i_hbm = refs[len(parts)]
        o_hbms = refs[len(parts) + 1:]
        for y_hbm, o_hbm in zip(y_hbms, o_hbms):
            def body(i_vmem, o_vmem, y_hbm=y_hbm):
                pltpu.sync_copy(y_hbm.at[i_vmem.at[0]], o_vmem)

            pltpu.emit_pipeline(
                body,
                grid=(count // SC_WINDOW,),
                in_specs=[pl.BlockSpec((1, SC_WINDOW), lambda i: (0, i))],
                out_specs=[pl.BlockSpec((SC_WINDOW, width), lambda i: (i, 0))],
                core_axis_name=("core", "subcore"),
                dimension_semantics=(pltpu.PARALLEL,),
            )(i_hbm, o_hbm)

    return gather(*parts, idx_flat)


def kernel(x, ln_in_g, ln_in_b, w_in, hgrn_lb_logits, hgrn_norm_w, w_branch_att, w_branch_hgrn,
           w_out, ln1_g, ln1_b, router_w, router_bias, expert_w_in, expert_w_out, shared_w_in,
           shared_w_out, ln2_g, ln2_b):
    batch, seq, d = x.shape
    n = batch * seq
    x2 = x.reshape(n, d)
    row = lambda v: v.reshape(1, -1).astype(F32)

    lower_bounds = jnp.cumsum(jax.nn.softmax(hgrn_lb_logits.astype(F32), axis=0), axis=0)
    l = 0
    w_l = w_in[l]
    w_qkv = w_l[:, :3 * ATT_WIDTH].astype(BF16)
    w_main = w_l[:, 3 * ATT_WIDTH:].astype(BF16)
    gin, bin_ = row(ln_in_g), row(ln_in_b)

    proj = _inproj_main(x2, gin, bin_, w_main)
    qkv0, qkv1, qkv2 = _inproj_qkv(x2, gin, bin_, w_qkv, batch, seq)
    qkv0 = qkv0.reshape(3, batch, 1, seq, ATT_GROUP_WIDTH)
    att = [_attention_group(qkv) for qkv in (qkv0, qkv1, qkv2)]
    o0, l0 = att[0]
    att[0] = (o0.reshape(n, ATT_GROUP_WIDTH), l0.reshape(n, ATT_GROUP_WIDTH))

    ob = _hgrn(proj, row(lower_bounds[l]), row(hgrn_norm_w[l]), batch, seq)

    rw_t = router_w[l].T.astype(F32)
    rwh = rw_t.astype(BF16)
    rwl = (rw_t - rwh.astype(F32)).astype(BF16)
    h1, hp0, hp1, logits_t = _mix(
        x2, gin, bin_, att, ob, proj,
        w_branch_att[l].astype(BF16), w_branch_hgrn[l].astype(BF16), w_out[l].astype(BF16),
        row(ln1_g[l]), row(ln1_b[l]), rwh, rwl, batch, seq)

    idx_t, gate_t, rank_t, counts = _route(logits_t, router_bias[l].reshape(-1, 1).astype(F32))
    p_total = n * TOP_K + N_EXPERTS * MOE_BLOCK
    dest, start_blk = _plan(idx_t, rank_t, counts)
    n_rows = counts[:, 0].astype(I32)
    n_blk = (n_rows + (MOE_BLOCK - 1)) // MOE_BLOCK
    xs = _dispatch_rows((hp0, hp1), dest, p_total)
    ys = _experts(start_blk[:, 0], n_blk, n_rows, xs, expert_w_in[l], expert_w_out[l])
    yg = [g.reshape(TOP_K, n, PACK_WIDTH)
          for g in _gather_rows(ys, dest.reshape(1, TOP_K * n))]
    out = _combine(h1, yg, gate_t.T, shared_w_in[l].astype(BF16), shared_w_out[l].astype(BF16),
                   row(ln2_g[l]), row(ln2_b[l]))
    return out.reshape(batch, seq, d)
```

```python
import functools

import jax
import jax.numpy as jnp
from jax import lax
from jax.experimental import pallas as pl
from jax.experimental.pallas import tpu as pltpu
from jax.experimental.pallas import tpu_sc as plsc

F32 = jnp.float32
BF16 = jnp.bfloat16
U32 = jnp.uint32
I32 = jnp.int32

D_MODEL = 1024
ATT_GROUPS = ((128, 1), (512, 4), (2048, 16))
ATT_HEADS = 8
ATT_HEAD_DIM = 64
ATT_GROUP_WIDTH = ATT_HEADS * ATT_HEAD_DIM
ATT_WIDTH = len(ATT_GROUPS) * ATT_GROUP_WIDTH
ATT_BLOCK = 128
HGRN_HEAD_DIM = 128
HGRN_HEADS = D_MODEL // HGRN_HEAD_DIM
HGRN_WIDTH = HGRN_HEADS * HGRN_HEAD_DIM
HGRN_CHUNK = 32
N_EXPERTS = 256
TOP_K = 8
N_EXPERT_GROUPS = 8
TOPK_GROUPS = 4
EXPERT_DIM = 256
SHARED_DIM = 256
ROUTED_SCALE = 2.5
MOE_BLOCK = 128
LN_EPS = 1e-5
RMS_EPS = 1e-6
DEPTH = 1
DEEPNORM_ALPHA = (2 * DEPTH) ** 0.25

VMEM_LIMIT = 56 * 1024 * 1024
LANES = 128

ROW_TILE = 1024
MAIN_COL_TILE = 1536
MIX_ROW_TILE = 512
OUT_ROW_TILE = 512


def _params(sem, vmem=VMEM_LIMIT):
    return pltpu.CompilerParams(dimension_semantics=sem, vmem_limit_bytes=vmem)


def _layer_norm(x, g, b):
    mu = jnp.mean(x, -1, keepdims=True)
    xc = x - mu
    var = jnp.mean(xc * xc, -1, keepdims=True)
    return xc * lax.rsqrt(var + LN_EPS) * g + b


def _sigmoid(x):
    return 1.0 / (1.0 + jnp.exp(-x))


def _dot(a, b):
    return jnp.dot(a, b, preferred_element_type=F32)


def _dot_nt(a, b):
    return lax.dot_general(a, b, (((1,), (1,)), ((), ())), preferred_element_type=F32)


def _dot_tn(a, b):
    return lax.dot_general(a, b, (((0,), (0,)), ((), ())), preferred_element_type=F32)


def _pack_bf16_pair(lo, hi):
    lo_bits = pltpu.bitcast(lo.astype(BF16).astype(F32), U32) >> 16
    hi_bits = pltpu.bitcast(hi.astype(BF16).astype(F32), U32) & jnp.uint32(0xFFFF0000)
    return hi_bits | lo_bits


def _unpack_bf16_pair(w):
    lo = pltpu.bitcast(w << 16, F32)
    hi = pltpu.bitcast(w & jnp.uint32(0xFFFF0000), F32)
    return lo, hi


PACK_PARTS = 2
PACK_WIDTH = D_MODEL // 2 // PACK_PARTS


def _pack_rows(v):
    half = D_MODEL // 2
    parts = []
    for j in range(PACK_PARTS):
        lo = v[:, j * PACK_WIDTH:(j + 1) * PACK_WIDTH]
        hi = v[:, half + j * PACK_WIDTH:half + (j + 1) * PACK_WIDTH]
        parts.append(pltpu.bitcast(_pack_bf16_pair(lo, hi), I32))
    return parts


def _unpack_rows(parts):
    pairs = [_unpack_bf16_pair(pltpu.bitcast(p, U32)) for p in parts]
    return jnp.concatenate([lo for lo, _ in pairs] + [hi for _, hi in pairs], axis=-1)


def _inproj_main_kernel(x_ref, g_ref, b_ref, w_ref, o_ref, h_scr):
    @pl.when(pl.program_id(1) == 0)
    def _():
        h_scr[...] = _layer_norm(x_ref[...], g_ref[...], b_ref[...]).astype(BF16)

    o_ref[...] = _dot(h_scr[...], w_ref[...]).astype(BF16)


def _inproj_main(x2, g, b, w_main):
    n = x2.shape[0]
    width = w_main.shape[1]
    return pl.pallas_call(
        _inproj_main_kernel,
        out_shape=jax.ShapeDtypeStruct((n, width), BF16),
        grid=(n // ROW_TILE, width // MAIN_COL_TILE),
        in_specs=[
            pl.BlockSpec((ROW_TILE, D_MODEL), lambda i, j: (i, 0)),
            pl.BlockSpec((1, D_MODEL), lambda i, j: (0, 0)),
            pl.BlockSpec((1, D_MODEL), lambda i, j: (0, 0)),
            pl.BlockSpec((D_MODEL, MAIN_COL_TILE), lambda i, j: (0, j)),
        ],
        out_specs=pl.BlockSpec((ROW_TILE, MAIN_COL_TILE), lambda i, j: (i, j)),
        scratch_shapes=[pltpu.VMEM((ROW_TILE, D_MODEL), BF16)],
        compiler_params=_params(("parallel", "arbitrary")),
        name="inproj_main",
    )(x2, g, b, w_main)


def _inproj_qkv_kernel(x_ref, g_ref, b_ref, w_ref, o0_ref, o1_ref, o2_ref,
                       hf_scr, h0_scr, h1_scr, h2_scr):
    @pl.when(pl.program_id(1) == 0)
    def _():
        hf = _layer_norm(x_ref[...], g_ref[...], b_ref[...])
        h0_scr[...] = hf.astype(BF16)
        for c in range(D_MODEL // LANES):
            hf_scr[c] = hf[:, c * LANES:(c + 1) * LANES]
        for h_scr, (_, dil) in ((h1_scr, ATT_GROUPS[1]), (h2_scr, ATT_GROUPS[2])):
            rows = ROW_TILE // dil
            for r in range(dil):
                for c in range(D_MODEL // LANES):
                    h_scr[r * rows:(r + 1) * rows, c * LANES:(c + 1) * LANES] = (
                        hf_scr[c, pl.ds(r, rows, stride=dil), :].astype(BF16))

    gw = ATT_GROUP_WIDTH
    o0_ref[0] = _dot(h0_scr[...], w_ref[:, 0:gw]).astype(BF16)
    d1 = ATT_GROUPS[1][1]
    o1_ref[0, 0] = _dot(h1_scr[...], w_ref[:, gw:2 * gw]).astype(BF16).reshape(d1, ROW_TILE // d1, gw)
    d2 = ATT_GROUPS[2][1]
    o2_ref[0, 0] = _dot(h2_scr[...], w_ref[:, 2 * gw:3 * gw]).astype(BF16).reshape(d2, ROW_TILE // d2, gw)


def _inproj_qkv(x2, g, b, w_qkv, batch, seq):
    n = x2.shape[0]
    gw = ATT_GROUP_WIDTH
    tiles_per_seq = seq // ROW_TILE
    d1, d2 = ATT_GROUPS[1][1], ATT_GROUPS[2][1]
    out_shape = (
        jax.ShapeDtypeStruct((3, n, gw), BF16),
        jax.ShapeDtypeStruct((3, batch, d1, seq // d1, gw), BF16),
        jax.ShapeDtypeStruct((3, batch, d2, seq // d2, gw), BF16),
    )
    return pl.pallas_call(
        _inproj_qkv_kernel,
        out_shape=out_shape,
        grid=(n // ROW_TILE, 3),
        in_specs=[
            pl.BlockSpec((ROW_TILE, D_MODEL), lambda i, t: (i, 0)),
            pl.BlockSpec((1, D_MODEL), lambda i, t: (0, 0)),
            pl.BlockSpec((1, D_MODEL), lambda i, t: (0, 0)),
            pl.BlockSpec((D_MODEL, ATT_WIDTH), lambda i, t: (0, t)),
        ],
        out_specs=(
            pl.BlockSpec((1, ROW_TILE, gw), lambda i, t: (t, i, 0)),
            pl.BlockSpec((1, 1, d1, ROW_TILE // d1, gw),
                         lambda i, t: (t, i // tiles_per_seq, 0, i % tiles_per_seq, 0)),
            pl.BlockSpec((1, 1, d2, ROW_TILE // d2, gw),
                         lambda i, t: (t, i // tiles_per_seq, 0, i % tiles_per_seq, 0)),
        ),
        scratch_shapes=[
            pltpu.VMEM((D_MODEL // LANES, ROW_TILE, LANES), F32),
            pltpu.VMEM((ROW_TILE, D_MODEL), BF16),
            pltpu.VMEM((ROW_TILE, D_MODEL), BF16),
            pltpu.VMEM((ROW_TILE, D_MODEL), BF16),
        ],
        compiler_params=_params(("parallel", "arbitrary")),
        name="inproj_qkv",
    )(x2, g, b, w_qkv)


ATT_HEADS_PER_MATMUL = 4


def _attn_kernel(q_ref, kc_ref, kp_ref, v_ref, o_ref, lse_ref, vt_scr):
    blk = ATT_BLOCK
    hd = ATT_HEAD_DIM
    gw = ATT_GROUP_WIDTH
    hpm = ATT_HEADS_PER_MATMUL
    width = hpm * hd
    lb = pl.program_id(2)

    vt_cur = jnp.concatenate(
        [v_ref[:, c * LANES:(c + 1) * LANES].astype(F32).T.astype(BF16)
         for c in range(gw // LANES)], axis=0)
    vt_prev = jnp.where(lb > 0, vt_scr[...], jnp.zeros_like(vt_cur))
    vt_scr[...] = vt_cur

    key_i = lax.broadcasted_iota(I32, (2 * blk, blk), 0)
    qry_i = lax.broadcasted_iota(I32, (2 * blk, blk), 1)
    live = (((key_i < blk) & (key_i >= qry_i) & (lb > 0))
            | ((key_i >= blk) & (key_i - blk <= qry_i)))
    bias = jnp.concatenate([jnp.where(live, 0.0, -jnp.inf)] * hpm, axis=1)
    lane_head = lax.broadcasted_iota(I32, (blk, width), 1) // hd
    per_tile = LANES // hd
    feats = [slice(g * width, (g + 1) * width) for g in range(ATT_HEADS // hpm)]
    scores = []
    for feat in feats:
        q_g = q_ref[:, feat] * (hd ** -0.5)
        q_bd = jnp.concatenate([jnp.where(lane_head == i, q_g, jnp.zeros_like(q_g))
                                for i in range(hpm)], axis=0)
        k_g = jnp.concatenate([kp_ref[:, feat], kc_ref[:, feat]], axis=0)
        scores.append(_dot_nt(k_g, q_bd))
    for g, feat in enumerate(feats):
        s_t = scores[g] + bias
        m = jnp.max(s_t, axis=0, keepdims=True)
        p = jnp.exp(s_t - m)
        l = jnp.sum(p, axis=0, keepdims=True)
        v_t = jnp.concatenate([vt_prev[feat, :], vt_cur[feat, :]], axis=1)
        o_t = _dot(v_t, p.astype(BF16))
        lse = m + jnp.log(l)
        inv_l = 1.0 / l
        for c in range(width // LANES):
            tile = []
            for i in range(c * per_tile, (c + 1) * per_tile):
                cols = slice(i * blk, (i + 1) * blk)
                tile.append(o_t[i * hd:(i + 1) * hd, cols] * inv_l[:, cols])
                lse_ref[g * hpm + i:g * hpm + i + 1, :] = lse[:, cols]
            lanes = slice(g * width + c * LANES, g * width + (c + 1) * LANES)
            o_ref[:, lanes] = jnp.concatenate(tile, axis=0).T.astype(BF16)


def _attention_group(qkv):
    _, batch, dil, sub_len, gw = qkv.shape
    blk = ATT_BLOCK

    def spec(t, prev):
        def index(b, r, i):
            return (t, b, r, jnp.maximum(i - 1, 0) if prev else i, 0)
        return pl.BlockSpec((None, None, None, blk, gw), index)

    return pl.pallas_call(
        _attn_kernel,
        out_shape=(jax.ShapeDtypeStruct((batch, dil, sub_len, gw), BF16),
                   jax.ShapeDtypeStruct((batch, dil, ATT_HEADS, sub_len), F32)),
        grid=(batch, dil, sub_len // blk),
        in_specs=[spec(0, False), spec(1, False), spec(1, True), spec(2, False)],
        out_specs=(pl.BlockSpec((None, None, blk, gw), lambda b, r, i: (b, r, i, 0)),
                   pl.BlockSpec((None, None, ATT_HEADS, blk), lambda b, r, i: (b, r, 0, i))),
        scratch_shapes=[pltpu.VMEM((gw, blk), BF16)],
        compiler_params=_params(("parallel", "parallel", "arbitrary")),
        name=f"dilated_attention_d{dil}",
    )(qkv, qkv, qkv, qkv)


HGRN_ROWS = 256
HGRN_HEADS_PER_STEP = 4


def _hgrn_rows(q, z, v, gate, st, lb, nw, tri, same_chunk_causal, chunk_mask):
    c = HGRN_CHUNK
    dk = HGRN_HEAD_DIM
    rows = HGRN_ROWS
    nchunk = rows // c
    one_m_lb = 1.0 - lb
    z = z.astype(F32)
    log_f = jnp.log(lb + one_m_lb * _sigmoid(z))
    key = one_m_lb * _sigmoid(-z)
    p0 = log_f.astype(BF16)
    rem = log_f - p0.astype(F32)
    p1 = rem.astype(BF16)
    p2 = (rem - p1.astype(F32)).astype(BF16)
    bcum = _dot(tri, p0) + _dot(tri, p1) + _dot(tri, p2)

    b3 = bcum.reshape(nchunk, c, dk)
    b_mid = b3[:, c // 2:c // 2 + 1, :]
    b_last = b3[:, c - 1:c, :]
    q3 = q.astype(F32).reshape(nchunk, c, dk) * (dk ** -0.5)
    q_t3 = q3 * jnp.exp(b3 - b_mid)
    k_t3 = key.reshape(nchunk, c, dk) * jnp.exp(b_mid - b3)
    q_t = q_t3.reshape(rows, dk).astype(BF16)
    k_t = k_t3.reshape(rows, dk).astype(BF16)
    q_in = (q_t3 * jnp.exp(b_mid)).reshape(rows, dk).astype(BF16)
    k_st = (k_t3 * jnp.exp(b_last - b_mid)).reshape(rows, dk).astype(BF16)

    att = jnp.where(same_chunk_causal, _dot_nt(q_t, k_t), 0.0)
    o = _dot(att.astype(BF16), v)

    k_spread = jnp.concatenate([k_st * chunk_mask[j] for j in range(nchunk)], axis=1)
    incr = _dot_tn(v, k_spread)
    decay = jnp.exp(b_last.reshape(nchunk, dk))
    before = []
    for j in range(nchunk):
        before.append(st.astype(BF16))
        st = st * decay[j:j + 1] + incr[:, j * dk:(j + 1) * dk]
    q_spread = jnp.concatenate([q_in * chunk_mask[j] for j in range(nchunk)], axis=1)
    o = o + _dot_nt(q_spread, jnp.concatenate(before, axis=1))

    o = o * lax.rsqrt(jnp.mean(o * o, -1, keepdims=True) + RMS_EPS) * nw
    gate = gate.astype(F32)
    return o * gate * _sigmoid(gate), st


def _hgrn_kernel(q_ref, f_ref, i_ref, g_ref, lb_ref, nw_ref, o_ref, state_scr, mask_scr):
    seq = q_ref.shape[0]
    c = HGRN_CHUNK
    rows = HGRN_ROWS
    dk = HGRN_HEAD_DIM
    nw = nw_ref[...]

    r_i = lax.broadcasted_iota(I32, (rows, rows), 0)
    c_i = lax.broadcasted_iota(I32, (rows, rows), 1)
    same_chunk_causal = (r_i // c == c_i // c) & (c_i <= r_i)
    tri = jnp.where(same_chunk_causal, 1.0, 0.0).astype(BF16)
    row_chunk = lax.broadcasted_iota(I32, (rows, dk), 0) // c
    for j in range(rows // c):
        mask_scr[j] = jnp.where(row_chunk == j, 1.0, 0.0).astype(BF16)

    state_scr[...] = jnp.zeros_like(state_scr)

    def body(gi, carry):
        rs = pl.ds(pl.multiple_of(gi * rows, rows), rows)
        for h in range(HGRN_HEADS_PER_STEP):
            cs = slice(h * dk, (h + 1) * dk)
            o, st = _hgrn_rows(q_ref[rs, cs], f_ref[rs, cs], i_ref[rs, cs], g_ref[rs, cs],
                               state_scr[h], lb_ref[:, cs], nw, tri, same_chunk_causal, mask_scr)
            state_scr[h] = st
            o_ref[rs, cs] = o.astype(BF16)
        return carry

    lax.fori_loop(0, seq // rows, body, 0)


def _hgrn(proj, lower_bound, norm_w, batch, seq):
    n = proj.shape[0]
    dk = HGRN_HEAD_DIM
    width = HGRN_HEADS_PER_STEP * dk
    steps = HGRN_HEADS // HGRN_HEADS_PER_STEP

    def seg(k):
        return pl.BlockSpec((seq, width), lambda b, h: (b, k * steps + h))

    return pl.pallas_call(
        _hgrn_kernel,
        out_shape=jax.ShapeDtypeStruct((n, HGRN_WIDTH), BF16),
        grid=(batch, steps),
        in_specs=[seg(0), seg(1), seg(2), seg(3),
                  pl.BlockSpec((1, width), lambda b, h: (0, h)),
                  pl.BlockSpec((1, dk), lambda b, h: (0, 0))],
        out_specs=pl.BlockSpec((seq, width), lambda b, h: (b, h)),
        scratch_shapes=[pltpu.VMEM((HGRN_HEADS_PER_STEP, dk, dk), F32),
                        pltpu.VMEM((HGRN_ROWS // HGRN_CHUNK, HGRN_ROWS, dk), BF16)],
        compiler_params=_params(("parallel", "parallel")),
        name="hgrn2",
    )(proj, proj, proj, proj, lower_bound, norm_w)


def _mix_kernel(x_ref, gin_ref, bin_ref, o0_ref, o1_ref, o2_ref, l0_ref, l1_ref, l2_ref,
                ob_ref, ga_ref, gb_ref, wa_ref, wb_ref, wo_ref, g1_ref, b1_ref,
                rwh_ref, rwl_ref,
                h_ref, hp0_ref, hp1_ref, lg_ref,
                so1_scr, so2_scr):
    tm = x_ref.shape[0]
    gw = ATT_GROUP_WIDTH
    for o_ref, so_scr, (_, dil) in ((o1_ref, so1_scr, ATT_GROUPS[1]),
                                    (o2_ref, so2_scr, ATT_GROUPS[2])):
        for r in range(dil):
            o_r = o_ref[r].astype(F32)
            for c in range(gw // LANES):
                so_scr[c, pl.ds(r, tm // dil, stride=dil), :] = o_r[:, c * LANES:(c + 1) * LANES]

    def natural(scr):
        return jnp.concatenate([scr[c] for c in range(gw // LANES)], axis=-1)

    l0, l1, l2 = l0_ref[...], l1_ref[...], l2_ref[...]
    m = jnp.maximum(jnp.maximum(l0, l1), l2)
    e0 = jnp.exp(l0 - m)
    e1 = jnp.exp(l1 - m)
    e2 = jnp.exp(l2 - m)
    inv = 1.0 / (e0 + e1 + e2)
    head_lanes = jnp.where(
        lax.broadcasted_iota(I32, (ATT_HEADS, gw), 1) // ATT_HEAD_DIM
        == lax.broadcasted_iota(I32, (ATT_HEADS, gw), 0), 1.0, 0.0).astype(BF16)

    def widen(w):
        hi = w.astype(BF16)
        lo = (w - hi.astype(F32)).astype(BF16)
        return _dot_tn(hi, head_lanes) + _dot_tn(lo, head_lanes)

    o_att = (widen(e0 * inv) * o0_ref[...].astype(F32) + widen(e1 * inv) * natural(so1_scr)
             + widen(e2 * inv) * natural(so2_scr))

    y_a = _dot(o_att.astype(BF16), wa_ref[...])
    y_b = _dot(ob_ref[...], wb_ref[...])
    merged = _sigmoid(ga_ref[...].astype(F32)) * y_a + _sigmoid(gb_ref[...].astype(F32)) * y_b
    mix = _dot(merged.astype(BF16), wo_ref[...])
    h_in = _layer_norm(x_ref[...], gin_ref[...], bin_ref[...])
    h1 = _layer_norm(DEEPNORM_ALPHA * h_in + mix, g1_ref[...], b1_ref[...])
    h_ref[...] = h1

    hp0_ref[...], hp1_ref[...] = _pack_rows(h1)

    h_hi = h1.astype(BF16)
    h_lo = (h1 - h_hi.astype(F32)).astype(BF16)
    rwh = rwh_ref[...]
    lg_ref[...] = _dot_nt(rwh, h_hi) + _dot_nt(rwh, h_lo) + _dot_nt(rwl_ref[...], h_hi)


def _mix(x2, gin, bin_, att, ob, proj, wa, wb, wo, g1, b1, rwh, rwl, batch, seq):
    n = x2.shape[0]
    tm = MIX_ROW_TILE
    gw = ATT_GROUP_WIDTH
    tiles_per_seq = seq // tm
    (o0, l0), (o1, l1), (o2, l2) = att
    d1, d2 = ATT_GROUPS[1][1], ATT_GROUPS[2][1]
    lse_spec = pl.BlockSpec((None, ATT_HEADS, tm),
                            lambda i: (i // tiles_per_seq, 0, i % tiles_per_seq))

    def full(shape):
        return pl.BlockSpec(shape, lambda i: (0,) * len(shape))

    def dil_spec(dil):
        return pl.BlockSpec((None, dil, tm // dil, gw),
                            lambda i: (i // tiles_per_seq, 0, i % tiles_per_seq, 0))

    nat_spec = pl.BlockSpec((tm, gw), lambda i: (i, 0))
    return pl.pallas_call(
        _mix_kernel,
        out_shape=(jax.ShapeDtypeStruct((n, D_MODEL), F32),
                   jax.ShapeDtypeStruct((n, PACK_WIDTH), I32),
                   jax.ShapeDtypeStruct((n, PACK_WIDTH), I32),
                   jax.ShapeDtypeStruct((N_EXPERTS, n), F32)),
        grid=(n // tm,),
        in_specs=[
            pl.BlockSpec((tm, D_MODEL), lambda i: (i, 0)),
            full((1, D_MODEL)), full((1, D_MODEL)),
            nat_spec, dil_spec(d1), dil_spec(d2),
            lse_spec, lse_spec, lse_spec,
            pl.BlockSpec((tm, HGRN_WIDTH), lambda i: (i, 0)),
            pl.BlockSpec((tm, D_MODEL), lambda i: (i, 4)),
            pl.BlockSpec((tm, D_MODEL), lambda i: (i, 5)),
            full((gw, D_MODEL)), full((HGRN_WIDTH, D_MODEL)), full((D_MODEL, D_MODEL)),
            full((1, D_MODEL)), full((1, D_MODEL)),
            full((N_EXPERTS, D_MODEL)), full((N_EXPERTS, D_MODEL)),
        ],
        out_specs=(pl.BlockSpec((tm, D_MODEL), lambda i: (i, 0)),
                   pl.BlockSpec((tm, PACK_WIDTH), lambda i: (i, 0)),
                   pl.BlockSpec((tm, PACK_WIDTH), lambda i: (i, 0)),
                   pl.BlockSpec((N_EXPERTS, tm), lambda i: (0, i))),
        scratch_shapes=[pltpu.VMEM((gw // LANES, tm, LANES), F32)] * 2,
        compiler_params=_params(("parallel",)),
        name="branch_mix",
    )(x2, gin, bin_, o0, o1, o2, l0, l1, l2, ob, proj, proj, wa, wb, wo, g1, b1, rwh, rwl)


ROW_SLOTS = 16
ROW_GROUP = 4
ROW_AHEAD = ROW_SLOTS - ROW_GROUP
W_SLOTS = 3


def _expert_kernel(sblk_ref, nblk_ref, cnt_ref, xs0_hbm, xs1_hbm, wi_hbm, wo_hbm,
                   ys0_hbm, ys1_hbm, xbuf0, xbuf1, ybuf0, ybuf1, wibuf, wobuf,
                   in_sem, out_sem, w_sem, wi_scr, wo_scr):
    e = pl.program_id(0)
    n_exp = pl.num_programs(0)
    first_blk = sblk_ref[e]
    n_blk = nblk_ref[e]
    n_rows = cnt_ref[e]
    total = sblk_ref[n_exp - 1] + nblk_ref[n_exp - 1]
    xs_hbm, ys_hbm = (xs0_hbm, xs1_hbm), (ys0_hbm, ys1_hbm)
    xbuf, ybuf = (xbuf0, xbuf1), (ybuf0, ybuf1)

    def rows_of(b):
        return pl.ds(pl.multiple_of(b * MOE_BLOCK, MOE_BLOCK), MOE_BLOCK)

    def x_copy(b, part):
        slot = b % ROW_SLOTS
        return pltpu.make_async_copy(xs_hbm[part].at[rows_of(b)], xbuf[part].at[slot],
                                     in_sem.at[part, slot])

    def y_copy(b, part):
        slot = b % ROW_SLOTS
        return pltpu.make_async_copy(ybuf[part].at[slot], ys_hbm[part].at[rows_of(b)],
                                     out_sem.at[part, slot])

    def w_copies(ex):
        slot = ex % W_SLOTS
        return (pltpu.make_async_copy(wi_hbm.at[ex], wibuf.at[slot], w_sem.at[0, slot]),
                pltpu.make_async_copy(wo_hbm.at[ex], wobuf.at[slot], w_sem.at[1, slot]))

    def start_weights(ex):
        @pl.when((ex < n_exp) & (nblk_ref[jnp.minimum(ex, n_exp - 1)] > 0))
        def _():
            for cp in w_copies(ex):
                cp.start()

    @pl.when(e == 0)
    def _():
        for b in range(ROW_AHEAD):
            @pl.when(b < total)
            def _():
                for part in range(PACK_PARTS):
                    x_copy(b, part).start()
        for ex in range(W_SLOTS - 1):
            start_weights(ex)

    start_weights(e + (W_SLOTS - 1))

    def step(j, group):
        b0 = first_blk + j
        for i in range(group):
            for part in range(PACK_PARTS):
                x_copy(b0 + i, part).wait()
        for i in range(group):
            @pl.when(b0 + ROW_AHEAD + i < total)
            def _():
                for part in range(PACK_PARTS):
                    x_copy(b0 + ROW_AHEAD + i, part).start()
        for i in range(group):
            @pl.when(b0 + i >= ROW_SLOTS)
            def _():
                for part in range(PACK_PARTS):
                    y_copy(b0 + i - ROW_SLOTS, part).wait()

        rows = group * MOE_BLOCK
        live = (lax.broadcasted_iota(I32, (rows, PACK_WIDTH), 0)
                < n_rows - j * MOE_BLOCK)
        parts = [jnp.concatenate([xbuf[part][(b0 + i) % ROW_SLOTS] for i in range(group)], axis=0)
                 for part in range(PACK_PARTS)]
        x = _unpack_rows([jnp.where(live, p, 0) for p in parts])
        hu = _dot(x.astype(BF16), wi_scr[...])
        hg = hu[:, :EXPERT_DIM]
        hv = hu[:, EXPERT_DIM:]
        act = (hg * _sigmoid(hg) * hv).astype(BF16)
        packed = _pack_rows(_dot(act, wo_scr[...]))
        for i in range(group):
            for part in range(PACK_PARTS):
                ybuf[part][(b0 + i) % ROW_SLOTS] = packed[part][i * MOE_BLOCK:(i + 1) * MOE_BLOCK]
                y_copy(b0 + i, part).start()

    @pl.when(n_blk > 0)
    def _():
        for cp in w_copies(e):
            cp.wait()
        slot = e % W_SLOTS
        wi_scr[...] = wibuf[slot].astype(BF16)
        wo_scr[...] = wobuf[slot].astype(BF16)

        def full_group(g, carry):
            step(g * ROW_GROUP, ROW_GROUP)
            return carry

        lax.fori_loop(0, n_blk // ROW_GROUP, full_group, 0)
        done = n_blk // ROW_GROUP * ROW_GROUP
        group = ROW_GROUP // 2
        while group >= 1:
            @pl.when((n_blk & group) != 0)
            def _(group=group, done=done):
                step(done, group)
            done = done + (n_blk & group)
            group //= 2

    @pl.when(e == n_exp - 1)
    def _():
        for back in range(1, ROW_SLOTS + 1):
            @pl.when(total >= back)
            def _():
                for part in range(PACK_PARTS):
                    y_copy(total - back, part).wait()


def _experts(start_blk, n_blk, n_rows, xs, w_in_e, w_out_e):
    p = xs[0].shape[0]
    n_exp = w_in_e.shape[0]
    any_spec = pl.BlockSpec(memory_space=pl.ANY)
    row_buf = pltpu.VMEM((ROW_SLOTS, MOE_BLOCK, PACK_WIDTH), I32)
    grid_spec = pltpu.PrefetchScalarGridSpec(
        num_scalar_prefetch=3,
        grid=(n_exp,),
        in_specs=[any_spec, any_spec, any_spec, any_spec],
        out_specs=(any_spec, any_spec),
        scratch_shapes=[row_buf, row_buf, row_buf, row_buf,
                        pltpu.VMEM((W_SLOTS, D_MODEL, 2 * EXPERT_DIM), F32),
                        pltpu.VMEM((W_SLOTS, EXPERT_DIM, D_MODEL), F32),
                        pltpu.SemaphoreType.DMA((PACK_PARTS, ROW_SLOTS)),
                        pltpu.SemaphoreType.DMA((PACK_PARTS, ROW_SLOTS)),
                        pltpu.SemaphoreType.DMA((2, W_SLOTS)),
                        pltpu.VMEM((D_MODEL, 2 * EXPERT_DIM), BF16),
                        pltpu.VMEM((EXPERT_DIM, D_MODEL), BF16)],
    )
    return pl.pallas_call(
        _expert_kernel,
        out_shape=(jax.ShapeDtypeStruct((p, PACK_WIDTH), I32),) * PACK_PARTS,
        grid_spec=grid_spec,
        compiler_params=_params(("arbitrary",)),
        name="routed_experts",
    )(start_blk, n_blk, n_rows, xs[0], xs[1], w_in_e, w_out_e)


def _combine_kernel(h_ref, yg0_ref, yg1_ref, gate_ref, wsi_ref, wso_ref, g2_ref, b2_ref, o_ref):
    h1 = h_ref[...]
    gate = gate_ref[...]
    routed = None
    for k in range(TOP_K):
        y_k = _unpack_rows([yg0_ref[k], yg1_ref[k]]) * gate[:, k:k + 1]
        routed = y_k if routed is None else routed + y_k
    hs = _dot(h1.astype(BF16), wsi_ref[...])
    sg = hs[:, :SHARED_DIM]
    sv = hs[:, SHARED_DIM:]
    shared = _dot((sg * _sigmoid(sg) * sv).astype(BF16), wso_ref[...])
    o_ref[...] = _layer_norm(DEEPNORM_ALPHA * h1 + routed + shared, g2_ref[...], b2_ref[...])


def _combine(h1, yg, gate_nk, wsi, wso, g2, b2):
    n = h1.shape[0]
    tm = OUT_ROW_TILE
    yg_spec = pl.BlockSpec((TOP_K, tm, PACK_WIDTH), lambda i: (0, i, 0))

    def full(shape):
        return pl.BlockSpec(shape, lambda i: (0,) * len(shape))

    return pl.pallas_call(
        _combine_kernel,
        out_shape=jax.ShapeDtypeStruct((n, D_MODEL), F32),
        grid=(n // tm,),
        in_specs=[
            pl.BlockSpec((tm, D_MODEL), lambda i: (i, 0)),
            yg_spec, yg_spec,
            pl.BlockSpec((tm, TOP_K), lambda i: (i, 0)),
            full((D_MODEL, 2 * SHARED_DIM)), full((SHARED_DIM, D_MODEL)),
            full((1, D_MODEL)), full((1, D_MODEL)),
        ],
        out_specs=pl.BlockSpec((tm, D_MODEL), lambda i: (i, 0)),
        compiler_params=_params(("parallel",)),
        name="shared_combine",
    )(h1, yg[0], yg[1], gate_nk, wsi, wso, g2, b2)


ROUTE_TILE = 512


def _pick_first_max(vals, iota, axis, size):
    m = jnp.max(vals, axis=axis, keepdims=True)
    idx = jnp.min(jnp.where(vals == m, iota, size), axis=axis, keepdims=True)
    return m, idx


def _route_kernel(lg_ref, bias_ref, idx_ref, gate_ref, rank_ref, cnt_ref, carry_scr):
    e, tn = lg_ref.shape
    groups = N_EXPERT_GROUPS
    gsz = e // groups
    neg = -jnp.inf

    @pl.when(pl.program_id(0) == 0)
    def _():
        carry_scr[...] = jnp.zeros_like(carry_scr)

    scores = _sigmoid(lg_ref[...])
    biased = scores + bias_ref[...]

    b3 = biased.reshape(groups, gsz, tn)
    io3 = lax.broadcasted_iota(I32, b3.shape, 1)
    m1, i1 = _pick_first_max(b3, io3, 1, gsz)
    m2 = jnp.max(jnp.where(io3 == i1, neg, b3), axis=1, keepdims=True)
    grp = m1 + m2
    iog = lax.broadcasted_iota(I32, grp.shape, 0)
    keep = jnp.zeros(grp.shape, F32)
    for _ in range(TOPK_GROUPS):
        _, gi = _pick_first_max(grp, iog, 0, groups)
        hit = iog == gi
        keep = jnp.where(hit, 1.0, keep)
        grp = jnp.where(hit, neg, grp)
    masked = jnp.where(keep > 0.0, b3, neg).reshape(e, tn)

    ioe = lax.broadcasted_iota(I32, (e, tn), 0)
    onehot = jnp.zeros((e, tn), F32)
    idxs, gates = [], []
    for _ in range(TOP_K):
        _, ei = _pick_first_max(masked, ioe, 0, e)
        hit = ioe == ei
        gates.append(jnp.sum(jnp.where(hit, scores, 0.0), axis=0, keepdims=True))
        onehot = jnp.where(hit, 1.0, onehot)
        masked = jnp.where(hit, neg, masked)
        idxs.append(ei)
    gsum = gates[0]
    for g in gates[1:]:
        gsum = gsum + g
    idx_ref[...] = jnp.concatenate(idxs, axis=0)
    gate_ref[...] = jnp.concatenate(gates, axis=0) / gsum * ROUTED_SCALE

    t_r = lax.broadcasted_iota(I32, (tn, tn), 0)
    t_c = lax.broadcasted_iota(I32, (tn, tn), 1)
    earlier = jnp.where(t_r < t_c, 1.0, 0.0).astype(BF16)
    oh = onehot.astype(BF16)
    base = carry_scr[...]
    before = _dot(oh, earlier) + jnp.concatenate([base] * (tn // LANES), axis=1)
    ranks = [jnp.sum(jnp.where(ioe == ei, before, 0.0), axis=0, keepdims=True) for ei in idxs]
    rank_ref[...] = jnp.concatenate(ranks, axis=0).astype(I32)
    total = base + _dot(oh, jnp.ones((tn, LANES), BF16))
    carry_scr[...] = total
    cnt_ref[...] = total


def _route(logits_t, bias_col):
    e, n = logits_t.shape
    tn = ROUTE_TILE
    tok_spec = pl.BlockSpec((TOP_K, tn), lambda i: (0, i))
    return pl.pallas_call(
        _route_kernel,
        out_shape=(jax.ShapeDtypeStruct((TOP_K, n), I32),
                   jax.ShapeDtypeStruct((TOP_K, n), F32),
                   jax.ShapeDtypeStruct((TOP_K, n), I32),
                   jax.ShapeDtypeStruct((e, LANES), F32)),
        grid=(n // tn,),
        in_specs=[pl.BlockSpec((e, tn), lambda i: (0, i)),
                  pl.BlockSpec((e, 1), lambda i: (0, 0))],
        out_specs=(tok_spec, tok_spec, tok_spec, pl.BlockSpec((e, LANES), lambda i: (0, 0))),
        scratch_shapes=[pltpu.VMEM((e, LANES), F32)],
        compiler_params=_params(("arbitrary",)),
        name="router_topk",
    )(logits_t, bias_col)


def _plan_kernel(idx_ref, rank_ref, cnt_ref, dest_ref, sblk_ref):
    e = cnt_ref.shape[0]
    tn = idx_ref.shape[1]
    cnt = cnt_ref[...]
    nblk = jnp.floor((cnt + (MOE_BLOCK - 1)) * (1.0 / MOE_BLOCK))
    e_r = lax.broadcasted_iota(I32, (e, e), 0)
    e_c = lax.broadcasted_iota(I32, (e, e), 1)
    lower = jnp.where(e_c < e_r, 1.0, 0.0).astype(BF16)
    start_blk = _dot(lower, nblk.astype(BF16))

    ioe = lax.broadcasted_iota(I32, (e, tn), 0)
    start_row = jnp.concatenate([start_blk * MOE_BLOCK] * (tn // LANES), axis=1)
    idx = idx_ref[...]
    dests = [jnp.sum(jnp.where(ioe == idx[k:k + 1], start_row, 0.0), axis=0, keepdims=True)
             for k in range(TOP_K)]
    dest_ref[...] = jnp.concatenate(dests, axis=0).astype(I32) + rank_ref[...]
    sblk_ref[...] = start_blk.astype(I32)


def _plan(idx_t, rank_t, counts):
    k, n = idx_t.shape
    e = counts.shape[0]
    tn = ROUTE_TILE
    tok_spec = pl.BlockSpec((k, tn), lambda i: (0, i))
    exp_spec = pl.BlockSpec((e, LANES), lambda i: (0, 0))
    return pl.pallas_call(
        _plan_kernel,
        out_shape=(jax.ShapeDtypeStruct((k, n), I32), jax.ShapeDtypeStruct((e, LANES), I32)),
        grid=(n // tn,),
        in_specs=[tok_spec, tok_spec, exp_spec],
        out_specs=(tok_spec, exp_spec),
        compiler_params=_params(("arbitrary",)),
        name="dispatch_plan",
    )(idx_t, rank_t, counts)


SC_WINDOW = 128


def _sc_mesh():
    return plsc.VectorSubcoreMesh(core_axis_name="core", subcore_axis_name="subcore")


def _dispatch_rows(parts, dest, p_total):
    n, width = parts[0].shape
    top_k = dest.shape[0]
    out_type = (jax.ShapeDtypeStruct((p_total, width), parts[0].dtype),) * len(parts)

    @functools.partial(pl.kernel, mesh=_sc_mesh(), scratch_types=[], out_type=out_type,
                       name="dispatch_rows")
    def scatter(*refs):
        x_hbms = refs[:len(parts)]
        i_hbm = refs[len(parts)]
        o_hbms = refs[len(parts) + 1:]
        for x_hbm, o_hbm in zip(x_hbms, o_hbms):
            def body(x_vmem, i_vmem, o_hbm=o_hbm):
                for k in range(top_k):
                    pltpu.sync_copy(x_vmem, o_hbm.at[i_vmem.at[k]])

            pltpu.emit_pipeline(
                body,
                grid=(n // SC_WINDOW,),
                in_specs=[pl.BlockSpec((SC_WINDOW, width), lambda i: (i, 0)),
                          pl.BlockSpec((top_k, SC_WINDOW), lambda i: (0, i))],
                out_specs=[],
                core_axis_name=("core", "subcore"),
                dimension_semantics=(pltpu.PARALLEL,),
            )(x_hbm, i_hbm)

    return scatter(*parts, dest)


def _gather_rows(parts, idx_flat):
    count = idx_flat.shape[1]
    width = parts[0].shape[1]
    out_type = (jax.ShapeDtypeStruct((count, width), parts[0].dtype),) * len(parts)

    @functools.partial(pl.kernel, mesh=_sc_mesh(), scratch_types=[], out_type=out_type,
                       name="combine_rows")
    def gather(*refs):
        y_hbms = refs[:len(parts)]
        i_hbm = refs[len(parts)]
        o_hbms = refs[len(parts) + 1:]
        for y_hbm, o_hbm in zip(y_hbms, o_hbms):
            def body(i_vmem, o_vmem, y_hbm=y_hbm):
                pltpu.sync_copy(y_hbm.at[i_vmem.at[0]], o_vmem)

            pltpu.emit_pipeline(
                body,
                grid=(count // SC_WINDOW,),
                in_specs=[pl.BlockSpec((1, SC_WINDOW), lambda i: (0, i))],
                out_specs=[pl.BlockSpec((SC_WINDOW, width), lambda i: (i, 0))],
                core_axis_name=("core", "subcore"),
                dimension_semantics=(pltpu.PARALLEL,),
            )(i_hbm, o_hbm)

    return gather(*parts, idx_flat)


def kernel(x, ln_in_g, ln_in_b, w_in, hgrn_lb_logits, hgrn_norm_w, w_branch_att, w_branch_hgrn,
           w_out, ln1_g, ln1_b, router_w, router_bias, expert_w_in, expert_w_out, shared_w_in,
           shared_w_out, ln2_g, ln2_b):
    batch, seq, d = x.shape
    n = batch * seq
    x2 = x.reshape(n, d)
    row = lambda v: v.reshape(1, -1).astype(F32)

    lower_bounds = jnp.cumsum(jax.nn.softmax(hgrn_lb_logits.astype(F32), axis=0), axis=0)
    l = 0
    w_l = w_in[l]
    w_qkv = w_l[:, :3 * ATT_WIDTH].astype(BF16)
    w_main = w_l[:, 3 * ATT_WIDTH:].astype(BF16)
    gin, bin_ = row(ln_in_g), row(ln_in_b)

    proj = _inproj_main(x2, gin, bin_, w_main)
    qkv0, qkv1, qkv2 = _inproj_qkv(x2, gin, bin_, w_qkv, batch, seq)
    qkv0 = qkv0.reshape(3, batch, 1, seq, ATT_GROUP_WIDTH)
    att = []
    for qkv in (qkv0, qkv1, qkv2):
        o_g, lse_g = _attention_group(qkv)
        att.append((o_g, lse_g.transpose(0, 2, 3, 1).reshape(batch, ATT_HEADS, seq)))
    att[0] = (att[0][0].reshape(n, ATT_GROUP_WIDTH), att[0][1])

    ob = _hgrn(proj, row(lower_bounds[l]), row(hgrn_norm_w[l]), batch, seq)

    rw_t = router_w[l].T.astype(F32)
    rwh = rw_t.astype(BF16)
    rwl = (rw_t - rwh.astype(F32)).astype(BF16)
    h1, hp0, hp1, logits_t = _mix(
        x2, gin, bin_, att, ob, proj,
        w_branch_att[l].astype(BF16), w_branch_hgrn[l].astype(BF16), w_out[l].astype(BF16),
        row(ln1_g[l]), row(ln1_b[l]), rwh, rwl, batch, seq)

    idx_t, gate_t, rank_t, counts = _route(logits_t, router_bias[l].reshape(-1, 1).astype(F32))
    p_total = n * TOP_K + N_EXPERTS * MOE_BLOCK
    dest, start_blk = _plan(idx_t, rank_t, counts)
    n_rows = counts[:, 0].astype(I32)
    n_blk = (n_rows + (MOE_BLOCK - 1)) // MOE_BLOCK
    xs = _dispatch_rows((hp0, hp1), dest, p_total)
    ys = _experts(start_blk[:, 0], n_blk, n_rows, xs, expert_w_in[l], expert_w_out[l])
    yg = [g.reshape(TOP_K, n, PACK_WIDTH)
          for g in _gather_rows(ys, dest.reshape(1, TOP_K * n))]
    out = _combine(h1, yg, gate_t.T, shared_w_in[l].astype(BF16), shared_w_out[l].astype(BF16),
                   row(ln2_g[l]), row(ln2_b[l]))
    return out.reshape(batch, seq, d)
```

```python
import functools

import jax
import jax.numpy as jnp
from jax import lax
from jax.experimental import pallas as pl
from jax.experimental.pallas import tpu as pltpu
from jax.experimental.pallas import tpu_sc as plsc

F32 = jnp.float32
BF16 = jnp.bfloat16
U32 = jnp.uint32
I32 = jnp.int32

D_MODEL = 1024
ATT_GROUPS = ((128, 1), (512, 4), (2048, 16))
ATT_HEADS = 8
ATT_HEAD_DIM = 64
ATT_GROUP_WIDTH = ATT_HEADS * ATT_HEAD_DIM
ATT_WIDTH = len(ATT_GROUPS) * ATT_GROUP_WIDTH
ATT_BLOCK = 128
HGRN_HEAD_DIM = 128
HGRN_HEADS = D_MODEL // HGRN_HEAD_DIM
HGRN_WIDTH = HGRN_HEADS * HGRN_HEAD_DIM
HGRN_CHUNK = 32
N_EXPERTS = 256
TOP_K = 8
N_EXPERT_GROUPS = 8
TOPK_GROUPS = 4
EXPERT_DIM = 256
SHARED_DIM = 256
ROUTED_SCALE = 2.5
MOE_BLOCK = 128
LN_EPS = 1e-5
RMS_EPS = 1e-6
DEPTH = 1
DEEPNORM_ALPHA = (2 * DEPTH) ** 0.25

VMEM_LIMIT = 56 * 1024 * 1024
LANES = 128

ROW_TILE = 1024
MAIN_COL_TILE = 1536
MIX_ROW_TILE = 512
OUT_ROW_TILE = 512


def _params(sem, vmem=VMEM_LIMIT):
    return pltpu.CompilerParams(dimension_semantics=sem, vmem_limit_bytes=vmem)


def _layer_norm(x, g, b):
    mu = jnp.mean(x, -1, keepdims=True)
    xc = x - mu
    var = jnp.mean(xc * xc, -1, keepdims=True)
    return xc * lax.rsqrt(var + LN_EPS) * g + b


def _sigmoid(x):
    return 1.0 / (1.0 + jnp.exp(-x))


def _dot(a, b):
    return jnp.dot(a, b, preferred_element_type=F32)


def _dot_nt(a, b):
    return lax.dot_general(a, b, (((1,), (1,)), ((), ())), preferred_element_type=F32)


def _dot_tn(a, b):
    return lax.dot_general(a, b, (((0,), (0,)), ((), ())), preferred_element_type=F32)


def _pack_bf16_pair(lo, hi):
    lo_bits = pltpu.bitcast(lo.astype(BF16).astype(F32), U32) >> 16
    hi_bits = pltpu.bitcast(hi.astype(BF16).astype(F32), U32) & jnp.uint32(0xFFFF0000)
    return hi_bits | lo_bits


def _unpack_bf16_pair(w):
    lo = pltpu.bitcast(w << 16, F32)
    hi = pltpu.bitcast(w & jnp.uint32(0xFFFF0000), F32)
    return lo, hi


PACK_PARTS = 2
PACK_WIDTH = D_MODEL // 2 // PACK_PARTS


def _pack_rows(v):
    half = D_MODEL // 2
    parts = []
    for j in range(PACK_PARTS):
        lo = v[:, j * PACK_WIDTH:(j + 1) * PACK_WIDTH]
        hi = v[:, half + j * PACK_WIDTH:half + (j + 1) * PACK_WIDTH]
        parts.append(pltpu.bitcast(_pack_bf16_pair(lo, hi), I32))
    return parts


def _unpack_rows(parts):
    pairs = [_unpack_bf16_pair(pltpu.bitcast(p, U32)) for p in parts]
    return jnp.concatenate([lo for lo, _ in pairs] + [hi for _, hi in pairs], axis=-1)


def _inproj_main_kernel(x_ref, g_ref, b_ref, w_ref, o_ref, h_scr):
    @pl.when(pl.program_id(1) == 0)
    def _():
        h_scr[...] = _layer_norm(x_ref[...], g_ref[...], b_ref[...]).astype(BF16)

    o_ref[...] = _dot(h_scr[...], w_ref[...]).astype(BF16)


def _inproj_main(x2, g, b, w_main):
    n = x2.shape[0]
    width = w_main.shape[1]
    return pl.pallas_call(
        _inproj_main_kernel,
        out_shape=jax.ShapeDtypeStruct((n, width), BF16),
        grid=(n // ROW_TILE, width // MAIN_COL_TILE),
        in_specs=[
            pl.BlockSpec((ROW_TILE, D_MODEL), lambda i, j: (i, 0)),
            pl.BlockSpec((1, D_MODEL), lambda i, j: (0, 0)),
            pl.BlockSpec((1, D_MODEL), lambda i, j: (0, 0)),
            pl.BlockSpec((D_MODEL, MAIN_COL_TILE), lambda i, j: (0, j)),
        ],
        out_specs=pl.BlockSpec((ROW_TILE, MAIN_COL_TILE), lambda i, j: (i, j)),
        scratch_shapes=[pltpu.VMEM((ROW_TILE, D_MODEL), BF16)],
        compiler_params=_params(("parallel", "arbitrary")),
        name="inproj_main",
    )(x2, g, b, w_main)


def _inproj_qkv_kernel(x_ref, g_ref, b_ref, w_ref, o0_ref, o1_ref, o2_ref,
                       hf_scr, h0_scr, h1_scr, h2_scr):
    @pl.when(pl.program_id(1) == 0)
    def _():
        hf = _layer_norm(x_ref[...], g_ref[...], b_ref[...])
        h0_scr[...] = hf.astype(BF16)
        for c in range(D_MODEL // LANES):
            hf_scr[c] = hf[:, c * LANES:(c + 1) * LANES]
        for h_scr, (_, dil) in ((h1_scr, ATT_GROUPS[1]), (h2_scr, ATT_GROUPS[2])):
            rows = ROW_TILE // dil
            for r in range(dil):
                for c in range(D_MODEL // LANES):
                    h_scr[r * rows:(r + 1) * rows, c * LANES:(c + 1) * LANES] = (
                        hf_scr[c, pl.ds(r, rows, stride=dil), :].astype(BF16))

    gw = ATT_GROUP_WIDTH
    o0_ref[0] = _dot(h0_scr[...], w_ref[:, 0:gw]).astype(BF16)
    d1 = ATT_GROUPS[1][1]
    o1_ref[0, 0] = _dot(h1_scr[...], w_ref[:, gw:2 * gw]).astype(BF16).reshape(d1, ROW_TILE // d1, gw)
    d2 = ATT_GROUPS[2][1]
    o2_ref[0, 0] = _dot(h2_scr[...], w_ref[:, 2 * gw:3 * gw]).astype(BF16).reshape(d2, ROW_TILE // d2, gw)


def _inproj_qkv(x2, g, b, w_qkv, batch, seq):
    n = x2.shape[0]
    gw = ATT_GROUP_WIDTH
    tiles_per_seq = seq // ROW_TILE
    d1, d2 = ATT_GROUPS[1][1], ATT_GROUPS[2][1]
    out_shape = (
        jax.ShapeDtypeStruct((3, n, gw), BF16),
        jax.ShapeDtypeStruct((3, batch, d1, seq // d1, gw), BF16),
        jax.ShapeDtypeStruct((3, batch, d2, seq // d2, gw), BF16),
    )
    return pl.pallas_call(
        _inproj_qkv_kernel,
        out_shape=out_shape,
        grid=(n // ROW_TILE, 3),
        in_specs=[
            pl.BlockSpec((ROW_TILE, D_MODEL), lambda i, t: (i, 0)),
            pl.BlockSpec((1, D_MODEL), lambda i, t: (0, 0)),
            pl.BlockSpec((1, D_MODEL), lambda i, t: (0, 0)),
            pl.BlockSpec((D_MODEL, ATT_WIDTH), lambda i, t: (0, t)),
        ],
        out_specs=(
            pl.BlockSpec((1, ROW_TILE, gw), lambda i, t: (t, i, 0)),
            pl.BlockSpec((1, 1, d1, ROW_TILE // d1, gw),
                         lambda i, t: (t, i // tiles_per_seq, 0, i % tiles_per_seq, 0)),
            pl.BlockSpec((1, 1, d2, ROW_TILE // d2, gw),
                         lambda i, t: (t, i // tiles_per_seq, 0, i % tiles_per_seq, 0)),
        ),
        scratch_shapes=[
            pltpu.VMEM((D_MODEL // LANES, ROW_TILE, LANES), F32),
            pltpu.VMEM((ROW_TILE, D_MODEL), BF16),
            pltpu.VMEM((ROW_TILE, D_MODEL), BF16),
            pltpu.VMEM((ROW_TILE, D_MODEL), BF16),
        ],
        compiler_params=_params(("parallel", "arbitrary")),
        name="inproj_qkv",
    )(x2, g, b, w_qkv)


ATT_HEADS_PER_MATMUL = 4


def _attn_kernel(q_ref, kc_ref, kp_ref, v_ref, o_ref, lse_ref, vt_scr):
    blk = ATT_BLOCK
    hd = ATT_HEAD_DIM
    gw = ATT_GROUP_WIDTH
    hpm = ATT_HEADS_PER_MATMUL
    width = hpm * hd
    lb = pl.program_id(2)

    vt_cur = jnp.concatenate(
        [v_ref[:, c * LANES:(c + 1) * LANES].astype(F32).T.astype(BF16)
         for c in range(gw // LANES)], axis=0)
    vt_prev = jnp.where(lb > 0, vt_scr[...], jnp.zeros_like(vt_cur))
    vt_scr[...] = vt_cur

    key_i = lax.broadcasted_iota(I32, (2 * blk, blk), 0)
    qry_i = lax.broadcasted_iota(I32, (2 * blk, blk), 1)
    live = (((key_i < blk) & (key_i >= qry_i) & (lb > 0))
            | ((key_i >= blk) & (key_i - blk <= qry_i)))
    bias = jnp.concatenate([jnp.where(live, 0.0, -jnp.inf)] * hpm, axis=1)
    lane_head = lax.broadcasted_iota(I32, (blk, width), 1) // hd
    per_tile = LANES // hd
    feats = [slice(g * width, (g + 1) * width) for g in range(ATT_HEADS // hpm)]
    scores = []
    for feat in feats:
        q_g = q_ref[:, feat] * (hd ** -0.5)
        q_bd = jnp.concatenate([jnp.where(lane_head == i, q_g, jnp.zeros_like(q_g))
                                for i in range(hpm)], axis=0)
        k_g = jnp.concatenate([kp_ref[:, feat], kc_ref[:, feat]], axis=0)
        scores.append(_dot_nt(k_g, q_bd))
    for g, feat in enumerate(feats):
        s_t = scores[g] + bias
        m = jnp.max(s_t, axis=0, keepdims=True)
        p = jnp.exp(s_t - m)
        l = jnp.sum(p, axis=0, keepdims=True)
        v_t = jnp.concatenate([vt_prev[feat, :], vt_cur[feat, :]], axis=1)
        o_t = _dot(v_t, p.astype(BF16))
        lse = m + jnp.log(l)
        inv_l = 1.0 / l
        for c in range(width // LANES):
            tile = []
            for i in range(c * per_tile, (c + 1) * per_tile):
                cols = slice(i * blk, (i + 1) * blk)
                tile.append(o_t[i * hd:(i + 1) * hd, cols] * inv_l[:, cols])
                lse_ref[g * hpm + i:g * hpm + i + 1, :] = lse[:, cols]
            lanes = slice(g * width + c * LANES, g * width + (c + 1) * LANES)
            o_ref[:, lanes] = jnp.concatenate(tile, axis=0).T.astype(BF16)


def _attention_group(qkv):
    _, batch, dil, sub_len, gw = qkv.shape
    blk = ATT_BLOCK

    def spec(t, prev):
        def index(b, r, i):
            return (t, b, r, jnp.maximum(i - 1, 0) if prev else i, 0)
        return pl.BlockSpec((None, None, None, blk, gw), index)

    return pl.pallas_call(
        _attn_kernel,
        out_shape=(jax.ShapeDtypeStruct((batch, dil, sub_len, gw), BF16),
                   jax.ShapeDtypeStruct((batch, dil, ATT_HEADS, sub_len), F32)),
        grid=(batch, dil, sub_len // blk),
        in_specs=[spec(0, False), spec(1, False), spec(1, True), spec(2, False)],
        out_specs=(pl.BlockSpec((None, None, blk, gw), lambda b, r, i: (b, r, i, 0)),
                   pl.BlockSpec((None, None, ATT_HEADS, blk), lambda b, r, i: (b, r, 0, i))),
        scratch_shapes=[pltpu.VMEM((gw, blk), BF16)],
        compiler_params=_params(("parallel", "parallel", "arbitrary")),
        name=f"dilated_attention_d{dil}",
    )(qkv, qkv, qkv, qkv)


HGRN_ROWS = 256
HGRN_HEADS_PER_STEP = 4


def _hgrn_rows(q, z, v, gate, st, lb, nw, tri, same_chunk_causal, chunk_mask):
    c = HGRN_CHUNK
    dk = HGRN_HEAD_DIM
    rows = HGRN_ROWS
    nchunk = rows // c
    one_m_lb = 1.0 - lb
    z = z.astype(F32)
    log_f = jnp.log(lb + one_m_lb * _sigmoid(z))
    key = one_m_lb * _sigmoid(-z)
    p0 = log_f.astype(BF16)
    p1 = (log_f - p0.astype(F32)).astype(BF16)
    bcum = _dot(tri, p0) + _dot(tri, p1)

    b3 = bcum.reshape(nchunk, c, dk)
    b_mid = b3[:, c // 2:c // 2 + 1, :]
    b_last = b3[:, c - 1:c, :]
    q3 = q.astype(F32).reshape(nchunk, c, dk) * (dk ** -0.5)
    q_t3 = q3 * jnp.exp(b3 - b_mid)
    k_t3 = key.reshape(nchunk, c, dk) * jnp.exp(b_mid - b3)
    q_t = q_t3.reshape(rows, dk).astype(BF16)
    k_t = k_t3.reshape(rows, dk).astype(BF16)
    q_in = (q_t3 * jnp.exp(b_mid)).reshape(rows, dk).astype(BF16)
    k_st = (k_t3 * jnp.exp(b_last - b_mid)).reshape(rows, dk).astype(BF16)

    att = jnp.where(same_chunk_causal, _dot_nt(q_t, k_t), 0.0)
    o = _dot(att.astype(BF16), v)

    k_spread = jnp.concatenate([k_st * chunk_mask[j] for j in range(nchunk)], axis=1)
    incr = _dot_tn(v, k_spread)
    decay = jnp.exp(b_last.reshape(nchunk, dk))
    before = []
    for j in range(nchunk):
        before.append(st.astype(BF16))
        st = st * decay[j:j + 1] + incr[:, j * dk:(j + 1) * dk]
    q_spread = jnp.concatenate([q_in * chunk_mask[j] for j in range(nchunk)], axis=1)
    o = o + _dot_nt(q_spread, jnp.concatenate(before, axis=1))

    o = o * lax.rsqrt(jnp.mean(o * o, -1, keepdims=True) + RMS_EPS) * nw
    gate = gate.astype(F32)
    return o * gate * _sigmoid(gate), st


def _hgrn_kernel(q_ref, f_ref, i_ref, g_ref, lb_ref, nw_ref, o_ref, state_scr, mask_scr):
    seq = q_ref.shape[0]
    c = HGRN_CHUNK
    rows = HGRN_ROWS
    dk = HGRN_HEAD_DIM
    nw = nw_ref[...]

    r_i = lax.broadcasted_iota(I32, (rows, rows), 0)
    c_i = lax.broadcasted_iota(I32, (rows, rows), 1)
    same_chunk_causal = (r_i // c == c_i // c) & (c_i <= r_i)
    tri = jnp.where(same_chunk_causal, 1.0, 0.0).astype(BF16)
    row_chunk = lax.broadcasted_iota(I32, (rows, dk), 0) // c
    for j in range(rows // c):
        mask_scr[j] = jnp.where(row_chunk == j, 1.0, 0.0).astype(BF16)

    state_scr[...] = jnp.zeros_like(state_scr)

    def body(gi, carry):
        rs = pl.ds(pl.multiple_of(gi * rows, rows), rows)
        for h in range(HGRN_HEADS_PER_STEP):
            cs = slice(h * dk, (h + 1) * dk)
            o, st = _hgrn_rows(q_ref[rs, cs], f_ref[rs, cs], i_ref[rs, cs], g_ref[rs, cs],
                               state_scr[h], lb_ref[:, cs], nw, tri, same_chunk_causal, mask_scr)
            state_scr[h] = st
            o_ref[rs, cs] = o.astype(BF16)
        return carry

    lax.fori_loop(0, seq // rows, body, 0)


def _hgrn(proj, lower_bound, norm_w, batch, seq):
    n = proj.shape[0]
    dk = HGRN_HEAD_DIM
    width = HGRN_HEADS_PER_STEP * dk
    steps = HGRN_HEADS // HGRN_HEADS_PER_STEP

    def seg(k):
        return pl.BlockSpec((seq, width), lambda b, h: (b, k * steps + h))

    return pl.pallas_call(
        _hgrn_kernel,
        out_shape=jax.ShapeDtypeStruct((n, HGRN_WIDTH), BF16),
        grid=(batch, steps),
        in_specs=[seg(0), seg(1), seg(2), seg(3),
                  pl.BlockSpec((1, width), lambda b, h: (0, h)),
                  pl.BlockSpec((1, dk), lambda b, h: (0, 0))],
        out_specs=pl.BlockSpec((seq, width), lambda b, h: (b, h)),
        scratch_shapes=[pltpu.VMEM((HGRN_HEADS_PER_STEP, dk, dk), F32),
                        pltpu.VMEM((HGRN_ROWS // HGRN_CHUNK, HGRN_ROWS, dk), BF16)],
        compiler_params=_params(("parallel", "parallel")),
        name="hgrn2",
    )(proj, proj, proj, proj, lower_bound, norm_w)


def _mix_kernel(x_ref, gin_ref, bin_ref, o0_ref, o1_ref, o2_ref, l0_ref, l1_ref, l2_ref,
                ob_ref, ga_ref, gb_ref, wa_ref, wb_ref, wo_ref, g1_ref, b1_ref,
                rwh_ref, rwl_ref,
                h_ref, hp0_ref, hp1_ref, lg_ref,
                so1_scr, so2_scr):
    tm = x_ref.shape[0]
    gw = ATT_GROUP_WIDTH
    for o_ref, so_scr, (_, dil) in ((o1_ref, so1_scr, ATT_GROUPS[1]),
                                    (o2_ref, so2_scr, ATT_GROUPS[2])):
        for r in range(dil):
            o_r = o_ref[r].astype(F32)
            for c in range(gw // LANES):
                so_scr[c, pl.ds(r, tm // dil, stride=dil), :] = o_r[:, c * LANES:(c + 1) * LANES]

    def natural(scr):
        return jnp.concatenate([scr[c] for c in range(gw // LANES)], axis=-1)

    l0, l1, l2 = l0_ref[...], l1_ref[...], l2_ref[...]
    m = jnp.maximum(jnp.maximum(l0, l1), l2)
    e0 = jnp.exp(l0 - m)
    e1 = jnp.exp(l1 - m)
    e2 = jnp.exp(l2 - m)
    inv = 1.0 / (e0 + e1 + e2)
    head_lanes = jnp.where(
        lax.broadcasted_iota(I32, (ATT_HEADS, gw), 1) // ATT_HEAD_DIM
        == lax.broadcasted_iota(I32, (ATT_HEADS, gw), 0), 1.0, 0.0).astype(BF16)

    def widen(w):
        hi = w.astype(BF16)
        lo = (w - hi.astype(F32)).astype(BF16)
        return _dot_tn(hi, head_lanes) + _dot_tn(lo, head_lanes)

    o_att = (widen(e0 * inv) * o0_ref[...].astype(F32) + widen(e1 * inv) * natural(so1_scr)
             + widen(e2 * inv) * natural(so2_scr))

    y_a = _dot(o_att.astype(BF16), wa_ref[...])
    y_b = _dot(ob_ref[...], wb_ref[...])
    merged = _sigmoid(ga_ref[...].astype(F32)) * y_a + _sigmoid(gb_ref[...].astype(F32)) * y_b
    mix = _dot(merged.astype(BF16), wo_ref[...])
    h_in = _layer_norm(x_ref[...], gin_ref[...], bin_ref[...])
    h1 = _layer_norm(DEEPNORM_ALPHA * h_in + mix, g1_ref[...], b1_ref[...])
    h_ref[...] = h1

    hp0_ref[...], hp1_ref[...] = _pack_rows(h1)

    h_hi = h1.astype(BF16)
    h_lo = (h1 - h_hi.astype(F32)).astype(BF16)
    rwh = rwh_ref[...]
    lg_ref[...] = _dot_nt(rwh, h_hi) + _dot_nt(rwh, h_lo) + _dot_nt(rwl_ref[...], h_hi)


def _mix(x2, gin, bin_, att, ob, proj, wa, wb, wo, g1, b1, rwh, rwl, batch, seq):
    n = x2.shape[0]
    tm = MIX_ROW_TILE
    gw = ATT_GROUP_WIDTH
    tiles_per_seq = seq // tm
    (o0, l0), (o1, l1), (o2, l2) = att
    d1, d2 = ATT_GROUPS[1][1], ATT_GROUPS[2][1]
    lse_spec = pl.BlockSpec((None, ATT_HEADS, tm),
                            lambda i: (i // tiles_per_seq, 0, i % tiles_per_seq))

    def full(shape):
        return pl.BlockSpec(shape, lambda i: (0,) * len(shape))

    def dil_spec(dil):
        return pl.BlockSpec((None, dil, tm // dil, gw),
                            lambda i: (i // tiles_per_seq, 0, i % tiles_per_seq, 0))

    nat_spec = pl.BlockSpec((tm, gw), lambda i: (i, 0))
    return pl.pallas_call(
        _mix_kernel,
        out_shape=(jax.ShapeDtypeStruct((n, D_MODEL), F32),
                   jax.ShapeDtypeStruct((n, PACK_WIDTH), I32),
                   jax.ShapeDtypeStruct((n, PACK_WIDTH), I32),
                   jax.ShapeDtypeStruct((N_EXPERTS, n), F32)),
        grid=(n // tm,),
        in_specs=[
            pl.BlockSpec((tm, D_MODEL), lambda i: (i, 0)),
            full((1, D_MODEL)), full((1, D_MODEL)),
            nat_spec, dil_spec(d1), dil_spec(d2),
            lse_spec, lse_spec, lse_spec,
            pl.BlockSpec((tm, HGRN_WIDTH), lambda i: (i, 0)),
            pl.BlockSpec((tm, D_MODEL), lambda i: (i, 4)),
            pl.BlockSpec((tm, D_MODEL), lambda i: (i, 5)),
            full((gw, D_MODEL)), full((HGRN_WIDTH, D_MODEL)), full((D_MODEL, D_MODEL)),
            full((1, D_MODEL)), full((1, D_MODEL)),
            full((N_EXPERTS, D_MODEL)), full((N_EXPERTS, D_MODEL)),
        ],
        out_specs=(pl.BlockSpec((tm, D_MODEL), lambda i: (i, 0)),
                   pl.BlockSpec((tm, PACK_WIDTH), lambda i: (i, 0)),
                   pl.BlockSpec((tm, PACK_WIDTH), lambda i: (i, 0)),
                   pl.BlockSpec((N_EXPERTS, tm), lambda i: (0, i))),
        scratch_shapes=[pltpu.VMEM((gw // LANES, tm, LANES), F32)] * 2,
        compiler_params=_params(("parallel",)),
        name="branch_mix",
    )(x2, gin, bin_, o0, o1, o2, l0, l1, l2, ob, proj, proj, wa, wb, wo, g1, b1, rwh, rwl)


ROW_SLOTS = 16
ROW_GROUP = 4
ROW_AHEAD = ROW_SLOTS - ROW_GROUP
W_SLOTS = 3


def _expert_kernel(sblk_ref, nblk_ref, cnt_ref, xs0_hbm, xs1_hbm, wi_hbm, wo_hbm,
                   ys0_hbm, ys1_hbm, xbuf0, xbuf1, ybuf0, ybuf1, wibuf, wobuf,
                   in_sem, out_sem, w_sem, wi_scr, wo_scr):
    e = pl.program_id(0)
    n_exp = pl.num_programs(0)
    first_blk = sblk_ref[e]
    n_blk = nblk_ref[e]
    n_rows = cnt_ref[e]
    total = sblk_ref[n_exp - 1] + nblk_ref[n_exp - 1]
    xs_hbm, ys_hbm = (xs0_hbm, xs1_hbm), (ys0_hbm, ys1_hbm)
    xbuf, ybuf = (xbuf0, xbuf1), (ybuf0, ybuf1)

    def rows_of(b):
        return pl.ds(pl.multiple_of(b * MOE_BLOCK, MOE_BLOCK), MOE_BLOCK)

    def x_copy(b, part):
        slot = b % ROW_SLOTS
        return pltpu.make_async_copy(xs_hbm[part].at[rows_of(b)], xbuf[part].at[slot],
                                     in_sem.at[part, slot])

    def y_copy(b, part):
        slot = b % ROW_SLOTS
        return pltpu.make_async_copy(ybuf[part].at[slot], ys_hbm[part].at[rows_of(b)],
                                     out_sem.at[part, slot])

    def w_copies(ex):
        slot = ex % W_SLOTS
        return (pltpu.make_async_copy(wi_hbm.at[ex], wibuf.at[slot], w_sem.at[0, slot]),
                pltpu.make_async_copy(wo_hbm.at[ex], wobuf.at[slot], w_sem.at[1, slot]))

    def start_weights(ex):
        @pl.when((ex < n_exp) & (nblk_ref[jnp.minimum(ex, n_exp - 1)] > 0))
        def _():
            for cp in w_copies(ex):
                cp.start()

    @pl.when(e == 0)
    def _():
        for b in range(ROW_AHEAD):
            @pl.when(b < total)
            def _():
                for part in range(PACK_PARTS):
                    x_copy(b, part).start()
        for ex in range(W_SLOTS - 1):
            start_weights(ex)

    start_weights(e + (W_SLOTS - 1))

    def step(j, group):
        b0 = first_blk + j
        for i in range(group):
            for part in range(PACK_PARTS):
                x_copy(b0 + i, part).wait()
        for i in range(group):
            @pl.when(b0 + ROW_AHEAD + i < total)
            def _():
                for part in range(PACK_PARTS):
                    x_copy(b0 + ROW_AHEAD + i, part).start()
        for i in range(group):
            @pl.when(b0 + i >= ROW_SLOTS)
            def _():
                for part in range(PACK_PARTS):
                    y_copy(b0 + i - ROW_SLOTS, part).wait()

        rows = group * MOE_BLOCK
        live = (lax.broadcasted_iota(I32, (rows, PACK_WIDTH), 0)
                < n_rows - j * MOE_BLOCK)
        parts = [jnp.concatenate([xbuf[part][(b0 + i) % ROW_SLOTS] for i in range(group)], axis=0)
                 for part in range(PACK_PARTS)]
        x = _unpack_rows([jnp.where(live, p, 0) for p in parts])
        hu = _dot(x.astype(BF16), wi_scr[...])
        hg = hu[:, :EXPERT_DIM]
        hv = hu[:, EXPERT_DIM:]
        act = (hg * _sigmoid(hg) * hv).astype(BF16)
        packed = _pack_rows(_dot(act, wo_scr[...]))
        for i in range(group):
            for part in range(PACK_PARTS):
                ybuf[part][(b0 + i) % ROW_SLOTS] = packed[part][i * MOE_BLOCK:(i + 1) * MOE_BLOCK]
                y_copy(b0 + i, part).start()

    @pl.when(n_blk > 0)
    def _():
        for cp in w_copies(e):
            cp.wait()
        slot = e % W_SLOTS
        wi_scr[...] = wibuf[slot].astype(BF16)
        wo_scr[...] = wobuf[slot].astype(BF16)

        def full_group(g, carry):
            step(g * ROW_GROUP, ROW_GROUP)
            return carry

        lax.fori_loop(0, n_blk // ROW_GROUP, full_group, 0)
        done = n_blk // ROW_GROUP * ROW_GROUP
        group = ROW_GROUP // 2
        while group >= 1:
            @pl.when((n_blk & group) != 0)
            def _(group=group, done=done):
                step(done, group)
            done = done + (n_blk & group)
            group //= 2

    @pl.when(e == n_exp - 1)
    def _():
        for back in range(1, ROW_SLOTS + 1):
            @pl.when(total >= back)
            def _():
                for part in range(PACK_PARTS):
                    y_copy(total - back, part).wait()


def _experts(start_blk, n_blk, n_rows, xs, w_in_e, w_out_e):
    p = xs[0].shape[0]
    n_exp = w_in_e.shape[0]
    any_spec = pl.BlockSpec(memory_space=pl.ANY)
    row_buf = pltpu.VMEM((ROW_SLOTS, MOE_BLOCK, PACK_WIDTH), I32)
    grid_spec = pltpu.PrefetchScalarGridSpec(
        num_scalar_prefetch=3,
        grid=(n_exp,),
        in_specs=[any_spec, any_spec, any_spec, any_spec],
        out_specs=(any_spec, any_spec),
        scratch_shapes=[row_buf, row_buf, row_buf, row_buf,
                        pltpu.VMEM((W_SLOTS, D_MODEL, 2 * EXPERT_DIM), F32),
                        pltpu.VMEM((W_SLOTS, EXPERT_DIM, D_MODEL), F32),
                        pltpu.SemaphoreType.DMA((PACK_PARTS, ROW_SLOTS)),
                        pltpu.SemaphoreType.DMA((PACK_PARTS, ROW_SLOTS)),
                        pltpu.SemaphoreType.DMA((2, W_SLOTS)),
                        pltpu.VMEM((D_MODEL, 2 * EXPERT_DIM), BF16),
                        pltpu.VMEM((EXPERT_DIM, D_MODEL), BF16)],
    )
    return pl.pallas_call(
        _expert_kernel,
        out_shape=(jax.ShapeDtypeStruct((p, PACK_WIDTH), I32),) * PACK_PARTS,
        grid_spec=grid_spec,
        compiler_params=_params(("arbitrary",)),
        name="routed_experts",
    )(start_blk, n_blk, n_rows, xs[0], xs[1], w_in_e, w_out_e)


COMBINE_CHUNKS = 2


def _combine_kernel(h_ref, yg0_ref, yg1_ref, gate_ref, wsi_ref, wso_ref, g2_ref, b2_ref, *rest):
    o_ref = rest[-1]
    h1 = h_ref[...]
    gate = gate_ref[...]
    routed = None
    for k in range(TOP_K):
        y_k = _unpack_rows([yg0_ref[k], yg1_ref[k]]) * gate[:, k:k + 1]
        routed = y_k if routed is None else routed + y_k
    hs = _dot(h1.astype(BF16), wsi_ref[...])
    sg = hs[:, :SHARED_DIM]
    sv = hs[:, SHARED_DIM:]
    shared = _dot((sg * _sigmoid(sg) * sv).astype(BF16), wso_ref[...])
    o_ref[...] = _layer_norm(DEEPNORM_ALPHA * h1 + routed + shared, g2_ref[...], b2_ref[...])


def _combine(h1, yg, gate_nk, wsi, wso, g2, b2, chunk, out_so_far):
    n = h1.shape[0]
    tm = OUT_ROW_TILE
    steps = n // COMBINE_CHUNKS // tm
    first = chunk * steps
    yg_spec = pl.BlockSpec((TOP_K, tm, PACK_WIDTH), lambda i: (0, i, 0))

    def full(shape):
        return pl.BlockSpec(shape, lambda i: (0,) * len(shape))

    in_specs = [
        pl.BlockSpec((tm, D_MODEL), lambda i: (first + i, 0)),
        yg_spec, yg_spec,
        pl.BlockSpec((tm, TOP_K), lambda i: (first + i, 0)),
        full((D_MODEL, 2 * SHARED_DIM)), full((SHARED_DIM, D_MODEL)),
        full((1, D_MODEL)), full((1, D_MODEL)),
    ]
    args = [h1, yg[0], yg[1], gate_nk, wsi, wso, g2, b2]
    aliases = {}
    if out_so_far is not None:
        in_specs.append(pl.BlockSpec(memory_space=pl.ANY))
        args.append(out_so_far)
        aliases = {len(args) - 1: 0}
    return pl.pallas_call(
        _combine_kernel,
        out_shape=jax.ShapeDtypeStruct((n, D_MODEL), F32),
        grid=(steps,),
        in_specs=in_specs,
        out_specs=pl.BlockSpec((tm, D_MODEL), lambda i: (first + i, 0)),
        input_output_aliases=aliases,
        compiler_params=_params(("parallel",)),
        name=f"shared_combine_{chunk}",
    )(*args)


ROUTE_TILE = 512


def _pick_first_max(vals, iota, axis, size):
    m = jnp.max(vals, axis=axis, keepdims=True)
    idx = jnp.min(jnp.where(vals == m, iota, size), axis=axis, keepdims=True)
    return m, idx


def _route_kernel(lg_ref, bias_ref, idx_ref, gate_ref, rank_ref, cnt_ref, carry_scr):
    e, tn = lg_ref.shape
    groups = N_EXPERT_GROUPS
    gsz = e // groups
    neg = -jnp.inf

    @pl.when(pl.program_id(0) == 0)
    def _():
        carry_scr[...] = jnp.zeros_like(carry_scr)

    scores = _sigmoid(lg_ref[...])
    biased = scores + bias_ref[...]

    b3 = biased.reshape(groups, gsz, tn)
    io3 = lax.broadcasted_iota(I32, b3.shape, 1)
    m1, i1 = _pick_first_max(b3, io3, 1, gsz)
    m2 = jnp.max(jnp.where(io3 == i1, neg, b3), axis=1, keepdims=True)
    grp = m1 + m2
    iog = lax.broadcasted_iota(I32, grp.shape, 0)
    keep = jnp.zeros(grp.shape, F32)
    for _ in range(TOPK_GROUPS):
        _, gi = _pick_first_max(grp, iog, 0, groups)
        hit = iog == gi
        keep = jnp.where(hit, 1.0, keep)
        grp = jnp.where(hit, neg, grp)
    masked = jnp.where(keep > 0.0, b3, neg).reshape(e, tn)

    ioe = lax.broadcasted_iota(I32, (e, tn), 0)
    onehot = jnp.zeros((e, tn), F32)
    idxs, gates = [], []
    for _ in range(TOP_K):
        _, ei = _pick_first_max(masked, ioe, 0, e)
        hit = ioe == ei
        gates.append(jnp.sum(jnp.where(hit, scores, 0.0), axis=0, keepdims=True))
        onehot = jnp.where(hit, 1.0, onehot)
        masked = jnp.where(hit, neg, masked)
        idxs.append(ei)
    gsum = gates[0]
    for g in gates[1:]:
        gsum = gsum + g
    idx_ref[...] = jnp.concatenate(idxs, axis=0)
    gate_ref[...] = jnp.concatenate(gates, axis=0) / gsum * ROUTED_SCALE

    t_r = lax.broadcasted_iota(I32, (tn, tn), 0)
    t_c = lax.broadcasted_iota(I32, (tn, tn), 1)
    earlier = jnp.where(t_r < t_c, 1.0, 0.0).astype(BF16)
    oh = onehot.astype(BF16)
    base = carry_scr[...]
    before = _dot(oh, earlier) + jnp.concatenate([base] * (tn // LANES), axis=1)
    ranks = [jnp.sum(jnp.where(ioe == ei, before, 0.0), axis=0, keepdims=True) for ei in idxs]
    rank_ref[...] = jnp.concatenate(ranks, axis=0).astype(I32)
    total = base + _dot(oh, jnp.ones((tn, LANES), BF16))
    carry_scr[...] = total
    cnt_ref[...] = total


def _route(logits_t, bias_col):
    e, n = logits_t.shape
    tn = ROUTE_TILE
    tok_spec = pl.BlockSpec((TOP_K, tn), lambda i: (0, i))
    return pl.pallas_call(
        _route_kernel,
        out_shape=(jax.ShapeDtypeStruct((TOP_K, n), I32),
                   jax.ShapeDtypeStruct((TOP_K, n), F32),
                   jax.ShapeDtypeStruct((TOP_K, n), I32),
                   jax.ShapeDtypeStruct((e, LANES), F32)),
        grid=(n // tn,),
        in_specs=[pl.BlockSpec((e, tn), lambda i: (0, i)),
                  pl.BlockSpec((e, 1), lambda i: (0, 0))],
        out_specs=(tok_spec, tok_spec, tok_spec, pl.BlockSpec((e, LANES), lambda i: (0, 0))),
        scratch_shapes=[pltpu.VMEM((e, LANES), F32)],
        compiler_params=_params(("arbitrary",)),
        name="router_topk",
    )(logits_t, bias_col)


def _plan_kernel(idx_ref, rank_ref, cnt_ref, dest_ref, sblk_ref):
    e = cnt_ref.shape[0]
    tn = idx_ref.shape[1]
    cnt = cnt_ref[...]
    nblk = jnp.floor((cnt + (MOE_BLOCK - 1)) * (1.0 / MOE_BLOCK))
    e_r = lax.broadcasted_iota(I32, (e, e), 0)
    e_c = lax.broadcasted_iota(I32, (e, e), 1)
    lower = jnp.where(e_c < e_r, 1.0, 0.0).astype(BF16)
    start_blk = _dot(lower, nblk.astype(BF16))

    ioe = lax.broadcasted_iota(I32, (e, tn), 0)
    start_row = jnp.concatenate([start_blk * MOE_BLOCK] * (tn // LANES), axis=1)
    idx = idx_ref[...]
    dests = [jnp.sum(jnp.where(ioe == idx[k:k + 1], start_row, 0.0), axis=0, keepdims=True)
             for k in range(TOP_K)]
    dest_ref[...] = jnp.concatenate(dests, axis=0).astype(I32) + rank_ref[...]
    sblk_ref[...] = start_blk.astype(I32)


def _plan(idx_t, rank_t, counts):
    k, n = idx_t.shape
    e = counts.shape[0]
    tn = ROUTE_TILE
    tok_spec = pl.BlockSpec((k, tn), lambda i: (0, i))
    exp_spec = pl.BlockSpec((e, LANES), lambda i: (0, 0))
    return pl.pallas_call(
        _plan_kernel,
        out_shape=(jax.ShapeDtypeStruct((k, n), I32), jax.ShapeDtypeStruct((e, LANES), I32)),
        grid=(n // tn,),
        in_specs=[tok_spec, tok_spec, exp_spec],
        out_specs=(tok_spec, exp_spec),
        compiler_params=_params(("arbitrary",)),
        name="dispatch_plan",
    )(idx_t, rank_t, counts)


SC_WINDOW = 128


def _sc_mesh():
    return plsc.VectorSubcoreMesh(core_axis_name="core", subcore_axis_name="subcore")


def _dispatch_rows(parts, dest, p_total):
    n, width = parts[0].shape
    top_k = dest.shape[0]
    out_type = (jax.ShapeDtypeStruct((p_total, width), parts[0].dtype),) * len(parts)

    @functools.partial(pl.kernel, mesh=_sc_mesh(), scratch_types=[], out_type=out_type,
                       name="dispatch_rows")
    def scatter(*refs):
        x_hbms = refs[:len(parts)]
        i_hbm = refs[len(parts)]
        o_hbms = refs[len(parts) + 1:]
        for x_hbm, o_hbm in zip(x_hbms, o_hbms):
            def body(x_vmem, i_vmem, o_hbm=o_hbm):
                for k in range(top_k):
                    pltpu.sync_copy(x_vmem, o_hbm.at[i_vmem.at[k]])

            pltpu.emit_pipeline(
                body,
                grid=(n // SC_WINDOW,),
                in_specs=[pl.BlockSpec((SC_WINDOW, width), lambda i: (i, 0)),
                          pl.BlockSpec((top_k, SC_WINDOW), lambda i: (0, i))],
                out_specs=[],
                core_axis_name=("core", "subcore"),
                dimension_semantics=(pltpu.PARALLEL,),
            )(x_hbm, i_hbm)

    return scatter(*parts, dest)


def _gather_rows(parts, idx_flat):
    count = idx_flat.shape[1]
    width = parts[0].shape[1]
    out_type = (jax.ShapeDtypeStruct((count, width), parts[0].dtype),) * len(parts)

    @functools.partial(pl.kernel, mesh=_sc_mesh(), scratch_types=[], out_type=out_type,
                       name="combine_rows")
    def gather(*refs):
        y_hbms = refs[:len(parts)]
        i_hbm = refs[len(parts)]
        o_hbms = refs[len(parts) + 1:]
        for y_hbm, o_hbm in zip(y_hbms, o_hbms):
            def body(i_vmem, o_vmem, y_hbm=y_hbm):
                pltpu.sync_copy(y_hbm.at[i_vmem.at[0]], o_vmem)

            pltpu.emit_pipeline(
                body,
                grid=(count // SC_WINDOW,),
                in_specs=[pl.BlockSpec((1, SC_WINDOW), lambda i: (0, i))],
                out_specs=[pl.BlockSpec((SC_WINDOW, width), lambda i: (i, 0))],
                core_axis_name=("core", "subcore"),
                dimension_semantics=(pltpu.PARALLEL,),
            )(i_hbm, o_hbm)

    return gather(*parts, idx_flat)


def kernel(x, ln_in_g, ln_in_b, w_in, hgrn_lb_logits, hgrn_norm_w, w_branch_att, w_branch_hgrn,
           w_out, ln1_g, ln1_b, router_w, router_bias, expert_w_in, expert_w_out, shared_w_in,
           shared_w_out, ln2_g, ln2_b):
    batch, seq, d = x.shape
    n = batch * seq
    x2 = x.reshape(n, d)
    row = lambda v: v.reshape(1, -1).astype(F32)

    lower_bounds = jnp.cumsum(jax.nn.softmax(hgrn_lb_logits.astype(F32), axis=0), axis=0)
    l = 0
    w_l = w_in[l]
    w_qkv = w_l[:, :3 * ATT_WIDTH].astype(BF16)
    w_main = w_l[:, 3 * ATT_WIDTH:].astype(BF16)
    gin, bin_ = row(ln_in_g), row(ln_in_b)

    proj = _inproj_main(x2, gin, bin_, w_main)
    qkv0, qkv1, qkv2 = _inproj_qkv(x2, gin, bin_, w_qkv, batch, seq)
    qkv0 = qkv0.reshape(3, batch, 1, seq, ATT_GROUP_WIDTH)
    att = []
    for qkv in (qkv0, qkv1, qkv2):
        o_g, lse_g = _attention_group(qkv)
        att.append((o_g, lse_g.transpose(0, 2, 3, 1).reshape(batch, ATT_HEADS, seq)))
    att[0] = (att[0][0].reshape(n, ATT_GROUP_WIDTH), att[0][1])

    ob = _hgrn(proj, row(lower_bounds[l]), row(hgrn_norm_w[l]), batch, seq)

    rw_t = router_w[l].T.astype(F32)
    rwh = rw_t.astype(BF16)
    rwl = (rw_t - rwh.astype(F32)).astype(BF16)
    h1, hp0, hp1, logits_t = _mix(
        x2, gin, bin_, att, ob, proj,
        w_branch_att[l].astype(BF16), w_branch_hgrn[l].astype(BF16), w_out[l].astype(BF16),
        row(ln1_g[l]), row(ln1_b[l]), rwh, rwl, batch, seq)

    idx_t, gate_t, rank_t, counts = _route(logits_t, router_bias[l].reshape(-1, 1).astype(F32))
    p_total = n * TOP_K + N_EXPERTS * MOE_BLOCK
    dest, start_blk = _plan(idx_t, rank_t, counts)
    n_rows = counts[:, 0].astype(I32)
    n_blk = (n_rows + (MOE_BLOCK - 1)) // MOE_BLOCK
    xs = _dispatch_rows((hp0, hp1), dest, p_total)
    ys = _experts(start_blk[:, 0], n_blk, n_rows, xs, expert_w_in[l], expert_w_out[l])
    gate_nk = gate_t.T
    wsi, wso = shared_w_in[l].astype(BF16), shared_w_out[l].astype(BF16)
    chunk_n = n // COMBINE_CHUNKS
    out = None
    for c in range(COMBINE_CHUNKS):
        idx_c = dest[:, c * chunk_n:(c + 1) * chunk_n].reshape(1, TOP_K * chunk_n)
        yg = [g.reshape(TOP_K, chunk_n, PACK_WIDTH) for g in _gather_rows(ys, idx_c)]
        out = _combine(h1, yg, gate_nk, wsi, wso, row(ln2_g[l]), row(ln2_b[l]), c, out)
    return out.reshape(batch, seq, d)
```

```python
import functools

import jax
import jax.numpy as jnp
from jax import lax
from jax.experimental import pallas as pl
from jax.experimental.pallas import tpu as pltpu
from jax.experimental.pallas import tpu_sc as plsc

F32 = jnp.float32
BF16 = jnp.bfloat16
U32 = jnp.uint32
I32 = jnp.int32

D_MODEL = 1024
ATT_GROUPS = ((128, 1), (512, 4), (2048, 16))
ATT_HEADS = 8
ATT_HEAD_DIM = 64
ATT_GROUP_WIDTH = ATT_HEADS * ATT_HEAD_DIM
ATT_WIDTH = len(ATT_GROUPS) * ATT_GROUP_WIDTH
ATT_BLOCK = 128
HGRN_HEAD_DIM = 128
HGRN_HEADS = D_MODEL // HGRN_HEAD_DIM
HGRN_WIDTH = HGRN_HEADS * HGRN_HEAD_DIM
HGRN_CHUNK = 32
N_EXPERTS = 256
TOP_K = 8
N_EXPERT_GROUPS = 8
TOPK_GROUPS = 4
EXPERT_DIM = 256
SHARED_DIM = 256
ROUTED_SCALE = 2.5
MOE_BLOCK = 128
LN_EPS = 1e-5
RMS_EPS = 1e-6
DEPTH = 1
DEEPNORM_ALPHA = (2 * DEPTH) ** 0.25

VMEM_LIMIT = 56 * 1024 * 1024
LANES = 128

ROW_TILE = 1024
MAIN_COL_TILE = 1536
MIX_ROW_TILE = 512
OUT_ROW_TILE = 512


def _params(sem, vmem=VMEM_LIMIT):
    return pltpu.CompilerParams(dimension_semantics=sem, vmem_limit_bytes=vmem)


def _layer_norm(x, g, b):
    mu = jnp.mean(x, -1, keepdims=True)
    xc = x - mu
    var = jnp.mean(xc * xc, -1, keepdims=True)
    return xc * lax.rsqrt(var + LN_EPS) * g + b


def _sigmoid(x):
    return 1.0 / (1.0 + jnp.exp(-x))


def _dot(a, b):
    return jnp.dot(a, b, preferred_element_type=F32)


def _dot_nt(a, b):
    return lax.dot_general(a, b, (((1,), (1,)), ((), ())), preferred_element_type=F32)


def _dot_tn(a, b):
    return lax.dot_general(a, b, (((0,), (0,)), ((), ())), preferred_element_type=F32)


def _pack_bf16_pair(lo, hi):
    lo_bits = pltpu.bitcast(lo.astype(BF16).astype(F32), U32) >> 16
    hi_bits = pltpu.bitcast(hi.astype(BF16).astype(F32), U32) & jnp.uint32(0xFFFF0000)
    return hi_bits | lo_bits


def _unpack_bf16_pair(w):
    lo = pltpu.bitcast(w << 16, F32)
    hi = pltpu.bitcast(w & jnp.uint32(0xFFFF0000), F32)
    return lo, hi


PACK_PARTS = 2
PACK_WIDTH = D_MODEL // 2 // PACK_PARTS


def _pack_rows(v):
    half = D_MODEL // 2
    parts = []
    for j in range(PACK_PARTS):
        lo = v[:, j * PACK_WIDTH:(j + 1) * PACK_WIDTH]
        hi = v[:, half + j * PACK_WIDTH:half + (j + 1) * PACK_WIDTH]
        parts.append(pltpu.bitcast(_pack_bf16_pair(lo, hi), I32))
    return parts


def _unpack_rows(parts):
    pairs = [_unpack_bf16_pair(pltpu.bitcast(p, U32)) for p in parts]
    return jnp.concatenate([lo for lo, _ in pairs] + [hi for _, hi in pairs], axis=-1)


def _inproj_main_kernel(x_ref, g_ref, b_ref, w_ref, o_ref, h_scr):
    @pl.when(pl.program_id(1) == 0)
    def _():
        h_scr[...] = _layer_norm(x_ref[...], g_ref[...], b_ref[...]).astype(BF16)

    o_ref[...] = _dot(h_scr[...], w_ref[...]).astype(BF16)


def _inproj_main(x2, g, b, w_main):
    n = x2.shape[0]
    width = w_main.shape[1]
    return pl.pallas_call(
        _inproj_main_kernel,
        out_shape=jax.ShapeDtypeStruct((n, width), BF16),
        grid=(n // ROW_TILE, width // MAIN_COL_TILE),
        in_specs=[
            pl.BlockSpec((ROW_TILE, D_MODEL), lambda i, j: (i, 0)),
            pl.BlockSpec((1, D_MODEL), lambda i, j: (0, 0)),
            pl.BlockSpec((1, D_MODEL), lambda i, j: (0, 0)),
            pl.BlockSpec((D_MODEL, MAIN_COL_TILE), lambda i, j: (0, j)),
        ],
        out_specs=pl.BlockSpec((ROW_TILE, MAIN_COL_TILE), lambda i, j: (i, j)),
        scratch_shapes=[pltpu.VMEM((ROW_TILE, D_MODEL), BF16)],
        compiler_params=_params(("parallel", "arbitrary")),
        name="inproj_main",
    )(x2, g, b, w_main)


def _inproj_qkv_kernel(x_ref, g_ref, b_ref, w_ref, o0_ref, o1_ref, o2_ref,
                       hf_scr, h0_scr, h1_scr, h2_scr):
    @pl.when(pl.program_id(1) == 0)
    def _():
        hf = _layer_norm(x_ref[...], g_ref[...], b_ref[...])
        h0_scr[...] = hf.astype(BF16)
        for c in range(D_MODEL // LANES):
            hf_scr[c] = hf[:, c * LANES:(c + 1) * LANES]
        for h_scr, (_, dil) in ((h1_scr, ATT_GROUPS[1]), (h2_scr, ATT_GROUPS[2])):
            rows = ROW_TILE // dil
            for r in range(dil):
                for c in range(D_MODEL // LANES):
                    h_scr[r * rows:(r + 1) * rows, c * LANES:(c + 1) * LANES] = (
                        hf_scr[c, pl.ds(r, rows, stride=dil), :].astype(BF16))

    gw = ATT_GROUP_WIDTH
    o0_ref[0] = _dot(h0_scr[...], w_ref[:, 0:gw]).astype(BF16)
    d1 = ATT_GROUPS[1][1]
    o1_ref[0, 0] = _dot(h1_scr[...], w_ref[:, gw:2 * gw]).astype(BF16).reshape(d1, ROW_TILE // d1, gw)
    d2 = ATT_GROUPS[2][1]
    o2_ref[0, 0] = _dot(h2_scr[...], w_ref[:, 2 * gw:3 * gw]).astype(BF16).reshape(d2, ROW_TILE // d2, gw)


def _inproj_qkv(x2, g, b, w_qkv, batch, seq):
    n = x2.shape[0]
    gw = ATT_GROUP_WIDTH
    tiles_per_seq = seq // ROW_TILE
    d1, d2 = ATT_GROUPS[1][1], ATT_GROUPS[2][1]
    out_shape = (
        jax.ShapeDtypeStruct((3, n, gw), BF16),
        jax.ShapeDtypeStruct((3, batch, d1, seq // d1, gw), BF16),
        jax.ShapeDtypeStruct((3, batch, d2, seq // d2, gw), BF16),
    )
    return pl.pallas_call(
        _inproj_qkv_kernel,
        out_shape=out_shape,
        grid=(n // ROW_TILE, 3),
        in_specs=[
            pl.BlockSpec((ROW_TILE, D_MODEL), lambda i, t: (i, 0)),
            pl.BlockSpec((1, D_MODEL), lambda i, t: (0, 0)),
            pl.BlockSpec((1, D_MODEL), lambda i, t: (0, 0)),
            pl.BlockSpec((D_MODEL, ATT_WIDTH), lambda i, t: (0, t)),
        ],
        out_specs=(
            pl.BlockSpec((1, ROW_TILE, gw), lambda i, t: (t, i, 0)),
            pl.BlockSpec((1, 1, d1, ROW_TILE // d1, gw),
                         lambda i, t: (t, i // tiles_per_seq, 0, i % tiles_per_seq, 0)),
            pl.BlockSpec((1, 1, d2, ROW_TILE // d2, gw),
                         lambda i, t: (t, i // tiles_per_seq, 0, i % tiles_per_seq, 0)),
        ),
        scratch_shapes=[
            pltpu.VMEM((D_MODEL // LANES, ROW_TILE, LANES), F32),
            pltpu.VMEM((ROW_TILE, D_MODEL), BF16),
            pltpu.VMEM((ROW_TILE, D_MODEL), BF16),
            pltpu.VMEM((ROW_TILE, D_MODEL), BF16),
        ],
        compiler_params=_params(("parallel", "arbitrary")),
        name="inproj_qkv",
    )(x2, g, b, w_qkv)


ATT_HEADS_PER_MATMUL = 4
ATT_BATCH_PER_STEP = 4


def _attn_block(q_ref, kc_ref, kp_ref, v_ref, o_ref, lse_ref, vt_ref, bias, has_prev):
    blk = ATT_BLOCK
    hd = ATT_HEAD_DIM
    gw = ATT_GROUP_WIDTH
    hpm = ATT_HEADS_PER_MATMUL
    width = hpm * hd

    vt_cur = jnp.concatenate(
        [v_ref[:, c * LANES:(c + 1) * LANES].astype(F32).T.astype(BF16)
         for c in range(gw // LANES)], axis=0)
    vt_prev = jnp.where(has_prev, vt_ref[...], jnp.zeros_like(vt_cur))
    vt_ref[...] = vt_cur

    lane_head = lax.broadcasted_iota(I32, (blk, width), 1) // hd
    per_tile = LANES // hd
    for g in range(ATT_HEADS // hpm):
        feat = slice(g * width, (g + 1) * width)
        q_g = q_ref[:, feat] * (hd ** -0.5)
        q_bd = jnp.concatenate([jnp.where(lane_head == i, q_g, jnp.zeros_like(q_g))
                                for i in range(hpm)], axis=0)
        k_g = jnp.concatenate([kp_ref[:, feat], kc_ref[:, feat]], axis=0)
        s_t = _dot_nt(k_g, q_bd) + bias
        m = jnp.max(s_t, axis=0, keepdims=True)
        p = jnp.exp(s_t - m)
        l = jnp.sum(p, axis=0, keepdims=True)
        v_t = jnp.concatenate([vt_prev[feat, :], vt_cur[feat, :]], axis=1)
        o_t = _dot(v_t, p.astype(BF16))
        lse = m + jnp.log(l)
        inv_l = 1.0 / l
        for c in range(width // LANES):
            tile = []
            for i in range(c * per_tile, (c + 1) * per_tile):
                cols = slice(i * blk, (i + 1) * blk)
                tile.append(o_t[i * hd:(i + 1) * hd, cols] * inv_l[:, cols])
                lse_ref[g * hpm + i:g * hpm + i + 1, :] = lse[:, cols]
            lanes = slice(g * width + c * LANES, g * width + (c + 1) * LANES)
            o_ref[:, lanes] = jnp.concatenate(tile, axis=0).T.astype(BF16)


def _attn_kernel(q_ref, kc_ref, kp_ref, v_ref, o_ref, lse_ref, vt_scr):
    blk = ATT_BLOCK
    has_prev = pl.program_id(2) > 0
    key_i = lax.broadcasted_iota(I32, (2 * blk, blk), 0)
    qry_i = lax.broadcasted_iota(I32, (2 * blk, blk), 1)
    live = (((key_i < blk) & (key_i >= qry_i) & has_prev)
            | ((key_i >= blk) & (key_i - blk <= qry_i)))
    bias = jnp.concatenate([jnp.where(live, 0.0, -jnp.inf)] * ATT_HEADS_PER_MATMUL, axis=1)
    for j in range(ATT_BATCH_PER_STEP):
        _attn_block(q_ref.at[j], kc_ref.at[j], kp_ref.at[j], v_ref.at[j], o_ref.at[j],
                    lse_ref.at[j], vt_scr.at[j], bias, has_prev)


def _attention_group(qkv):
    _, batch, dil, sub_len, gw = qkv.shape
    blk = ATT_BLOCK
    bps = ATT_BATCH_PER_STEP

    def spec(t, prev):
        def index(b, r, i):
            return (t, b, r, jnp.maximum(i - 1, 0) if prev else i, 0)
        return pl.BlockSpec((None, bps, None, blk, gw), index)

    return pl.pallas_call(
        _attn_kernel,
        out_shape=(jax.ShapeDtypeStruct((batch, dil, sub_len, gw), BF16),
                   jax.ShapeDtypeStruct((batch, dil, ATT_HEADS, sub_len), F32)),
        grid=(batch // bps, dil, sub_len // blk),
        in_specs=[spec(0, False), spec(1, False), spec(1, True), spec(2, False)],
        out_specs=(pl.BlockSpec((bps, None, blk, gw), lambda b, r, i: (b, r, i, 0)),
                   pl.BlockSpec((bps, None, ATT_HEADS, blk), lambda b, r, i: (b, r, 0, i))),
        scratch_shapes=[pltpu.VMEM((bps, gw, blk), BF16)],
        compiler_params=_params(("parallel", "parallel", "arbitrary")),
        name=f"dilated_attention_d{dil}",
    )(qkv, qkv, qkv, qkv)


HGRN_ROWS = 256
HGRN_HEADS_PER_STEP = 4


def _hgrn_rows(q, z, v, gate, st, lb, nw, tri, same_chunk_causal, chunk_mask):
    c = HGRN_CHUNK
    dk = HGRN_HEAD_DIM
    rows = HGRN_ROWS
    nchunk = rows // c
    one_m_lb = 1.0 - lb
    z = z.astype(F32)
    log_f = jnp.log(lb + one_m_lb * _sigmoid(z))
    key = one_m_lb * _sigmoid(-z)
    p0 = log_f.astype(BF16)
    p1 = (log_f - p0.astype(F32)).astype(BF16)
    bcum = _dot(tri, p0) + _dot(tri, p1)

    b3 = bcum.reshape(nchunk, c, dk)
    b_mid = b3[:, c // 2:c // 2 + 1, :]
    b_last = b3[:, c - 1:c, :]
    q3 = q.astype(F32).reshape(nchunk, c, dk) * (dk ** -0.5)
    q_t3 = q3 * jnp.exp(b3 - b_mid)
    k_t3 = key.reshape(nchunk, c, dk) * jnp.exp(b_mid - b3)
    q_t = q_t3.reshape(rows, dk).astype(BF16)
    k_t = k_t3.reshape(rows, dk).astype(BF16)
    q_in = (q_t3 * jnp.exp(b_mid)).reshape(rows, dk).astype(BF16)
    k_st = (k_t3 * jnp.exp(b_last - b_mid)).reshape(rows, dk).astype(BF16)

    att = jnp.where(same_chunk_causal, _dot_nt(q_t, k_t), 0.0)
    o = _dot(att.astype(BF16), v)

    k_spread = jnp.concatenate([k_st * chunk_mask[j] for j in range(nchunk)], axis=1)
    incr = _dot_tn(v, k_spread)
    decay = jnp.exp(b_last.reshape(nchunk, dk))
    before = []
    for j in range(nchunk):
        before.append(st.astype(BF16))
        st = st * decay[j:j + 1] + incr[:, j * dk:(j + 1) * dk]
    q_spread = jnp.concatenate([q_in * chunk_mask[j] for j in range(nchunk)], axis=1)
    o = o + _dot_nt(q_spread, jnp.concatenate(before, axis=1))

    o = o * lax.rsqrt(jnp.mean(o * o, -1, keepdims=True) + RMS_EPS) * nw
    gate = gate.astype(F32)
    return o * gate * _sigmoid(gate), st


def _hgrn_kernel(q_ref, f_ref, i_ref, g_ref, lb_ref, nw_ref, o_ref, state_scr, mask_scr):
    seq = q_ref.shape[0]
    c = HGRN_CHUNK
    rows = HGRN_ROWS
    dk = HGRN_HEAD_DIM
    nw = nw_ref[...]

    r_i = lax.broadcasted_iota(I32, (rows, rows), 0)
    c_i = lax.broadcasted_iota(I32, (rows, rows), 1)
    same_chunk_causal = (r_i // c == c_i // c) & (c_i <= r_i)
    tri = jnp.where(same_chunk_causal, 1.0, 0.0).astype(BF16)
    row_chunk = lax.broadcasted_iota(I32, (rows, dk), 0) // c
    for j in range(rows // c):
        mask_scr[j] = jnp.where(row_chunk == j, 1.0, 0.0).astype(BF16)

    state_scr[...] = jnp.zeros_like(state_scr)

    def body(gi, carry):
        rs = pl.ds(pl.multiple_of(gi * rows, rows), rows)
        for h in range(HGRN_HEADS_PER_STEP):
            cs = slice(h * dk, (h + 1) * dk)
            o, st = _hgrn_rows(q_ref[rs, cs], f_ref[rs, cs], i_ref[rs, cs], g_ref[rs, cs],
                               state_scr[h], lb_ref[:, cs], nw, tri, same_chunk_causal, mask_scr)
            state_scr[h] = st
            o_ref[rs, cs] = o.astype(BF16)
        return carry

    lax.fori_loop(0, seq // rows, body, 0)


def _hgrn(proj, lower_bound, norm_w, batch, seq):
    n = proj.shape[0]
    dk = HGRN_HEAD_DIM
    width = HGRN_HEADS_PER_STEP * dk
    steps = HGRN_HEADS // HGRN_HEADS_PER_STEP

    def seg(k):
        return pl.BlockSpec((seq, width), lambda b, h: (b, k * steps + h))

    return pl.pallas_call(
        _hgrn_kernel,
        out_shape=jax.ShapeDtypeStruct((n, HGRN_WIDTH), BF16),
        grid=(batch, steps),
        in_specs=[seg(0), seg(1), seg(2), seg(3),
                  pl.BlockSpec((1, width), lambda b, h: (0, h)),
                  pl.BlockSpec((1, dk), lambda b, h: (0, 0))],
        out_specs=pl.BlockSpec((seq, width), lambda b, h: (b, h)),
        scratch_shapes=[pltpu.VMEM((HGRN_HEADS_PER_STEP, dk, dk), F32),
                        pltpu.VMEM((HGRN_ROWS // HGRN_CHUNK, HGRN_ROWS, dk), BF16)],
        compiler_params=_params(("parallel", "parallel")),
        name="hgrn2",
    )(proj, proj, proj, proj, lower_bound, norm_w)


def _mix_kernel(x_ref, gin_ref, bin_ref, o0_ref, o1_ref, o2_ref, l0_ref, l1_ref, l2_ref,
                ob_ref, ga_ref, gb_ref, wa_ref, wb_ref, wo_ref, g1_ref, b1_ref,
                rwh_ref, rwl_ref,
                h_ref, hp0_ref, hp1_ref, lg_ref,
                so1_scr, so2_scr):
    tm = x_ref.shape[0]
    gw = ATT_GROUP_WIDTH
    for o_ref, so_scr, (_, dil) in ((o1_ref, so1_scr, ATT_GROUPS[1]),
                                    (o2_ref, so2_scr, ATT_GROUPS[2])):
        for r in range(dil):
            o_r = o_ref[r].astype(F32)
            for c in range(gw // LANES):
                so_scr[c, pl.ds(r, tm // dil, stride=dil), :] = o_r[:, c * LANES:(c + 1) * LANES]

    def natural(scr):
        return jnp.concatenate([scr[c] for c in range(gw // LANES)], axis=-1)

    l0, l1, l2 = l0_ref[...], l1_ref[...], l2_ref[...]
    m = jnp.maximum(jnp.maximum(l0, l1), l2)
    e0 = jnp.exp(l0 - m)
    e1 = jnp.exp(l1 - m)
    e2 = jnp.exp(l2 - m)
    inv = 1.0 / (e0 + e1 + e2)
    head_lanes = jnp.where(
        lax.broadcasted_iota(I32, (ATT_HEADS, gw), 1) // ATT_HEAD_DIM
        == lax.broadcasted_iota(I32, (ATT_HEADS, gw), 0), 1.0, 0.0).astype(BF16)

    def widen(w):
        hi = w.astype(BF16)
        lo = (w - hi.astype(F32)).astype(BF16)
        return _dot_tn(hi, head_lanes) + _dot_tn(lo, head_lanes)

    o_att = (widen(e0 * inv) * o0_ref[...].astype(F32) + widen(e1 * inv) * natural(so1_scr)
             + widen(e2 * inv) * natural(so2_scr))

    y_a = _dot(o_att.astype(BF16), wa_ref[...])
    y_b = _dot(ob_ref[...], wb_ref[...])
    merged = _sigmoid(ga_ref[...].astype(F32)) * y_a + _sigmoid(gb_ref[...].astype(F32)) * y_b
    mix = _dot(merged.astype(BF16), wo_ref[...])
    h_in = _layer_norm(x_ref[...], gin_ref[...], bin_ref[...])
    h1 = _layer_norm(DEEPNORM_ALPHA * h_in + mix, g1_ref[...], b1_ref[...])
    h_ref[...] = h1

    hp0_ref[...], hp1_ref[...] = _pack_rows(h1)

    h_hi = h1.astype(BF16)
    h_lo = (h1 - h_hi.astype(F32)).astype(BF16)
    rwh = rwh_ref[...]
    lg_ref[...] = _dot_nt(rwh, h_hi) + _dot_nt(rwh, h_lo) + _dot_nt(rwl_ref[...], h_hi)


def _mix(x2, gin, bin_, att, ob, proj, wa, wb, wo, g1, b1, rwh, rwl, batch, seq):
    n = x2.shape[0]
    tm = MIX_ROW_TILE
    gw = ATT_GROUP_WIDTH
    tiles_per_seq = seq // tm
    (o0, l0), (o1, l1), (o2, l2) = att
    d1, d2 = ATT_GROUPS[1][1], ATT_GROUPS[2][1]
    lse_spec = pl.BlockSpec((None, ATT_HEADS, tm),
                            lambda i: (i // tiles_per_seq, 0, i % tiles_per_seq))

    def full(shape):
        return pl.BlockSpec(shape, lambda i: (0,) * len(shape))

    def dil_spec(dil):
        return pl.BlockSpec((None, dil, tm // dil, gw),
                            lambda i: (i // tiles_per_seq, 0, i % tiles_per_seq, 0))

    nat_spec = pl.BlockSpec((tm, gw), lambda i: (i, 0))
    return pl.pallas_call(
        _mix_kernel,
        out_shape=(jax.ShapeDtypeStruct((n, D_MODEL), F32),
                   jax.ShapeDtypeStruct((n, PACK_WIDTH), I32),
                   jax.ShapeDtypeStruct((n, PACK_WIDTH), I32),
                   jax.ShapeDtypeStruct((N_EXPERTS, n), F32)),
        grid=(n // tm,),
        in_specs=[
            pl.BlockSpec((tm, D_MODEL), lambda i: (i, 0)),
            full((1, D_MODEL)), full((1, D_MODEL)),
            nat_spec, dil_spec(d1), dil_spec(d2),
            lse_spec, lse_spec, lse_spec,
            pl.BlockSpec((tm, HGRN_WIDTH), lambda i: (i, 0)),
            pl.BlockSpec((tm, D_MODEL), lambda i: (i, 4)),
            pl.BlockSpec((tm, D_MODEL), lambda i: (i, 5)),
            full((gw, D_MODEL)), full((HGRN_WIDTH, D_MODEL)), full((D_MODEL, D_MODEL)),
            full((1, D_MODEL)), full((1, D_MODEL)),
            full((N_EXPERTS, D_MODEL)), full((N_EXPERTS, D_MODEL)),
        ],
        out_specs=(pl.BlockSpec((tm, D_MODEL), lambda i: (i, 0)),
                   pl.BlockSpec((tm, PACK_WIDTH), lambda i: (i, 0)),
                   pl.BlockSpec((tm, PACK_WIDTH), lambda i: (i, 0)),
                   pl.BlockSpec((N_EXPERTS, tm), lambda i: (0, i))),
        scratch_shapes=[pltpu.VMEM((gw // LANES, tm, LANES), F32)] * 2,
        compiler_params=_params(("parallel",)),
        name="branch_mix",
    )(x2, gin, bin_, o0, o1, o2, l0, l1, l2, ob, proj, proj, wa, wb, wo, g1, b1, rwh, rwl)


ROW_SLOTS = 16
ROW_GROUP = 4
ROW_AHEAD = ROW_SLOTS - ROW_GROUP
W_SLOTS = 3


def _expert_kernel(sblk_ref, nblk_ref, cnt_ref, xs0_hbm, xs1_hbm, wi_hbm, wo_hbm,
                   ys0_hbm, ys1_hbm, xbuf0, xbuf1, ybuf0, ybuf1, wibuf, wobuf,
                   in_sem, out_sem, w_sem, wi_scr, wo_scr):
    e = pl.program_id(0)
    n_exp = pl.num_programs(0)
    first_blk = sblk_ref[e]
    n_blk = nblk_ref[e]
    n_rows = cnt_ref[e]
    total = sblk_ref[n_exp - 1] + nblk_ref[n_exp - 1]
    xs_hbm, ys_hbm = (xs0_hbm, xs1_hbm), (ys0_hbm, ys1_hbm)
    xbuf, ybuf = (xbuf0, xbuf1), (ybuf0, ybuf1)

    def rows_of(b):
        return pl.ds(pl.multiple_of(b * MOE_BLOCK, MOE_BLOCK), MOE_BLOCK)

    def x_copy(b, part):
        slot = b % ROW_SLOTS
        return pltpu.make_async_copy(xs_hbm[part].at[rows_of(b)], xbuf[part].at[slot],
                                     in_sem.at[part, slot])

    def y_copy(b, part):
        slot = b % ROW_SLOTS
        return pltpu.make_async_copy(ybuf[part].at[slot], ys_hbm[part].at[rows_of(b)],
                                     out_sem.at[part, slot])

    def w_copies(ex):
        slot = ex % W_SLOTS
        return (pltpu.make_async_copy(wi_hbm.at[ex], wibuf.at[slot], w_sem.at[0, slot]),
                pltpu.make_async_copy(wo_hbm.at[ex], wobuf.at[slot], w_sem.at[1, slot]))

    def start_weights(ex):
        @pl.when((ex < n_exp) & (nblk_ref[jnp.minimum(ex, n_exp - 1)] > 0))
        def _():
            for cp in w_copies(ex):
                cp.start()

    @pl.when(e == 0)
    def _():
        for b in range(ROW_AHEAD):
            @pl.when(b < total)
            def _():
                for part in range(PACK_PARTS):
                    x_copy(b, part).start()
        for ex in range(W_SLOTS - 1):
            start_weights(ex)

    start_weights(e + (W_SLOTS - 1))

    def step(j, group):
        b0 = first_blk + j
        for i in range(group):
            for part in range(PACK_PARTS):
                x_copy(b0 + i, part).wait()
        for i in range(group):
            @pl.when(b0 + ROW_AHEAD + i < total)
            def _():
                for part in range(PACK_PARTS):
                    x_copy(b0 + ROW_AHEAD + i, part).start()
        for i in range(group):
            @pl.when(b0 + i >= ROW_SLOTS)
            def _():
                for part in range(PACK_PARTS):
                    y_copy(b0 + i - ROW_SLOTS, part).wait()

        rows = group * MOE_BLOCK
        live = (lax.broadcasted_iota(I32, (rows, PACK_WIDTH), 0)
                < n_rows - j * MOE_BLOCK)
        parts = [jnp.concatenate([xbuf[part][(b0 + i) % ROW_SLOTS] for i in range(group)], axis=0)
                 for part in range(PACK_PARTS)]
        x = _unpack_rows([jnp.where(live, p, 0) for p in parts])
        hu = _dot(x.astype(BF16), wi_scr[...])
        hg = hu[:, :EXPERT_DIM]
        hv = hu[:, EXPERT_DIM:]
        act = (hg * _sigmoid(hg) * hv).astype(BF16)
        packed = _pack_rows(_dot(act, wo_scr[...]))
        for i in range(group):
            for part in range(PACK_PARTS):
                ybuf[part][(b0 + i) % ROW_SLOTS] = packed[part][i * MOE_BLOCK:(i + 1) * MOE_BLOCK]
                y_copy(b0 + i, part).start()

    @pl.when(n_blk > 0)
    def _():
        for cp in w_copies(e):
            cp.wait()
        slot = e % W_SLOTS
        wi_scr[...] = wibuf[slot].astype(BF16)
        wo_scr[...] = wobuf[slot].astype(BF16)

        def full_group(g, carry):
            step(g * ROW_GROUP, ROW_GROUP)
            return carry

        lax.fori_loop(0, n_blk // ROW_GROUP, full_group, 0)
        done = n_blk // ROW_GROUP * ROW_GROUP
        group = ROW_GROUP // 2
        while group >= 1:
            @pl.when((n_blk & group) != 0)
            def _(group=group, done=done):
                step(done, group)
            done = done + (n_blk & group)
            group //= 2

    @pl.when(e == n_exp - 1)
    def _():
        for back in range(1, ROW_SLOTS + 1):
            @pl.when(total >= back)
            def _():
                for part in range(PACK_PARTS):
                    y_copy(total - back, part).wait()


def _experts(start_blk, n_blk, n_rows, xs, w_in_e, w_out_e):
    p = xs[0].shape[0]
    n_exp = w_in_e.shape[0]
    any_spec = pl.BlockSpec(memory_space=pl.ANY)
    row_buf = pltpu.VMEM((ROW_SLOTS, MOE_BLOCK, PACK_WIDTH), I32)
    grid_spec = pltpu.PrefetchScalarGridSpec(
        num_scalar_prefetch=3,
        grid=(n_exp,),
        in_specs=[any_spec, any_spec, any_spec, any_spec],
        out_specs=(any_spec, any_spec),
        scratch_shapes=[row_buf, row_buf, row_buf, row_buf,
                        pltpu.VMEM((W_SLOTS, D_MODEL, 2 * EXPERT_DIM), F32),
                        pltpu.VMEM((W_SLOTS, EXPERT_DIM, D_MODEL), F32),
                        pltpu.SemaphoreType.DMA((PACK_PARTS, ROW_SLOTS)),
                        pltpu.SemaphoreType.DMA((PACK_PARTS, ROW_SLOTS)),
                        pltpu.SemaphoreType.DMA((2, W_SLOTS)),
                        pltpu.VMEM((D_MODEL, 2 * EXPERT_DIM), BF16),
                        pltpu.VMEM((EXPERT_DIM, D_MODEL), BF16)],
    )
    return pl.pallas_call(
        _expert_kernel,
        out_shape=(jax.ShapeDtypeStruct((p, PACK_WIDTH), I32),) * PACK_PARTS,
        grid_spec=grid_spec,
        compiler_params=_params(("arbitrary",)),
        name="routed_experts",
    )(start_blk, n_blk, n_rows, xs[0], xs[1], w_in_e, w_out_e)


def _combine_kernel(h_ref, yg0_ref, yg1_ref, gate_ref, wsi_ref, wso_ref, g2_ref, b2_ref, o_ref):
    h1 = h_ref[...]
    gate = gate_ref[...]
    routed = None
    for k in range(TOP_K):
        y_k = _unpack_rows([yg0_ref[k], yg1_ref[k]]) * gate[:, k:k + 1]
        routed = y_k if routed is None else routed + y_k
    hs = _dot(h1.astype(BF16), wsi_ref[...])
    sg = hs[:, :SHARED_DIM]
    sv = hs[:, SHARED_DIM:]
    shared = _dot((sg * _sigmoid(sg) * sv).astype(BF16), wso_ref[...])
    o_ref[...] = _layer_norm(DEEPNORM_ALPHA * h1 + routed + shared, g2_ref[...], b2_ref[...])


def _combine(h1, yg, gate_nk, wsi, wso, g2, b2):
    n = h1.shape[0]
    tm = OUT_ROW_TILE
    yg_spec = pl.BlockSpec((TOP_K, tm, PACK_WIDTH), lambda i: (0, i, 0))

    def full(shape):
        return pl.BlockSpec(shape, lambda i: (0,) * len(shape))

    return pl.pallas_call(
        _combine_kernel,
        out_shape=jax.ShapeDtypeStruct((n, D_MODEL), F32),
        grid=(n // tm,),
        in_specs=[
            pl.BlockSpec((tm, D_MODEL), lambda i: (i, 0)),
            yg_spec, yg_spec,
            pl.BlockSpec((tm, TOP_K), lambda i: (i, 0)),
            full((D_MODEL, 2 * SHARED_DIM)), full((SHARED_DIM, D_MODEL)),
            full((1, D_MODEL)), full((1, D_MODEL)),
        ],
        out_specs=pl.BlockSpec((tm, D_MODEL), lambda i: (i, 0)),
        compiler_params=_params(("parallel",)),
        name="shared_combine",
    )(h1, yg[0], yg[1], gate_nk, wsi, wso, g2, b2)


ROUTE_TILE = 512


def _pick_first_max(vals, iota, axis, size):
    m = jnp.max(vals, axis=axis, keepdims=True)
    idx = jnp.min(jnp.where(vals == m, iota, size), axis=axis, keepdims=True)
    return m, idx


def _route_kernel(lg_ref, bias_ref, idx_ref, gate_ref, rank_ref, cnt_ref, carry_scr):
    e, tn = lg_ref.shape
    groups = N_EXPERT_GROUPS
    gsz = e // groups
    neg = -jnp.inf

    @pl.when(pl.program_id(0) == 0)
    def _():
        carry_scr[...] = jnp.zeros_like(carry_scr)

    scores = _sigmoid(lg_ref[...])
    biased = scores + bias_ref[...]

    b3 = biased.reshape(groups, gsz, tn)
    io3 = lax.broadcasted_iota(I32, b3.shape, 1)
    m1, i1 = _pick_first_max(b3, io3, 1, gsz)
    m2 = jnp.max(jnp.where(io3 == i1, neg, b3), axis=1, keepdims=True)
    grp = m1 + m2
    iog = lax.broadcasted_iota(I32, grp.shape, 0)
    keep = jnp.zeros(grp.shape, F32)
    for _ in range(TOPK_GROUPS):
        _, gi = _pick_first_max(grp, iog, 0, groups)
        hit = iog == gi
        keep = jnp.where(hit, 1.0, keep)
        grp = jnp.where(hit, neg, grp)
    masked = jnp.where(keep > 0.0, b3, neg).reshape(e, tn)

    ioe = lax.broadcasted_iota(I32, (e, tn), 0)
    onehot = jnp.zeros((e, tn), F32)
    idxs, gates = [], []
    for _ in range(TOP_K):
        _, ei = _pick_first_max(masked, ioe, 0, e)
        hit = ioe == ei
        gates.append(jnp.sum(jnp.where(hit, scores, 0.0), axis=0, keepdims=True))
        onehot = jnp.where(hit, 1.0, onehot)
        masked = jnp.where(hit, neg, masked)
        idxs.append(ei)
    gsum = gates[0]
    for g in gates[1:]:
        gsum = gsum + g
    idx_ref[...] = jnp.concatenate(idxs, axis=0)
    gate_ref[...] = jnp.concatenate(gates, axis=0) / gsum * ROUTED_SCALE

    t_r = lax.broadcasted_iota(I32, (tn, tn), 0)
    t_c = lax.broadcasted_iota(I32, (tn, tn), 1)
    earlier = jnp.where(t_r < t_c, 1.0, 0.0).astype(BF16)
    oh = onehot.astype(BF16)
    base = carry_scr[...]
    before = _dot(oh, earlier) + jnp.concatenate([base] * (tn // LANES), axis=1)
    ranks = [jnp.sum(jnp.where(ioe == ei, before, 0.0), axis=0, keepdims=True) for ei in idxs]
    rank_ref[...] = jnp.concatenate(ranks, axis=0).astype(I32)
    total = base + _dot(oh, jnp.ones((tn, LANES), BF16))
    carry_scr[...] = total
    cnt_ref[...] = total


def _route(logits_t, bias_col):
    e, n = logits_t.shape
    tn = ROUTE_TILE
    tok_spec = pl.BlockSpec((TOP_K, tn), lambda i: (0, i))
    return pl.pallas_call(
        _route_kernel,
        out_shape=(jax.ShapeDtypeStruct((TOP_K, n), I32),
                   jax.ShapeDtypeStruct((TOP_K, n), F32),
                   jax.ShapeDtypeStruct((TOP_K, n), I32),
                   jax.ShapeDtypeStruct((e, LANES), F32)),
        grid=(n // tn,),
        in_specs=[pl.BlockSpec((e, tn), lambda i: (0, i)),
                  pl.BlockSpec((e, 1), lambda i: (0, 0))],
        out_specs=(tok_spec, tok_spec, tok_spec, pl.BlockSpec((e, LANES), lambda i: (0, 0))),
        scratch_shapes=[pltpu.VMEM((e, LANES), F32)],
        compiler_params=_params(("arbitrary",)),
        name="router_topk",
    )(logits_t, bias_col)


def _plan_kernel(idx_ref, rank_ref, cnt_ref, dest_ref, sblk_ref):
    e = cnt_ref.shape[0]
    tn = idx_ref.shape[1]
    cnt = cnt_ref[...]
    nblk = jnp.floor((cnt + (MOE_BLOCK - 1)) * (1.0 / MOE_BLOCK))
    e_r = lax.broadcasted_iota(I32, (e, e), 0)
    e_c = lax.broadcasted_iota(I32, (e, e), 1)
    lower = jnp.where(e_c < e_r, 1.0, 0.0).astype(BF16)
    start_blk = _dot(lower, nblk.astype(BF16))

    ioe = lax.broadcasted_iota(I32, (e, tn), 0)
    start_row = jnp.concatenate([start_blk * MOE_BLOCK] * (tn // LANES), axis=1)
    idx = idx_ref[...]
    dests = [jnp.sum(jnp.where(ioe == idx[k:k + 1], start_row, 0.0), axis=0, keepdims=True)
             for k in range(TOP_K)]
    dest_ref[...] = jnp.concatenate(dests, axis=0).astype(I32) + rank_ref[...]
    sblk_ref[...] = start_blk.astype(I32)


def _plan(idx_t, rank_t, counts):
    k, n = idx_t.shape
    e = counts.shape[0]
    tn = ROUTE_TILE
    tok_spec = pl.BlockSpec((k, tn), lambda i: (0, i))
    exp_spec = pl.BlockSpec((e, LANES), lambda i: (0, 0))
    return pl.pallas_call(
        _plan_kernel,
        out_shape=(jax.ShapeDtypeStruct((k, n), I32), jax.ShapeDtypeStruct((e, LANES), I32)),
        grid=(n // tn,),
        in_specs=[tok_spec, tok_spec, exp_spec],
        out_specs=(tok_spec, exp_spec),
        compiler_params=_params(("arbitrary",)),
        name="dispatch_plan",
    )(idx_t, rank_t, counts)


SC_WINDOW = 128


def _sc_mesh():
    return plsc.VectorSubcoreMesh(core_axis_name="core", subcore_axis_name="subcore")


def _dispatch_rows(parts, dest, p_total):
    n, width = parts[0].shape
    top_k = dest.shape[0]
    out_type = (jax.ShapeDtypeStruct((p_total, width), parts[0].dtype),) * len(parts)

    @functools.partial(pl.kernel, mesh=_sc_mesh(), scratch_types=[], out_type=out_type,
                       name="dispatch_rows")
    def scatter(*refs):
        x_hbms = refs[:len(parts)]
        i_hbm = refs[len(parts)]
        o_hbms = refs[len(parts) + 1:]
        for x_hbm, o_hbm in zip(x_hbms, o_hbms):
            def body(x_vmem, i_vmem, o_hbm=o_hbm):
                for k in range(top_k):
                    pltpu.sync_copy(x_vmem, o_hbm.at[i_vmem.at[k]])

            pltpu.emit_pipeline(
                body,
                grid=(n // SC_WINDOW,),
                in_specs=[pl.BlockSpec((SC_WINDOW, width), lambda i: (i, 0)),
                          pl.BlockSpec((top_k, SC_WINDOW), lambda i: (0, i))],
                out_specs=[],
                core_axis_name=("core", "subcore"),
                dimension_semantics=(pltpu.PARALLEL,),
            )(x_hbm, i_hbm)

    return scatter(*parts, dest)


def _gather_rows(parts, idx_flat):
    count = idx_flat.shape[1]
    width = parts[0].shape[1]
    out_type = (jax.ShapeDtypeStruct((count, width), parts[0].dtype),) * len(parts)

    @functools.partial(pl.kernel, mesh=_sc_mesh(), scratch_types=[], out_type=out_type,
                       name="combine_rows")
    def gather(*refs):
        y_hbms = refs[:len(parts)]
        i_hbm = refs[len(parts)]
        o_hbms = refs[len(parts) + 1:]
        for y_hbm, o_hbm in zip(y_hbms, o_hbms):
            def body(i_vmem, o_vmem, y_hbm=y_hbm):
                pltpu.sync_copy(y_hbm.at[i_vmem.at[0]], o_vmem)

            pltpu.emit_pipeline(
                body,
                grid=(count // SC_WINDOW,),
                in_specs=[pl.BlockSpec((1, SC_WINDOW), lambda i: (0, i))],
                out_specs=[pl.BlockSpec((SC_WINDOW, width), lambda i: (i, 0))],
                core_axis_name=("core", "subcore"),
                dimension_semantics=(pltpu.PARALLEL,),
            )(i_hbm, o_hbm)

    return gather(*parts, idx_flat)


def kernel(x, ln_in_g, ln_in_b, w_in, hgrn_lb_logits, hgrn_norm_w, w_branch_att, w_branch_hgrn,
           w_out, ln1_g, ln1_b, router_w, router_bias, expert_w_in, expert_w_out, shared_w_in,
           shared_w_out, ln2_g, ln2_b):
    batch, seq, d = x.shape
    n = batch * seq
    x2 = x.reshape(n, d)
    row = lambda v: v.reshape(1, -1).astype(F32)

    lower_bounds = jnp.cumsum(jax.nn.softmax(hgrn_lb_logits.astype(F32), axis=0), axis=0)
    l = 0
    w_l = w_in[l]
    w_qkv = w_l[:, :3 * ATT_WIDTH].astype(BF16)
    w_main = w_l[:, 3 * ATT_WIDTH:].astype(BF16)
    gin, bin_ = row(ln_in_g), row(ln_in_b)

    proj = _inproj_main(x2, gin, bin_, w_main)
    qkv0, qkv1, qkv2 = _inproj_qkv(x2, gin, bin_, w_qkv, batch, seq)
    qkv0 = qkv0.reshape(3, batch, 1, seq, ATT_GROUP_WIDTH)
    att = []
    for qkv in (qkv0, qkv1, qkv2):
        o_g, lse_g = _attention_group(qkv)
        att.append((o_g, lse_g.transpose(0, 2, 3, 1).reshape(batch, ATT_HEADS, seq)))
    att[0] = (att[0][0].reshape(n, ATT_GROUP_WIDTH), att[0][1])

    ob = _hgrn(proj, row(lower_bounds[l]), row(hgrn_norm_w[l]), batch, seq)

    rw_t = router_w[l].T.astype(F32)
    rwh = rw_t.astype(BF16)
    rwl = (rw_t - rwh.astype(F32)).astype(BF16)
    h1, hp0, hp1, logits_t = _mix(
        x2, gin, bin_, att, ob, proj,
        w_branch_att[l].astype(BF16), w_branch_hgrn[l].astype(BF16), w_out[l].astype(BF16),
        row(ln1_g[l]), row(ln1_b[l]), rwh, rwl, batch, seq)

    idx_t, gate_t, rank_t, counts = _route(logits_t, router_bias[l].reshape(-1, 1).astype(F32))
    p_total = n * TOP_K + N_EXPERTS * MOE_BLOCK
    dest, start_blk = _plan(idx_t, rank_t, counts)
    n_rows = counts[:, 0].astype(I32)
    n_blk = (n_rows + (MOE_BLOCK - 1)) // MOE_BLOCK
    xs = _dispatch_rows((hp0, hp1), dest, p_total)
    ys = _experts(start_blk[:, 0], n_blk, n_rows, xs, expert_w_in[l], expert_w_out[l])
    yg = [g.reshape(TOP_K, n, PACK_WIDTH)
          for g in _gather_rows(ys, dest.reshape(1, TOP_K * n))]
    out = _combine(h1, yg, gate_t.T, shared_w_in[l].astype(BF16), shared_w_out[l].astype(BF16),
                   row(ln2_g[l]), row(ln2_b[l]))
    return out.reshape(batch, seq, d)
```

```python
import functools

import jax
import jax.numpy as jnp
from jax import lax
from jax.experimental import pallas as pl
from jax.experimental.pallas import tpu as pltpu
from jax.experimental.pallas import tpu_sc as plsc

F32 = jnp.float32
BF16 = jnp.bfloat16
U32 = jnp.uint32
I32 = jnp.int32

D_MODEL = 1024
ATT_GROUPS = ((128, 1), (512, 4), (2048, 16))
ATT_HEADS = 8
ATT_HEAD_DIM = 64
ATT_GROUP_WIDTH = ATT_HEADS * ATT_HEAD_DIM
ATT_WIDTH = len(ATT_GROUPS) * ATT_GROUP_WIDTH
ATT_BLOCK = 128
HGRN_HEAD_DIM = 128
HGRN_HEADS = D_MODEL // HGRN_HEAD_DIM
HGRN_WIDTH = HGRN_HEADS * HGRN_HEAD_DIM
HGRN_CHUNK = 32
N_EXPERTS = 256
TOP_K = 8
N_EXPERT_GROUPS = 8
TOPK_GROUPS = 4
EXPERT_DIM = 256
SHARED_DIM = 256
ROUTED_SCALE = 2.5
MOE_BLOCK = 128
LN_EPS = 1e-5
RMS_EPS = 1e-6
DEPTH = 1
DEEPNORM_ALPHA = (2 * DEPTH) ** 0.25

VMEM_LIMIT = 56 * 1024 * 1024
LANES = 128

ROW_TILE = 1024
MAIN_COL_TILE = 1536
MIX_ROW_TILE = 512
OUT_ROW_TILE = 512


def _params(sem, vmem=VMEM_LIMIT):
    return pltpu.CompilerParams(dimension_semantics=sem, vmem_limit_bytes=vmem)


def _layer_norm(x, g, b):
    mu = jnp.mean(x, -1, keepdims=True)
    xc = x - mu
    var = jnp.mean(xc * xc, -1, keepdims=True)
    return xc * lax.rsqrt(var + LN_EPS) * g + b


def _sigmoid(x):
    return 1.0 / (1.0 + jnp.exp(-x))


def _dot(a, b):
    return jnp.dot(a, b, preferred_element_type=F32)


def _dot_nt(a, b):
    return lax.dot_general(a, b, (((1,), (1,)), ((), ())), preferred_element_type=F32)


def _dot_tn(a, b):
    return lax.dot_general(a, b, (((0,), (0,)), ((), ())), preferred_element_type=F32)


def _pack_bf16_pair(lo, hi):
    lo_bits = pltpu.bitcast(lo.astype(BF16).astype(F32), U32) >> 16
    hi_bits = pltpu.bitcast(hi.astype(BF16).astype(F32), U32) & jnp.uint32(0xFFFF0000)
    return hi_bits | lo_bits


def _unpack_bf16_pair(w):
    lo = pltpu.bitcast(w << 16, F32)
    hi = pltpu.bitcast(w & jnp.uint32(0xFFFF0000), F32)
    return lo, hi


PACK_PARTS = 2
PACK_WIDTH = D_MODEL // 2 // PACK_PARTS


def _pack_rows(v):
    half = D_MODEL // 2
    parts = []
    for j in range(PACK_PARTS):
        lo = v[:, j * PACK_WIDTH:(j + 1) * PACK_WIDTH]
        hi = v[:, half + j * PACK_WIDTH:half + (j + 1) * PACK_WIDTH]
        parts.append(pltpu.bitcast(_pack_bf16_pair(lo, hi), I32))
    return parts


def _unpack_rows(parts):
    pairs = [_unpack_bf16_pair(pltpu.bitcast(p, U32)) for p in parts]
    return jnp.concatenate([lo for lo, _ in pairs] + [hi for _, hi in pairs], axis=-1)


def _inproj_main_kernel(x_ref, g_ref, b_ref, w_ref, o_ref, h_scr):
    @pl.when(pl.program_id(1) == 0)
    def _():
        h_scr[...] = _layer_norm(x_ref[...], g_ref[...], b_ref[...]).astype(BF16)

    o_ref[...] = _dot(h_scr[...], w_ref[...]).astype(BF16)


def _inproj_main(x2, g, b, w_main):
    n = x2.shape[0]
    width = w_main.shape[1]
    return pl.pallas_call(
        _inproj_main_kernel,
        out_shape=jax.ShapeDtypeStruct((n, width), BF16),
        grid=(n // ROW_TILE, width // MAIN_COL_TILE),
        in_specs=[
            pl.BlockSpec((ROW_TILE, D_MODEL), lambda i, j: (i, 0)),
            pl.BlockSpec((1, D_MODEL), lambda i, j: (0, 0)),
            pl.BlockSpec((1, D_MODEL), lambda i, j: (0, 0)),
            pl.BlockSpec((D_MODEL, MAIN_COL_TILE), lambda i, j: (0, j)),
        ],
        out_specs=pl.BlockSpec((ROW_TILE, MAIN_COL_TILE), lambda i, j: (i, j)),
        scratch_shapes=[pltpu.VMEM((ROW_TILE, D_MODEL), BF16)],
        compiler_params=_params(("parallel", "arbitrary")),
        name="inproj_main",
    )(x2, g, b, w_main)


def _inproj_qkv_kernel(x_ref, g_ref, b_ref, w_ref, o0_ref, o1_ref, o2_ref,
                       hf_scr, h0_scr, h1_scr, h2_scr):
    @pl.when(pl.program_id(1) == 0)
    def _():
        hf = _layer_norm(x_ref[...], g_ref[...], b_ref[...])
        h0_scr[...] = hf.astype(BF16)
        for c in range(D_MODEL // LANES):
            hf_scr[c] = hf[:, c * LANES:(c + 1) * LANES]
        for h_scr, (_, dil) in ((h1_scr, ATT_GROUPS[1]), (h2_scr, ATT_GROUPS[2])):
            rows = ROW_TILE // dil
            for r in range(dil):
                for c in range(D_MODEL // LANES):
                    h_scr[r * rows:(r + 1) * rows, c * LANES:(c + 1) * LANES] = (
                        hf_scr[c, pl.ds(r, rows, stride=dil), :].astype(BF16))

    gw = ATT_GROUP_WIDTH
    o0_ref[0] = _dot(h0_scr[...], w_ref[:, 0:gw]).astype(BF16)
    d1 = ATT_GROUPS[1][1]
    o1_ref[0, 0] = _dot(h1_scr[...], w_ref[:, gw:2 * gw]).astype(BF16).reshape(d1, ROW_TILE // d1, gw)
    d2 = ATT_GROUPS[2][1]
    o2_ref[0, 0] = _dot(h2_scr[...], w_ref[:, 2 * gw:3 * gw]).astype(BF16).reshape(d2, ROW_TILE // d2, gw)


def _inproj_qkv(x2, g, b, w_qkv, batch, seq):
    n = x2.shape[0]
    gw = ATT_GROUP_WIDTH
    tiles_per_seq = seq // ROW_TILE
    d1, d2 = ATT_GROUPS[1][1], ATT_GROUPS[2][1]
    out_shape = (
        jax.ShapeDtypeStruct((3, n, gw), BF16),
        jax.ShapeDtypeStruct((3, batch, d1, seq // d1, gw), BF16),
        jax.ShapeDtypeStruct((3, batch, d2, seq // d2, gw), BF16),
    )
    return pl.pallas_call(
        _inproj_qkv_kernel,
        out_shape=out_shape,
        grid=(n // ROW_TILE, 3),
        in_specs=[
            pl.BlockSpec((ROW_TILE, D_MODEL), lambda i, t: (i, 0)),
            pl.BlockSpec((1, D_MODEL), lambda i, t: (0, 0)),
            pl.BlockSpec((1, D_MODEL), lambda i, t: (0, 0)),
            pl.BlockSpec((D_MODEL, ATT_WIDTH), lambda i, t: (0, t)),
        ],
        out_specs=(
            pl.BlockSpec((1, ROW_TILE, gw), lambda i, t: (t, i, 0)),
            pl.BlockSpec((1, 1, d1, ROW_TILE // d1, gw),
                         lambda i, t: (t, i // tiles_per_seq, 0, i % tiles_per_seq, 0)),
            pl.BlockSpec((1, 1, d2, ROW_TILE // d2, gw),
                         lambda i, t: (t, i // tiles_per_seq, 0, i % tiles_per_seq, 0)),
        ),
        scratch_shapes=[
            pltpu.VMEM((D_MODEL // LANES, ROW_TILE, LANES), F32),
            pltpu.VMEM((ROW_TILE, D_MODEL), BF16),
            pltpu.VMEM((ROW_TILE, D_MODEL), BF16),
            pltpu.VMEM((ROW_TILE, D_MODEL), BF16),
        ],
        compiler_params=_params(("parallel", "arbitrary")),
        name="inproj_qkv",
    )(x2, g, b, w_qkv)


ATT_HEADS_PER_MATMUL = 4
ATT_BATCH_PER_STEP = 8


def _attn_block(q_ref, kc_ref, kp_ref, v_ref, o_ref, lse_ref, vt_ref, bias, has_prev):
    blk = ATT_BLOCK
    hd = ATT_HEAD_DIM
    gw = ATT_GROUP_WIDTH
    hpm = ATT_HEADS_PER_MATMUL
    width = hpm * hd

    vt_cur = jnp.concatenate(
        [v_ref[:, c * LANES:(c + 1) * LANES].astype(F32).T.astype(BF16)
         for c in range(gw // LANES)], axis=0)
    vt_prev = jnp.where(has_prev, vt_ref[...], jnp.zeros_like(vt_cur))
    vt_ref[...] = vt_cur

    lane_head = lax.broadcasted_iota(I32, (blk, width), 1) // hd
    per_tile = LANES // hd
    for g in range(ATT_HEADS // hpm):
        feat = slice(g * width, (g + 1) * width)
        q_g = q_ref[:, feat] * (hd ** -0.5)
        q_bd = jnp.concatenate([jnp.where(lane_head == i, q_g, jnp.zeros_like(q_g))
                                for i in range(hpm)], axis=0)
        k_g = jnp.concatenate([kp_ref[:, feat], kc_ref[:, feat]], axis=0)
        s_t = _dot_nt(k_g, q_bd) + bias
        m = jnp.max(s_t, axis=0, keepdims=True)
        p = jnp.exp(s_t - m)
        l = jnp.sum(p, axis=0, keepdims=True)
        v_t = jnp.concatenate([vt_prev[feat, :], vt_cur[feat, :]], axis=1)
        o_t = _dot(v_t, p.astype(BF16))
        lse = m + jnp.log(l)
        inv_l = 1.0 / l
        for c in range(width // LANES):
            tile = []
            for i in range(c * per_tile, (c + 1) * per_tile):
                cols = slice(i * blk, (i + 1) * blk)
                tile.append(o_t[i * hd:(i + 1) * hd, cols] * inv_l[:, cols])
                lse_ref[g * hpm + i:g * hpm + i + 1, :] = lse[:, cols]
            lanes = slice(g * width + c * LANES, g * width + (c + 1) * LANES)
            o_ref[:, lanes] = jnp.concatenate(tile, axis=0).T.astype(BF16)


def _attn_kernel(q_ref, kc_ref, kp_ref, v_ref, o_ref, lse_ref, vt_scr):
    blk = ATT_BLOCK
    has_prev = pl.program_id(2) > 0
    key_i = lax.broadcasted_iota(I32, (2 * blk, blk), 0)
    qry_i = lax.broadcasted_iota(I32, (2 * blk, blk), 1)
    live = (((key_i < blk) & (key_i >= qry_i) & has_prev)
            | ((key_i >= blk) & (key_i - blk <= qry_i)))
    bias = jnp.concatenate([jnp.where(live, 0.0, -jnp.inf)] * ATT_HEADS_PER_MATMUL, axis=1)
    for j in range(ATT_BATCH_PER_STEP):
        _attn_block(q_ref.at[j], kc_ref.at[j], kp_ref.at[j], v_ref.at[j], o_ref.at[j],
                    lse_ref.at[j], vt_scr.at[j], bias, has_prev)


def _attention_group(qkv):
    _, batch, dil, sub_len, gw = qkv.shape
    blk = ATT_BLOCK
    bps = ATT_BATCH_PER_STEP

    def spec(t, prev):
        def index(b, r, i):
            return (t, b, r, jnp.maximum(i - 1, 0) if prev else i, 0)
        return pl.BlockSpec((None, bps, None, blk, gw), index)

    return pl.pallas_call(
        _attn_kernel,
        out_shape=(jax.ShapeDtypeStruct((batch, dil, sub_len, gw), BF16),
                   jax.ShapeDtypeStruct((batch, dil, ATT_HEADS, sub_len), F32)),
        grid=(batch // bps, dil, sub_len // blk),
        in_specs=[spec(0, False), spec(1, False), spec(1, True), spec(2, False)],
        out_specs=(pl.BlockSpec((bps, None, blk, gw), lambda b, r, i: (b, r, i, 0)),
                   pl.BlockSpec((bps, None, ATT_HEADS, blk), lambda b, r, i: (b, r, 0, i))),
        scratch_shapes=[pltpu.VMEM((bps, gw, blk), BF16)],
        compiler_params=_params(("parallel", "parallel", "arbitrary")),
        name=f"dilated_attention_d{dil}",
    )(qkv, qkv, qkv, qkv)


HGRN_ROWS = 256
HGRN_HEADS_PER_STEP = 4


def _hgrn_rows(q, z, v, gate, st, lb, nw, tri, same_chunk_causal, chunk_mask):
    c = HGRN_CHUNK
    dk = HGRN_HEAD_DIM
    rows = HGRN_ROWS
    nchunk = rows // c
    one_m_lb = 1.0 - lb
    z = z.astype(F32)
    log_f = jnp.log(lb + one_m_lb * _sigmoid(z))
    key = one_m_lb * _sigmoid(-z)
    p0 = log_f.astype(BF16)
    p1 = (log_f - p0.astype(F32)).astype(BF16)
    bcum = _dot(tri, p0) + _dot(tri, p1)

    b3 = bcum.reshape(nchunk, c, dk)
    b_mid = b3[:, c // 2:c // 2 + 1, :]
    b_last = b3[:, c - 1:c, :]
    q3 = q.astype(F32).reshape(nchunk, c, dk) * (dk ** -0.5)
    q_t3 = q3 * jnp.exp(b3 - b_mid)
    k_t3 = key.reshape(nchunk, c, dk) * jnp.exp(b_mid - b3)
    q_t = q_t3.reshape(rows, dk).astype(BF16)
    k_t = k_t3.reshape(rows, dk).astype(BF16)
    q_in = (q_t3 * jnp.exp(b_mid)).reshape(rows, dk).astype(BF16)
    k_st = (k_t3 * jnp.exp(b_last - b_mid)).reshape(rows, dk).astype(BF16)

    att = jnp.where(same_chunk_causal, _dot_nt(q_t, k_t), 0.0)
    o = _dot(att.astype(BF16), v)

    k_spread = jnp.concatenate([k_st * chunk_mask[j] for j in range(nchunk)], axis=1)
    incr = _dot_tn(v, k_spread)
    decay = jnp.exp(b_last.reshape(nchunk, dk))
    before = []
    for j in range(nchunk):
        before.append(st.astype(BF16))
        st = st * decay[j:j + 1] + incr[:, j * dk:(j + 1) * dk]
    q_spread = jnp.concatenate([q_in * chunk_mask[j] for j in range(nchunk)], axis=1)
    o = o + _dot_nt(q_spread, jnp.concatenate(before, axis=1))

    o = o * lax.rsqrt(jnp.mean(o * o, -1, keepdims=True) + RMS_EPS) * nw
    gate = gate.astype(F32)
    return o * gate * _sigmoid(gate), st


def _hgrn_kernel(q_ref, f_ref, i_ref, g_ref, lb_ref, nw_ref, o_ref, state_scr, mask_scr):
    seq = q_ref.shape[0]
    c = HGRN_CHUNK
    rows = HGRN_ROWS
    dk = HGRN_HEAD_DIM
    nw = nw_ref[...]

    r_i = lax.broadcasted_iota(I32, (rows, rows), 0)
    c_i = lax.broadcasted_iota(I32, (rows, rows), 1)
    same_chunk_causal = (r_i // c == c_i // c) & (c_i <= r_i)
    tri = jnp.where(same_chunk_causal, 1.0, 0.0).astype(BF16)
    row_chunk = lax.broadcasted_iota(I32, (rows, dk), 0) // c
    for j in range(rows // c):
        mask_scr[j] = jnp.where(row_chunk == j, 1.0, 0.0).astype(BF16)

    state_scr[...] = jnp.zeros_like(state_scr)

    def body(gi, carry):
        rs = pl.ds(pl.multiple_of(gi * rows, rows), rows)
        for h in range(HGRN_HEADS_PER_STEP):
            cs = slice(h * dk, (h + 1) * dk)
            o, st = _hgrn_rows(q_ref[rs, cs], f_ref[rs, cs], i_ref[rs, cs], g_ref[rs, cs],
                               state_scr[h], lb_ref[:, cs], nw, tri, same_chunk_causal, mask_scr)
            state_scr[h] = st
            o_ref[rs, cs] = o.astype(BF16)
        return carry

    lax.fori_loop(0, seq // rows, body, 0)


def _hgrn(proj, lower_bound, norm_w, batch, seq):
    n = proj.shape[0]
    dk = HGRN_HEAD_DIM
    width = HGRN_HEADS_PER_STEP * dk
    steps = HGRN_HEADS // HGRN_HEADS_PER_STEP

    def seg(k):
        return pl.BlockSpec((seq, width), lambda b, h: (b, k * steps + h))

    return pl.pallas_call(
        _hgrn_kernel,
        out_shape=jax.ShapeDtypeStruct((n, HGRN_WIDTH), BF16),
        grid=(batch, steps),
        in_specs=[seg(0), seg(1), seg(2), seg(3),
                  pl.BlockSpec((1, width), lambda b, h: (0, h)),
                  pl.BlockSpec((1, dk), lambda b, h: (0, 0))],
        out_specs=pl.BlockSpec((seq, width), lambda b, h: (b, h)),
        scratch_shapes=[pltpu.VMEM((HGRN_HEADS_PER_STEP, dk, dk), F32),
                        pltpu.VMEM((HGRN_ROWS // HGRN_CHUNK, HGRN_ROWS, dk), BF16)],
        compiler_params=_params(("parallel", "parallel")),
        name="hgrn2",
    )(proj, proj, proj, proj, lower_bound, norm_w)


def _pick_first_max(vals, iota, axis, size):
    m = jnp.max(vals, axis=axis, keepdims=True)
    idx = jnp.min(jnp.where(vals == m, iota, size), axis=axis, keepdims=True)
    return m, idx


def _route_tile(logits, bias, base):
    e, tn = logits.shape
    groups = N_EXPERT_GROUPS
    gsz = e // groups
    neg = -jnp.inf

    scores = _sigmoid(logits)
    biased = scores + bias

    b3 = biased.reshape(groups, gsz, tn)
    io3 = lax.broadcasted_iota(I32, b3.shape, 1)
    m1, i1 = _pick_first_max(b3, io3, 1, gsz)
    m2 = jnp.max(jnp.where(io3 == i1, neg, b3), axis=1, keepdims=True)
    grp = m1 + m2
    iog = lax.broadcasted_iota(I32, grp.shape, 0)
    keep = jnp.zeros(grp.shape, F32)
    for _ in range(TOPK_GROUPS):
        _, gi = _pick_first_max(grp, iog, 0, groups)
        hit = iog == gi
        keep = jnp.where(hit, 1.0, keep)
        grp = jnp.where(hit, neg, grp)
    masked = jnp.where(keep > 0.0, b3, neg).reshape(e, tn)

    ioe = lax.broadcasted_iota(I32, (e, tn), 0)
    onehot = jnp.zeros((e, tn), F32)
    idxs, gates = [], []
    for _ in range(TOP_K):
        _, ei = _pick_first_max(masked, ioe, 0, e)
        hit = ioe == ei
        gates.append(jnp.sum(jnp.where(hit, scores, 0.0), axis=0, keepdims=True))
        onehot = jnp.where(hit, 1.0, onehot)
        masked = jnp.where(hit, neg, masked)
        idxs.append(ei)
    gsum = gates[0]
    for g in gates[1:]:
        gsum = gsum + g
    gate = jnp.concatenate(gates, axis=0) / gsum * ROUTED_SCALE

    t_r = lax.broadcasted_iota(I32, (tn, tn), 0)
    t_c = lax.broadcasted_iota(I32, (tn, tn), 1)
    earlier = jnp.where(t_r < t_c, 1.0, 0.0).astype(BF16)
    oh = onehot.astype(BF16)
    before = _dot(oh, earlier) + jnp.concatenate([base] * (tn // LANES), axis=1)
    ranks = [jnp.sum(jnp.where(ioe == ei, before, 0.0), axis=0, keepdims=True) for ei in idxs]
    total = base + _dot(oh, jnp.ones((tn, LANES), BF16))
    return (jnp.concatenate(idxs, axis=0), gate,
            jnp.concatenate(ranks, axis=0).astype(I32), total)


def _mix_kernel(x_ref, gin_ref, bin_ref, o0_ref, o1_ref, o2_ref, l0_ref, l1_ref, l2_ref,
                ob_ref, ga_ref, gb_ref, wa_ref, wb_ref, wo_ref, g1_ref, b1_ref,
                rwh_ref, rwl_ref, rbias_ref,
                h_ref, hp0_ref, hp1_ref, idx_ref, gate_ref, rank_ref, cnt_ref,
                so1_scr, so2_scr, lg_scr, carry_scr):
    tm = x_ref.shape[0]
    gw = ATT_GROUP_WIDTH
    step = pl.program_id(0)

    @pl.when(step == 0)
    def _():
        lg_scr[...] = jnp.zeros_like(lg_scr)
        carry_scr[...] = jnp.zeros_like(carry_scr)

    counts = carry_scr[...]
    idx, gate, rank, total = _route_tile(lg_scr[...], rbias_ref[...], counts)
    idx_ref[...] = idx
    gate_ref[...] = gate
    rank_ref[...] = rank
    counts = jnp.where(step > 0, total, counts)
    carry_scr[...] = counts
    cnt_ref[...] = counts

    for o_ref, so_scr, (_, dil) in ((o1_ref, so1_scr, ATT_GROUPS[1]),
                                    (o2_ref, so2_scr, ATT_GROUPS[2])):
        for r in range(dil):
            o_r = o_ref[r].astype(F32)
            for c in range(gw // LANES):
                so_scr[c, pl.ds(r, tm // dil, stride=dil), :] = o_r[:, c * LANES:(c + 1) * LANES]

    def natural(scr):
        return jnp.concatenate([scr[c] for c in range(gw // LANES)], axis=-1)

    l0, l1, l2 = l0_ref[...], l1_ref[...], l2_ref[...]
    m = jnp.maximum(jnp.maximum(l0, l1), l2)
    e0 = jnp.exp(l0 - m)
    e1 = jnp.exp(l1 - m)
    e2 = jnp.exp(l2 - m)
    inv = 1.0 / (e0 + e1 + e2)
    terms = []
    for w in (e0 * inv, e1 * inv, e2 * inv):
        hi = w.astype(BF16).astype(F32)
        terms += [hi, w - hi]
    terms = jnp.concatenate(terms, axis=0).astype(BF16)
    n_groups = len(ATT_GROUPS)
    t_i = lax.broadcasted_iota(I32, (n_groups * 2 * ATT_HEADS, n_groups * gw), 0)
    c_i = lax.broadcasted_iota(I32, (n_groups * 2 * ATT_HEADS, n_groups * gw), 1)
    spread = jnp.where((t_i // (2 * ATT_HEADS) == c_i // gw)
                       & (t_i % ATT_HEADS == c_i % gw // ATT_HEAD_DIM), 1.0, 0.0).astype(BF16)
    wide = _dot_tn(terms, spread)
    o_att = (wide[:, :gw] * o0_ref[...].astype(F32) + wide[:, gw:2 * gw] * natural(so1_scr)
             + wide[:, 2 * gw:] * natural(so2_scr))

    y_a = _dot(o_att.astype(BF16), wa_ref[...])
    y_b = _dot(ob_ref[...], wb_ref[...])
    merged = _sigmoid(ga_ref[...].astype(F32)) * y_a + _sigmoid(gb_ref[...].astype(F32)) * y_b
    mix = _dot(merged.astype(BF16), wo_ref[...])
    h_in = _layer_norm(x_ref[...], gin_ref[...], bin_ref[...])
    h1 = _layer_norm(DEEPNORM_ALPHA * h_in + mix, g1_ref[...], b1_ref[...])
    h_ref[...] = h1

    hp0_ref[...], hp1_ref[...] = _pack_rows(h1)

    h_hi = h1.astype(BF16)
    h_lo = (h1 - h_hi.astype(F32)).astype(BF16)
    rwh = rwh_ref[...]
    lg_scr[...] = _dot_nt(rwh, h_hi) + _dot_nt(rwh, h_lo) + _dot_nt(rwl_ref[...], h_hi)


def _mix(x2, gin, bin_, att, ob, proj, wa, wb, wo, g1, b1, rwh, rwl, rbias, batch, seq):
    n = x2.shape[0]
    tm = MIX_ROW_TILE
    gw = ATT_GROUP_WIDTH
    tiles_per_seq = seq // tm
    last = n // tm - 1
    (o0, l0), (o1, l1), (o2, l2) = att
    d1, d2 = ATT_GROUPS[1][1], ATT_GROUPS[2][1]

    def tile(i):
        return jnp.minimum(i, last)

    def routed(i):
        return jnp.maximum(i - 1, 0)

    lse_spec = pl.BlockSpec((None, ATT_HEADS, tm),
                            lambda i: (tile(i) // tiles_per_seq, 0, tile(i) % tiles_per_seq))

    def full(shape):
        return pl.BlockSpec(shape, lambda i: (0,) * len(shape))

    def dil_spec(dil):
        return pl.BlockSpec((None, dil, tm // dil, gw),
                            lambda i: (tile(i) // tiles_per_seq, 0, tile(i) % tiles_per_seq, 0))

    def rows(width, col=0):
        return pl.BlockSpec((tm, width), lambda i: (tile(i), col))

    tok_spec = pl.BlockSpec((TOP_K, tm), lambda i: (0, routed(i)))
    return pl.pallas_call(
        _mix_kernel,
        out_shape=(jax.ShapeDtypeStruct((n, D_MODEL), F32),
                   jax.ShapeDtypeStruct((n, PACK_WIDTH), I32),
                   jax.ShapeDtypeStruct((n, PACK_WIDTH), I32),
                   jax.ShapeDtypeStruct((TOP_K, n), I32),
                   jax.ShapeDtypeStruct((TOP_K, n), F32),
                   jax.ShapeDtypeStruct((TOP_K, n), I32),
                   jax.ShapeDtypeStruct((N_EXPERTS, LANES), F32)),
        grid=(n // tm + 1,),
        in_specs=[
            rows(D_MODEL),
            full((1, D_MODEL)), full((1, D_MODEL)),
            rows(gw), dil_spec(d1), dil_spec(d2),
            lse_spec, lse_spec, lse_spec,
            rows(HGRN_WIDTH), rows(D_MODEL, 4), rows(D_MODEL, 5),
            full((gw, D_MODEL)), full((HGRN_WIDTH, D_MODEL)), full((D_MODEL, D_MODEL)),
            full((1, D_MODEL)), full((1, D_MODEL)),
            full((N_EXPERTS, D_MODEL)), full((N_EXPERTS, D_MODEL)), full((N_EXPERTS, 1)),
        ],
        out_specs=(rows(D_MODEL), rows(PACK_WIDTH), rows(PACK_WIDTH),
                   tok_spec, tok_spec, tok_spec, full((N_EXPERTS, LANES))),
        scratch_shapes=[pltpu.VMEM((gw // LANES, tm, LANES), F32),
                        pltpu.VMEM((gw // LANES, tm, LANES), F32),
                        pltpu.VMEM((N_EXPERTS, tm), F32),
                        pltpu.VMEM((N_EXPERTS, LANES), F32)],
        compiler_params=_params(("arbitrary",)),
        name="branch_mix_route",
    )(x2, gin, bin_, o0, o1, o2, l0, l1, l2, ob, proj, proj, wa, wb, wo, g1, b1, rwh, rwl, rbias)


ROW_SLOTS = 16
ROW_GROUP = 4
ROW_AHEAD = ROW_SLOTS - ROW_GROUP
W_SLOTS = 3


def _expert_kernel(sblk_ref, nblk_ref, cnt_ref, xs0_hbm, xs1_hbm, wi_hbm, wo_hbm,
                   ys0_hbm, ys1_hbm, xbuf0, xbuf1, ybuf0, ybuf1, wibuf, wobuf,
                   in_sem, out_sem, w_sem, wi_scr, wo_scr):
    e = pl.program_id(0)
    n_exp = pl.num_programs(0)
    first_blk = sblk_ref[e]
    n_blk = nblk_ref[e]
    n_rows = cnt_ref[e]
    total = sblk_ref[n_exp - 1] + nblk_ref[n_exp - 1]
    xs_hbm, ys_hbm = (xs0_hbm, xs1_hbm), (ys0_hbm, ys1_hbm)
    xbuf, ybuf = (xbuf0, xbuf1), (ybuf0, ybuf1)

    def rows_of(b):
        return pl.ds(pl.multiple_of(b * MOE_BLOCK, MOE_BLOCK), MOE_BLOCK)

    def x_copy(b, part):
        slot = b % ROW_SLOTS
        return pltpu.make_async_copy(xs_hbm[part].at[rows_of(b)], xbuf[part].at[slot],
                                     in_sem.at[part, slot])

    def y_copy(b, part):
        slot = b % ROW_SLOTS
        return pltpu.make_async_copy(ybuf[part].at[slot], ys_hbm[part].at[rows_of(b)],
                                     out_sem.at[part, slot])

    def w_copies(ex):
        slot = ex % W_SLOTS
        return (pltpu.make_async_copy(wi_hbm.at[ex], wibuf.at[slot], w_sem.at[0, slot]),
                pltpu.make_async_copy(wo_hbm.at[ex], wobuf.at[slot], w_sem.at[1, slot]))

    def start_weights(ex):
        @pl.when((ex < n_exp) & (nblk_ref[jnp.minimum(ex, n_exp - 1)] > 0))
        def _():
            for cp in w_copies(ex):
                cp.start()

    @pl.when(e == 0)
    def _():
        for b in range(ROW_AHEAD):
            @pl.when(b < total)
            def _():
                for part in range(PACK_PARTS):
                    x_copy(b, part).start()
        for ex in range(W_SLOTS - 1):
            start_weights(ex)

    start_weights(e + (W_SLOTS - 1))

    def step(j, group):
        b0 = first_blk + j
        for i in range(group):
            for part in range(PACK_PARTS):
                x_copy(b0 + i, part).wait()
        for i in range(group):
            @pl.when(b0 + ROW_AHEAD + i < total)
            def _():
                for part in range(PACK_PARTS):
                    x_copy(b0 + ROW_AHEAD + i, part).start()
        for i in range(group):
            @pl.when(b0 + i >= ROW_SLOTS)
            def _():
                for part in range(PACK_PARTS):
                    y_copy(b0 + i - ROW_SLOTS, part).wait()

        rows = group * MOE_BLOCK
        live = (lax.broadcasted_iota(I32, (rows, PACK_WIDTH), 0)
                < n_rows - j * MOE_BLOCK)
        parts = [jnp.concatenate([xbuf[part][(b0 + i) % ROW_SLOTS] for i in range(group)], axis=0)
                 for part in range(PACK_PARTS)]
        x = _unpack_rows([jnp.where(live, p, 0) for p in parts])
        hu = _dot(x.astype(BF16), wi_scr[...])
        hg = hu[:, :EXPERT_DIM]
        hv = hu[:, EXPERT_DIM:]
        act = (hg * _sigmoid(hg) * hv).astype(BF16)
        packed = _pack_rows(_dot(act, wo_scr[...]))
        for i in range(group):
            for part in range(PACK_PARTS):
                ybuf[part][(b0 + i) % ROW_SLOTS] = packed[part][i * MOE_BLOCK:(i + 1) * MOE_BLOCK]
                y_copy(b0 + i, part).start()

    @pl.when(n_blk > 0)
    def _():
        for cp in w_copies(e):
            cp.wait()
        slot = e % W_SLOTS
        wi_scr[...] = wibuf[slot].astype(BF16)
        wo_scr[...] = wobuf[slot].astype(BF16)

        def full_group(g, carry):
            step(g * ROW_GROUP, ROW_GROUP)
            return carry

        lax.fori_loop(0, n_blk // ROW_GROUP, full_group, 0)
        done = n_blk // ROW_GROUP * ROW_GROUP
        group = ROW_GROUP // 2
        while group >= 1:
            @pl.when((n_blk & group) != 0)
            def _(group=group, done=done):
                step(done, group)
            done = done + (n_blk & group)
            group //= 2

    @pl.when(e == n_exp - 1)
    def _():
        for back in range(1, ROW_SLOTS + 1):
            @pl.when(total >= back)
            def _():
                for part in range(PACK_PARTS):
                    y_copy(total - back, part).wait()


def _experts(start_blk, n_blk, n_rows, xs, w_in_e, w_out_e):
    p = xs[0].shape[0]
    n_exp = w_in_e.shape[0]
    any_spec = pl.BlockSpec(memory_space=pl.ANY)
    row_buf = pltpu.VMEM((ROW_SLOTS, MOE_BLOCK, PACK_WIDTH), I32)
    grid_spec = pltpu.PrefetchScalarGridSpec(
        num_scalar_prefetch=3,
        grid=(n_exp,),
        in_specs=[any_spec, any_spec, any_spec, any_spec],
        out_specs=(any_spec, any_spec),
        scratch_shapes=[row_buf, row_buf, row_buf, row_buf,
                        pltpu.VMEM((W_SLOTS, D_MODEL, 2 * EXPERT_DIM), F32),
                        pltpu.VMEM((W_SLOTS, EXPERT_DIM, D_MODEL), F32),
                        pltpu.SemaphoreType.DMA((PACK_PARTS, ROW_SLOTS)),
                        pltpu.SemaphoreType.DMA((PACK_PARTS, ROW_SLOTS)),
                        pltpu.SemaphoreType.DMA((2, W_SLOTS)),
                        pltpu.VMEM((D_MODEL, 2 * EXPERT_DIM), BF16),
                        pltpu.VMEM((EXPERT_DIM, D_MODEL), BF16)],
    )
    return pl.pallas_call(
        _expert_kernel,
        out_shape=(jax.ShapeDtypeStruct((p, PACK_WIDTH), I32),) * PACK_PARTS,
        grid_spec=grid_spec,
        compiler_params=_params(("arbitrary",)),
        name="routed_experts",
    )(start_blk, n_blk, n_rows, xs[0], xs[1], w_in_e, w_out_e)


def _combine_kernel(h_ref, yg0_ref, yg1_ref, gate_ref, wsi_ref, wso_ref, g2_ref, b2_ref, o_ref):
    h1 = h_ref[...]
    gate = gate_ref[...]
    routed = None
    for k in range(TOP_K):
        y_k = _unpack_rows([yg0_ref[k], yg1_ref[k]]) * gate[:, k:k + 1]
        routed = y_k if routed is None else routed + y_k
    hs = _dot(h1.astype(BF16), wsi_ref[...])
    sg = hs[:, :SHARED_DIM]
    sv = hs[:, SHARED_DIM:]
    shared = _dot((sg * _sigmoid(sg) * sv).astype(BF16), wso_ref[...])
    o_ref[...] = _layer_norm(DEEPNORM_ALPHA * h1 + routed + shared, g2_ref[...], b2_ref[...])


def _combine(h1, yg, gate_nk, wsi, wso, g2, b2):
    n = h1.shape[0]
    tm = OUT_ROW_TILE
    yg_spec = pl.BlockSpec((TOP_K, tm, PACK_WIDTH), lambda i: (0, i, 0))

    def full(shape):
        return pl.BlockSpec(shape, lambda i: (0,) * len(shape))

    return pl.pallas_call(
        _combine_kernel,
        out_shape=jax.ShapeDtypeStruct((n, D_MODEL), F32),
        grid=(n // tm,),
        in_specs=[
            pl.BlockSpec((tm, D_MODEL), lambda i: (i, 0)),
            yg_spec, yg_spec,
            pl.BlockSpec((tm, TOP_K), lambda i: (i, 0)),
            full((D_MODEL, 2 * SHARED_DIM)), full((SHARED_DIM, D_MODEL)),
            full((1, D_MODEL)), full((1, D_MODEL)),
        ],
        out_specs=pl.BlockSpec((tm, D_MODEL), lambda i: (i, 0)),
        compiler_params=_params(("parallel",)),
        name="shared_combine",
    )(h1, yg[0], yg[1], gate_nk, wsi, wso, g2, b2)


def _plan_kernel(idx_ref, rank_ref, cnt_ref, dest_ref, sblk_ref):
    e = cnt_ref.shape[0]
    tn = idx_ref.shape[1]
    cnt = cnt_ref[...]
    nblk = jnp.floor((cnt + (MOE_BLOCK - 1)) * (1.0 / MOE_BLOCK))
    e_r = lax.broadcasted_iota(I32, (e, e), 0)
    e_c = lax.broadcasted_iota(I32, (e, e), 1)
    lower = jnp.where(e_c < e_r, 1.0, 0.0).astype(BF16)
    start_blk = _dot(lower, nblk.astype(BF16))

    ioe = lax.broadcasted_iota(I32, (e, tn), 0)
    start_row = jnp.concatenate([start_blk * MOE_BLOCK] * (tn // LANES), axis=1)
    idx = idx_ref[...]
    dests = [jnp.sum(jnp.where(ioe == idx[k:k + 1], start_row, 0.0), axis=0, keepdims=True)
             for k in range(TOP_K)]
    dest_ref[...] = jnp.concatenate(dests, axis=0).astype(I32) + rank_ref[...]
    sblk_ref[...] = start_blk.astype(I32)


def _plan(idx_t, rank_t, counts):
    k, n = idx_t.shape
    e = counts.shape[0]
    tn = MIX_ROW_TILE
    tok_spec = pl.BlockSpec((k, tn), lambda i: (0, i))
    exp_spec = pl.BlockSpec((e, LANES), lambda i: (0, 0))
    return pl.pallas_call(
        _plan_kernel,
        out_shape=(jax.ShapeDtypeStruct((k, n), I32), jax.ShapeDtypeStruct((e, LANES), I32)),
        grid=(n // tn,),
        in_specs=[tok_spec, tok_spec, exp_spec],
        out_specs=(tok_spec, exp_spec),
        compiler_params=_params(("arbitrary",)),
        name="dispatch_plan",
    )(idx_t, rank_t, counts)


SC_WINDOW = 128


def _sc_mesh():
    return plsc.VectorSubcoreMesh(core_axis_name="core", subcore_axis_name="subcore")


def _dispatch_rows(parts, dest, p_total):
    n, width = parts[0].shape
    top_k = dest.shape[0]
    out_type = (jax.ShapeDtypeStruct((p_total, width), parts[0].dtype),) * len(parts)

    @functools.partial(pl.kernel, mesh=_sc_mesh(), scratch_types=[], out_type=out_type,
                       name="dispatch_rows")
    def scatter(*refs):
        x_hbms = refs[:len(parts)]
        i_hbm = refs[len(parts)]
        o_hbms = refs[len(parts) + 1:]
        for x_hbm, o_hbm in zip(x_hbms, o_hbms):
            def body(x_vmem, i_vmem, o_hbm=o_hbm):
                for k in range(top_k):
                    pltpu.sync_copy(x_vmem, o_hbm.at[i_vmem.at[k]])

            pltpu.emit_pipeline(
                body,
                grid=(n // SC_WINDOW,),
                in_specs=[pl.BlockSpec((SC_WINDOW, width), lambda i: (i, 0)),
                          pl.BlockSpec((top_k, SC_WINDOW), lambda i: (0, i))],
                out_specs=[],
                core_axis_name=("core", "subcore"),
                dimension_semantics=(pltpu.PARALLEL,),
            )(x_hbm, i_hbm)

    return scatter(*parts, dest)


def _gather_rows(parts, idx_flat):
    count = idx_flat.shape[1]
    width = parts[0].shape[1]
    out_type = (jax.ShapeDtypeStruct((count, width), parts[0].dtype),) * len(parts)

    @functools.partial(pl.kernel, mesh=_sc_mesh(), scratch_types=[], out_type=out_type,
                       name="combine_rows")
    def gather(*refs):
        y_hbms = refs[:len(parts)]
        i_hbm = refs[len(parts)]
        o_hbms = refs[len(parts) + 1:]
        for y_hbm, o_hbm in zip(y_hbms, o_hbms):
            def body(i_vmem, o_vmem, y_hbm=y_hbm):
                pltpu.sync_copy(y_hbm.at[i_vmem.at[0]], o_vmem)

            pltpu.emit_pipeline(
                body,
                grid=(count // SC_WINDOW,),
                in_specs=[pl.BlockSpec((1, SC_WINDOW), lambda i: (0, i))],
                out_specs=[pl.BlockSpec((SC_WINDOW, width), lambda i: (i, 0))],
                core_axis_name=("core", "subcore"),
                dimension_semantics=(pltpu.PARALLEL,),
            )(i_hbm, o_hbm)

    return gather(*parts, idx_flat)


def kernel(x, ln_in_g, ln_in_b, w_in, hgrn_lb_logits, hgrn_norm_w, w_branch_att, w_branch_hgrn,
           w_out, ln1_g, ln1_b, router_w, router_bias, expert_w_in, expert_w_out, shared_w_in,
           shared_w_out, ln2_g, ln2_b):
    batch, seq, d = x.shape
    n = batch * seq
    x2 = x.reshape(n, d)
    row = lambda v: v.reshape(1, -1).astype(F32)

    lower_bounds = jnp.cumsum(jax.nn.softmax(hgrn_lb_logits.astype(F32), axis=0), axis=0)
    l = 0
    w_l = w_in[l]
    w_qkv = w_l[:, :3 * ATT_WIDTH].astype(BF16)
    w_main = w_l[:, 3 * ATT_WIDTH:].astype(BF16)
    gin, bin_ = row(ln_in_g), row(ln_in_b)

    proj = _inproj_main(x2, gin, bin_, w_main)
    qkv0, qkv1, qkv2 = _inproj_qkv(x2, gin, bin_, w_qkv, batch, seq)
    qkv0 = qkv0.reshape(3, batch, 1, seq, ATT_GROUP_WIDTH)
    att = []
    for qkv in (qkv0, qkv1, qkv2):
        o_g, lse_g = _attention_group(qkv)
        att.append((o_g, lse_g.transpose(0, 2, 3, 1).reshape(batch, ATT_HEADS, seq)))
    att[0] = (att[0][0].reshape(n, ATT_GROUP_WIDTH), att[0][1])

    ob = _hgrn(proj, row(lower_bounds[l]), row(hgrn_norm_w[l]), batch, seq)

    rw_t = router_w[l].T.astype(F32)
    rwh = rw_t.astype(BF16)
    rwl = (rw_t - rwh.astype(F32)).astype(BF16)
    h1, hp0, hp1, idx_t, gate_t, rank_t, counts = _mix(
        x2, gin, bin_, att, ob, proj,
        w_branch_att[l].astype(BF16), w_branch_hgrn[l].astype(BF16), w_out[l].astype(BF16),
        row(ln1_g[l]), row(ln1_b[l]), rwh, rwl, router_bias[l].reshape(-1, 1).astype(F32),
        batch, seq)
    p_total = n * TOP_K + N_EXPERTS * MOE_BLOCK
    dest, start_blk = _plan(idx_t, rank_t, counts)
    n_rows = counts[:, 0].astype(I32)
    n_blk = (n_rows + (MOE_BLOCK - 1)) // MOE_BLOCK
    xs = _dispatch_rows((hp0, hp1), dest, p_total)
    ys = _experts(start_blk[:, 0], n_blk, n_rows, xs, expert_w_in[l], expert_w_out[l])
    yg = [g.reshape(TOP_K, n, PACK_WIDTH)
          for g in _gather_rows(ys, dest.reshape(1, TOP_K * n))]
    out = _combine(h1, yg, gate_t.T, shared_w_in[l].astype(BF16), shared_w_out[l].astype(BF16),
                   row(ln2_g[l]), row(ln2_b[l]))
    return out.reshape(batch, seq, d)
```

```python
import functools

import jax
import jax.numpy as jnp
from jax import lax
from jax.experimental import pallas as pl
from jax.experimental.pallas import tpu as pltpu
from jax.experimental.pallas import tpu_sc as plsc

F32 = jnp.float32
BF16 = jnp.bfloat16
U32 = jnp.uint32
I32 = jnp.int32

D_MODEL = 1024
ATT_GROUPS = ((128, 1), (512, 4), (2048, 16))
ATT_HEADS = 8
ATT_HEAD_DIM = 64
ATT_GROUP_WIDTH = ATT_HEADS * ATT_HEAD_DIM
ATT_WIDTH = len(ATT_GROUPS) * ATT_GROUP_WIDTH
ATT_BLOCK = 128
HGRN_HEAD_DIM = 128
HGRN_HEADS = D_MODEL // HGRN_HEAD_DIM
HGRN_WIDTH = HGRN_HEADS * HGRN_HEAD_DIM
HGRN_CHUNK = 32
N_EXPERTS = 256
TOP_K = 8
N_EXPERT_GROUPS = 8
TOPK_GROUPS = 4
EXPERT_DIM = 256
SHARED_DIM = 256
ROUTED_SCALE = 2.5
MOE_BLOCK = 128
LN_EPS = 1e-5
RMS_EPS = 1e-6
DEPTH = 1
DEEPNORM_ALPHA = (2 * DEPTH) ** 0.25

VMEM_LIMIT = 56 * 1024 * 1024
LANES = 128

ROW_TILE = 1024
MAIN_COL_TILE = 1536
MIX_ROW_TILE = 512
OUT_ROW_TILE = 512


def _params(sem, vmem=VMEM_LIMIT):
    return pltpu.CompilerParams(dimension_semantics=sem, vmem_limit_bytes=vmem)


def _layer_norm(x, g, b):
    mu = jnp.mean(x, -1, keepdims=True)
    xc = x - mu
    var = jnp.mean(xc * xc, -1, keepdims=True)
    return xc * lax.rsqrt(var + LN_EPS) * g + b


def _sigmoid(x):
    return 1.0 / (1.0 + jnp.exp(-x))


def _dot(a, b):
    return jnp.dot(a, b, preferred_element_type=F32)


def _dot_nt(a, b):
    return lax.dot_general(a, b, (((1,), (1,)), ((), ())), preferred_element_type=F32)


def _dot_tn(a, b):
    return lax.dot_general(a, b, (((0,), (0,)), ((), ())), preferred_element_type=F32)


def _pack_bf16_pair(lo, hi):
    lo_bits = pltpu.bitcast(lo.astype(BF16).astype(F32), U32) >> 16
    hi_bits = pltpu.bitcast(hi.astype(BF16).astype(F32), U32) & jnp.uint32(0xFFFF0000)
    return hi_bits | lo_bits


def _unpack_bf16_pair(w):
    lo = pltpu.bitcast(w << 16, F32)
    hi = pltpu.bitcast(w & jnp.uint32(0xFFFF0000), F32)
    return lo, hi


PACK_PARTS = 2
PACK_WIDTH = D_MODEL // 2 // PACK_PARTS


def _pack_rows(v):
    half = D_MODEL // 2
    parts = []
    for j in range(PACK_PARTS):
        lo = v[:, j * PACK_WIDTH:(j + 1) * PACK_WIDTH]
        hi = v[:, half + j * PACK_WIDTH:half + (j + 1) * PACK_WIDTH]
        parts.append(pltpu.bitcast(_pack_bf16_pair(lo, hi), I32))
    return parts


def _unpack_rows(parts):
    pairs = [_unpack_bf16_pair(pltpu.bitcast(p, U32)) for p in parts]
    return jnp.concatenate([lo for lo, _ in pairs] + [hi for _, hi in pairs], axis=-1)


def _inproj_main_kernel(x_ref, g_ref, b_ref, w_ref, o_ref, h_scr):
    @pl.when(pl.program_id(1) == 0)
    def _():
        h_scr[...] = _layer_norm(x_ref[...], g_ref[...], b_ref[...]).astype(BF16)

    o_ref[...] = _dot(h_scr[...], w_ref[...]).astype(BF16)


def _inproj_main(x2, g, b, w_main):
    n = x2.shape[0]
    width = w_main.shape[1]
    return pl.pallas_call(
        _inproj_main_kernel,
        out_shape=jax.ShapeDtypeStruct((n, width), BF16),
        grid=(n // ROW_TILE, width // MAIN_COL_TILE),
        in_specs=[
            pl.BlockSpec((ROW_TILE, D_MODEL), lambda i, j: (i, 0)),
            pl.BlockSpec((1, D_MODEL), lambda i, j: (0, 0)),
            pl.BlockSpec((1, D_MODEL), lambda i, j: (0, 0)),
            pl.BlockSpec((D_MODEL, MAIN_COL_TILE), lambda i, j: (0, j)),
        ],
        out_specs=pl.BlockSpec((ROW_TILE, MAIN_COL_TILE), lambda i, j: (i, j)),
        scratch_shapes=[pltpu.VMEM((ROW_TILE, D_MODEL), BF16)],
        compiler_params=_params(("parallel", "arbitrary")),
        name="inproj_main",
    )(x2, g, b, w_main)


def _inproj_qkv_kernel(x_ref, g_ref, b_ref, w_ref, o0_ref, o1_ref, o2_ref,
                       hf_scr, h0_scr, h1_scr, h2_scr):
    @pl.when(pl.program_id(1) == 0)
    def _():
        hf = _layer_norm(x_ref[...], g_ref[...], b_ref[...])
        h0_scr[...] = hf.astype(BF16)
        for c in range(D_MODEL // LANES):
            hf_scr[c] = hf[:, c * LANES:(c + 1) * LANES]
        for h_scr, (_, dil) in ((h1_scr, ATT_GROUPS[1]), (h2_scr, ATT_GROUPS[2])):
            rows = ROW_TILE // dil
            for r in range(dil):
                for c in range(D_MODEL // LANES):
                    h_scr[r * rows:(r + 1) * rows, c * LANES:(c + 1) * LANES] = (
                        hf_scr[c, pl.ds(r, rows, stride=dil), :].astype(BF16))

    gw = ATT_GROUP_WIDTH
    o0_ref[0] = _dot(h0_scr[...], w_ref[:, 0:gw]).astype(BF16)
    d1 = ATT_GROUPS[1][1]
    o1_ref[0, 0] = _dot(h1_scr[...], w_ref[:, gw:2 * gw]).astype(BF16).reshape(d1, ROW_TILE // d1, gw)
    d2 = ATT_GROUPS[2][1]
    o2_ref[0, 0] = _dot(h2_scr[...], w_ref[:, 2 * gw:3 * gw]).astype(BF16).reshape(d2, ROW_TILE // d2, gw)


def _inproj_qkv(x2, g, b, w_qkv, batch, seq):
    n = x2.shape[0]
    gw = ATT_GROUP_WIDTH
    tiles_per_seq = seq // ROW_TILE
    d1, d2 = ATT_GROUPS[1][1], ATT_GROUPS[2][1]
    out_shape = (
        jax.ShapeDtypeStruct((3, n, gw), BF16),
        jax.ShapeDtypeStruct((3, batch, d1, seq // d1, gw), BF16),
        jax.ShapeDtypeStruct((3, batch, d2, seq // d2, gw), BF16),
    )
    return pl.pallas_call(
        _inproj_qkv_kernel,
        out_shape=out_shape,
        grid=(n // ROW_TILE, 3),
        in_specs=[
            pl.BlockSpec((ROW_TILE, D_MODEL), lambda i, t: (i, 0)),
            pl.BlockSpec((1, D_MODEL), lambda i, t: (0, 0)),
            pl.BlockSpec((1, D_MODEL), lambda i, t: (0, 0)),
            pl.BlockSpec((D_MODEL, ATT_WIDTH), lambda i, t: (0, t)),
        ],
        out_specs=(
            pl.BlockSpec((1, ROW_TILE, gw), lambda i, t: (t, i, 0)),
            pl.BlockSpec((1, 1, d1, ROW_TILE // d1, gw),
                         lambda i, t: (t, i // tiles_per_seq, 0, i % tiles_per_seq, 0)),
            pl.BlockSpec((1, 1, d2, ROW_TILE // d2, gw),
                         lambda i, t: (t, i // tiles_per_seq, 0, i % tiles_per_seq, 0)),
        ),
        scratch_shapes=[
            pltpu.VMEM((D_MODEL // LANES, ROW_TILE, LANES), F32),
            pltpu.VMEM((ROW_TILE, D_MODEL), BF16),
            pltpu.VMEM((ROW_TILE, D_MODEL), BF16),
            pltpu.VMEM((ROW_TILE, D_MODEL), BF16),
        ],
        compiler_params=_params(("parallel", "arbitrary")),
        name="inproj_qkv",
    )(x2, g, b, w_qkv)


ATT_HEADS_PER_MATMUL = 4
ATT_BATCH_PER_STEP = 8


def _attn_block(q_ref, kc_ref, kp_ref, v_ref, o_ref, lse_ref, vt_ref, bias, has_prev):
    blk = ATT_BLOCK
    hd = ATT_HEAD_DIM
    gw = ATT_GROUP_WIDTH
    hpm = ATT_HEADS_PER_MATMUL
    width = hpm * hd

    vt_cur = jnp.concatenate(
        [v_ref[:, c * LANES:(c + 1) * LANES].astype(F32).T.astype(BF16)
         for c in range(gw // LANES)], axis=0)
    vt_prev = jnp.where(has_prev, vt_ref[...], jnp.zeros_like(vt_cur))
    vt_ref[...] = vt_cur

    lane_head = lax.broadcasted_iota(I32, (blk, width), 1) // hd
    per_tile = LANES // hd
    for g in range(ATT_HEADS // hpm):
        feat = slice(g * width, (g + 1) * width)
        q_g = q_ref[:, feat] * (hd ** -0.5)
        q_bd = jnp.concatenate([jnp.where(lane_head == i, q_g, jnp.zeros_like(q_g))
                                for i in range(hpm)], axis=0)
        k_g = jnp.concatenate([kp_ref[:, feat], kc_ref[:, feat]], axis=0)
        s_t = _dot_nt(k_g, q_bd) + bias
        m = jnp.max(s_t, axis=0, keepdims=True)
        p = jnp.exp(s_t - m)
        l = jnp.sum(p, axis=0, keepdims=True)
        v_t = jnp.concatenate([vt_prev[feat, :], vt_cur[feat, :]], axis=1)
        o_t = _dot(v_t, p.astype(BF16))
        lse = m + jnp.log(l)
        inv_l = 1.0 / l
        for c in range(width // LANES):
            tile = []
            for i in range(c * per_tile, (c + 1) * per_tile):
                cols = slice(i * blk, (i + 1) * blk)
                tile.append(o_t[i * hd:(i + 1) * hd, cols] * inv_l[:, cols])
                lse_ref[g * hpm + i:g * hpm + i + 1, :] = lse[:, cols]
            lanes = slice(g * width + c * LANES, g * width + (c + 1) * LANES)
            o_ref[:, lanes] = jnp.concatenate(tile, axis=0).T.astype(BF16)


def _attn_kernel(q_ref, kc_ref, kp_ref, v_ref, o_ref, lse_ref, vt_scr):
    blk = ATT_BLOCK
    has_prev = pl.program_id(2) > 0
    key_i = lax.broadcasted_iota(I32, (2 * blk, blk), 0)
    qry_i = lax.broadcasted_iota(I32, (2 * blk, blk), 1)
    live = (((key_i < blk) & (key_i >= qry_i) & has_prev)
            | ((key_i >= blk) & (key_i - blk <= qry_i)))
    bias = jnp.concatenate([jnp.where(live, 0.0, -jnp.inf)] * ATT_HEADS_PER_MATMUL, axis=1)
    for j in range(ATT_BATCH_PER_STEP):
        _attn_block(q_ref.at[j], kc_ref.at[j], kp_ref.at[j], v_ref.at[j], o_ref.at[j],
                    lse_ref.at[j], vt_scr.at[j], bias, has_prev)


def _attention_group(qkv):
    _, batch, dil, sub_len, gw = qkv.shape
    blk = ATT_BLOCK
    bps = ATT_BATCH_PER_STEP

    def spec(t, prev):
        def index(b, r, i):
            return (t, b, r, jnp.maximum(i - 1, 0) if prev else i, 0)
        return pl.BlockSpec((None, bps, None, blk, gw), index)

    return pl.pallas_call(
        _attn_kernel,
        out_shape=(jax.ShapeDtypeStruct((batch, dil, sub_len, gw), BF16),
                   jax.ShapeDtypeStruct((batch, dil, ATT_HEADS, sub_len), F32)),
        grid=(batch // bps, dil, sub_len // blk),
        in_specs=[spec(0, False), spec(1, False), spec(1, True), spec(2, False)],
        out_specs=(pl.BlockSpec((bps, None, blk, gw), lambda b, r, i: (b, r, i, 0)),
                   pl.BlockSpec((bps, None, ATT_HEADS, blk), lambda b, r, i: (b, r, 0, i))),
        scratch_shapes=[pltpu.VMEM((bps, gw, blk), BF16)],
        compiler_params=_params(("parallel", "parallel", "arbitrary")),
        name=f"dilated_attention_d{dil}",
    )(qkv, qkv, qkv, qkv)


HGRN_ROWS = 256
HGRN_HEADS_PER_STEP = 4


def _hgrn_rows(q, key, bcum, v, gate, st, nw, same_chunk_causal, chunk_mask):
    c = HGRN_CHUNK
    dk = HGRN_HEAD_DIM
    rows = HGRN_ROWS
    nchunk = rows // c

    b3 = bcum.reshape(nchunk, c, dk)
    b_mid = b3[:, c // 2:c // 2 + 1, :]
    b_last = b3[:, c - 1:c, :]
    q3 = q.astype(F32).reshape(nchunk, c, dk) * (dk ** -0.5)
    q_t3 = q3 * jnp.exp(b3 - b_mid)
    k_t3 = key.reshape(nchunk, c, dk) * jnp.exp(b_mid - b3)
    q_t = q_t3.reshape(rows, dk).astype(BF16)
    k_t = k_t3.reshape(rows, dk).astype(BF16)
    q_in = (q_t3 * jnp.exp(b_mid)).reshape(rows, dk).astype(BF16)
    k_st = (k_t3 * jnp.exp(b_last - b_mid)).reshape(rows, dk).astype(BF16)

    att = jnp.where(same_chunk_causal, _dot_nt(q_t, k_t), 0.0)
    o = _dot(att.astype(BF16), v)

    k_spread = jnp.concatenate([k_st * chunk_mask[j] for j in range(nchunk)], axis=1)
    incr = _dot_tn(v, k_spread)
    decay = jnp.exp(b_last.reshape(nchunk, dk))
    before = []
    for j in range(nchunk):
        before.append(st.astype(BF16))
        st = st * decay[j:j + 1] + incr[:, j * dk:(j + 1) * dk]
    q_spread = jnp.concatenate([q_in * chunk_mask[j] for j in range(nchunk)], axis=1)
    o = o + _dot_nt(q_spread, jnp.concatenate(before, axis=1))

    o = o * lax.rsqrt(jnp.mean(o * o, -1, keepdims=True) + RMS_EPS) * nw
    gate = gate.astype(F32)
    return o * gate * _sigmoid(gate), st


def _hgrn_kernel(q_ref, f_ref, i_ref, g_ref, lb_ref, nw_ref, o_ref, state_scr, mask_scr):
    seq = q_ref.shape[0]
    c = HGRN_CHUNK
    rows = HGRN_ROWS
    dk = HGRN_HEAD_DIM
    width = HGRN_HEADS_PER_STEP * dk
    nw = nw_ref[...]

    r_i = lax.broadcasted_iota(I32, (rows, rows), 0)
    c_i = lax.broadcasted_iota(I32, (rows, rows), 1)
    same_chunk_causal = (r_i // c == c_i // c) & (c_i <= r_i)
    tri = jnp.where(same_chunk_causal, 1.0, 0.0).astype(BF16)
    row_chunk = lax.broadcasted_iota(I32, (rows, dk), 0) // c
    for j in range(rows // c):
        mask_scr[j] = jnp.where(row_chunk == j, 1.0, 0.0).astype(BF16)

    state_scr[...] = jnp.zeros_like(state_scr)

    def body(gi, carry):
        rs = pl.ds(pl.multiple_of(gi * rows, rows), rows)
        lb = lb_ref[...]
        one_m_lb = 1.0 - lb
        z = f_ref[rs, :].astype(F32)
        log_f = jnp.log(lb + one_m_lb * _sigmoid(z))
        key = one_m_lb * _sigmoid(-z)
        p0 = log_f.astype(BF16)
        p1 = (log_f - p0.astype(F32)).astype(BF16)
        both = _dot(tri, jnp.concatenate([p0, p1], axis=1))
        bcum = both[:, :width] + both[:, width:]
        for h in range(HGRN_HEADS_PER_STEP):
            cs = slice(h * dk, (h + 1) * dk)
            o, st = _hgrn_rows(q_ref[rs, cs], key[:, cs], bcum[:, cs], i_ref[rs, cs], g_ref[rs, cs],
                               state_scr[h], nw, same_chunk_causal, mask_scr)
            state_scr[h] = st
            o_ref[rs, cs] = o.astype(BF16)
        return carry

    lax.fori_loop(0, seq // rows, body, 0)


def _hgrn(proj, lower_bound, norm_w, batch, seq):
    n = proj.shape[0]
    dk = HGRN_HEAD_DIM
    width = HGRN_HEADS_PER_STEP * dk
    steps = HGRN_HEADS // HGRN_HEADS_PER_STEP

    def seg(k):
        return pl.BlockSpec((seq, width), lambda b, h: (b, k * steps + h))

    return pl.pallas_call(
        _hgrn_kernel,
        out_shape=jax.ShapeDtypeStruct((n, HGRN_WIDTH), BF16),
        grid=(batch, steps),
        in_specs=[seg(0), seg(1), seg(2), seg(3),
                  pl.BlockSpec((1, width), lambda b, h: (0, h)),
                  pl.BlockSpec((1, dk), lambda b, h: (0, 0))],
        out_specs=pl.BlockSpec((seq, width), lambda b, h: (b, h)),
        scratch_shapes=[pltpu.VMEM((HGRN_HEADS_PER_STEP, dk, dk), F32),
                        pltpu.VMEM((HGRN_ROWS // HGRN_CHUNK, HGRN_ROWS, dk), BF16)],
        compiler_params=_params(("parallel", "parallel")),
        name="hgrn2",
    )(proj, proj, proj, proj, lower_bound, norm_w)


def _pick_first_max(vals, iota, axis, size):
    m = jnp.max(vals, axis=axis, keepdims=True)
    idx = jnp.min(jnp.where(vals == m, iota, size), axis=axis, keepdims=True)
    return m, idx


def _route_tile(logits, bias, base):
    e, tn = logits.shape
    groups = N_EXPERT_GROUPS
    gsz = e // groups
    neg = -jnp.inf

    scores = _sigmoid(logits)
    biased = scores + bias

    b3 = biased.reshape(groups, gsz, tn)
    io3 = lax.broadcasted_iota(I32, b3.shape, 1)
    m1, i1 = _pick_first_max(b3, io3, 1, gsz)
    m2 = jnp.max(jnp.where(io3 == i1, neg, b3), axis=1, keepdims=True)
    grp = m1 + m2
    iog = lax.broadcasted_iota(I32, grp.shape, 0)
    keep = jnp.zeros(grp.shape, F32)
    for _ in range(TOPK_GROUPS):
        _, gi = _pick_first_max(grp, iog, 0, groups)
        hit = iog == gi
        keep = jnp.where(hit, 1.0, keep)
        grp = jnp.where(hit, neg, grp)
    masked = jnp.where(keep > 0.0, b3, neg).reshape(e, tn)

    ioe = lax.broadcasted_iota(I32, (e, tn), 0)
    onehot = jnp.zeros((e, tn), F32)
    idxs, gates = [], []
    for _ in range(TOP_K):
        _, ei = _pick_first_max(masked, ioe, 0, e)
        hit = ioe == ei
        gates.append(jnp.sum(jnp.where(hit, scores, 0.0), axis=0, keepdims=True))
        onehot = jnp.where(hit, 1.0, onehot)
        masked = jnp.where(hit, neg, masked)
        idxs.append(ei)
    gsum = gates[0]
    for g in gates[1:]:
        gsum = gsum + g
    gate = jnp.concatenate(gates, axis=0) / gsum * ROUTED_SCALE

    t_r = lax.broadcasted_iota(I32, (tn, tn), 0)
    t_c = lax.broadcasted_iota(I32, (tn, tn), 1)
    earlier = jnp.where(t_r < t_c, 1.0, 0.0).astype(BF16)
    oh = onehot.astype(BF16)
    before = _dot(oh, earlier) + jnp.concatenate([base] * (tn // LANES), axis=1)
    ranks = [jnp.sum(jnp.where(ioe == ei, before, 0.0), axis=0, keepdims=True) for ei in idxs]
    total = base + _dot(oh, jnp.ones((tn, LANES), BF16))
    return (jnp.concatenate(idxs, axis=0), gate,
            jnp.concatenate(ranks, axis=0).astype(I32), total)


def _mix_kernel(x_ref, gin_ref, bin_ref, o0_ref, o1_ref, o2_ref, l0_ref, l1_ref, l2_ref,
                ob_ref, ga_ref, gb_ref, wa_ref, wb_ref, wo_ref, g1_ref, b1_ref,
                rwh_ref, rwl_ref, rbias_ref,
                h_ref, hp0_ref, hp1_ref, idx_ref, gate_ref, rank_ref, cnt_ref,
                so1_scr, so2_scr, lg_scr, carry_scr):
    tm = x_ref.shape[0]
    gw = ATT_GROUP_WIDTH
    step = pl.program_id(0)

    @pl.when(step == 0)
    def _():
        lg_scr[...] = jnp.zeros_like(lg_scr)
        carry_scr[...] = jnp.zeros_like(carry_scr)

    counts = carry_scr[...]
    idx, gate, rank, total = _route_tile(lg_scr[...], rbias_ref[...], counts)
    idx_ref[...] = idx
    gate_ref[...] = gate
    rank_ref[...] = rank
    counts = jnp.where(step > 0, total, counts)
    carry_scr[...] = counts
    cnt_ref[...] = counts

    for o_ref, so_scr, (_, dil) in ((o1_ref, so1_scr, ATT_GROUPS[1]),
                                    (o2_ref, so2_scr, ATT_GROUPS[2])):
        for r in range(dil):
            o_r = o_ref[r].astype(F32)
            for c in range(gw // LANES):
                so_scr[c, pl.ds(r, tm // dil, stride=dil), :] = o_r[:, c * LANES:(c + 1) * LANES]

    def natural(scr):
        return jnp.concatenate([scr[c] for c in range(gw // LANES)], axis=-1)

    l0, l1, l2 = l0_ref[...], l1_ref[...], l2_ref[...]
    m = jnp.maximum(jnp.maximum(l0, l1), l2)
    e0 = jnp.exp(l0 - m)
    e1 = jnp.exp(l1 - m)
    e2 = jnp.exp(l2 - m)
    inv = 1.0 / (e0 + e1 + e2)
    terms = []
    for w in (e0 * inv, e1 * inv, e2 * inv):
        hi = w.astype(BF16).astype(F32)
        terms += [hi, w - hi]
    terms = jnp.concatenate(terms, axis=0).astype(BF16)
    n_groups = len(ATT_GROUPS)
    t_i = lax.broadcasted_iota(I32, (n_groups * 2 * ATT_HEADS, n_groups * gw), 0)
    c_i = lax.broadcasted_iota(I32, (n_groups * 2 * ATT_HEADS, n_groups * gw), 1)
    spread = jnp.where((t_i // (2 * ATT_HEADS) == c_i // gw)
                       & (t_i % ATT_HEADS == c_i % gw // ATT_HEAD_DIM), 1.0, 0.0).astype(BF16)
    wide = _dot_tn(terms, spread)
    o_att = (wide[:, :gw] * o0_ref[...].astype(F32) + wide[:, gw:2 * gw] * natural(so1_scr)
             + wide[:, 2 * gw:] * natural(so2_scr))

    y_a = _dot(o_att.astype(BF16), wa_ref[...])
    y_b = _dot(ob_ref[...], wb_ref[...])
    merged = _sigmoid(ga_ref[...].astype(F32)) * y_a + _sigmoid(gb_ref[...].astype(F32)) * y_b
    mix = _dot(merged.astype(BF16), wo_ref[...])
    h_in = _layer_norm(x_ref[...], gin_ref[...], bin_ref[...])
    h1 = _layer_norm(DEEPNORM_ALPHA * h_in + mix, g1_ref[...], b1_ref[...])
    h_ref[...] = h1

    hp0_ref[...], hp1_ref[...] = _pack_rows(h1)

    h_hi = h1.astype(BF16)
    h_lo = (h1 - h_hi.astype(F32)).astype(BF16)
    rwh = rwh_ref[...]
    lg_scr[...] = _dot_nt(rwh, h_hi) + _dot_nt(rwh, h_lo) + _dot_nt(rwl_ref[...], h_hi)


def _mix(x2, gin, bin_, att, ob, proj, wa, wb, wo, g1, b1, rwh, rwl, rbias, batch, seq):
    n = x2.shape[0]
    tm = MIX_ROW_TILE
    gw = ATT_GROUP_WIDTH
    tiles_per_seq = seq // tm
    last = n // tm - 1
    (o0, l0), (o1, l1), (o2, l2) = att
    d1, d2 = ATT_GROUPS[1][1], ATT_GROUPS[2][1]

    def tile(i):
        return jnp.minimum(i, last)

    def routed(i):
        return jnp.maximum(i - 1, 0)

    lse_spec = pl.BlockSpec((None, ATT_HEADS, tm),
                            lambda i: (tile(i) // tiles_per_seq, 0, tile(i) % tiles_per_seq))

    def full(shape):
        return pl.BlockSpec(shape, lambda i: (0,) * len(shape))

    def dil_spec(dil):
        return pl.BlockSpec((None, dil, tm // dil, gw),
                            lambda i: (tile(i) // tiles_per_seq, 0, tile(i) % tiles_per_seq, 0))

    def rows(width, col=0):
        return pl.BlockSpec((tm, width), lambda i: (tile(i), col))

    tok_spec = pl.BlockSpec((TOP_K, tm), lambda i: (0, routed(i)))
    return pl.pallas_call(
        _mix_kernel,
        out_shape=(jax.ShapeDtypeStruct((n, D_MODEL), F32),
                   jax.ShapeDtypeStruct((n, PACK_WIDTH), I32),
                   jax.ShapeDtypeStruct((n, PACK_WIDTH), I32),
                   jax.ShapeDtypeStruct((TOP_K, n), I32),
                   jax.ShapeDtypeStruct((TOP_K, n), F32),
                   jax.ShapeDtypeStruct((TOP_K, n), I32),
                   jax.ShapeDtypeStruct((N_EXPERTS, LANES), F32)),
        grid=(n // tm + 1,),
        in_specs=[
            rows(D_MODEL),
            full((1, D_MODEL)), full((1, D_MODEL)),
            rows(gw), dil_spec(d1), dil_spec(d2),
            lse_spec, lse_spec, lse_spec,
            rows(HGRN_WIDTH), rows(D_MODEL, 4), rows(D_MODEL, 5),
            full((gw, D_MODEL)), full((HGRN_WIDTH, D_MODEL)), full((D_MODEL, D_MODEL)),
            full((1, D_MODEL)), full((1, D_MODEL)),
            full((N_EXPERTS, D_MODEL)), full((N_EXPERTS, D_MODEL)), full((N_EXPERTS, 1)),
        ],
        out_specs=(rows(D_MODEL), rows(PACK_WIDTH), rows(PACK_WIDTH),
                   tok_spec, tok_spec, tok_spec, full((N_EXPERTS, LANES))),
        scratch_shapes=[pltpu.VMEM((gw // LANES, tm, LANES), F32),
                        pltpu.VMEM((gw // LANES, tm, LANES), F32),
                        pltpu.VMEM((N_EXPERTS, tm), F32),
                        pltpu.VMEM((N_EXPERTS, LANES), F32)],
        compiler_params=_params(("arbitrary",)),
        name="branch_mix_route",
    )(x2, gin, bin_, o0, o1, o2, l0, l1, l2, ob, proj, proj, wa, wb, wo, g1, b1, rwh, rwl, rbias)


ROW_SLOTS = 16
ROW_GROUP = 4
ROW_AHEAD = ROW_SLOTS - ROW_GROUP
W_SLOTS = 3


def _expert_kernel(sblk_ref, nblk_ref, cnt_ref, xs0_hbm, xs1_hbm, wi_hbm, wo_hbm,
                   ys0_hbm, ys1_hbm, xbuf0, xbuf1, ybuf0, ybuf1, wibuf, wobuf,
                   in_sem, out_sem, w_sem, wi_scr, wo_scr):
    e = pl.program_id(0)
    n_exp = pl.num_programs(0)
    first_blk = sblk_ref[e]
    n_blk = nblk_ref[e]
    n_rows = cnt_ref[e]
    total = sblk_ref[n_exp - 1] + nblk_ref[n_exp - 1]
    xs_hbm, ys_hbm = (xs0_hbm, xs1_hbm), (ys0_hbm, ys1_hbm)
    xbuf, ybuf = (xbuf0, xbuf1), (ybuf0, ybuf1)

    def rows_of(b):
        return pl.ds(pl.multiple_of(b * MOE_BLOCK, MOE_BLOCK), MOE_BLOCK)

    def x_copy(b, part):
        slot = b % ROW_SLOTS
        return pltpu.make_async_copy(xs_hbm[part].at[rows_of(b)], xbuf[part].at[slot],
                                     in_sem.at[part, slot])

    def y_copy(b, part):
        slot = b % ROW_SLOTS
        return pltpu.make_async_copy(ybuf[part].at[slot], ys_hbm[part].at[rows_of(b)],
                                     out_sem.at[part, slot])

    def w_copies(ex):
        slot = ex % W_SLOTS
        return (pltpu.make_async_copy(wi_hbm.at[ex], wibuf.at[slot], w_sem.at[0, slot]),
                pltpu.make_async_copy(wo_hbm.at[ex], wobuf.at[slot], w_sem.at[1, slot]))

    def start_weights(ex):
        @pl.when((ex < n_exp) & (nblk_ref[jnp.minimum(ex, n_exp - 1)] > 0))
        def _():
            for cp in w_copies(ex):
                cp.start()

    @pl.when(e == 0)
    def _():
        for b in range(ROW_AHEAD):
            @pl.when(b < total)
            def _():
                for part in range(PACK_PARTS):
                    x_copy(b, part).start()
        for ex in range(W_SLOTS - 1):
            start_weights(ex)

    start_weights(e + (W_SLOTS - 1))

    def step(j, group):
        b0 = first_blk + j
        for i in range(group):
            for part in range(PACK_PARTS):
                x_copy(b0 + i, part).wait()
        for i in range(group):
            @pl.when(b0 + ROW_AHEAD + i < total)
            def _():
                for part in range(PACK_PARTS):
                    x_copy(b0 + ROW_AHEAD + i, part).start()
        for i in range(group):
            @pl.when(b0 + i >= ROW_SLOTS)
            def _():
                for part in range(PACK_PARTS):
                    y_copy(b0 + i - ROW_SLOTS, part).wait()

        rows = group * MOE_BLOCK
        live = (lax.broadcasted_iota(I32, (rows, PACK_WIDTH), 0)
                < n_rows - j * MOE_BLOCK)
        parts = [jnp.concatenate([xbuf[part][(b0 + i) % ROW_SLOTS] for i in range(group)], axis=0)
                 for part in range(PACK_PARTS)]
        x = _unpack_rows([jnp.where(live, p, 0) for p in parts])
        hu = _dot(x.astype(BF16), wi_scr[...])
        hg = hu[:, :EXPERT_DIM]
        hv = hu[:, EXPERT_DIM:]
        act = (hg * _sigmoid(hg) * hv).astype(BF16)
        packed = _pack_rows(_dot(act, wo_scr[...]))
        for i in range(group):
            for part in range(PACK_PARTS):
                ybuf[part][(b0 + i) % ROW_SLOTS] = packed[part][i * MOE_BLOCK:(i + 1) * MOE_BLOCK]
                y_copy(b0 + i, part).start()

    @pl.when(n_blk > 0)
    def _():
        for cp in w_copies(e):
            cp.wait()
        slot = e % W_SLOTS
        wi_scr[...] = wibuf[slot].astype(BF16)
        wo_scr[...] = wobuf[slot].astype(BF16)

        def full_group(g, carry):
            step(g * ROW_GROUP, ROW_GROUP)
            return carry

        lax.fori_loop(0, n_blk // ROW_GROUP, full_group, 0)
        done = n_blk // ROW_GROUP * ROW_GROUP
        group = ROW_GROUP // 2
        while group >= 1:
            @pl.when((n_blk & group) != 0)
            def _(group=group, done=done):
                step(done, group)
            done = done + (n_blk & group)
            group //= 2

    @pl.when(e == n_exp - 1)
    def _():
        for back in range(1, ROW_SLOTS + 1):
            @pl.when(total >= back)
            def _():
                for part in range(PACK_PARTS):
                    y_copy(total - back, part).wait()


def _experts(start_blk, n_blk, n_rows, xs, w_in_e, w_out_e):
    p = xs[0].shape[0]
    n_exp = w_in_e.shape[0]
    any_spec = pl.BlockSpec(memory_space=pl.ANY)
    row_buf = pltpu.VMEM((ROW_SLOTS, MOE_BLOCK, PACK_WIDTH), I32)
    grid_spec = pltpu.PrefetchScalarGridSpec(
        num_scalar_prefetch=3,
        grid=(n_exp,),
        in_specs=[any_spec, any_spec, any_spec, any_spec],
        out_specs=(any_spec, any_spec),
        scratch_shapes=[row_buf, row_buf, row_buf, row_buf,
                        pltpu.VMEM((W_SLOTS, D_MODEL, 2 * EXPERT_DIM), F32),
                        pltpu.VMEM((W_SLOTS, EXPERT_DIM, D_MODEL), F32),
                        pltpu.SemaphoreType.DMA((PACK_PARTS, ROW_SLOTS)),
                        pltpu.SemaphoreType.DMA((PACK_PARTS, ROW_SLOTS)),
                        pltpu.SemaphoreType.DMA((2, W_SLOTS)),
                        pltpu.VMEM((D_MODEL, 2 * EXPERT_DIM), BF16),
                        pltpu.VMEM((EXPERT_DIM, D_MODEL), BF16)],
    )
    return pl.pallas_call(
        _expert_kernel,
        out_shape=(jax.ShapeDtypeStruct((p, PACK_WIDTH), I32),) * PACK_PARTS,
        grid_spec=grid_spec,
        compiler_params=_params(("arbitrary",)),
        name="routed_experts",
    )(start_blk, n_blk, n_rows, xs[0], xs[1], w_in_e, w_out_e)


def _combine_kernel(h_ref, yg0_ref, yg1_ref, gate_ref, wsi_ref, wso_ref, g2_ref, b2_ref, o_ref):
    h1 = h_ref[...]
    gate = gate_ref[...]
    routed = None
    for k in range(TOP_K):
        y_k = _unpack_rows([yg0_ref[k], yg1_ref[k]]) * gate[:, k:k + 1]
        routed = y_k if routed is None else routed + y_k
    hs = _dot(h1.astype(BF16), wsi_ref[...])
    sg = hs[:, :SHARED_DIM]
    sv = hs[:, SHARED_DIM:]
    shared = _dot((sg * _sigmoid(sg) * sv).astype(BF16), wso_ref[...])
    o_ref[...] = _layer_norm(DEEPNORM_ALPHA * h1 + routed + shared, g2_ref[...], b2_ref[...])


def _combine(h1, yg, gate_nk, wsi, wso, g2, b2):
    n = h1.shape[0]
    tm = OUT_ROW_TILE
    yg_spec = pl.BlockSpec((TOP_K, tm, PACK_WIDTH), lambda i: (0, i, 0))

    def full(shape):
        return pl.BlockSpec(shape, lambda i: (0,) * len(shape))

    return pl.pallas_call(
        _combine_kernel,
        out_shape=jax.ShapeDtypeStruct((n, D_MODEL), F32),
        grid=(n // tm,),
        in_specs=[
            pl.BlockSpec((tm, D_MODEL), lambda i: (i, 0)),
            yg_spec, yg_spec,
            pl.BlockSpec((tm, TOP_K), lambda i: (i, 0)),
            full((D_MODEL, 2 * SHARED_DIM)), full((SHARED_DIM, D_MODEL)),
            full((1, D_MODEL)), full((1, D_MODEL)),
        ],
        out_specs=pl.BlockSpec((tm, D_MODEL), lambda i: (i, 0)),
        compiler_params=_params(("parallel",)),
        name="shared_combine",
    )(h1, yg[0], yg[1], gate_nk, wsi, wso, g2, b2)


def _plan_kernel(idx_ref, rank_ref, cnt_ref, dest_ref, sblk_ref):
    e = cnt_ref.shape[0]
    tn = idx_ref.shape[1]
    cnt = cnt_ref[...]
    nblk = jnp.floor((cnt + (MOE_BLOCK - 1)) * (1.0 / MOE_BLOCK))
    e_r = lax.broadcasted_iota(I32, (e, e), 0)
    e_c = lax.broadcasted_iota(I32, (e, e), 1)
    lower = jnp.where(e_c < e_r, 1.0, 0.0).astype(BF16)
    start_blk = _dot(lower, nblk.astype(BF16))

    ioe = lax.broadcasted_iota(I32, (e, tn), 0)
    start_row = jnp.concatenate([start_blk * MOE_BLOCK] * (tn // LANES), axis=1)
    idx = idx_ref[...]
    dests = [jnp.sum(jnp.where(ioe == idx[k:k + 1], start_row, 0.0), axis=0, keepdims=True)
             for k in range(TOP_K)]
    dest_ref[...] = jnp.concatenate(dests, axis=0).astype(I32) + rank_ref[...]
    sblk_ref[...] = start_blk.astype(I32)


def _plan(idx_t, rank_t, counts):
    k, n = idx_t.shape
    e = counts.shape[0]
    tn = MIX_ROW_TILE
    tok_spec = pl.BlockSpec((k, tn), lambda i: (0, i))
    exp_spec = pl.BlockSpec((e, LANES), lambda i: (0, 0))
    return pl.pallas_call(
        _plan_kernel,
        out_shape=(jax.ShapeDtypeStruct((k, n), I32), jax.ShapeDtypeStruct((e, LANES), I32)),
        grid=(n // tn,),
        in_specs=[tok_spec, tok_spec, exp_spec],
        out_specs=(tok_spec, exp_spec),
        compiler_params=_params(("arbitrary",)),
        name="dispatch_plan",
    )(idx_t, rank_t, counts)


SC_WINDOW = 128


def _sc_mesh():
    return plsc.VectorSubcoreMesh(core_axis_name="core", subcore_axis_name="subcore")


def _dispatch_rows(parts, dest, p_total):
    n, width = parts[0].shape
    top_k = dest.shape[0]
    out_type = (jax.ShapeDtypeStruct((p_total, width), parts[0].dtype),) * len(parts)

    @functools.partial(pl.kernel, mesh=_sc_mesh(), scratch_types=[], out_type=out_type,
                       name="dispatch_rows")
    def scatter(*refs):
        x_hbms = refs[:len(parts)]
        i_hbm = refs[len(parts)]
        o_hbms = refs[len(parts) + 1:]
        for x_hbm, o_hbm in zip(x_hbms, o_hbms):
            def body(x_vmem, i_vmem, o_hbm=o_hbm):
                for k in range(top_k):
                    pltpu.sync_copy(x_vmem, o_hbm.at[i_vmem.at[k]])

            pltpu.emit_pipeline(
                body,
                grid=(n // SC_WINDOW,),
                in_specs=[pl.BlockSpec((SC_WINDOW, width), lambda i: (i, 0)),
                          pl.BlockSpec((top_k, SC_WINDOW), lambda i: (0, i))],
                out_specs=[],
                core_axis_name=("core", "subcore"),
                dimension_semantics=(pltpu.PARALLEL,),
            )(x_hbm, i_hbm)

    return scatter(*parts, dest)


def _gather_rows(parts, idx_flat):
    count = idx_flat.shape[1]
    width = parts[0].shape[1]
    out_type = (jax.ShapeDtypeStruct((count, width), parts[0].dtype),) * len(parts)

    @functools.partial(pl.kernel, mesh=_sc_mesh(), scratch_types=[], out_type=out_type,
                       name="combine_rows")
    def gather(*refs):
        y_hbms = refs[:len(parts)]
        i_hbm = refs[len(parts)]
        o_hbms = refs[len(parts) + 1:]
        for y_hbm, o_hbm in zip(y_hbms, o_hbms):
            def body(i_vmem, o_vmem, y_hbm=y_hbm):
                pltpu.sync_copy(y_hbm.at[i_vmem.at[0]], o_vmem)

            pltpu.emit_pipeline(
                body,
                grid=(count // SC_WINDOW,),
                in_specs=[pl.BlockSpec((1, SC_WINDOW), lambda i: (0, i))],
                out_specs=[pl.BlockSpec((SC_WINDOW, width), lambda i: (i, 0))],
                core_axis_name=("core", "subcore"),
                dimension_semantics=(pltpu.PARALLEL,),
            )(i_hbm, o_hbm)

    return gather(*parts, idx_flat)


def kernel(x, ln_in_g, ln_in_b, w_in, hgrn_lb_logits, hgrn_norm_w, w_branch_att, w_branch_hgrn,
           w_out, ln1_g, ln1_b, router_w, router_bias, expert_w_in, expert_w_out, shared_w_in,
           shared_w_out, ln2_g, ln2_b):
    batch, seq, d = x.shape
    n = batch * seq
    x2 = x.reshape(n, d)
    row = lambda v: v.reshape(1, -1).astype(F32)

    lower_bounds = jnp.cumsum(jax.nn.softmax(hgrn_lb_logits.astype(F32), axis=0), axis=0)
    l = 0
    w_l = w_in[l]
    w_qkv = w_l[:, :3 * ATT_WIDTH].astype(BF16)
    w_main = w_l[:, 3 * ATT_WIDTH:].astype(BF16)
    gin, bin_ = row(ln_in_g), row(ln_in_b)

    proj = _inproj_main(x2, gin, bin_, w_main)
    qkv0, qkv1, qkv2 = _inproj_qkv(x2, gin, bin_, w_qkv, batch, seq)
    qkv0 = qkv0.reshape(3, batch, 1, seq, ATT_GROUP_WIDTH)
    att = []
    for qkv in (qkv0, qkv1, qkv2):
        o_g, lse_g = _attention_group(qkv)
        att.append((o_g, lse_g.transpose(0, 2, 3, 1).reshape(batch, ATT_HEADS, seq)))
    att[0] = (att[0][0].reshape(n, ATT_GROUP_WIDTH), att[0][1])

    ob = _hgrn(proj, row(lower_bounds[l]), row(hgrn_norm_w[l]), batch, seq)

    rw_t = router_w[l].T.astype(F32)
    rwh = rw_t.astype(BF16)
    rwl = (rw_t - rwh.astype(F32)).astype(BF16)
    h1, hp0, hp1, idx_t, gate_t, rank_t, counts = _mix(
        x2, gin, bin_, att, ob, proj,
        w_branch_att[l].astype(BF16), w_branch_hgrn[l].astype(BF16), w_out[l].astype(BF16),
        row(ln1_g[l]), row(ln1_b[l]), rwh, rwl, router_bias[l].reshape(-1, 1).astype(F32),
        batch, seq)
    p_total = n * TOP_K + N_EXPERTS * MOE_BLOCK
    dest, start_blk = _plan(idx_t, rank_t, counts)
    n_rows = counts[:, 0].astype(I32)
    n_blk = (n_rows + (MOE_BLOCK - 1)) // MOE_BLOCK
    xs = _dispatch_rows((hp0, hp1), dest, p_total)
    ys = _experts(start_blk[:, 0], n_blk, n_rows, xs, expert_w_in[l], expert_w_out[l])
    yg = [g.reshape(TOP_K, n, PACK_WIDTH)
          for g in _gather_rows(ys, dest.reshape(1, TOP_K * n))]
    out = _combine(h1, yg, gate_t.T, shared_w_in[l].astype(BF16), shared_w_out[l].astype(BF16),
                   row(ln2_g[l]), row(ln2_b[l]))
    return out.reshape(batch, seq, d)
```

```python
import functools

import jax
import jax.numpy as jnp
from jax import lax
from jax.experimental import pallas as pl
from jax.experimental.pallas import tpu as pltpu
from jax.experimental.pallas import tpu_sc as plsc

F32 = jnp.float32
BF16 = jnp.bfloat16
U32 = jnp.uint32
I32 = jnp.int32

D_MODEL = 1024
ATT_GROUPS = ((128, 1), (512, 4), (2048, 16))
ATT_HEADS = 8
ATT_HEAD_DIM = 64
ATT_GROUP_WIDTH = ATT_HEADS * ATT_HEAD_DIM
ATT_WIDTH = len(ATT_GROUPS) * ATT_GROUP_WIDTH
ATT_BLOCK = 128
HGRN_HEAD_DIM = 128
HGRN_HEADS = D_MODEL // HGRN_HEAD_DIM
HGRN_WIDTH = HGRN_HEADS * HGRN_HEAD_DIM
HGRN_CHUNK = 32
N_EXPERTS = 256
TOP_K = 8
N_EXPERT_GROUPS = 8
TOPK_GROUPS = 4
EXPERT_DIM = 256
SHARED_DIM = 256
ROUTED_SCALE = 2.5
MOE_BLOCK = 128
LN_EPS = 1e-5
RMS_EPS = 1e-6
DEPTH = 1
DEEPNORM_ALPHA = (2 * DEPTH) ** 0.25

VMEM_LIMIT = 56 * 1024 * 1024
LANES = 128

ROW_TILE = 1024
MAIN_COL_TILE = 1536
MIX_ROW_TILE = 512
OUT_ROW_TILE = 512


def _params(sem, vmem=VMEM_LIMIT):
    return pltpu.CompilerParams(dimension_semantics=sem, vmem_limit_bytes=vmem)


def _layer_norm(x, g, b):
    mu = jnp.mean(x, -1, keepdims=True)
    xc = x - mu
    var = jnp.mean(xc * xc, -1, keepdims=True)
    return xc * lax.rsqrt(var + LN_EPS) * g + b


def _sigmoid(x):
    return 1.0 / (1.0 + jnp.exp(-x))


def _dot(a, b):
    return jnp.dot(a, b, preferred_element_type=F32)


def _dot_nt(a, b):
    return lax.dot_general(a, b, (((1,), (1,)), ((), ())), preferred_element_type=F32)


def _dot_tn(a, b):
    return lax.dot_general(a, b, (((0,), (0,)), ((), ())), preferred_element_type=F32)


def _pack_bf16_pair(lo, hi):
    lo_bits = pltpu.bitcast(lo.astype(BF16).astype(F32), U32) >> 16
    hi_bits = pltpu.bitcast(hi.astype(BF16).astype(F32), U32) & jnp.uint32(0xFFFF0000)
    return hi_bits | lo_bits


def _unpack_bf16_pair(w):
    lo = pltpu.bitcast(w << 16, F32)
    hi = pltpu.bitcast(w & jnp.uint32(0xFFFF0000), F32)
    return lo, hi


PACK_PARTS = 2
PACK_WIDTH = D_MODEL // 2 // PACK_PARTS


def _pack_rows(v):
    half = D_MODEL // 2
    parts = []
    for j in range(PACK_PARTS):
        lo = v[:, j * PACK_WIDTH:(j + 1) * PACK_WIDTH]
        hi = v[:, half + j * PACK_WIDTH:half + (j + 1) * PACK_WIDTH]
        parts.append(pltpu.bitcast(_pack_bf16_pair(lo, hi), I32))
    return parts


def _unpack_rows(parts):
    pairs = [_unpack_bf16_pair(pltpu.bitcast(p, U32)) for p in parts]
    return jnp.concatenate([lo for lo, _ in pairs] + [hi for _, hi in pairs], axis=-1)


def _inproj_main_kernel(x_ref, g_ref, b_ref, w_ref, o_ref, h_scr):
    @pl.when(pl.program_id(1) == 0)
    def _():
        h_scr[...] = _layer_norm(x_ref[...], g_ref[...], b_ref[...]).astype(BF16)

    o_ref[...] = _dot(h_scr[...], w_ref[...]).astype(BF16)


def _inproj_main(x2, g, b, w_all):
    n = x2.shape[0]
    first = 3 * ATT_WIDTH // MAIN_COL_TILE
    width = w_all.shape[1] - 3 * ATT_WIDTH
    return pl.pallas_call(
        _inproj_main_kernel,
        out_shape=jax.ShapeDtypeStruct((n, width), BF16),
        grid=(n // ROW_TILE, width // MAIN_COL_TILE),
        in_specs=[
            pl.BlockSpec((ROW_TILE, D_MODEL), lambda i, j: (i, 0)),
            pl.BlockSpec((1, D_MODEL), lambda i, j: (0, 0)),
            pl.BlockSpec((1, D_MODEL), lambda i, j: (0, 0)),
            pl.BlockSpec((D_MODEL, MAIN_COL_TILE), lambda i, j: (0, first + j)),
        ],
        out_specs=pl.BlockSpec((ROW_TILE, MAIN_COL_TILE), lambda i, j: (i, j)),
        scratch_shapes=[pltpu.VMEM((ROW_TILE, D_MODEL), BF16)],
        compiler_params=_params(("parallel", "arbitrary")),
        name="inproj_main",
    )(x2, g, b, w_all)


def _inproj_qkv_kernel(x_ref, g_ref, b_ref, w_ref, o0_ref, o1_ref, o2_ref,
                       hf_scr, h0_scr, h1_scr, h2_scr):
    @pl.when(pl.program_id(1) == 0)
    def _():
        hf = _layer_norm(x_ref[...], g_ref[...], b_ref[...])
        h0_scr[...] = hf.astype(BF16)
        for c in range(D_MODEL // LANES):
            hf_scr[c] = hf[:, c * LANES:(c + 1) * LANES]
        for h_scr, (_, dil) in ((h1_scr, ATT_GROUPS[1]), (h2_scr, ATT_GROUPS[2])):
            rows = ROW_TILE // dil
            for r in range(dil):
                for c in range(D_MODEL // LANES):
                    h_scr[r * rows:(r + 1) * rows, c * LANES:(c + 1) * LANES] = (
                        hf_scr[c, pl.ds(r, rows, stride=dil), :].astype(BF16))

    gw = ATT_GROUP_WIDTH
    o0_ref[0] = _dot(h0_scr[...], w_ref[:, 0:gw]).astype(BF16)
    d1 = ATT_GROUPS[1][1]
    o1_ref[0, 0] = _dot(h1_scr[...], w_ref[:, gw:2 * gw]).astype(BF16).reshape(d1, ROW_TILE // d1, gw)
    d2 = ATT_GROUPS[2][1]
    o2_ref[0, 0] = _dot(h2_scr[...], w_ref[:, 2 * gw:3 * gw]).astype(BF16).reshape(d2, ROW_TILE // d2, gw)


def _inproj_qkv(x2, g, b, w_all, batch, seq):
    n = x2.shape[0]
    gw = ATT_GROUP_WIDTH
    tiles_per_seq = seq // ROW_TILE
    d1, d2 = ATT_GROUPS[1][1], ATT_GROUPS[2][1]
    out_shape = (
        jax.ShapeDtypeStruct((3, n, gw), BF16),
        jax.ShapeDtypeStruct((3, batch, d1, seq // d1, gw), BF16),
        jax.ShapeDtypeStruct((3, batch, d2, seq // d2, gw), BF16),
    )
    return pl.pallas_call(
        _inproj_qkv_kernel,
        out_shape=out_shape,
        grid=(n // ROW_TILE, 3),
        in_specs=[
            pl.BlockSpec((ROW_TILE, D_MODEL), lambda i, t: (i, 0)),
            pl.BlockSpec((1, D_MODEL), lambda i, t: (0, 0)),
            pl.BlockSpec((1, D_MODEL), lambda i, t: (0, 0)),
            pl.BlockSpec((D_MODEL, ATT_WIDTH), lambda i, t: (0, t)),
        ],
        out_specs=(
            pl.BlockSpec((1, ROW_TILE, gw), lambda i, t: (t, i, 0)),
            pl.BlockSpec((1, 1, d1, ROW_TILE // d1, gw),
                         lambda i, t: (t, i // tiles_per_seq, 0, i % tiles_per_seq, 0)),
            pl.BlockSpec((1, 1, d2, ROW_TILE // d2, gw),
                         lambda i, t: (t, i // tiles_per_seq, 0, i % tiles_per_seq, 0)),
        ),
        scratch_shapes=[
            pltpu.VMEM((D_MODEL // LANES, ROW_TILE, LANES), F32),
            pltpu.VMEM((ROW_TILE, D_MODEL), BF16),
            pltpu.VMEM((ROW_TILE, D_MODEL), BF16),
            pltpu.VMEM((ROW_TILE, D_MODEL), BF16),
        ],
        compiler_params=_params(("parallel", "arbitrary")),
        name="inproj_qkv",
    )(x2, g, b, w_all)


ATT_HEADS_PER_MATMUL = 4
ATT_BATCH_PER_STEP = 8


def _attn_block(q_ref, kc_ref, kp_ref, v_ref, o_ref, lse_ref, vt_ref, bias, has_prev):
    blk = ATT_BLOCK
    hd = ATT_HEAD_DIM
    gw = ATT_GROUP_WIDTH
    hpm = ATT_HEADS_PER_MATMUL
    width = hpm * hd

    vt_cur = jnp.concatenate(
        [v_ref[:, c * LANES:(c + 1) * LANES].astype(F32).T.astype(BF16)
         for c in range(gw // LANES)], axis=0)
    vt_prev = jnp.where(has_prev, vt_ref[...], jnp.zeros_like(vt_cur))
    vt_ref[...] = vt_cur

    lane_head = lax.broadcasted_iota(I32, (blk, width), 1) // hd
    per_tile = LANES // hd
    for g in range(ATT_HEADS // hpm):
        feat = slice(g * width, (g + 1) * width)
        q_g = q_ref[:, feat] * (hd ** -0.5)
        q_bd = jnp.concatenate([jnp.where(lane_head == i, q_g, jnp.zeros_like(q_g))
                                for i in range(hpm)], axis=0)
        k_g = jnp.concatenate([kp_ref[:, feat], kc_ref[:, feat]], axis=0)
        s_t = _dot_nt(k_g, q_bd) + bias
        m = jnp.max(s_t, axis=0, keepdims=True)
        p = jnp.exp(s_t - m)
        l = jnp.sum(p, axis=0, keepdims=True)
        v_t = jnp.concatenate([vt_prev[feat, :], vt_cur[feat, :]], axis=1)
        o_t = _dot(v_t, p.astype(BF16))
        lse = m + jnp.log(l)
        inv_l = 1.0 / l
        for c in range(width // LANES):
            tile = []
            for i in range(c * per_tile, (c + 1) * per_tile):
                cols = slice(i * blk, (i + 1) * blk)
                tile.append(o_t[i * hd:(i + 1) * hd, cols] * inv_l[:, cols])
                lse_ref[g * hpm + i:g * hpm + i + 1, :] = lse[:, cols]
            lanes = slice(g * width + c * LANES, g * width + (c + 1) * LANES)
            o_ref[:, lanes] = jnp.concatenate(tile, axis=0).T.astype(BF16)


def _attn_kernel(q_ref, kc_ref, kp_ref, v_ref, o_ref, lse_ref, vt_scr):
    blk = ATT_BLOCK
    has_prev = pl.program_id(2) > 0
    key_i = lax.broadcasted_iota(I32, (2 * blk, blk), 0)
    qry_i = lax.broadcasted_iota(I32, (2 * blk, blk), 1)
    live = (((key_i < blk) & (key_i >= qry_i) & has_prev)
            | ((key_i >= blk) & (key_i - blk <= qry_i)))
    bias = jnp.concatenate([jnp.where(live, 0.0, -jnp.inf)] * ATT_HEADS_PER_MATMUL, axis=1)
    for j in range(ATT_BATCH_PER_STEP):
        _attn_block(q_ref.at[j], kc_ref.at[j], kp_ref.at[j], v_ref.at[j], o_ref.at[j],
                    lse_ref.at[j], vt_scr.at[j], bias, has_prev)


def _attention_group(qkv):
    _, batch, dil, sub_len, gw = qkv.shape
    blk = ATT_BLOCK
    bps = ATT_BATCH_PER_STEP

    def spec(t, prev):
        def index(b, r, i):
            return (t, b, r, jnp.maximum(i - 1, 0) if prev else i, 0)
        return pl.BlockSpec((None, bps, None, blk, gw), index)

    return pl.pallas_call(
        _attn_kernel,
        out_shape=(jax.ShapeDtypeStruct((batch, dil, sub_len, gw), BF16),
                   jax.ShapeDtypeStruct((batch, dil, ATT_HEADS, sub_len), F32)),
        grid=(batch // bps, dil, sub_len // blk),
        in_specs=[spec(0, False), spec(1, False), spec(1, True), spec(2, False)],
        out_specs=(pl.BlockSpec((bps, None, blk, gw), lambda b, r, i: (b, r, i, 0)),
                   pl.BlockSpec((bps, None, ATT_HEADS, blk), lambda b, r, i: (b, r, 0, i))),
        scratch_shapes=[pltpu.VMEM((bps, gw, blk), BF16)],
        compiler_params=_params(("parallel", "parallel", "arbitrary")),
        name=f"dilated_attention_d{dil}",
    )(qkv, qkv, qkv, qkv)


HGRN_ROWS = 256
HGRN_HEADS_PER_STEP = 8


def _hgrn_rows(q, key, bcum, v, gate, st, nw, same_chunk_causal, chunk_mask):
    c = HGRN_CHUNK
    dk = HGRN_HEAD_DIM
    rows = HGRN_ROWS
    nchunk = rows // c

    b3 = bcum.reshape(nchunk, c, dk)
    b_mid = b3[:, c // 2:c // 2 + 1, :]
    b_last = b3[:, c - 1:c, :]
    q3 = q.astype(F32).reshape(nchunk, c, dk)
    q_t3 = q3 * jnp.exp(b3 - b_mid)
    k_t3 = key.reshape(nchunk, c, dk) * jnp.exp(b_mid - b3)
    q_t = q_t3.reshape(rows, dk).astype(BF16)
    k_t = k_t3.reshape(rows, dk).astype(BF16)
    q_in = (q_t3 * jnp.exp(b_mid)).reshape(rows, dk).astype(BF16)
    k_st = (k_t3 * jnp.exp(b_last - b_mid)).reshape(rows, dk).astype(BF16)

    att = jnp.where(same_chunk_causal, _dot_nt(q_t, k_t), 0.0)
    o = _dot(att.astype(BF16), v)

    k_spread = jnp.concatenate([k_st * chunk_mask[j] for j in range(nchunk)], axis=1)
    incr = _dot_tn(v, k_spread)
    decay = jnp.exp(b_last.reshape(nchunk, dk))
    before = []
    for j in range(nchunk):
        before.append(st.astype(BF16))
        st = st * decay[j:j + 1] + incr[:, j * dk:(j + 1) * dk]
    q_spread = jnp.concatenate([q_in * chunk_mask[j] for j in range(nchunk)], axis=1)
    o = o + _dot_nt(q_spread, jnp.concatenate(before, axis=1))

    o = o * lax.rsqrt(jnp.mean(o * o, -1, keepdims=True) + RMS_EPS * dk) * nw
    gate = gate.astype(F32)
    return o * gate * _sigmoid(gate), st


def _hgrn_kernel(q_ref, f_ref, i_ref, g_ref, lb_ref, nw_ref, o_ref, state_scr, mask_scr):
    seq = q_ref.shape[0]
    c = HGRN_CHUNK
    rows = HGRN_ROWS
    dk = HGRN_HEAD_DIM
    width = HGRN_HEADS_PER_STEP * dk
    nw = nw_ref[...]

    r_i = lax.broadcasted_iota(I32, (rows, rows), 0)
    c_i = lax.broadcasted_iota(I32, (rows, rows), 1)
    same_chunk_causal = (r_i // c == c_i // c) & (c_i <= r_i)
    tri = jnp.where(same_chunk_causal, 1.0, 0.0).astype(BF16)
    row_chunk = lax.broadcasted_iota(I32, (rows, dk), 0) // c
    for j in range(rows // c):
        mask_scr[j] = jnp.where(row_chunk == j, 1.0, 0.0).astype(BF16)

    state_scr[...] = jnp.zeros_like(state_scr)

    def body(gi, carry):
        rs = pl.ds(pl.multiple_of(gi * rows, rows), rows)
        lb = lb_ref[...]
        one_m_lb = 1.0 - lb
        z = f_ref[rs, :].astype(F32)
        log_f = jnp.log(lb + one_m_lb * _sigmoid(z))
        key = one_m_lb * _sigmoid(-z)
        p0 = log_f.astype(BF16)
        p1 = (log_f - p0.astype(F32)).astype(BF16)
        both = _dot(tri, jnp.concatenate([p0, p1], axis=1))
        bcum = both[:, :width] + both[:, width:]
        for h in range(HGRN_HEADS_PER_STEP):
            cs = slice(h * dk, (h + 1) * dk)
            o, st = _hgrn_rows(q_ref[rs, cs], key[:, cs], bcum[:, cs], i_ref[rs, cs], g_ref[rs, cs],
                               state_scr[h], nw, same_chunk_causal, mask_scr)
            state_scr[h] = st
            o_ref[rs, cs] = o.astype(BF16)
        return carry

    lax.fori_loop(0, seq // rows, body, 0)


def _hgrn(proj, lower_bound, norm_w, batch, seq):
    n = proj.shape[0]
    dk = HGRN_HEAD_DIM
    width = HGRN_HEADS_PER_STEP * dk
    steps = HGRN_HEADS // HGRN_HEADS_PER_STEP

    def seg(k):
        return pl.BlockSpec((seq, width), lambda b, h: (b, k * steps + h))

    return pl.pallas_call(
        _hgrn_kernel,
        out_shape=jax.ShapeDtypeStruct((n, HGRN_WIDTH), BF16),
        grid=(batch, steps),
        in_specs=[seg(0), seg(1), seg(2), seg(3),
                  pl.BlockSpec((1, width), lambda b, h: (0, h)),
                  pl.BlockSpec((1, dk), lambda b, h: (0, 0))],
        out_specs=pl.BlockSpec((seq, width), lambda b, h: (b, h)),
        scratch_shapes=[pltpu.VMEM((HGRN_HEADS_PER_STEP, dk, dk), F32),
                        pltpu.VMEM((HGRN_ROWS // HGRN_CHUNK, HGRN_ROWS, dk), BF16)],
        compiler_params=_params(("parallel", "parallel")),
        name="hgrn2",
    )(proj, proj, proj, proj, lower_bound, norm_w)


def _pick_first_max(vals, iota, axis, size):
    m = jnp.max(vals, axis=axis, keepdims=True)
    idx = jnp.min(jnp.where(vals == m, iota, size), axis=axis, keepdims=True)
    return m, idx


def _route_tile(logits, bias, base):
    e, tn = logits.shape
    groups = N_EXPERT_GROUPS
    gsz = e // groups
    neg = -jnp.inf

    scores = _sigmoid(logits)
    biased = scores + bias

    b3 = biased.reshape(groups, gsz, tn)
    io3 = lax.broadcasted_iota(I32, b3.shape, 1)
    m1, i1 = _pick_first_max(b3, io3, 1, gsz)
    m2 = jnp.max(jnp.where(io3 == i1, neg, b3), axis=1, keepdims=True)
    grp = m1 + m2
    iog = lax.broadcasted_iota(I32, grp.shape, 0)
    keep = jnp.zeros(grp.shape, F32)
    for _ in range(TOPK_GROUPS):
        _, gi = _pick_first_max(grp, iog, 0, groups)
        hit = iog == gi
        keep = jnp.where(hit, 1.0, keep)
        grp = jnp.where(hit, neg, grp)
    masked = jnp.where(keep > 0.0, b3, neg).reshape(e, tn)

    ioe = lax.broadcasted_iota(I32, (e, tn), 0)
    onehot = jnp.zeros((e, tn), F32)
    idxs, gates = [], []
    for _ in range(TOP_K):
        _, ei = _pick_first_max(masked, ioe, 0, e)
        hit = ioe == ei
        gates.append(jnp.sum(jnp.where(hit, scores, 0.0), axis=0, keepdims=True))
        onehot = jnp.where(hit, 1.0, onehot)
        masked = jnp.where(hit, neg, masked)
        idxs.append(ei)
    gsum = gates[0]
    for g in gates[1:]:
        gsum = gsum + g
    gate = jnp.concatenate(gates, axis=0) / gsum * ROUTED_SCALE

    t_r = lax.broadcasted_iota(I32, (tn, tn), 0)
    t_c = lax.broadcasted_iota(I32, (tn, tn), 1)
    earlier = jnp.where(t_r < t_c, 1.0, 0.0).astype(BF16)
    oh = onehot.astype(BF16)
    before = _dot(oh, earlier) + jnp.concatenate([base] * (tn // LANES), axis=1)
    ranks = [jnp.sum(jnp.where(ioe == ei, before, 0.0), axis=0, keepdims=True) for ei in idxs]
    total = base + _dot(oh, jnp.ones((tn, LANES), BF16))
    return (jnp.concatenate(idxs, axis=0), gate,
            jnp.concatenate(ranks, axis=0).astype(I32), total)


def _mix_kernel(x_ref, gin_ref, bin_ref, o0_ref, o1_ref, o2_ref, l0_ref, l1_ref, l2_ref,
                ob_ref, ga_ref, gb_ref, wa_ref, wb_ref, wo_ref, g1_ref, b1_ref,
                rwh_ref, rwl_ref, rbias_ref,
                h_ref, hp0_ref, hp1_ref, idx_ref, gate_ref, rank_ref, cnt_ref,
                so1_scr, so2_scr, lg_scr, carry_scr):
    tm = x_ref.shape[0]
    gw = ATT_GROUP_WIDTH
    step = pl.program_id(0)

    @pl.when(step == 0)
    def _():
        lg_scr[...] = jnp.zeros_like(lg_scr)
        carry_scr[...] = jnp.zeros_like(carry_scr)

    counts = carry_scr[...]
    idx, gate, rank, total = _route_tile(lg_scr[...], rbias_ref[...], counts)
    idx_ref[...] = idx
    gate_ref[...] = gate
    rank_ref[...] = rank
    counts = jnp.where(step > 0, total, counts)
    carry_scr[...] = counts
    cnt_ref[...] = counts

    for o_ref, so_scr, (_, dil) in ((o1_ref, so1_scr, ATT_GROUPS[1]),
                                    (o2_ref, so2_scr, ATT_GROUPS[2])):
        for r in range(dil):
            o_r = o_ref[r].astype(F32)
            for c in range(gw // LANES):
                so_scr[c, pl.ds(r, tm // dil, stride=dil), :] = o_r[:, c * LANES:(c + 1) * LANES]

    def natural(scr):
        return jnp.concatenate([scr[c] for c in range(gw // LANES)], axis=-1)

    l0, l1, l2 = l0_ref[...], l1_ref[...], l2_ref[...]
    m = jnp.maximum(jnp.maximum(l0, l1), l2)
    e0 = jnp.exp(l0 - m)
    e1 = jnp.exp(l1 - m)
    e2 = jnp.exp(l2 - m)
    inv = 1.0 / (e0 + e1 + e2)
    terms = []
    for w in (e0 * inv, e1 * inv, e2 * inv):
        hi = w.astype(BF16).astype(F32)
        terms += [hi, w - hi]
    terms = jnp.concatenate(terms, axis=0).astype(BF16)
    n_groups = len(ATT_GROUPS)
    t_i = lax.broadcasted_iota(I32, (n_groups * 2 * ATT_HEADS, n_groups * gw), 0)
    c_i = lax.broadcasted_iota(I32, (n_groups * 2 * ATT_HEADS, n_groups * gw), 1)
    spread = jnp.where((t_i // (2 * ATT_HEADS) == c_i // gw)
                       & (t_i % ATT_HEADS == c_i % gw // ATT_HEAD_DIM), 1.0, 0.0).astype(BF16)
    wide = _dot_tn(terms, spread)
    o_att = (wide[:, :gw] * o0_ref[...].astype(F32) + wide[:, gw:2 * gw] * natural(so1_scr)
             + wide[:, 2 * gw:] * natural(so2_scr))

    y_a = _dot(o_att.astype(BF16), wa_ref[...])
    y_b = _dot(ob_ref[...], wb_ref[...])
    merged = _sigmoid(ga_ref[...].astype(F32)) * y_a + _sigmoid(gb_ref[...].astype(F32)) * y_b
    mix = _dot(merged.astype(BF16), wo_ref[...])
    h_in = _layer_norm(x_ref[...], gin_ref[...], bin_ref[...])
    h1 = _layer_norm(DEEPNORM_ALPHA * h_in + mix, g1_ref[...], b1_ref[...])
    h_ref[...] = h1

    hp0_ref[...], hp1_ref[...] = _pack_rows(h1)

    h_hi = h1.astype(BF16)
    h_lo = (h1 - h_hi.astype(F32)).astype(BF16)
    rwh = rwh_ref[...]
    lg_scr[...] = _dot_nt(rwh, h_hi) + _dot_nt(rwh, h_lo) + _dot_nt(rwl_ref[...], h_hi)


def _mix(x2, gin, bin_, att, ob, proj, wa, wb, wo, g1, b1, rwh, rwl, rbias, batch, seq):
    n = x2.shape[0]
    tm = MIX_ROW_TILE
    gw = ATT_GROUP_WIDTH
    tiles_per_seq = seq // tm
    last = n // tm - 1
    (o0, l0), (o1, l1), (o2, l2) = att
    d1, d2 = ATT_GROUPS[1][1], ATT_GROUPS[2][1]

    def tile(i):
        return jnp.minimum(i, last)

    def routed(i):
        return jnp.maximum(i - 1, 0)

    lse_spec = pl.BlockSpec((None, ATT_HEADS, tm),
                            lambda i: (tile(i) // tiles_per_seq, 0, tile(i) % tiles_per_seq))

    def full(shape):
        return pl.BlockSpec(shape, lambda i: (0,) * len(shape))

    def dil_spec(dil):
        return pl.BlockSpec((None, dil, tm // dil, gw),
                            lambda i: (tile(i) // tiles_per_seq, 0, tile(i) % tiles_per_seq, 0))

    def rows(width, col=0):
        return pl.BlockSpec((tm, width), lambda i: (tile(i), col))

    tok_spec = pl.BlockSpec((TOP_K, tm), lambda i: (0, routed(i)))
    return pl.pallas_call(
        _mix_kernel,
        out_shape=(jax.ShapeDtypeStruct((n, D_MODEL), F32),
                   jax.ShapeDtypeStruct((n, PACK_WIDTH), I32),
                   jax.ShapeDtypeStruct((n, PACK_WIDTH), I32),
                   jax.ShapeDtypeStruct((TOP_K, n), I32),
                   jax.ShapeDtypeStruct((TOP_K, n), F32),
                   jax.ShapeDtypeStruct((TOP_K, n), I32),
                   jax.ShapeDtypeStruct((N_EXPERTS, LANES), F32)),
        grid=(n // tm + 1,),
        in_specs=[
            rows(D_MODEL),
            full((1, D_MODEL)), full((1, D_MODEL)),
            rows(gw), dil_spec(d1), dil_spec(d2),
            lse_spec, lse_spec, lse_spec,
            rows(HGRN_WIDTH), rows(D_MODEL, 4), rows(D_MODEL, 5),
            full((gw, D_MODEL)), full((HGRN_WIDTH, D_MODEL)), full((D_MODEL, D_MODEL)),
            full((1, D_MODEL)), full((1, D_MODEL)),
            full((N_EXPERTS, D_MODEL)), full((N_EXPERTS, D_MODEL)), full((N_EXPERTS, 1)),
        ],
        out_specs=(rows(D_MODEL), rows(PACK_WIDTH), rows(PACK_WIDTH),
                   tok_spec, tok_spec, tok_spec, full((N_EXPERTS, LANES))),
        scratch_shapes=[pltpu.VMEM((gw // LANES, tm, LANES), F32),
                        pltpu.VMEM((gw // LANES, tm, LANES), F32),
                        pltpu.VMEM((N_EXPERTS, tm), F32),
                        pltpu.VMEM((N_EXPERTS, LANES), F32)],
        compiler_params=_params(("arbitrary",)),
        name="branch_mix_route",
    )(x2, gin, bin_, o0, o1, o2, l0, l1, l2, ob, proj, proj, wa, wb, wo, g1, b1, rwh, rwl, rbias)


ROW_SLOTS = 16
ROW_GROUP = 4
ROW_AHEAD = ROW_SLOTS - ROW_GROUP
W_SLOTS = 3


def _expert_kernel(sblk_ref, nblk_ref, cnt_ref, xs0_hbm, xs1_hbm, wi_hbm, wo_hbm,
                   ys0_hbm, ys1_hbm, xbuf0, xbuf1, ybuf0, ybuf1, wibuf, wobuf,
                   in_sem, out_sem, w_sem, wi_scr, wo_scr):
    e = pl.program_id(0)
    n_exp = pl.num_programs(0)
    first_blk = sblk_ref[e]
    n_blk = nblk_ref[e]
    n_rows = cnt_ref[e]
    total = sblk_ref[n_exp - 1] + nblk_ref[n_exp - 1]
    xs_hbm, ys_hbm = (xs0_hbm, xs1_hbm), (ys0_hbm, ys1_hbm)
    xbuf, ybuf = (xbuf0, xbuf1), (ybuf0, ybuf1)

    def rows_of(b):
        return pl.ds(pl.multiple_of(b * MOE_BLOCK, MOE_BLOCK), MOE_BLOCK)

    def x_copy(b, part):
        slot = b % ROW_SLOTS
        return pltpu.make_async_copy(xs_hbm[part].at[rows_of(b)], xbuf[part].at[slot],
                                     in_sem.at[part, slot])

    def y_copy(b, part):
        slot = b % ROW_SLOTS
        return pltpu.make_async_copy(ybuf[part].at[slot], ys_hbm[part].at[rows_of(b)],
                                     out_sem.at[part, slot])

    def w_copies(ex):
        slot = ex % W_SLOTS
        return (pltpu.make_async_copy(wi_hbm.at[ex], wibuf.at[slot], w_sem.at[0, slot]),
                pltpu.make_async_copy(wo_hbm.at[ex], wobuf.at[slot], w_sem.at[1, slot]))

    def start_weights(ex):
        @pl.when((ex < n_exp) & (nblk_ref[jnp.minimum(ex, n_exp - 1)] > 0))
        def _():
            for cp in w_copies(ex):
                cp.start()

    @pl.when(e == 0)
    def _():
        for b in range(ROW_AHEAD):
            @pl.when(b < total)
            def _():
                for part in range(PACK_PARTS):
                    x_copy(b, part).start()
        for ex in range(W_SLOTS - 1):
            start_weights(ex)

    start_weights(e + (W_SLOTS - 1))

    def step(j, group):
        b0 = first_blk + j
        for i in range(group):
            for part in range(PACK_PARTS):
                x_copy(b0 + i, part).wait()
        for i in range(group):
            @pl.when(b0 + ROW_AHEAD + i < total)
            def _():
                for part in range(PACK_PARTS):
                    x_copy(b0 + ROW_AHEAD + i, part).start()
        for i in range(group):
            @pl.when(b0 + i >= ROW_SLOTS)
            def _():
                for part in range(PACK_PARTS):
                    y_copy(b0 + i - ROW_SLOTS, part).wait()

        rows = group * MOE_BLOCK
        live = (lax.broadcasted_iota(I32, (rows, PACK_WIDTH), 0)
                < n_rows - j * MOE_BLOCK)
        parts = [jnp.concatenate([xbuf[part][(b0 + i) % ROW_SLOTS] for i in range(group)], axis=0)
                 for part in range(PACK_PARTS)]
        x = _unpack_rows([jnp.where(live, p, 0) for p in parts])
        hu = _dot(x.astype(BF16), wi_scr[...])
        hg = hu[:, :EXPERT_DIM]
        hv = hu[:, EXPERT_DIM:]
        act = (hg * _sigmoid(hg) * hv).astype(BF16)
        packed = _pack_rows(_dot(act, wo_scr[...]))
        for i in range(group):
            for part in range(PACK_PARTS):
                ybuf[part][(b0 + i) % ROW_SLOTS] = packed[part][i * MOE_BLOCK:(i + 1) * MOE_BLOCK]
                y_copy(b0 + i, part).start()

    @pl.when(n_blk > 0)
    def _():
        for cp in w_copies(e):
            cp.wait()
        slot = e % W_SLOTS
        wi_scr[...] = wibuf[slot].astype(BF16)
        wo_scr[...] = wobuf[slot].astype(BF16)

        def full_group(g, carry):
            step(g * ROW_GROUP, ROW_GROUP)
            return carry

        lax.fori_loop(0, n_blk // ROW_GROUP, full_group, 0)
        done = n_blk // ROW_GROUP * ROW_GROUP
        group = ROW_GROUP // 2
        while group >= 1:
            @pl.when((n_blk & group) != 0)
            def _(group=group, done=done):
                step(done, group)
            done = done + (n_blk & group)
            group //= 2

    @pl.when(e == n_exp - 1)
    def _():
        for back in range(1, ROW_SLOTS + 1):
            @pl.when(total >= back)
            def _():
                for part in range(PACK_PARTS):
                    y_copy(total - back, part).wait()


def _experts(start_blk, n_blk, n_rows, xs, w_in_e, w_out_e):
    p = xs[0].shape[0]
    n_exp = w_in_e.shape[0]
    any_spec = pl.BlockSpec(memory_space=pl.ANY)
    row_buf = pltpu.VMEM((ROW_SLOTS, MOE_BLOCK, PACK_WIDTH), I32)
    grid_spec = pltpu.PrefetchScalarGridSpec(
        num_scalar_prefetch=3,
        grid=(n_exp,),
        in_specs=[any_spec, any_spec, any_spec, any_spec],
        out_specs=(any_spec, any_spec),
        scratch_shapes=[row_buf, row_buf, row_buf, row_buf,
                        pltpu.VMEM((W_SLOTS, D_MODEL, 2 * EXPERT_DIM), F32),
                        pltpu.VMEM((W_SLOTS, EXPERT_DIM, D_MODEL), F32),
                        pltpu.SemaphoreType.DMA((PACK_PARTS, ROW_SLOTS)),
                        pltpu.SemaphoreType.DMA((PACK_PARTS, ROW_SLOTS)),
                        pltpu.SemaphoreType.DMA((2, W_SLOTS)),
                        pltpu.VMEM((D_MODEL, 2 * EXPERT_DIM), BF16),
                        pltpu.VMEM((EXPERT_DIM, D_MODEL), BF16)],
    )
    return pl.pallas_call(
        _expert_kernel,
        out_shape=(jax.ShapeDtypeStruct((p, PACK_WIDTH), I32),) * PACK_PARTS,
        grid_spec=grid_spec,
        compiler_params=_params(("arbitrary",)),
        name="routed_experts",
    )(start_blk, n_blk, n_rows, xs[0], xs[1], w_in_e, w_out_e)


def _combine_kernel(h_ref, yg0_ref, yg1_ref, gate_ref, wsi_ref, wso_ref, g2_ref, b2_ref, o_ref):
    h1 = h_ref[...]
    gate = gate_ref[...]
    routed = None
    for k in range(TOP_K):
        y_k = _unpack_rows([yg0_ref[k], yg1_ref[k]]) * gate[:, k:k + 1]
        routed = y_k if routed is None else routed + y_k
    hs = _dot(h1.astype(BF16), wsi_ref[...])
    sg = hs[:, :SHARED_DIM]
    sv = hs[:, SHARED_DIM:]
    shared = _dot((sg * _sigmoid(sg) * sv).astype(BF16), wso_ref[...])
    o_ref[...] = _layer_norm(DEEPNORM_ALPHA * h1 + routed + shared, g2_ref[...], b2_ref[...])


def _combine(h1, yg, gate_nk, wsi, wso, g2, b2):
    n = h1.shape[0]
    tm = OUT_ROW_TILE
    yg_spec = pl.BlockSpec((TOP_K, tm, PACK_WIDTH), lambda i: (0, i, 0))

    def full(shape):
        return pl.BlockSpec(shape, lambda i: (0,) * len(shape))

    return pl.pallas_call(
        _combine_kernel,
        out_shape=jax.ShapeDtypeStruct((n, D_MODEL), F32),
        grid=(n // tm,),
        in_specs=[
            pl.BlockSpec((tm, D_MODEL), lambda i: (i, 0)),
            yg_spec, yg_spec,
            pl.BlockSpec((tm, TOP_K), lambda i: (i, 0)),
            full((D_MODEL, 2 * SHARED_DIM)), full((SHARED_DIM, D_MODEL)),
            full((1, D_MODEL)), full((1, D_MODEL)),
        ],
        out_specs=pl.BlockSpec((tm, D_MODEL), lambda i: (i, 0)),
        compiler_params=_params(("parallel",)),
        name="shared_combine",
    )(h1, yg[0], yg[1], gate_nk, wsi, wso, g2, b2)


def _plan_kernel(idx_ref, rank_ref, cnt_ref, dest_ref, sblk_ref):
    e = cnt_ref.shape[0]
    tn = idx_ref.shape[1]
    cnt = cnt_ref[...]
    nblk = jnp.floor((cnt + (MOE_BLOCK - 1)) * (1.0 / MOE_BLOCK))
    e_r = lax.broadcasted_iota(I32, (e, e), 0)
    e_c = lax.broadcasted_iota(I32, (e, e), 1)
    lower = jnp.where(e_c < e_r, 1.0, 0.0).astype(BF16)
    start_blk = _dot(lower, nblk.astype(BF16))

    ioe = lax.broadcasted_iota(I32, (e, tn), 0)
    start_row = jnp.concatenate([start_blk * MOE_BLOCK] * (tn // LANES), axis=1)
    idx = idx_ref[...]
    dests = [jnp.sum(jnp.where(ioe == idx[k:k + 1], start_row, 0.0), axis=0, keepdims=True)
             for k in range(TOP_K)]
    dest_ref[...] = jnp.concatenate(dests, axis=0).astype(I32) + rank_ref[...]
    sblk_ref[...] = start_blk.astype(I32)


def _plan(idx_t, rank_t, counts):
    k, n = idx_t.shape
    e = counts.shape[0]
    tn = MIX_ROW_TILE
    tok_spec = pl.BlockSpec((k, tn), lambda i: (0, i))
    exp_spec = pl.BlockSpec((e, LANES), lambda i: (0, 0))
    return pl.pallas_call(
        _plan_kernel,
        out_shape=(jax.ShapeDtypeStruct((k, n), I32), jax.ShapeDtypeStruct((e, LANES), I32)),
        grid=(n // tn,),
        in_specs=[tok_spec, tok_spec, exp_spec],
        out_specs=(tok_spec, exp_spec),
        compiler_params=_params(("arbitrary",)),
        name="dispatch_plan",
    )(idx_t, rank_t, counts)


SC_WINDOW = 128


def _sc_mesh():
    return plsc.VectorSubcoreMesh(core_axis_name="core", subcore_axis_name="subcore")


def _dispatch_rows(parts, dest, p_total):
    n, width = parts[0].shape
    top_k = dest.shape[0]
    out_type = (jax.ShapeDtypeStruct((p_total, width), parts[0].dtype),) * len(parts)

    @functools.partial(pl.kernel, mesh=_sc_mesh(), scratch_types=[], out_type=out_type,
                       name="dispatch_rows")
    def scatter(*refs):
        x_hbms = refs[:len(parts)]
        i_hbm = refs[len(parts)]
        o_hbms = refs[len(parts) + 1:]
        for x_hbm, o_hbm in zip(x_hbms, o_hbms):
            def body(x_vmem, i_vmem, o_hbm=o_hbm):
                for k in range(top_k):
                    pltpu.sync_copy(x_vmem, o_hbm.at[i_vmem.at[k]])

            pltpu.emit_pipeline(
                body,
                grid=(n // SC_WINDOW,),
                in_specs=[pl.BlockSpec((SC_WINDOW, width), lambda i: (i, 0)),
                          pl.BlockSpec((top_k, SC_WINDOW), lambda i: (0, i))],
                out_specs=[],
                core_axis_name=("core", "subcore"),
                dimension_semantics=(pltpu.PARALLEL,),
            )(x_hbm, i_hbm)

    return scatter(*parts, dest)


def _gather_rows(parts, idx_flat):
    count = idx_flat.shape[1]
    width = parts[0].shape[1]
    out_type = (jax.ShapeDtypeStruct((count, width), parts[0].dtype),) * len(parts)

    @functools.partial(pl.kernel, mesh=_sc_mesh(), scratch_types=[], out_type=out_type,
                       name="combine_rows")
    def gather(*refs):
        y_hbms = refs[:len(parts)]
        i_hbm = refs[len(parts)]
        o_hbms = refs[len(parts) + 1:]
        for y_hbm, o_hbm in zip(y_hbms, o_hbms):
            def body(i_vmem, o_vmem, y_hbm=y_hbm):
                pltpu.sync_copy(y_hbm.at[i_vmem.at[0]], o_vmem)

            pltpu.emit_pipeline(
                body,
                grid=(count // SC_WINDOW,),
                in_specs=[pl.BlockSpec((1, SC_WINDOW), lambda i: (0, i))],
                out_specs=[pl.BlockSpec((SC_WINDOW, width), lambda i: (i, 0))],
                core_axis_name=("core", "subcore"),
                dimension_semantics=(pltpu.PARALLEL,),
            )(i_hbm, o_hbm)

    return gather(*parts, idx_flat)


def kernel(x, ln_in_g, ln_in_b, w_in, hgrn_lb_logits, hgrn_norm_w, w_branch_att, w_branch_hgrn,
           w_out, ln1_g, ln1_b, router_w, router_bias, expert_w_in, expert_w_out, shared_w_in,
           shared_w_out, ln2_g, ln2_b):
    batch, seq, d = x.shape
    n = batch * seq
    x2 = x.reshape(n, d)
    row = lambda v: v.reshape(1, -1).astype(F32)

    lower_bounds = jnp.cumsum(jax.nn.softmax(hgrn_lb_logits.astype(F32), axis=0), axis=0)
    l = 0
    w_l = w_in[l].astype(BF16)
    gin, bin_ = row(ln_in_g), row(ln_in_b)

    proj = _inproj_main(x2, gin, bin_, w_l)
    qkv0, qkv1, qkv2 = _inproj_qkv(x2, gin, bin_, w_l, batch, seq)
    qkv0 = qkv0.reshape(3, batch, 1, seq, ATT_GROUP_WIDTH)
    att = []
    for qkv in (qkv0, qkv1, qkv2):
        o_g, lse_g = _attention_group(qkv)
        att.append((o_g, lse_g.transpose(0, 2, 3, 1).reshape(batch, ATT_HEADS, seq)))
    att[0] = (att[0][0].reshape(n, ATT_GROUP_WIDTH), att[0][1])

    ob = _hgrn(proj, row(lower_bounds[l]), row(hgrn_norm_w[l]), batch, seq)

    rw_t = router_w[l].T.astype(F32)
    rwh = rw_t.astype(BF16)
    rwl = (rw_t - rwh.astype(F32)).astype(BF16)
    h1, hp0, hp1, idx_t, gate_t, rank_t, counts = _mix(
        x2, gin, bin_, att, ob, proj,
        w_branch_att[l].astype(BF16), w_branch_hgrn[l].astype(BF16), w_out[l].astype(BF16),
        row(ln1_g[l]), row(ln1_b[l]), rwh, rwl, router_bias[l].reshape(-1, 1).astype(F32),
        batch, seq)
    p_total = n * TOP_K + N_EXPERTS * MOE_BLOCK
    dest, start_blk = _plan(idx_t, rank_t, counts)
    n_rows = counts[:, 0].astype(I32)
    n_blk = (n_rows + (MOE_BLOCK - 1)) // MOE_BLOCK
    xs = _dispatch_rows((hp0, hp1), dest, p_total)
    ys = _experts(start_blk[:, 0], n_blk, n_rows, xs, expert_w_in[l], expert_w_out[l])
    yg = [g.reshape(TOP_K, n, PACK_WIDTH)
          for g in _gather_rows(ys, dest.reshape(1, TOP_K * n))]
    out = _combine(h1, yg, gate_t.T, shared_w_in[l].astype(BF16), shared_w_out[l].astype(BF16),
                   row(ln2_g[l]), row(ln2_b[l]))
    return out.reshape(batch, seq, d)
```

```python
import functools

import jax
import jax.numpy as jnp
from jax import lax
from jax.experimental import pallas as pl
from jax.experimental.pallas import tpu as pltpu
from jax.experimental.pallas import tpu_sc as plsc

F32 = jnp.float32
BF16 = jnp.bfloat16
U32 = jnp.uint32
I32 = jnp.int32

D_MODEL = 1024
ATT_GROUPS = ((128, 1), (512, 4), (2048, 16))
ATT_HEADS = 8
ATT_HEAD_DIM = 64
ATT_GROUP_WIDTH = ATT_HEADS * ATT_HEAD_DIM
ATT_WIDTH = len(ATT_GROUPS) * ATT_GROUP_WIDTH
ATT_BLOCK = 128
HGRN_HEAD_DIM = 128
HGRN_HEADS = D_MODEL // HGRN_HEAD_DIM
HGRN_WIDTH = HGRN_HEADS * HGRN_HEAD_DIM
HGRN_CHUNK = 32
N_EXPERTS = 256
TOP_K = 8
N_EXPERT_GROUPS = 8
TOPK_GROUPS = 4
EXPERT_DIM = 256
SHARED_DIM = 256
ROUTED_SCALE = 2.5
MOE_BLOCK = 128
LN_EPS = 1e-5
RMS_EPS = 1e-6
DEPTH = 1
DEEPNORM_ALPHA = (2 * DEPTH) ** 0.25

VMEM_LIMIT = 56 * 1024 * 1024
LANES = 128

ROW_TILE = 1024
MAIN_COL_TILE = 1536
MAIN_COL_STEPS = 4
MIX_ROW_TILE = 512
OUT_ROW_TILE = 512


def _params(sem, vmem=VMEM_LIMIT):
    return pltpu.CompilerParams(dimension_semantics=sem, vmem_limit_bytes=vmem)


def _layer_norm(x, g, b):
    mu = jnp.mean(x, -1, keepdims=True)
    xc = x - mu
    var = jnp.mean(xc * xc, -1, keepdims=True)
    return xc * lax.rsqrt(var + LN_EPS) * g + b


def _sigmoid(x):
    return 1.0 / (1.0 + jnp.exp(-x))


def _dot(a, b):
    return jnp.dot(a, b, preferred_element_type=F32)


def _dot_nt(a, b):
    return lax.dot_general(a, b, (((1,), (1,)), ((), ())), preferred_element_type=F32)


def _dot_tn(a, b):
    return lax.dot_general(a, b, (((0,), (0,)), ((), ())), preferred_element_type=F32)


def _pack_bf16_pair(lo, hi):
    lo_bits = pltpu.bitcast(lo.astype(BF16).astype(F32), U32) >> 16
    hi_bits = pltpu.bitcast(hi.astype(BF16).astype(F32), U32) & jnp.uint32(0xFFFF0000)
    return hi_bits | lo_bits


def _unpack_bf16_pair(w):
    lo = pltpu.bitcast(w << 16, F32)
    hi = pltpu.bitcast(w & jnp.uint32(0xFFFF0000), F32)
    return lo, hi


PACK_PARTS = 2
PACK_WIDTH = D_MODEL // 2 // PACK_PARTS


def _pack_rows(v):
    half = D_MODEL // 2
    parts = []
    for j in range(PACK_PARTS):
        lo = v[:, j * PACK_WIDTH:(j + 1) * PACK_WIDTH]
        hi = v[:, half + j * PACK_WIDTH:half + (j + 1) * PACK_WIDTH]
        parts.append(pltpu.bitcast(_pack_bf16_pair(lo, hi), I32))
    return parts


def _unpack_rows(parts):
    pairs = [_unpack_bf16_pair(pltpu.bitcast(p, U32)) for p in parts]
    return jnp.concatenate([lo for lo, _ in pairs] + [hi for _, hi in pairs], axis=-1)


def _inproj_main_kernel(x_ref, g_ref, b_ref, w_ref, o_ref, h0_ref, h1_ref, h2_ref, hf_scr):
    j = pl.program_id(1)

    @pl.when(j == 0)
    def _():
        hf = _layer_norm(x_ref[...], g_ref[...], b_ref[...])
        h0_ref[...] = hf.astype(BF16)
        for c in range(D_MODEL // LANES):
            hf_scr[c] = hf[:, c * LANES:(c + 1) * LANES]

    for h_ref, (_, dil) in ((h1_ref, ATT_GROUPS[1]), (h2_ref, ATT_GROUPS[2])):
        rows = ROW_TILE // dil
        share = dil // MAIN_COL_STEPS
        for q in range(share):
            r = j * share + q
            dst = pl.ds(pl.multiple_of(r * rows, rows), rows)
            for c in range(D_MODEL // LANES):
                h_ref[dst, c * LANES:(c + 1) * LANES] = (
                    hf_scr[c, pl.ds(r, rows, stride=dil), :].astype(BF16))

    o_ref[...] = _dot(h0_ref[...], w_ref[...]).astype(BF16)


def _inproj_main(x2, g, b, w_all):
    n = x2.shape[0]
    first = 3 * ATT_WIDTH // MAIN_COL_TILE
    width = w_all.shape[1] - 3 * ATT_WIDTH
    assert width == MAIN_COL_STEPS * MAIN_COL_TILE
    assert all(dil % MAIN_COL_STEPS == 0 for _, dil in ATT_GROUPS[1:])
    act = jax.ShapeDtypeStruct((n, D_MODEL), BF16)
    act_spec = pl.BlockSpec((ROW_TILE, D_MODEL), lambda i, j: (i, 0))
    return pl.pallas_call(
        _inproj_main_kernel,
        out_shape=(jax.ShapeDtypeStruct((n, width), BF16), act, act, act),
        grid=(n // ROW_TILE, MAIN_COL_STEPS),
        in_specs=[
            pl.BlockSpec((ROW_TILE, D_MODEL), lambda i, j: (i, 0)),
            pl.BlockSpec((1, D_MODEL), lambda i, j: (0, 0)),
            pl.BlockSpec((1, D_MODEL), lambda i, j: (0, 0)),
            pl.BlockSpec((D_MODEL, MAIN_COL_TILE), lambda i, j: (0, first + j)),
        ],
        out_specs=(pl.BlockSpec((ROW_TILE, MAIN_COL_TILE), lambda i, j: (i, j)),
                   act_spec, act_spec, act_spec),
        scratch_shapes=[pltpu.VMEM((D_MODEL // LANES, ROW_TILE, LANES), F32)],
        compiler_params=_params(("parallel", "arbitrary")),
        name="inproj_main",
    )(x2, g, b, w_all)


def _inproj_qkv_kernel(h0_ref, h1_ref, h2_ref, w_ref, o0_ref, o1_ref, o2_ref):
    gw = ATT_GROUP_WIDTH
    o0_ref[0] = _dot(h0_ref[...], w_ref[:, 0:gw]).astype(BF16)
    d1 = ATT_GROUPS[1][1]
    o1_ref[0, 0] = _dot(h1_ref[...], w_ref[:, gw:2 * gw]).astype(BF16).reshape(d1, ROW_TILE // d1, gw)
    d2 = ATT_GROUPS[2][1]
    o2_ref[0, 0] = _dot(h2_ref[...], w_ref[:, 2 * gw:3 * gw]).astype(BF16).reshape(d2, ROW_TILE // d2, gw)


def _inproj_qkv(acts, w_all, batch, seq):
    n = acts[0].shape[0]
    gw = ATT_GROUP_WIDTH
    tiles_per_seq = seq // ROW_TILE
    d1, d2 = ATT_GROUPS[1][1], ATT_GROUPS[2][1]
    out_shape = (
        jax.ShapeDtypeStruct((3, n, gw), BF16),
        jax.ShapeDtypeStruct((3, batch, d1, seq // d1, gw), BF16),
        jax.ShapeDtypeStruct((3, batch, d2, seq // d2, gw), BF16),
    )
    act_spec = pl.BlockSpec((ROW_TILE, D_MODEL), lambda i, t: (i, 0))
    return pl.pallas_call(
        _inproj_qkv_kernel,
        out_shape=out_shape,
        grid=(n // ROW_TILE, 3),
        in_specs=[act_spec, act_spec, act_spec,
                  pl.BlockSpec((D_MODEL, ATT_WIDTH), lambda i, t: (0, t))],
        out_specs=(
            pl.BlockSpec((1, ROW_TILE, gw), lambda i, t: (t, i, 0)),
            pl.BlockSpec((1, 1, d1, ROW_TILE // d1, gw),
                         lambda i, t: (t, i // tiles_per_seq, 0, i % tiles_per_seq, 0)),
            pl.BlockSpec((1, 1, d2, ROW_TILE // d2, gw),
                         lambda i, t: (t, i // tiles_per_seq, 0, i % tiles_per_seq, 0)),
        ),
        compiler_params=_params(("parallel", "arbitrary")),
        name="inproj_qkv",
    )(*acts, w_all)


ATT_HEADS_PER_MATMUL = 4
ATT_BATCH_PER_STEP = 8


def _attn_block(q_ref, kc_ref, kp_ref, v_ref, o_ref, lse_ref, vt_ref, bias, has_prev):
    blk = ATT_BLOCK
    hd = ATT_HEAD_DIM
    gw = ATT_GROUP_WIDTH
    hpm = ATT_HEADS_PER_MATMUL
    width = hpm * hd

    vt_cur = jnp.concatenate(
        [v_ref[:, c * LANES:(c + 1) * LANES].astype(F32).T.astype(BF16)
         for c in range(gw // LANES)], axis=0)
    vt_prev = jnp.where(has_prev, vt_ref[...], jnp.zeros_like(vt_cur))
    vt_ref[...] = vt_cur

    lane_head = lax.broadcasted_iota(I32, (blk, width), 1) // hd
    per_tile = LANES // hd
    for g in range(ATT_HEADS // hpm):
        feat = slice(g * width, (g + 1) * width)
        q_g = q_ref[:, feat] * (hd ** -0.5)
        q_bd = jnp.concatenate([jnp.where(lane_head == i, q_g, jnp.zeros_like(q_g))
                                for i in range(hpm)], axis=0)
        k_g = jnp.concatenate([kp_ref[:, feat], kc_ref[:, feat]], axis=0)
        s_t = _dot_nt(k_g, q_bd) + bias
        m = jnp.max(s_t, axis=0, keepdims=True)
        p = jnp.exp(s_t - m)
        l = jnp.sum(p, axis=0, keepdims=True)
        v_t = jnp.concatenate([vt_prev[feat, :], vt_cur[feat, :]], axis=1)
        o_t = _dot(v_t, p.astype(BF16))
        lse = m + jnp.log(l)
        inv_l = 1.0 / l
        for c in range(width // LANES):
            tile = []
            for i in range(c * per_tile, (c + 1) * per_tile):
                cols = slice(i * blk, (i + 1) * blk)
                tile.append(o_t[i * hd:(i + 1) * hd, cols] * inv_l[:, cols])
                lse_ref[g * hpm + i:g * hpm + i + 1, :] = lse[:, cols]
            lanes = slice(g * width + c * LANES, g * width + (c + 1) * LANES)
            o_ref[:, lanes] = jnp.concatenate(tile, axis=0).T.astype(BF16)


def _attn_kernel(q_ref, kc_ref, kp_ref, v_ref, o_ref, lse_ref, vt_scr):
    blk = ATT_BLOCK
    has_prev = pl.program_id(2) > 0
    key_i = lax.broadcasted_iota(I32, (2 * blk, blk), 0)
    qry_i = lax.broadcasted_iota(I32, (2 * blk, blk), 1)
    live = (((key_i < blk) & (key_i >= qry_i) & has_prev)
            | ((key_i >= blk) & (key_i - blk <= qry_i)))
    bias = jnp.concatenate([jnp.where(live, 0.0, -jnp.inf)] * ATT_HEADS_PER_MATMUL, axis=1)
    for j in range(ATT_BATCH_PER_STEP):
        _attn_block(q_ref.at[j], kc_ref.at[j], kp_ref.at[j], v_ref.at[j], o_ref.at[j],
                    lse_ref.at[j], vt_scr.at[j], bias, has_prev)


def _attention_group(qkv):
    _, batch, dil, sub_len, gw = qkv.shape
    blk = ATT_BLOCK
    bps = ATT_BATCH_PER_STEP

    def spec(t, prev):
        def index(b, r, i):
            return (t, b, r, jnp.maximum(i - 1, 0) if prev else i, 0)
        return pl.BlockSpec((None, bps, None, blk, gw), index)

    return pl.pallas_call(
        _attn_kernel,
        out_shape=(jax.ShapeDtypeStruct((batch, dil, sub_len, gw), BF16),
                   jax.ShapeDtypeStruct((batch, dil, ATT_HEADS, sub_len), F32)),
        grid=(batch // bps, dil, sub_len // blk),
        in_specs=[spec(0, False), spec(1, False), spec(1, True), spec(2, False)],
        out_specs=(pl.BlockSpec((bps, None, blk, gw), lambda b, r, i: (b, r, i, 0)),
                   pl.BlockSpec((bps, None, ATT_HEADS, blk), lambda b, r, i: (b, r, 0, i))),
        scratch_shapes=[pltpu.VMEM((bps, gw, blk), BF16)],
        compiler_params=_params(("parallel", "parallel", "arbitrary")),
        name=f"dilated_attention_d{dil}",
    )(qkv, qkv, qkv, qkv)


HGRN_ROWS = 256
HGRN_HEADS_PER_STEP = 8


def _hgrn_rows(q, key, bcum, v, gate, st, nw, same_chunk_causal, chunk_mask):
    c = HGRN_CHUNK
    dk = HGRN_HEAD_DIM
    rows = HGRN_ROWS
    nchunk = rows // c

    b3 = bcum.reshape(nchunk, c, dk)
    b_mid = b3[:, c // 2:c // 2 + 1, :]
    b_last = b3[:, c - 1:c, :]
    q3 = q.astype(F32).reshape(nchunk, c, dk)
    q_t3 = q3 * jnp.exp(b3 - b_mid)
    k_t3 = key.reshape(nchunk, c, dk) * jnp.exp(b_mid - b3)
    q_t = q_t3.reshape(rows, dk).astype(BF16)
    k_t = k_t3.reshape(rows, dk).astype(BF16)
    q_in = (q_t3 * jnp.exp(b_mid)).reshape(rows, dk).astype(BF16)
    k_st = (k_t3 * jnp.exp(b_last - b_mid)).reshape(rows, dk).astype(BF16)

    att = jnp.where(same_chunk_causal, _dot_nt(q_t, k_t), 0.0)
    o = _dot(att.astype(BF16), v)

    k_spread = jnp.concatenate([k_st * chunk_mask[j] for j in range(nchunk)], axis=1)
    incr = _dot_tn(v, k_spread)
    decay = jnp.exp(b_last.reshape(nchunk, dk))
    before = []
    for j in range(nchunk):
        before.append(st.astype(BF16))
        st = st * decay[j:j + 1] + incr[:, j * dk:(j + 1) * dk]
    q_spread = jnp.concatenate([q_in * chunk_mask[j] for j in range(nchunk)], axis=1)
    o = o + _dot_nt(q_spread, jnp.concatenate(before, axis=1))

    o = o * lax.rsqrt(jnp.mean(o * o, -1, keepdims=True) + RMS_EPS * dk) * nw
    gate = gate.astype(F32)
    return o * gate * _sigmoid(gate), st


def _hgrn_kernel(q_ref, f_ref, i_ref, g_ref, lb_ref, nw_ref, o_ref, state_scr, mask_scr):
    seq = q_ref.shape[0]
    c = HGRN_CHUNK
    rows = HGRN_ROWS
    dk = HGRN_HEAD_DIM
    width = HGRN_HEADS_PER_STEP * dk
    nw = nw_ref[...]

    r_i = lax.broadcasted_iota(I32, (rows, rows), 0)
    c_i = lax.broadcasted_iota(I32, (rows, rows), 1)
    same_chunk_causal = (r_i // c == c_i // c) & (c_i <= r_i)
    tri = jnp.where(same_chunk_causal, 1.0, 0.0).astype(BF16)
    row_chunk = lax.broadcasted_iota(I32, (rows, dk), 0) // c
    for j in range(rows // c):
        mask_scr[j] = jnp.where(row_chunk == j, 1.0, 0.0).astype(BF16)

    state_scr[...] = jnp.zeros_like(state_scr)

    def body(gi, carry):
        rs = pl.ds(pl.multiple_of(gi * rows, rows), rows)
        lb = lb_ref[...]
        one_m_lb = 1.0 - lb
        z = f_ref[rs, :].astype(F32)
        log_f = jnp.log(lb + one_m_lb * _sigmoid(z))
        key = one_m_lb * _sigmoid(-z)
        p0 = log_f.astype(BF16)
        p1 = (log_f - p0.astype(F32)).astype(BF16)
        both = _dot(tri, jnp.concatenate([p0, p1], axis=1))
        bcum = both[:, :width] + both[:, width:]
        for h in range(HGRN_HEADS_PER_STEP):
            cs = slice(h * dk, (h + 1) * dk)
            o, st = _hgrn_rows(q_ref[rs, cs], key[:, cs], bcum[:, cs], i_ref[rs, cs], g_ref[rs, cs],
                               state_scr[h], nw, same_chunk_causal, mask_scr)
            state_scr[h] = st
            o_ref[rs, cs] = o.astype(BF16)
        return carry

    lax.fori_loop(0, seq // rows, body, 0)


def _hgrn(proj, lower_bound, norm_w, batch, seq):
    n = proj.shape[0]
    dk = HGRN_HEAD_DIM
    width = HGRN_HEADS_PER_STEP * dk
    steps = HGRN_HEADS // HGRN_HEADS_PER_STEP

    def seg(k):
        return pl.BlockSpec((seq, width), lambda b, h: (b, k * steps + h))

    return pl.pallas_call(
        _hgrn_kernel,
        out_shape=jax.ShapeDtypeStruct((n, HGRN_WIDTH), BF16),
        grid=(batch, steps),
        in_specs=[seg(0), seg(1), seg(2), seg(3),
                  pl.BlockSpec((1, width), lambda b, h: (0, h)),
                  pl.BlockSpec((1, dk), lambda b, h: (0, 0))],
        out_specs=pl.BlockSpec((seq, width), lambda b, h: (b, h)),
        scratch_shapes=[pltpu.VMEM((HGRN_HEADS_PER_STEP, dk, dk), F32),
                        pltpu.VMEM((HGRN_ROWS // HGRN_CHUNK, HGRN_ROWS, dk), BF16)],
        compiler_params=_params(("parallel", "parallel")),
        name="hgrn2",
    )(proj, proj, proj, proj, lower_bound, norm_w)


def _pick_first_max(vals, iota, axis, size):
    m = jnp.max(vals, axis=axis, keepdims=True)
    idx = jnp.min(jnp.where(vals == m, iota, size), axis=axis, keepdims=True)
    return m, idx


def _route_tile(logits, bias, base):
    e, tn = logits.shape
    groups = N_EXPERT_GROUPS
    gsz = e // groups
    neg = -jnp.inf

    scores = _sigmoid(logits)
    biased = scores + bias

    b3 = biased.reshape(groups, gsz, tn)
    io3 = lax.broadcasted_iota(I32, b3.shape, 1)
    m1, i1 = _pick_first_max(b3, io3, 1, gsz)
    m2 = jnp.max(jnp.where(io3 == i1, neg, b3), axis=1, keepdims=True)
    grp = m1 + m2
    iog = lax.broadcasted_iota(I32, grp.shape, 0)
    keep = jnp.zeros(grp.shape, F32)
    for _ in range(TOPK_GROUPS):
        _, gi = _pick_first_max(grp, iog, 0, groups)
        hit = iog == gi
        keep = jnp.where(hit, 1.0, keep)
        grp = jnp.where(hit, neg, grp)
    masked = jnp.where(keep > 0.0, b3, neg).reshape(e, tn)

    ioe = lax.broadcasted_iota(I32, (e, tn), 0)
    onehot = jnp.zeros((e, tn), F32)
    idxs, gates = [], []
    for _ in range(TOP_K):
        _, ei = _pick_first_max(masked, ioe, 0, e)
        hit = ioe == ei
        gates.append(jnp.sum(jnp.where(hit, scores, 0.0), axis=0, keepdims=True))
        onehot = jnp.where(hit, 1.0, onehot)
        masked = jnp.where(hit, neg, masked)
        idxs.append(ei)
    gsum = gates[0]
    for g in gates[1:]:
        gsum = gsum + g
    gate = jnp.concatenate(gates, axis=0) / gsum * ROUTED_SCALE

    t_r = lax.broadcasted_iota(I32, (tn, tn), 0)
    t_c = lax.broadcasted_iota(I32, (tn, tn), 1)
    earlier = jnp.where(t_r < t_c, 1.0, 0.0).astype(BF16)
    oh = onehot.astype(BF16)
    before = _dot(oh, earlier) + jnp.concatenate([base] * (tn // LANES), axis=1)
    ranks = [jnp.sum(jnp.where(ioe == ei, before, 0.0), axis=0, keepdims=True) for ei in idxs]
    total = base + _dot(oh, jnp.ones((tn, LANES), BF16))
    return (jnp.concatenate(idxs, axis=0), gate,
            jnp.concatenate(ranks, axis=0).astype(I32), total)


def _mix_kernel(x_ref, gin_ref, bin_ref, o0_ref, o1_ref, o2_ref, l0_ref, l1_ref, l2_ref,
                ob_ref, ga_ref, gb_ref, wa_ref, wb_ref, wo_ref, g1_ref, b1_ref,
                rwh_ref, rwl_ref, rbias_ref,
                h_ref, hp0_ref, hp1_ref, idx_ref, gate_ref, rank_ref, cnt_ref,
                so1_scr, so2_scr, lg_scr, carry_scr):
    tm = x_ref.shape[0]
    gw = ATT_GROUP_WIDTH
    step = pl.program_id(0)

    @pl.when(step == 0)
    def _():
        lg_scr[...] = jnp.zeros_like(lg_scr)
        carry_scr[...] = jnp.zeros_like(carry_scr)

    counts = carry_scr[...]
    idx, gate, rank, total = _route_tile(lg_scr[...], rbias_ref[...], counts)
    idx_ref[...] = idx
    gate_ref[...] = gate
    rank_ref[...] = rank
    counts = jnp.where(step > 0, total, counts)
    carry_scr[...] = counts
    cnt_ref[...] = counts

    for o_ref, so_scr, (_, dil) in ((o1_ref, so1_scr, ATT_GROUPS[1]),
                                    (o2_ref, so2_scr, ATT_GROUPS[2])):
        for r in range(dil):
            o_r = o_ref[r].astype(F32)
            for c in range(gw // LANES):
                so_scr[c, pl.ds(r, tm // dil, stride=dil), :] = o_r[:, c * LANES:(c + 1) * LANES]

    def natural(scr):
        return jnp.concatenate([scr[c] for c in range(gw // LANES)], axis=-1)

    l0, l1, l2 = l0_ref[...], l1_ref[...], l2_ref[...]
    m = jnp.maximum(jnp.maximum(l0, l1), l2)
    e0 = jnp.exp(l0 - m)
    e1 = jnp.exp(l1 - m)
    e2 = jnp.exp(l2 - m)
    inv = 1.0 / (e0 + e1 + e2)
    terms = []
    for w in (e0 * inv, e1 * inv, e2 * inv):
        hi = w.astype(BF16).astype(F32)
        terms += [hi, w - hi]
    terms = jnp.concatenate(terms, axis=0).astype(BF16)
    n_groups = len(ATT_GROUPS)
    t_i = lax.broadcasted_iota(I32, (n_groups * 2 * ATT_HEADS, n_groups * gw), 0)
    c_i = lax.broadcasted_iota(I32, (n_groups * 2 * ATT_HEADS, n_groups * gw), 1)
    spread = jnp.where((t_i // (2 * ATT_HEADS) == c_i // gw)
                       & (t_i % ATT_HEADS == c_i % gw // ATT_HEAD_DIM), 1.0, 0.0).astype(BF16)
    wide = _dot_tn(terms, spread)
    o_att = (wide[:, :gw] * o0_ref[...].astype(F32) + wide[:, gw:2 * gw] * natural(so1_scr)
             + wide[:, 2 * gw:] * natural(so2_scr))

    y_a = _dot(o_att.astype(BF16), wa_ref[...])
    y_b = _dot(ob_ref[...], wb_ref[...])
    merged = _sigmoid(ga_ref[...].astype(F32)) * y_a + _sigmoid(gb_ref[...].astype(F32)) * y_b
    mix = _dot(merged.astype(BF16), wo_ref[...])
    h_in = _layer_norm(x_ref[...], gin_ref[...], bin_ref[...])
    h1 = _layer_norm(DEEPNORM_ALPHA * h_in + mix, g1_ref[...], b1_ref[...])
    h_ref[...] = h1

    hp0_ref[...], hp1_ref[...] = _pack_rows(h1)

    h_hi = h1.astype(BF16)
    h_lo = (h1 - h_hi.astype(F32)).astype(BF16)
    rwh = rwh_ref[...]
    lg_scr[...] = _dot_nt(rwh, h_hi) + _dot_nt(rwh, h_lo) + _dot_nt(rwl_ref[...], h_hi)


def _mix(x2, gin, bin_, att, ob, proj, wa, wb, wo, g1, b1, rwh, rwl, rbias, batch, seq):
    n = x2.shape[0]
    tm = MIX_ROW_TILE
    gw = ATT_GROUP_WIDTH
    tiles_per_seq = seq // tm
    last = n // tm - 1
    (o0, l0), (o1, l1), (o2, l2) = att
    d1, d2 = ATT_GROUPS[1][1], ATT_GROUPS[2][1]

    def tile(i):
        return jnp.minimum(i, last)

    def routed(i):
        return jnp.maximum(i - 1, 0)

    lse_spec = pl.BlockSpec((None, ATT_HEADS, tm),
                            lambda i: (tile(i) // tiles_per_seq, 0, tile(i) % tiles_per_seq))

    def full(shape):
        return pl.BlockSpec(shape, lambda i: (0,) * len(shape))

    def dil_spec(dil):
        return pl.BlockSpec((None, dil, tm // dil, gw),
                            lambda i: (tile(i) // tiles_per_seq, 0, tile(i) % tiles_per_seq, 0))

    def rows(width, col=0):
        return pl.BlockSpec((tm, width), lambda i: (tile(i), col))

    tok_spec = pl.BlockSpec((TOP_K, tm), lambda i: (0, routed(i)))
    return pl.pallas_call(
        _mix_kernel,
        out_shape=(jax.ShapeDtypeStruct((n, D_MODEL), F32),
                   jax.ShapeDtypeStruct((n, PACK_WIDTH), I32),
                   jax.ShapeDtypeStruct((n, PACK_WIDTH), I32),
                   jax.ShapeDtypeStruct((TOP_K, n), I32),
                   jax.ShapeDtypeStruct((TOP_K, n), F32),
                   jax.ShapeDtypeStruct((TOP_K, n), I32),
                   jax.ShapeDtypeStruct((N_EXPERTS, LANES), F32)),
        grid=(n // tm + 1,),
        in_specs=[
            rows(D_MODEL),
            full((1, D_MODEL)), full((1, D_MODEL)),
            rows(gw), dil_spec(d1), dil_spec(d2),
            lse_spec, lse_spec, lse_spec,
            rows(HGRN_WIDTH), rows(D_MODEL, 4), rows(D_MODEL, 5),
            full((gw, D_MODEL)), full((HGRN_WIDTH, D_MODEL)), full((D_MODEL, D_MODEL)),
            full((1, D_MODEL)), full((1, D_MODEL)),
            full((N_EXPERTS, D_MODEL)), full((N_EXPERTS, D_MODEL)), full((N_EXPERTS, 1)),
        ],
        out_specs=(rows(D_MODEL), rows(PACK_WIDTH), rows(PACK_WIDTH),
                   tok_spec, tok_spec, tok_spec, full((N_EXPERTS, LANES))),
        scratch_shapes=[pltpu.VMEM((gw // LANES, tm, LANES), F32),
                        pltpu.VMEM((gw // LANES, tm, LANES), F32),
                        pltpu.VMEM((N_EXPERTS, tm), F32),
                        pltpu.VMEM((N_EXPERTS, LANES), F32)],
        compiler_params=_params(("arbitrary",)),
        name="branch_mix_route",
    )(x2, gin, bin_, o0, o1, o2, l0, l1, l2, ob, proj, proj, wa, wb, wo, g1, b1, rwh, rwl, rbias)


ROW_SLOTS = 16
ROW_GROUP = 4
ROW_AHEAD = ROW_SLOTS - ROW_GROUP
W_SLOTS = 3


def _expert_kernel(sblk_ref, nblk_ref, cnt_ref, xs0_hbm, xs1_hbm, wi_hbm, wo_hbm,
                   ys0_hbm, ys1_hbm, xbuf0, xbuf1, ybuf0, ybuf1, wibuf, wobuf,
                   in_sem, out_sem, w_sem, wi_scr, wo_scr):
    e = pl.program_id(0)
    n_exp = pl.num_programs(0)
    first_blk = sblk_ref[e]
    n_blk = nblk_ref[e]
    n_rows = cnt_ref[e]
    total = sblk_ref[n_exp - 1] + nblk_ref[n_exp - 1]
    xs_hbm, ys_hbm = (xs0_hbm, xs1_hbm), (ys0_hbm, ys1_hbm)
    xbuf, ybuf = (xbuf0, xbuf1), (ybuf0, ybuf1)

    def rows_of(b):
        return pl.ds(pl.multiple_of(b * MOE_BLOCK, MOE_BLOCK), MOE_BLOCK)

    def x_copy(b, part):
        slot = b % ROW_SLOTS
        return pltpu.make_async_copy(xs_hbm[part].at[rows_of(b)], xbuf[part].at[slot],
                                     in_sem.at[part, slot])

    def y_copy(b, part):
        slot = b % ROW_SLOTS
        return pltpu.make_async_copy(ybuf[part].at[slot], ys_hbm[part].at[rows_of(b)],
                                     out_sem.at[part, slot])

    def w_copies(ex):
        slot = ex % W_SLOTS
        return (pltpu.make_async_copy(wi_hbm.at[ex], wibuf.at[slot], w_sem.at[0, slot]),
                pltpu.make_async_copy(wo_hbm.at[ex], wobuf.at[slot], w_sem.at[1, slot]))

    def start_weights(ex):
        @pl.when((ex < n_exp) & (nblk_ref[jnp.minimum(ex, n_exp - 1)] > 0))
        def _():
            for cp in w_copies(ex):
                cp.start()

    @pl.when(e == 0)
    def _():
        for b in range(ROW_AHEAD):
            @pl.when(b < total)
            def _():
                for part in range(PACK_PARTS):
                    x_copy(b, part).start()
        for ex in range(W_SLOTS - 1):
            start_weights(ex)

    start_weights(e + (W_SLOTS - 1))

    def step(j, group):
        b0 = first_blk + j
        for i in range(group):
            for part in range(PACK_PARTS):
                x_copy(b0 + i, part).wait()
        for i in range(group):
            @pl.when(b0 + ROW_AHEAD + i < total)
            def _():
                for part in range(PACK_PARTS):
                    x_copy(b0 + ROW_AHEAD + i, part).start()
        for i in range(group):
            @pl.when(b0 + i >= ROW_SLOTS)
            def _():
                for part in range(PACK_PARTS):
                    y_copy(b0 + i - ROW_SLOTS, part).wait()

        rows = group * MOE_BLOCK
        live = (lax.broadcasted_iota(I32, (rows, PACK_WIDTH), 0)
                < n_rows - j * MOE_BLOCK)
        parts = [jnp.concatenate([xbuf[part][(b0 + i) % ROW_SLOTS] for i in range(group)], axis=0)
                 for part in range(PACK_PARTS)]
        x = _unpack_rows([jnp.where(live, p, 0) for p in parts])
        hu = _dot(x.astype(BF16), wi_scr[...])
        hg = hu[:, :EXPERT_DIM]
        hv = hu[:, EXPERT_DIM:]
        act = (hg * _sigmoid(hg) * hv).astype(BF16)
        packed = _pack_rows(_dot(act, wo_scr[...]))
        for i in range(group):
            for part in range(PACK_PARTS):
                ybuf[part][(b0 + i) % ROW_SLOTS] = packed[part][i * MOE_BLOCK:(i + 1) * MOE_BLOCK]
                y_copy(b0 + i, part).start()

    @pl.when(n_blk > 0)
    def _():
        for cp in w_copies(e):
            cp.wait()
        slot = e % W_SLOTS
        wi_scr[...] = wibuf[slot].astype(BF16)
        wo_scr[...] = wobuf[slot].astype(BF16)

        def full_group(g, carry):
            step(g * ROW_GROUP, ROW_GROUP)
            return carry

        lax.fori_loop(0, n_blk // ROW_GROUP, full_group, 0)
        done = n_blk // ROW_GROUP * ROW_GROUP
        group = ROW_GROUP // 2
        while group >= 1:
            @pl.when((n_blk & group) != 0)
            def _(group=group, done=done):
                step(done, group)
            done = done + (n_blk & group)
            group //= 2

    @pl.when(e == n_exp - 1)
    def _():
        for back in range(1, ROW_SLOTS + 1):
            @pl.when(total >= back)
            def _():
                for part in range(PACK_PARTS):
                    y_copy(total - back, part).wait()


def _experts(start_blk, n_blk, n_rows, xs, w_in_e, w_out_e):
    p = xs[0].shape[0]
    n_exp = w_in_e.shape[0]
    any_spec = pl.BlockSpec(memory_space=pl.ANY)
    row_buf = pltpu.VMEM((ROW_SLOTS, MOE_BLOCK, PACK_WIDTH), I32)
    grid_spec = pltpu.PrefetchScalarGridSpec(
        num_scalar_prefetch=3,
        grid=(n_exp,),
        in_specs=[any_spec, any_spec, any_spec, any_spec],
        out_specs=(any_spec, any_spec),
        scratch_shapes=[row_buf, row_buf, row_buf, row_buf,
                        pltpu.VMEM((W_SLOTS, D_MODEL, 2 * EXPERT_DIM), F32),
                        pltpu.VMEM((W_SLOTS, EXPERT_DIM, D_MODEL), F32),
                        pltpu.SemaphoreType.DMA((PACK_PARTS, ROW_SLOTS)),
                        pltpu.SemaphoreType.DMA((PACK_PARTS, ROW_SLOTS)),
                        pltpu.SemaphoreType.DMA((2, W_SLOTS)),
                        pltpu.VMEM((D_MODEL, 2 * EXPERT_DIM), BF16),
                        pltpu.VMEM((EXPERT_DIM, D_MODEL), BF16)],
    )
    return pl.pallas_call(
        _expert_kernel,
        out_shape=(jax.ShapeDtypeStruct((p, PACK_WIDTH), I32),) * PACK_PARTS,
        grid_spec=grid_spec,
        compiler_params=_params(("arbitrary",)),
        name="routed_experts",
    )(start_blk, n_blk, n_rows, xs[0], xs[1], w_in_e, w_out_e)


def _combine_kernel(h_ref, yg0_ref, yg1_ref, gate_ref, wsi_ref, wso_ref, g2_ref, b2_ref, o_ref):
    h1 = h_ref[...]
    gate = gate_ref[...]
    routed = None
    for k in range(TOP_K):
        y_k = _unpack_rows([yg0_ref[k], yg1_ref[k]]) * gate[:, k:k + 1]
        routed = y_k if routed is None else routed + y_k
    hs = _dot(h1.astype(BF16), wsi_ref[...])
    sg = hs[:, :SHARED_DIM]
    sv = hs[:, SHARED_DIM:]
    shared = _dot((sg * _sigmoid(sg) * sv).astype(BF16), wso_ref[...])
    o_ref[...] = _layer_norm(DEEPNORM_ALPHA * h1 + routed + shared, g2_ref[...], b2_ref[...])


def _combine(h1, yg, gate_nk, wsi, wso, g2, b2):
    n = h1.shape[0]
    tm = OUT_ROW_TILE
    yg_spec = pl.BlockSpec((TOP_K, tm, PACK_WIDTH), lambda i: (0, i, 0))

    def full(shape):
        return pl.BlockSpec(shape, lambda i: (0,) * len(shape))

    return pl.pallas_call(
        _combine_kernel,
        out_shape=jax.ShapeDtypeStruct((n, D_MODEL), F32),
        grid=(n // tm,),
        in_specs=[
            pl.BlockSpec((tm, D_MODEL), lambda i: (i, 0)),
            yg_spec, yg_spec,
            pl.BlockSpec((tm, TOP_K), lambda i: (i, 0)),
            full((D_MODEL, 2 * SHARED_DIM)), full((SHARED_DIM, D_MODEL)),
            full((1, D_MODEL)), full((1, D_MODEL)),
        ],
        out_specs=pl.BlockSpec((tm, D_MODEL), lambda i: (i, 0)),
        compiler_params=_params(("parallel",)),
        name="shared_combine",
    )(h1, yg[0], yg[1], gate_nk, wsi, wso, g2, b2)


def _plan_kernel(idx_ref, rank_ref, cnt_ref, dest_ref, sblk_ref):
    e = cnt_ref.shape[0]
    tn = idx_ref.shape[1]
    cnt = cnt_ref[...]
    nblk = jnp.floor((cnt + (MOE_BLOCK - 1)) * (1.0 / MOE_BLOCK))
    e_r = lax.broadcasted_iota(I32, (e, e), 0)
    e_c = lax.broadcasted_iota(I32, (e, e), 1)
    lower = jnp.where(e_c < e_r, 1.0, 0.0).astype(BF16)
    start_blk = _dot(lower, nblk.astype(BF16))

    ioe = lax.broadcasted_iota(I32, (e, tn), 0)
    start_row = jnp.concatenate([start_blk * MOE_BLOCK] * (tn // LANES), axis=1)
    idx = idx_ref[...]
    dests = [jnp.sum(jnp.where(ioe == idx[k:k + 1], start_row, 0.0), axis=0, keepdims=True)
             for k in range(TOP_K)]
    dest_ref[...] = jnp.concatenate(dests, axis=0).astype(I32) + rank_ref[...]
    sblk_ref[...] = start_blk.astype(I32)


def _plan(idx_t, rank_t, counts):
    k, n = idx_t.shape
    e = counts.shape[0]
    tn = MIX_ROW_TILE
    tok_spec = pl.BlockSpec((k, tn), lambda i: (0, i))
    exp_spec = pl.BlockSpec((e, LANES), lambda i: (0, 0))
    return pl.pallas_call(
        _plan_kernel,
        out_shape=(jax.ShapeDtypeStruct((k, n), I32), jax.ShapeDtypeStruct((e, LANES), I32)),
        grid=(n // tn,),
        in_specs=[tok_spec, tok_spec, exp_spec],
        out_specs=(tok_spec, exp_spec),
        compiler_params=_params(("arbitrary",)),
        name="dispatch_plan",
    )(idx_t, rank_t, counts)


SC_WINDOW = 128


def _sc_mesh():
    return plsc.VectorSubcoreMesh(core_axis_name="core", subcore_axis_name="subcore")


def _dispatch_rows(parts, dest, p_total):
    n, width = parts[0].shape
    top_k = dest.shape[0]
    out_type = (jax.ShapeDtypeStruct((p_total, width), parts[0].dtype),) * len(parts)

    @functools.partial(pl.kernel, mesh=_sc_mesh(), scratch_types=[], out_type=out_type,
                       name="dispatch_rows")
    def scatter(*refs):
        x_hbms = refs[:len(parts)]
        i_hbm = refs[len(parts)]
        o_hbms = refs[len(parts) + 1:]
        for x_hbm, o_hbm in zip(x_hbms, o_hbms):
            def body(x_vmem, i_vmem, o_hbm=o_hbm):
                for k in range(top_k):
                    pltpu.sync_copy(x_vmem, o_hbm.at[i_vmem.at[k]])

            pltpu.emit_pipeline(
                body,
                grid=(n // SC_WINDOW,),
                in_specs=[pl.BlockSpec((SC_WINDOW, width), lambda i: (i, 0)),
                          pl.BlockSpec((top_k, SC_WINDOW), lambda i: (0, i))],
                out_specs=[],
                core_axis_name=("core", "subcore"),
                dimension_semantics=(pltpu.PARALLEL,),
            )(x_hbm, i_hbm)

    return scatter(*parts, dest)


def _gather_rows(parts, idx_flat):
    count = idx_flat.shape[1]
    width = parts[0].shape[1]
    out_type = (jax.ShapeDtypeStruct((count, width), parts[0].dtype),) * len(parts)

    @functools.partial(pl.kernel, mesh=_sc_mesh(), scratch_types=[], out_type=out_type,
                       name="combine_rows")
    def gather(*refs):
        y_hbms = refs[:len(parts)]
        i_hbm = refs[len(parts)]
        o_hbms = refs[len(parts) + 1:]
        for y_hbm, o_hbm in zip(y_hbms, o_hbms):
            def body(i_vmem, o_vmem, y_hbm=y_hbm):
                pltpu.sync_copy(y_hbm.at[i_vmem.at[0]], o_vmem)

            pltpu.emit_pipeline(
                body,
                grid=(count // SC_WINDOW,),
                in_specs=[pl.BlockSpec((1, SC_WINDOW), lambda i: (0, i))],
                out_specs=[pl.BlockSpec((SC_WINDOW, width), lambda i: (i, 0))],
                core_axis_name=("core", "subcore"),
                dimension_semantics=(pltpu.PARALLEL,),
            )(i_hbm, o_hbm)

    return gather(*parts, idx_flat)


def kernel(x, ln_in_g, ln_in_b, w_in, hgrn_lb_logits, hgrn_norm_w, w_branch_att, w_branch_hgrn,
           w_out, ln1_g, ln1_b, router_w, router_bias, expert_w_in, expert_w_out, shared_w_in,
           shared_w_out, ln2_g, ln2_b):
    batch, seq, d = x.shape
    n = batch * seq
    x2 = x.reshape(n, d)
    row = lambda v: v.reshape(1, -1).astype(F32)

    lower_bounds = jnp.cumsum(jax.nn.softmax(hgrn_lb_logits.astype(F32), axis=0), axis=0)
    l = 0
    w_l = w_in[l].astype(BF16)
    gin, bin_ = row(ln_in_g), row(ln_in_b)

    proj, *acts = _inproj_main(x2, gin, bin_, w_l)
    qkv0, qkv1, qkv2 = _inproj_qkv(acts, w_l, batch, seq)
    qkv0 = qkv0.reshape(3, batch, 1, seq, ATT_GROUP_WIDTH)
    att = []
    for qkv in (qkv0, qkv1, qkv2):
        o_g, lse_g = _attention_group(qkv)
        att.append((o_g, lse_g.transpose(0, 2, 3, 1).reshape(batch, ATT_HEADS, seq)))
    att[0] = (att[0][0].reshape(n, ATT_GROUP_WIDTH), att[0][1])

    ob = _hgrn(proj, row(lower_bounds[l]), row(hgrn_norm_w[l]), batch, seq)

    rw_t = router_w[l].T.astype(F32)
    rwh = rw_t.astype(BF16)
    rwl = (rw_t - rwh.astype(F32)).astype(BF16)
    h1, hp0, hp1, idx_t, gate_t, rank_t, counts = _mix(
        x2, gin, bin_, att, ob, proj,
        w_branch_att[l].astype(BF16), w_branch_hgrn[l].astype(BF16), w_out[l].astype(BF16),
        row(ln1_g[l]), row(ln1_b[l]), rwh, rwl, router_bias[l].reshape(-1, 1).astype(F32),
        batch, seq)
    p_total = n * TOP_K + N_EXPERTS * MOE_BLOCK
    dest, start_blk = _plan(idx_t, rank_t, counts)
    n_rows = counts[:, 0].astype(I32)
    n_blk = (n_rows + (MOE_BLOCK - 1)) // MOE_BLOCK
    xs = _dispatch_rows((hp0, hp1), dest, p_total)
    ys = _experts(start_blk[:, 0], n_blk, n_rows, xs, expert_w_in[l], expert_w_out[l])
    yg = [g.reshape(TOP_K, n, PACK_WIDTH)
          for g in _gather_rows(ys, dest.reshape(1, TOP_K * n))]
    out = _combine(h1, yg, gate_t.T, shared_w_in[l].astype(BF16), shared_w_out[l].astype(BF16),
                   row(ln2_g[l]), row(ln2_b[l]))
    return out.reshape(batch, seq, d)
```

```python
import functools

import jax
import jax.numpy as jnp
from jax import lax
from jax.experimental import pallas as pl
from jax.experimental.pallas import tpu as pltpu
from jax.experimental.pallas import tpu_sc as plsc

F32 = jnp.float32
BF16 = jnp.bfloat16
U32 = jnp.uint32
I32 = jnp.int32

D_MODEL = 1024
ATT_GROUPS = ((128, 1), (512, 4), (2048, 16))
ATT_HEADS = 8
ATT_HEAD_DIM = 64
ATT_GROUP_WIDTH = ATT_HEADS * ATT_HEAD_DIM
ATT_WIDTH = len(ATT_GROUPS) * ATT_GROUP_WIDTH
ATT_BLOCK = 128
HGRN_HEAD_DIM = 128
HGRN_HEADS = D_MODEL // HGRN_HEAD_DIM
HGRN_WIDTH = HGRN_HEADS * HGRN_HEAD_DIM
HGRN_CHUNK = 32
N_EXPERTS = 256
TOP_K = 8
N_EXPERT_GROUPS = 8
TOPK_GROUPS = 4
EXPERT_DIM = 256
SHARED_DIM = 256
ROUTED_SCALE = 2.5
MOE_BLOCK = 128
LN_EPS = 1e-5
RMS_EPS = 1e-6
DEPTH = 1
DEEPNORM_ALPHA = (2 * DEPTH) ** 0.25

VMEM_LIMIT = 56 * 1024 * 1024
LANES = 128

ROW_TILE = 1024
MAIN_COL_TILE = 1536
MAIN_COL_STEPS = 4
MIX_ROW_TILE = 512
OUT_ROW_TILE = 512


def _params(sem, vmem=VMEM_LIMIT):
    return pltpu.CompilerParams(dimension_semantics=sem, vmem_limit_bytes=vmem)


def _layer_norm(x, g, b):
    mu = jnp.mean(x, -1, keepdims=True)
    xc = x - mu
    var = jnp.mean(xc * xc, -1, keepdims=True)
    return xc * lax.rsqrt(var + LN_EPS) * g + b


def _sigmoid(x):
    return 1.0 / (1.0 + jnp.exp(-x))


def _dot(a, b):
    return jnp.dot(a, b, preferred_element_type=F32)


def _dot_nt(a, b):
    return lax.dot_general(a, b, (((1,), (1,)), ((), ())), preferred_element_type=F32)


def _dot_tn(a, b):
    return lax.dot_general(a, b, (((0,), (0,)), ((), ())), preferred_element_type=F32)


def _pack_bf16_pair(lo, hi):
    lo_bits = pltpu.bitcast(lo.astype(BF16).astype(F32), U32) >> 16
    hi_bits = pltpu.bitcast(hi.astype(BF16).astype(F32), U32) & jnp.uint32(0xFFFF0000)
    return hi_bits | lo_bits


def _unpack_bf16_pair(w):
    lo = pltpu.bitcast(w << 16, F32)
    hi = pltpu.bitcast(w & jnp.uint32(0xFFFF0000), F32)
    return lo, hi


PACK_PARTS = 2
PACK_WIDTH = D_MODEL // 2 // PACK_PARTS


def _pack_rows(v):
    half = D_MODEL // 2
    parts = []
    for j in range(PACK_PARTS):
        lo = v[:, j * PACK_WIDTH:(j + 1) * PACK_WIDTH]
        hi = v[:, half + j * PACK_WIDTH:half + (j + 1) * PACK_WIDTH]
        parts.append(pltpu.bitcast(_pack_bf16_pair(lo, hi), I32))
    return parts


def _unpack_rows(parts):
    pairs = [_unpack_bf16_pair(pltpu.bitcast(p, U32)) for p in parts]
    return jnp.concatenate([lo for lo, _ in pairs] + [hi for _, hi in pairs], axis=-1)


def _inproj_main_kernel(x_ref, g_ref, b_ref, w_ref, o_ref, h0_ref, h1_ref, h2_ref, hf_scr):
    j = pl.program_id(1)

    @pl.when(j == 0)
    def _():
        hf = _layer_norm(x_ref[...], g_ref[...], b_ref[...])
        h0_ref[...] = hf.astype(BF16)
        for c in range(D_MODEL // LANES):
            hf_scr[c] = hf[:, c * LANES:(c + 1) * LANES]

    for h_ref, (_, dil) in ((h1_ref, ATT_GROUPS[1]), (h2_ref, ATT_GROUPS[2])):
        rows = ROW_TILE // dil
        share = dil // MAIN_COL_STEPS
        for q in range(share):
            r = j * share + q
            dst = pl.ds(pl.multiple_of(r * rows, rows), rows)
            for c in range(D_MODEL // LANES):
                h_ref[dst, c * LANES:(c + 1) * LANES] = (
                    hf_scr[c, pl.ds(r, rows, stride=dil), :].astype(BF16))

    o_ref[...] = _dot(h0_ref[...], w_ref[...]).astype(BF16)


def _inproj_main(x2, g, b, w_all):
    n = x2.shape[0]
    first = 3 * ATT_WIDTH // MAIN_COL_TILE
    width = w_all.shape[1] - 3 * ATT_WIDTH
    assert width == MAIN_COL_STEPS * MAIN_COL_TILE
    assert all(dil % MAIN_COL_STEPS == 0 for _, dil in ATT_GROUPS[1:])
    act = jax.ShapeDtypeStruct((n, D_MODEL), BF16)
    act_spec = pl.BlockSpec((ROW_TILE, D_MODEL), lambda i, j: (i, 0))
    return pl.pallas_call(
        _inproj_main_kernel,
        out_shape=(jax.ShapeDtypeStruct((n, width), BF16), act, act, act),
        grid=(n // ROW_TILE, MAIN_COL_STEPS),
        in_specs=[
            pl.BlockSpec((ROW_TILE, D_MODEL), lambda i, j: (i, 0)),
            pl.BlockSpec((1, D_MODEL), lambda i, j: (0, 0)),
            pl.BlockSpec((1, D_MODEL), lambda i, j: (0, 0)),
            pl.BlockSpec((D_MODEL, MAIN_COL_TILE), lambda i, j: (0, first + j)),
        ],
        out_specs=(pl.BlockSpec((ROW_TILE, MAIN_COL_TILE), lambda i, j: (i, j)),
                   act_spec, act_spec, act_spec),
        scratch_shapes=[pltpu.VMEM((D_MODEL // LANES, ROW_TILE, LANES), F32)],
        compiler_params=_params(("parallel", "arbitrary")),
        name="inproj_main",
    )(x2, g, b, w_all)


def _inproj_qkv_kernel(h0_ref, h1_ref, h2_ref, w_ref, o0_ref, o1_ref, o2_ref):
    gw = ATT_GROUP_WIDTH
    o0_ref[0] = _dot(h0_ref[...], w_ref[:, 0:gw]).astype(BF16)
    d1 = ATT_GROUPS[1][1]
    o1_ref[0, 0] = _dot(h1_ref[...], w_ref[:, gw:2 * gw]).astype(BF16).reshape(d1, ROW_TILE // d1, gw)
    d2 = ATT_GROUPS[2][1]
    o2_ref[0, 0] = _dot(h2_ref[...], w_ref[:, 2 * gw:3 * gw]).astype(BF16).reshape(d2, ROW_TILE // d2, gw)


def _inproj_qkv(acts, w_all, batch, seq):
    n = acts[0].shape[0]
    gw = ATT_GROUP_WIDTH
    tiles_per_seq = seq // ROW_TILE
    d1, d2 = ATT_GROUPS[1][1], ATT_GROUPS[2][1]
    out_shape = (
        jax.ShapeDtypeStruct((3, n, gw), BF16),
        jax.ShapeDtypeStruct((3, batch, d1, seq // d1, gw), BF16),
        jax.ShapeDtypeStruct((3, batch, d2, seq // d2, gw), BF16),
    )
    act_spec = pl.BlockSpec((ROW_TILE, D_MODEL), lambda i, t: (i, 0))
    return pl.pallas_call(
        _inproj_qkv_kernel,
        out_shape=out_shape,
        grid=(n // ROW_TILE, 3),
        in_specs=[act_spec, act_spec, act_spec,
                  pl.BlockSpec((D_MODEL, ATT_WIDTH), lambda i, t: (0, t))],
        out_specs=(
            pl.BlockSpec((1, ROW_TILE, gw), lambda i, t: (t, i, 0)),
            pl.BlockSpec((1, 1, d1, ROW_TILE // d1, gw),
                         lambda i, t: (t, i // tiles_per_seq, 0, i % tiles_per_seq, 0)),
            pl.BlockSpec((1, 1, d2, ROW_TILE // d2, gw),
                         lambda i, t: (t, i // tiles_per_seq, 0, i % tiles_per_seq, 0)),
        ),
        compiler_params=_params(("parallel", "arbitrary")),
        name="inproj_qkv",
    )(*acts, w_all)


ATT_HEADS_PER_MATMUL = 4
ATT_BATCH_PER_STEP = 8


def _attn_block(q_ref, kc_ref, kp_ref, v_ref, o_ref, lse_ref, vt_ref, bias, has_prev):
    blk = ATT_BLOCK
    hd = ATT_HEAD_DIM
    gw = ATT_GROUP_WIDTH
    hpm = ATT_HEADS_PER_MATMUL
    width = hpm * hd

    vt_cur = jnp.concatenate(
        [v_ref[:, c * LANES:(c + 1) * LANES].astype(F32).T.astype(BF16)
         for c in range(gw // LANES)], axis=0)
    vt_prev = jnp.where(has_prev, vt_ref[...], jnp.zeros_like(vt_cur))
    vt_ref[...] = vt_cur

    lane_head = lax.broadcasted_iota(I32, (blk, width), 1) // hd
    per_tile = LANES // hd
    for g in range(ATT_HEADS // hpm):
        feat = slice(g * width, (g + 1) * width)
        q_g = q_ref[:, feat] * (hd ** -0.5)
        q_bd = jnp.concatenate([jnp.where(lane_head == i, q_g, jnp.zeros_like(q_g))
                                for i in range(hpm)], axis=0)
        k_g = jnp.concatenate([kp_ref[:, feat], kc_ref[:, feat]], axis=0)
        s_t = _dot_nt(k_g, q_bd) + bias
        m = jnp.max(s_t, axis=0, keepdims=True)
        p = jnp.exp(s_t - m)
        l = jnp.sum(p, axis=0, keepdims=True)
        v_t = jnp.concatenate([vt_prev[feat, :], vt_cur[feat, :]], axis=1)
        o_t = _dot(v_t, p.astype(BF16))
        lse = m + jnp.log(l)
        inv_l = 1.0 / l
        for c in range(width // LANES):
            tile = []
            for i in range(c * per_tile, (c + 1) * per_tile):
                cols = slice(i * blk, (i + 1) * blk)
                tile.append(o_t[i * hd:(i + 1) * hd, cols] * inv_l[:, cols])
                lse_ref[g * hpm + i:g * hpm + i + 1, :] = lse[:, cols]
            lanes = slice(g * width + c * LANES, g * width + (c + 1) * LANES)
            o_ref[:, lanes] = jnp.concatenate(tile, axis=0).T.astype(BF16)


def _attn_kernel(q_ref, kc_ref, kp_ref, v_ref, o_ref, lse_ref, vt_scr):
    blk = ATT_BLOCK
    has_prev = pl.program_id(2) > 0
    key_i = lax.broadcasted_iota(I32, (2 * blk, blk), 0)
    qry_i = lax.broadcasted_iota(I32, (2 * blk, blk), 1)
    live = (((key_i < blk) & (key_i >= qry_i) & has_prev)
            | ((key_i >= blk) & (key_i - blk <= qry_i)))
    bias = jnp.concatenate([jnp.where(live, 0.0, -jnp.inf)] * ATT_HEADS_PER_MATMUL, axis=1)
    for j in range(ATT_BATCH_PER_STEP):
        _attn_block(q_ref.at[j], kc_ref.at[j], kp_ref.at[j], v_ref.at[j], o_ref.at[j],
                    lse_ref.at[j], vt_scr.at[j], bias, has_prev)


def _attention_group(qkv):
    _, batch, dil, sub_len, gw = qkv.shape
    blk = ATT_BLOCK
    bps = ATT_BATCH_PER_STEP

    def spec(t, prev):
        def index(b, r, i):
            return (t, b, r, jnp.maximum(i - 1, 0) if prev else i, 0)
        return pl.BlockSpec((None, bps, None, blk, gw), index)

    return pl.pallas_call(
        _attn_kernel,
        out_shape=(jax.ShapeDtypeStruct((batch, dil, sub_len, gw), BF16),
                   jax.ShapeDtypeStruct((batch, dil, ATT_HEADS, sub_len), F32)),
        grid=(batch // bps, dil, sub_len // blk),
        in_specs=[spec(0, False), spec(1, False), spec(1, True), spec(2, False)],
        out_specs=(pl.BlockSpec((bps, None, blk, gw), lambda b, r, i: (b, r, i, 0)),
                   pl.BlockSpec((bps, None, ATT_HEADS, blk), lambda b, r, i: (b, r, 0, i))),
        scratch_shapes=[pltpu.VMEM((bps, gw, blk), BF16)],
        compiler_params=_params(("parallel", "parallel", "arbitrary")),
        name=f"dilated_attention_d{dil}",
    )(qkv, qkv, qkv, qkv)


HGRN_ROWS = 256
HGRN_HEADS_PER_STEP = 8
HGRN_MAX_FACTOR_EXPONENT = 60.0


def _hgrn_pair(q, key, bcum, v, gate, states, nw, same_chunk_causal, chunk_mask):
    c = HGRN_CHUNK
    dk = HGRN_HEAD_DIM
    rows = HGRN_ROWS
    nchunk = rows // c
    pair = 2 * dk

    b3 = bcum.reshape(nchunk, c, pair)
    b_mid = b3[:, c // 2:c // 2 + 1, :]
    b_last = b3[:, c - 1:c, :]
    q3 = q.astype(F32).reshape(nchunk, c, pair)
    q_t3 = q3 * jnp.exp(b3 - b_mid)
    k_t3 = key.reshape(nchunk, c, pair) * jnp.exp(b_mid - b3)
    q_t = q_t3.reshape(rows, pair).astype(BF16)
    k_t = k_t3.reshape(rows, pair).astype(BF16)
    q_in = (q_t3 * jnp.exp(b_mid)).reshape(rows, pair).astype(BF16)
    k_st = (k_t3 * jnp.exp(b_last - b_mid)).reshape(rows, pair).astype(BF16)
    decay = jnp.exp(b_last.reshape(nchunk, pair))

    def block_diagonal(x):
        zero = jnp.zeros((rows, dk), x.dtype)
        return jnp.concatenate([jnp.concatenate([x[:, :dk], zero], axis=1),
                                jnp.concatenate([zero, x[:, dk:]], axis=1)], axis=0)

    scores = _dot_nt(block_diagonal(q_t), k_t)
    att = jnp.concatenate([jnp.where(same_chunk_causal, scores[:rows], 0.0),
                           jnp.where(same_chunk_causal, scores[rows:], 0.0)], axis=1)
    o = _dot(att.astype(BF16), block_diagonal(v))

    new_states, outs = [], []
    for i, st in enumerate(states):
        hs = slice(i * dk, (i + 1) * dk)
        k_spread = jnp.concatenate([k_st[:, hs] * chunk_mask[j] for j in range(nchunk)], axis=1)
        incr = _dot_tn(v[:, hs], k_spread)
        before = []
        for j in range(nchunk):
            before.append(st.astype(BF16))
            st = st * decay[j:j + 1, hs] + incr[:, j * dk:(j + 1) * dk]
        q_spread = jnp.concatenate([q_in[:, hs] * chunk_mask[j] for j in range(nchunk)], axis=1)
        o_h = o[:, hs] + _dot_nt(q_spread, jnp.concatenate(before, axis=1))
        outs.append(o_h * lax.rsqrt(jnp.mean(o_h * o_h, -1, keepdims=True) + RMS_EPS * dk) * nw)
        new_states.append(st)
    gate = gate.astype(F32)
    return jnp.concatenate(outs, axis=1) * gate * _sigmoid(gate), new_states


def _hgrn_chunk_direct(q, key, b, v, gate, st, nw):
    c = HGRN_CHUNK
    dk = HGRN_HEAD_DIM
    v32 = v.astype(F32)
    row = lax.broadcasted_iota(I32, (c, dk), 0)
    o = _dot_nt((q * jnp.exp(b)).astype(BF16), st.astype(BF16))
    for s_ in range(c):
        dec = jnp.exp(jnp.where(row >= s_, b - b[s_:s_ + 1], -jnp.inf))
        a = jnp.sum(q * key[s_:s_ + 1] * dec, axis=-1, keepdims=True)
        o = o + a * v32[s_:s_ + 1]
    b_last = b[c - 1:c]
    st = st * jnp.exp(b_last) + _dot_tn(v, (key * jnp.exp(b_last - b)).astype(BF16))
    o = o * lax.rsqrt(jnp.mean(o * o, -1, keepdims=True) + RMS_EPS * dk) * nw
    gate = gate.astype(F32)
    return o * gate * _sigmoid(gate), st


def _hgrn_kernel(q_ref, f_ref, i_ref, g_ref, lb_ref, nw_ref, o_ref, state_scr, mask_scr,
                 key_scr, b_scr):
    seq = q_ref.shape[0]
    c = HGRN_CHUNK
    rows = HGRN_ROWS
    dk = HGRN_HEAD_DIM
    width = HGRN_HEADS_PER_STEP * dk
    nw = nw_ref[...]

    r_i = lax.broadcasted_iota(I32, (rows, rows), 0)
    c_i = lax.broadcasted_iota(I32, (rows, rows), 1)
    same_chunk_causal = (r_i // c == c_i // c) & (c_i <= r_i)
    tri = jnp.where(same_chunk_causal, 1.0, 0.0).astype(BF16)
    row_chunk = lax.broadcasted_iota(I32, (rows, dk), 0) // c
    for j in range(rows // c):
        mask_scr[j] = jnp.where(row_chunk == j, 1.0, 0.0).astype(BF16)

    state_scr[...] = jnp.zeros_like(state_scr)

    def body(gi, carry):
        rs = pl.ds(pl.multiple_of(gi * rows, rows), rows)
        lb = lb_ref[...]
        one_m_lb = 1.0 - lb
        z = f_ref[rs, :].astype(F32)
        log_f = jnp.log(lb + one_m_lb * _sigmoid(z))
        p0 = log_f.astype(BF16)
        p1 = (log_f - p0.astype(F32)).astype(BF16)
        both = _dot(tri, jnp.concatenate([p0, p1], axis=1))
        bcum = both[:, :width] + both[:, width:]

        b3 = bcum.reshape(rows // c, c, width)
        span = jnp.maximum(b3[:, 0, :] - b3[:, c // 2, :], b3[:, c // 2, :] - b3[:, c - 1, :])
        factorable = jnp.max(span) < HGRN_MAX_FACTOR_EXPONENT

        @pl.when(factorable)
        def _():
            key = one_m_lb * _sigmoid(-z)
            for h in range(0, HGRN_HEADS_PER_STEP, 2):
                cs = slice(h * dk, (h + 2) * dk)
                o, sts = _hgrn_pair(q_ref[rs, cs], key[:, cs], bcum[:, cs], i_ref[rs, cs],
                                    g_ref[rs, cs], [state_scr[h], state_scr[h + 1]], nw,
                                    same_chunk_causal, mask_scr)
                state_scr[h], state_scr[h + 1] = sts
                o_ref[rs, cs] = o.astype(BF16)

        @pl.when(jnp.logical_not(factorable))
        def _():
            key_scr[...] = one_m_lb * _sigmoid(-z)
            b_scr[...] = bcum
            for h in range(HGRN_HEADS_PER_STEP):
                cs = slice(h * dk, (h + 1) * dk)

                def chunk(j, st, cs=cs):
                    local = pl.ds(pl.multiple_of(j * c, c), c)
                    r = pl.ds(pl.multiple_of(gi * rows + j * c, c), c)
                    o, st = _hgrn_chunk_direct(q_ref[r, cs].astype(F32), key_scr[local, cs],
                                               b_scr[local, cs], i_ref[r, cs], g_ref[r, cs], st, nw)
                    o_ref[r, cs] = o.astype(BF16)
                    return st

                state_scr[h] = lax.fori_loop(0, rows // c, chunk, state_scr[h])

        return carry

    lax.fori_loop(0, seq // rows, body, 0)


def _hgrn(proj, lower_bound, norm_w, batch, seq):
    n = proj.shape[0]
    dk = HGRN_HEAD_DIM
    width = HGRN_HEADS_PER_STEP * dk
    steps = HGRN_HEADS // HGRN_HEADS_PER_STEP

    def seg(k):
        return pl.BlockSpec((seq, width), lambda b, h: (b, k * steps + h))

    return pl.pallas_call(
        _hgrn_kernel,
        out_shape=jax.ShapeDtypeStruct((n, HGRN_WIDTH), BF16),
        grid=(batch, steps),
        in_specs=[seg(0), seg(1), seg(2), seg(3),
                  pl.BlockSpec((1, width), lambda b, h: (0, h)),
                  pl.BlockSpec((1, dk), lambda b, h: (0, 0))],
        out_specs=pl.BlockSpec((seq, width), lambda b, h: (b, h)),
        scratch_shapes=[pltpu.VMEM((HGRN_HEADS_PER_STEP, dk, dk), F32),
                        pltpu.VMEM((HGRN_ROWS // HGRN_CHUNK, HGRN_ROWS, dk), BF16),
                        pltpu.VMEM((HGRN_ROWS, width), F32),
                        pltpu.VMEM((HGRN_ROWS, width), F32)],
        compiler_params=_params(("parallel", "parallel")),
        name="hgrn2",
    )(proj, proj, proj, proj, lower_bound, norm_w)


def _pick_first_max(vals, iota, axis, size):
    m = jnp.max(vals, axis=axis, keepdims=True)
    idx = jnp.min(jnp.where(vals == m, iota, size), axis=axis, keepdims=True)
    return m, idx


def _route_tile(logits, bias, base):
    e, tn = logits.shape
    groups = N_EXPERT_GROUPS
    gsz = e // groups
    neg = -jnp.inf

    scores = _sigmoid(logits)
    biased = scores + bias

    b3 = biased.reshape(groups, gsz, tn)
    io3 = lax.broadcasted_iota(I32, b3.shape, 1)
    m1, i1 = _pick_first_max(b3, io3, 1, gsz)
    m2 = jnp.max(jnp.where(io3 == i1, neg, b3), axis=1, keepdims=True)
    grp = m1 + m2
    iog = lax.broadcasted_iota(I32, grp.shape, 0)
    keep = jnp.zeros(grp.shape, F32)
    for _ in range(TOPK_GROUPS):
        _, gi = _pick_first_max(grp, iog, 0, groups)
        hit = iog == gi
        keep = jnp.where(hit, 1.0, keep)
        grp = jnp.where(hit, neg, grp)
    masked = jnp.where(keep > 0.0, b3, neg).reshape(e, tn)

    ioe = lax.broadcasted_iota(I32, (e, tn), 0)
    onehot = jnp.zeros((e, tn), F32)
    idxs, gates = [], []
    for _ in range(TOP_K):
        _, ei = _pick_first_max(masked, ioe, 0, e)
        hit = ioe == ei
        gates.append(jnp.sum(jnp.where(hit, scores, 0.0), axis=0, keepdims=True))
        onehot = jnp.where(hit, 1.0, onehot)
        masked = jnp.where(hit, neg, masked)
        idxs.append(ei)
    gsum = gates[0]
    for g in gates[1:]:
        gsum = gsum + g
    gate = jnp.concatenate(gates, axis=0) / gsum * ROUTED_SCALE

    t_r = lax.broadcasted_iota(I32, (tn, tn), 0)
    t_c = lax.broadcasted_iota(I32, (tn, tn), 1)
    earlier = jnp.where(t_r < t_c, 1.0, 0.0).astype(BF16)
    oh = onehot.astype(BF16)
    before = _dot(oh, earlier) + jnp.concatenate([base] * (tn // LANES), axis=1)
    ranks = [jnp.sum(jnp.where(ioe == ei, before, 0.0), axis=0, keepdims=True) for ei in idxs]
    total = base + _dot(oh, jnp.ones((tn, LANES), BF16))
    return (jnp.concatenate(idxs, axis=0), gate,
            jnp.concatenate(ranks, axis=0).astype(I32), total)


def _mix_kernel(x_ref, gin_ref, bin_ref, o0_ref, o1_ref, o2_ref, l0_ref, l1_ref, l2_ref,
                ob_ref, ga_ref, gb_ref, wa_ref, wb_ref, wo_ref, g1_ref, b1_ref,
                rwh_ref, rwl_ref, rbias_ref,
                h_ref, hp0_ref, hp1_ref, idx_ref, gate_ref, rank_ref, cnt_ref,
                so1_scr, so2_scr, lg_scr, carry_scr):
    tm = x_ref.shape[0]
    gw = ATT_GROUP_WIDTH
    step = pl.program_id(0)

    @pl.when(step == 0)
    def _():
        lg_scr[...] = jnp.zeros_like(lg_scr)
        carry_scr[...] = jnp.zeros_like(carry_scr)

    counts = carry_scr[...]
    idx, gate, rank, total = _route_tile(lg_scr[...], rbias_ref[...], counts)
    idx_ref[...] = idx
    gate_ref[...] = gate
    rank_ref[...] = rank
    counts = jnp.where(step > 0, total, counts)
    carry_scr[...] = counts
    cnt_ref[...] = counts

    for o_ref, so_scr, (_, dil) in ((o1_ref, so1_scr, ATT_GROUPS[1]),
                                    (o2_ref, so2_scr, ATT_GROUPS[2])):
        for r in range(dil):
            o_r = o_ref[r].astype(F32)
            for c in range(gw // LANES):
                so_scr[c, pl.ds(r, tm // dil, stride=dil), :] = o_r[:, c * LANES:(c + 1) * LANES]

    def natural(scr):
        return jnp.concatenate([scr[c] for c in range(gw // LANES)], axis=-1)

    l0, l1, l2 = l0_ref[...], l1_ref[...], l2_ref[...]
    m = jnp.maximum(jnp.maximum(l0, l1), l2)
    e0 = jnp.exp(l0 - m)
    e1 = jnp.exp(l1 - m)
    e2 = jnp.exp(l2 - m)
    inv = 1.0 / (e0 + e1 + e2)
    terms = []
    for w in (e0 * inv, e1 * inv, e2 * inv):
        hi = w.astype(BF16).astype(F32)
        terms += [hi, w - hi]
    terms = jnp.concatenate(terms, axis=0).astype(BF16)
    n_groups = len(ATT_GROUPS)
    t_i = lax.broadcasted_iota(I32, (n_groups * 2 * ATT_HEADS, n_groups * gw), 0)
    c_i = lax.broadcasted_iota(I32, (n_groups * 2 * ATT_HEADS, n_groups * gw), 1)
    spread = jnp.where((t_i // (2 * ATT_HEADS) == c_i // gw)
                       & (t_i % ATT_HEADS == c_i % gw // ATT_HEAD_DIM), 1.0, 0.0).astype(BF16)
    wide = _dot_tn(terms, spread)
    o_att = (wide[:, :gw] * o0_ref[...].astype(F32) + wide[:, gw:2 * gw] * natural(so1_scr)
             + wide[:, 2 * gw:] * natural(so2_scr))

    y_a = _dot(o_att.astype(BF16), wa_ref[...])
    y_b = _dot(ob_ref[...], wb_ref[...])
    merged = _sigmoid(ga_ref[...].astype(F32)) * y_a + _sigmoid(gb_ref[...].astype(F32)) * y_b
    mix = _dot(merged.astype(BF16), wo_ref[...])
    h_in = _layer_norm(x_ref[...], gin_ref[...], bin_ref[...])
    h1 = _layer_norm(DEEPNORM_ALPHA * h_in + mix, g1_ref[...], b1_ref[...])
    h_ref[...] = h1

    hp0_ref[...], hp1_ref[...] = _pack_rows(h1)

    h_hi = h1.astype(BF16)
    h_lo = (h1 - h_hi.astype(F32)).astype(BF16)
    rwh = rwh_ref[...]
    lg_scr[...] = _dot_nt(rwh, h_hi) + _dot_nt(rwh, h_lo) + _dot_nt(rwl_ref[...], h_hi)


def _mix(x2, gin, bin_, att, ob, proj, wa, wb, wo, g1, b1, rwh, rwl, rbias, batch, seq):
    n = x2.shape[0]
    tm = MIX_ROW_TILE
    gw = ATT_GROUP_WIDTH
    tiles_per_seq = seq // tm
    last = n // tm - 1
    (o0, l0), (o1, l1), (o2, l2) = att
    d1, d2 = ATT_GROUPS[1][1], ATT_GROUPS[2][1]

    def tile(i):
        return jnp.minimum(i, last)

    def routed(i):
        return jnp.maximum(i - 1, 0)

    lse_spec = pl.BlockSpec((None, ATT_HEADS, tm),
                            lambda i: (tile(i) // tiles_per_seq, 0, tile(i) % tiles_per_seq))

    def full(shape):
        return pl.BlockSpec(shape, lambda i: (0,) * len(shape))

    def dil_spec(dil):
        return pl.BlockSpec((None, dil, tm // dil, gw),
                            lambda i: (tile(i) // tiles_per_seq, 0, tile(i) % tiles_per_seq, 0))

    def rows(width, col=0):
        return pl.BlockSpec((tm, width), lambda i: (tile(i), col))

    tok_spec = pl.BlockSpec((TOP_K, tm), lambda i: (0, routed(i)))
    return pl.pallas_call(
        _mix_kernel,
        out_shape=(jax.ShapeDtypeStruct((n, D_MODEL), F32),
                   jax.ShapeDtypeStruct((n, PACK_WIDTH), I32),
                   jax.ShapeDtypeStruct((n, PACK_WIDTH), I32),
                   jax.ShapeDtypeStruct((TOP_K, n), I32),
                   jax.ShapeDtypeStruct((TOP_K, n), F32),
                   jax.ShapeDtypeStruct((TOP_K, n), I32),
                   jax.ShapeDtypeStruct((N_EXPERTS, LANES), F32)),
        grid=(n // tm + 1,),
        in_specs=[
            rows(D_MODEL),
            full((1, D_MODEL)), full((1, D_MODEL)),
            rows(gw), dil_spec(d1), dil_spec(d2),
            lse_spec, lse_spec, lse_spec,
            rows(HGRN_WIDTH), rows(D_MODEL, 4), rows(D_MODEL, 5),
            full((gw, D_MODEL)), full((HGRN_WIDTH, D_MODEL)), full((D_MODEL, D_MODEL)),
            full((1, D_MODEL)), full((1, D_MODEL)),
            full((N_EXPERTS, D_MODEL)), full((N_EXPERTS, D_MODEL)), full((N_EXPERTS, 1)),
        ],
        out_specs=(rows(D_MODEL), rows(PACK_WIDTH), rows(PACK_WIDTH),
                   tok_spec, tok_spec, tok_spec, full((N_EXPERTS, LANES))),
        scratch_shapes=[pltpu.VMEM((gw // LANES, tm, LANES), F32),
                        pltpu.VMEM((gw // LANES, tm, LANES), F32),
                        pltpu.VMEM((N_EXPERTS, tm), F32),
                        pltpu.VMEM((N_EXPERTS, LANES), F32)],
        compiler_params=_params(("arbitrary",)),
        name="branch_mix_route",
    )(x2, gin, bin_, o0, o1, o2, l0, l1, l2, ob, proj, proj, wa, wb, wo, g1, b1, rwh, rwl, rbias)


ROW_SLOTS = 16
ROW_GROUP = 4
ROW_AHEAD = ROW_SLOTS - ROW_GROUP
W_SLOTS = 3


def _expert_kernel(sblk_ref, nblk_ref, cnt_ref, xs0_hbm, xs1_hbm, wi_hbm, wo_hbm,
                   ys0_hbm, ys1_hbm, xbuf0, xbuf1, ybuf0, ybuf1, wibuf, wobuf,
                   in_sem, out_sem, w_sem, wi_scr, wo_scr):
    e = pl.program_id(0)
    n_exp = pl.num_programs(0)
    first_blk = sblk_ref[e]
    n_blk = nblk_ref[e]
    n_rows = cnt_ref[e]
    total = sblk_ref[n_exp - 1] + nblk_ref[n_exp - 1]
    xs_hbm, ys_hbm = (xs0_hbm, xs1_hbm), (ys0_hbm, ys1_hbm)
    xbuf, ybuf = (xbuf0, xbuf1), (ybuf0, ybuf1)

    def rows_of(b):
        return pl.ds(pl.multiple_of(b * MOE_BLOCK, MOE_BLOCK), MOE_BLOCK)

    def x_copy(b, part):
        slot = b % ROW_SLOTS
        return pltpu.make_async_copy(xs_hbm[part].at[rows_of(b)], xbuf[part].at[slot],
                                     in_sem.at[part, slot])

    def y_copy(b, part):
        slot = b % ROW_SLOTS
        return pltpu.make_async_copy(ybuf[part].at[slot], ys_hbm[part].at[rows_of(b)],
                                     out_sem.at[part, slot])

    def w_copies(ex):
        slot = ex % W_SLOTS
        return (pltpu.make_async_copy(wi_hbm.at[ex], wibuf.at[slot], w_sem.at[0, slot]),
                pltpu.make_async_copy(wo_hbm.at[ex], wobuf.at[slot], w_sem.at[1, slot]))

    def start_weights(ex):
        @pl.when((ex < n_exp) & (nblk_ref[jnp.minimum(ex, n_exp - 1)] > 0))
        def _():
            for cp in w_copies(ex):
                cp.start()

    @pl.when(e == 0)
    def _():
        for b in range(ROW_AHEAD):
            @pl.when(b < total)
            def _():
                for part in range(PACK_PARTS):
                    x_copy(b, part).start()
        for ex in range(W_SLOTS - 1):
            start_weights(ex)

    start_weights(e + (W_SLOTS - 1))

    def step(j, group):
        b0 = first_blk + j
        for i in range(group):
            for part in range(PACK_PARTS):
                x_copy(b0 + i, part).wait()
        for i in range(group):
            @pl.when(b0 + ROW_AHEAD + i < total)
            def _():
                for part in range(PACK_PARTS):
                    x_copy(b0 + ROW_AHEAD + i, part).start()
        for i in range(group):
            @pl.when(b0 + i >= ROW_SLOTS)
            def _():
                for part in range(PACK_PARTS):
                    y_copy(b0 + i - ROW_SLOTS, part).wait()

        rows = group * MOE_BLOCK
        live = (lax.broadcasted_iota(I32, (rows, PACK_WIDTH), 0)
                < n_rows - j * MOE_BLOCK)
        parts = [jnp.concatenate([xbuf[part][(b0 + i) % ROW_SLOTS] for i in range(group)], axis=0)
                 for part in range(PACK_PARTS)]
        x = _unpack_rows([jnp.where(live, p, 0) for p in parts])
        hu = _dot(x.astype(BF16), wi_scr[...])
        hg = hu[:, :EXPERT_DIM]
        hv = hu[:, EXPERT_DIM:]
        act = (hg * _sigmoid(hg) * hv).astype(BF16)
        packed = _pack_rows(_dot(act, wo_scr[...]))
        for i in range(group):
            for part in range(PACK_PARTS):
                ybuf[part][(b0 + i) % ROW_SLOTS] = packed[part][i * MOE_BLOCK:(i + 1) * MOE_BLOCK]
                y_copy(b0 + i, part).start()

    @pl.when(n_blk > 0)
    def _():
        for cp in w_copies(e):
            cp.wait()
        slot = e % W_SLOTS
        wi_scr[...] = wibuf[slot].astype(BF16)
        wo_scr[...] = wobuf[slot].astype(BF16)

        def full_group(g, carry):
            step(g * ROW_GROUP, ROW_GROUP)
            return carry

        lax.fori_loop(0, n_blk // ROW_GROUP, full_group, 0)
        done = n_blk // ROW_GROUP * ROW_GROUP
        group = ROW_GROUP // 2
        while group >= 1:
            @pl.when((n_blk & group) != 0)
            def _(group=group, done=done):
                step(done, group)
            done = done + (n_blk & group)
            group //= 2

    @pl.when(e == n_exp - 1)
    def _():
        for back in range(1, ROW_SLOTS + 1):
            @pl.when(total >= back)
            def _():
                for part in range(PACK_PARTS):
                    y_copy(total - back, part).wait()


def _experts(start_blk, n_blk, n_rows, xs, w_in_e, w_out_e):
    p = xs[0].shape[0]
    n_exp = w_in_e.shape[0]
    any_spec = pl.BlockSpec(memory_space=pl.ANY)
    row_buf = pltpu.VMEM((ROW_SLOTS, MOE_BLOCK, PACK_WIDTH), I32)
    grid_spec = pltpu.PrefetchScalarGridSpec(
        num_scalar_prefetch=3,
        grid=(n_exp,),
        in_specs=[any_spec, any_spec, any_spec, any_spec],
        out_specs=(any_spec, any_spec),
        scratch_shapes=[row_buf, row_buf, row_buf, row_buf,
                        pltpu.VMEM((W_SLOTS, D_MODEL, 2 * EXPERT_DIM), F32),
                        pltpu.VMEM((W_SLOTS, EXPERT_DIM, D_MODEL), F32),
                        pltpu.SemaphoreType.DMA((PACK_PARTS, ROW_SLOTS)),
                        pltpu.SemaphoreType.DMA((PACK_PARTS, ROW_SLOTS)),
                        pltpu.SemaphoreType.DMA((2, W_SLOTS)),
                        pltpu.VMEM((D_MODEL, 2 * EXPERT_DIM), BF16),
                        pltpu.VMEM((EXPERT_DIM, D_MODEL), BF16)],
    )
    return pl.pallas_call(
        _expert_kernel,
        out_shape=(jax.ShapeDtypeStruct((p, PACK_WIDTH), I32),) * PACK_PARTS,
        grid_spec=grid_spec,
        compiler_params=_params(("arbitrary",)),
        name="routed_experts",
    )(start_blk, n_blk, n_rows, xs[0], xs[1], w_in_e, w_out_e)


def _combine_kernel(h_ref, yg0_ref, yg1_ref, gate_ref, wsi_ref, wso_ref, g2_ref, b2_ref, o_ref):
    h1 = h_ref[...]
    gate = gate_ref[...]
    routed = None
    for k in range(TOP_K):
        y_k = _unpack_rows([yg0_ref[k], yg1_ref[k]]) * gate[:, k:k + 1]
        routed = y_k if routed is None else routed + y_k
    hs = _dot(h1.astype(BF16), wsi_ref[...])
    sg = hs[:, :SHARED_DIM]
    sv = hs[:, SHARED_DIM:]
    shared = _dot((sg * _sigmoid(sg) * sv).astype(BF16), wso_ref[...])
    o_ref[...] = _layer_norm(DEEPNORM_ALPHA * h1 + routed + shared, g2_ref[...], b2_ref[...])


def _combine(h1, yg, gate_nk, wsi, wso, g2, b2):
    n = h1.shape[0]
    tm = OUT_ROW_TILE
    yg_spec = pl.BlockSpec((TOP_K, tm, PACK_WIDTH), lambda i: (0, i, 0))

    def full(shape):
        return pl.BlockSpec(shape, lambda i: (0,) * len(shape))

    return pl.pallas_call(
        _combine_kernel,
        out_shape=jax.ShapeDtypeStruct((n, D_MODEL), F32),
        grid=(n // tm,),
        in_specs=[
            pl.BlockSpec((tm, D_MODEL), lambda i: (i, 0)),
            yg_spec, yg_spec,
            pl.BlockSpec((tm, TOP_K), lambda i: (i, 0)),
            full((D_MODEL, 2 * SHARED_DIM)), full((SHARED_DIM, D_MODEL)),
            full((1, D_MODEL)), full((1, D_MODEL)),
        ],
        out_specs=pl.BlockSpec((tm, D_MODEL), lambda i: (i, 0)),
        compiler_params=_params(("parallel",)),
        name="shared_combine",
    )(h1, yg[0], yg[1], gate_nk, wsi, wso, g2, b2)


def _plan_kernel(idx_ref, rank_ref, cnt_ref, dest_ref, sblk_ref):
    e = cnt_ref.shape[0]
    tn = idx_ref.shape[1]
    cnt = cnt_ref[...]
    nblk = jnp.floor((cnt + (MOE_BLOCK - 1)) * (1.0 / MOE_BLOCK))
    e_r = lax.broadcasted_iota(I32, (e, e), 0)
    e_c = lax.broadcasted_iota(I32, (e, e), 1)
    lower = jnp.where(e_c < e_r, 1.0, 0.0).astype(BF16)
    start_blk = _dot(lower, nblk.astype(BF16))

    ioe = lax.broadcasted_iota(I32, (e, tn), 0)
    start_row = jnp.concatenate([start_blk * MOE_BLOCK] * (tn // LANES), axis=1)
    idx = idx_ref[...]
    dests = [jnp.sum(jnp.where(ioe == idx[k:k + 1], start_row, 0.0), axis=0, keepdims=True)
             for k in range(TOP_K)]
    dest_ref[...] = jnp.concatenate(dests, axis=0).astype(I32) + rank_ref[...]
    sblk_ref[...] = start_blk.astype(I32)


def _plan(idx_t, rank_t, counts):
    k, n = idx_t.shape
    e = counts.shape[0]
    tn = MIX_ROW_TILE
    tok_spec = pl.BlockSpec((k, tn), lambda i: (0, i))
    exp_spec = pl.BlockSpec((e, LANES), lambda i: (0, 0))
    return pl.pallas_call(
        _plan_kernel,
        out_shape=(jax.ShapeDtypeStruct((k, n), I32), jax.ShapeDtypeStruct((e, LANES), I32)),
        grid=(n // tn,),
        in_specs=[tok_spec, tok_spec, exp_spec],
        out_specs=(tok_spec, exp_spec),
        compiler_params=_params(("arbitrary",)),
        name="dispatch_plan",
    )(idx_t, rank_t, counts)


SC_WINDOW = 128


def _sc_mesh():
    return plsc.VectorSubcoreMesh(core_axis_name="core", subcore_axis_name="subcore")


def _dispatch_rows(parts, dest, p_total):
    n, width = parts[0].shape
    top_k = dest.shape[0]
    out_type = (jax.ShapeDtypeStruct((p_total, width), parts[0].dtype),) * len(parts)

    @functools.partial(pl.kernel, mesh=_sc_mesh(), scratch_types=[], out_type=out_type,
                       name="dispatch_rows")
    def scatter(*refs):
        x_hbms = refs[:len(parts)]
        i_hbm = refs[len(parts)]
        o_hbms = refs[len(parts) + 1:]
        for x_hbm, o_hbm in zip(x_hbms, o_hbms):
            def body(x_vmem, i_vmem, o_hbm=o_hbm):
                for k in range(top_k):
                    pltpu.sync_copy(x_vmem, o_hbm.at[i_vmem.at[k]])

            pltpu.emit_pipeline(
                body,
                grid=(n // SC_WINDOW,),
                in_specs=[pl.BlockSpec((SC_WINDOW, width), lambda i: (i, 0)),
                          pl.BlockSpec((top_k, SC_WINDOW), lambda i: (0, i))],
                out_specs=[],
                core_axis_name=("core", "subcore"),
                dimension_semantics=(pltpu.PARALLEL,),
            )(x_hbm, i_hbm)

    return scatter(*parts, dest)


def _gather_rows(parts, idx_flat):
    count = idx_flat.shape[1]
    width = parts[0].shape[1]
    out_type = (jax.ShapeDtypeStruct((count, width), parts[0].dtype),) * len(parts)

    @functools.partial(pl.kernel, mesh=_sc_mesh(), scratch_types=[], out_type=out_type,
                       name="combine_rows")
    def gather(*refs):
        y_hbms = refs[:len(parts)]
        i_hbm = refs[len(parts)]
        o_hbms = refs[len(parts) + 1:]
        for y_hbm, o_hbm in zip(y_hbms, o_hbms):
            def body(i_vmem, o_vmem, y_hbm=y_hbm):
                pltpu.sync_copy(y_hbm.at[i_vmem.at[0]], o_vmem)

            pltpu.emit_pipeline(
                body,
                grid=(count // SC_WINDOW,),
                in_specs=[pl.BlockSpec((1, SC_WINDOW), lambda i: (0, i))],
                out_specs=[pl.BlockSpec((SC_WINDOW, width), lambda i: (i, 0))],
                core_axis_name=("core", "subcore"),
                dimension_semantics=(pltpu.PARALLEL,),
            )(i_hbm, o_hbm)

    return gather(*parts, idx_flat)


def kernel(x, ln_in_g, ln_in_b, w_in, hgrn_lb_logits, hgrn_norm_w, w_branch_att, w_branch_hgrn,
           w_out, ln1_g, ln1_b, router_w, router_bias, expert_w_in, expert_w_out, shared_w_in,
           shared_w_out, ln2_g, ln2_b):
    batch, seq, d = x.shape
    n = batch * seq
    x2 = x.reshape(n, d)
    row = lambda v: v.reshape(1, -1).astype(F32)

    lower_bounds = jnp.cumsum(jax.nn.softmax(hgrn_lb_logits.astype(F32), axis=0), axis=0)
    l = 0
    w_l = w_in[l].astype(BF16)
    gin, bin_ = row(ln_in_g), row(ln_in_b)

    proj, *acts = _inproj_main(x2, gin, bin_, w_l)
    qkv0, qkv1, qkv2 = _inproj_qkv(acts, w_l, batch, seq)
    qkv0 = qkv0.reshape(3, batch, 1, seq, ATT_GROUP_WIDTH)
    att = []
    for qkv in (qkv0, qkv1, qkv2):
        o_g, lse_g = _attention_group(qkv)
        att.append((o_g, lse_g.transpose(0, 2, 3, 1).reshape(batch, ATT_HEADS, seq)))
    att[0] = (att[0][0].reshape(n, ATT_GROUP_WIDTH), att[0][1])

    ob = _hgrn(proj, row(lower_bounds[l]), row(hgrn_norm_w[l]), batch, seq)

    rw_t = router_w[l].T.astype(F32)
    rwh = rw_t.astype(BF16)
    rwl = (rw_t - rwh.astype(F32)).astype(BF16)
    h1, hp0, hp1, idx_t, gate_t, rank_t, counts = _mix(
        x2, gin, bin_, att, ob, proj,
        w_branch_att[l].astype(BF16), w_branch_hgrn[l].astype(BF16), w_out[l].astype(BF16),
        row(ln1_g[l]), row(ln1_b[l]), rwh, rwl, router_bias[l].reshape(-1, 1).astype(F32),
        batch, seq)
    p_total = n * TOP_K + N_EXPERTS * MOE_BLOCK
    dest, start_blk = _plan(idx_t, rank_t, counts)
    n_rows = counts[:, 0].astype(I32)
    n_blk = (n_rows + (MOE_BLOCK - 1)) // MOE_BLOCK
    xs = _dispatch_rows((hp0, hp1), dest, p_total)
    ys = _experts(start_blk[:, 0], n_blk, n_rows, xs, expert_w_in[l], expert_w_out[l])
    yg = [g.reshape(TOP_K, n, PACK_WIDTH)
          for g in _gather_rows(ys, dest.reshape(1, TOP_K * n))]
    out = _combine(h1, yg, gate_t.T, shared_w_in[l].astype(BF16), shared_w_out[l].astype(BF16),
                   row(ln2_g[l]), row(ln2_b[l]))
    return out.reshape(batch, seq, d)
```

```python
import functools

import jax
import jax.numpy as jnp
from jax import lax
from jax.experimental import pallas as pl
from jax.experimental.pallas import tpu as pltpu
from jax.experimental.pallas import tpu_sc as plsc

F32 = jnp.float32
BF16 = jnp.bfloat16
U32 = jnp.uint32
I32 = jnp.int32

D_MODEL = 1024
ATT_GROUPS = ((128, 1), (512, 4), (2048, 16))
ATT_HEADS = 8
ATT_HEAD_DIM = 64
ATT_GROUP_WIDTH = ATT_HEADS * ATT_HEAD_DIM
ATT_WIDTH = len(ATT_GROUPS) * ATT_GROUP_WIDTH
ATT_BLOCK = 128
HGRN_HEAD_DIM = 128
HGRN_HEADS = D_MODEL // HGRN_HEAD_DIM
HGRN_WIDTH = HGRN_HEADS * HGRN_HEAD_DIM
HGRN_CHUNK = 32
N_EXPERTS = 256
TOP_K = 8
N_EXPERT_GROUPS = 8
TOPK_GROUPS = 4
EXPERT_DIM = 256
SHARED_DIM = 256
ROUTED_SCALE = 2.5
MOE_BLOCK = 128
LN_EPS = 1e-5
RMS_EPS = 1e-6
DEPTH = 1
DEEPNORM_ALPHA = (2 * DEPTH) ** 0.25

VMEM_LIMIT = 56 * 1024 * 1024
LANES = 128

ROW_TILE = 1024
MAIN_COL_TILE = 1536
MAIN_COL_STEPS = 4
MIX_ROW_TILE = 512
OUT_ROW_TILE = 512


def _params(sem, vmem=VMEM_LIMIT):
    return pltpu.CompilerParams(dimension_semantics=sem, vmem_limit_bytes=vmem)


def _layer_norm(x, g, b):
    mu = jnp.mean(x, -1, keepdims=True)
    xc = x - mu
    var = jnp.mean(xc * xc, -1, keepdims=True)
    return xc * lax.rsqrt(var + LN_EPS) * g + b


def _sigmoid(x):
    return 1.0 / (1.0 + jnp.exp(-x))


def _dot(a, b):
    return jnp.dot(a, b, preferred_element_type=F32)


def _dot_nt(a, b):
    return lax.dot_general(a, b, (((1,), (1,)), ((), ())), preferred_element_type=F32)


def _dot_tn(a, b):
    return lax.dot_general(a, b, (((0,), (0,)), ((), ())), preferred_element_type=F32)


def _pack_bf16_pair(lo, hi):
    lo_bits = pltpu.bitcast(lo.astype(BF16).astype(F32), U32) >> 16
    hi_bits = pltpu.bitcast(hi.astype(BF16).astype(F32), U32) & jnp.uint32(0xFFFF0000)
    return hi_bits | lo_bits


def _unpack_bf16_pair(w):
    lo = pltpu.bitcast(w << 16, F32)
    hi = pltpu.bitcast(w & jnp.uint32(0xFFFF0000), F32)
    return lo, hi


PACK_PARTS = 2
PACK_WIDTH = D_MODEL // 2 // PACK_PARTS


def _pack_rows(v):
    half = D_MODEL // 2
    parts = []
    for j in range(PACK_PARTS):
        lo = v[:, j * PACK_WIDTH:(j + 1) * PACK_WIDTH]
        hi = v[:, half + j * PACK_WIDTH:half + (j + 1) * PACK_WIDTH]
        parts.append(pltpu.bitcast(_pack_bf16_pair(lo, hi), I32))
    return parts


def _unpack_rows(parts):
    pairs = [_unpack_bf16_pair(pltpu.bitcast(p, U32)) for p in parts]
    return jnp.concatenate([lo for lo, _ in pairs] + [hi for _, hi in pairs], axis=-1)


def _inproj_main_kernel(x_ref, g_ref, b_ref, w_ref, o_ref, h0_ref, h1_ref, h2_ref, hf_scr):
    j = pl.program_id(1)

    @pl.when(j == 0)
    def _():
        hf = _layer_norm(x_ref[...], g_ref[...], b_ref[...])
        h0_ref[...] = hf.astype(BF16)
        for c in range(D_MODEL // LANES):
            hf_scr[c] = hf[:, c * LANES:(c + 1) * LANES]

    for h_ref, (_, dil) in ((h1_ref, ATT_GROUPS[1]), (h2_ref, ATT_GROUPS[2])):
        rows = ROW_TILE // dil
        share = dil // MAIN_COL_STEPS
        for q in range(share):
            r = j * share + q
            dst = pl.ds(pl.multiple_of(r * rows, rows), rows)
            for c in range(D_MODEL // LANES):
                h_ref[dst, c * LANES:(c + 1) * LANES] = (
                    hf_scr[c, pl.ds(r, rows, stride=dil), :].astype(BF16))

    o_ref[...] = _dot(h0_ref[...], w_ref[...]).astype(BF16)


def _inproj_main(x2, g, b, w_all):
    n = x2.shape[0]
    first = 3 * ATT_WIDTH // MAIN_COL_TILE
    width = w_all.shape[1] - 3 * ATT_WIDTH
    assert width == MAIN_COL_STEPS * MAIN_COL_TILE
    assert all(dil % MAIN_COL_STEPS == 0 for _, dil in ATT_GROUPS[1:])
    act = jax.ShapeDtypeStruct((n, D_MODEL), BF16)
    act_spec = pl.BlockSpec((ROW_TILE, D_MODEL), lambda i, j: (i, 0))
    return pl.pallas_call(
        _inproj_main_kernel,
        out_shape=(jax.ShapeDtypeStruct((n, width), BF16), act, act, act),
        grid=(n // ROW_TILE, MAIN_COL_STEPS),
        in_specs=[
            pl.BlockSpec((ROW_TILE, D_MODEL), lambda i, j: (i, 0)),
            pl.BlockSpec((1, D_MODEL), lambda i, j: (0, 0)),
            pl.BlockSpec((1, D_MODEL), lambda i, j: (0, 0)),
            pl.BlockSpec((D_MODEL, MAIN_COL_TILE), lambda i, j: (0, first + j)),
        ],
        out_specs=(pl.BlockSpec((ROW_TILE, MAIN_COL_TILE), lambda i, j: (i, j)),
                   act_spec, act_spec, act_spec),
        scratch_shapes=[pltpu.VMEM((D_MODEL // LANES, ROW_TILE, LANES), F32)],
        compiler_params=_params(("parallel", "arbitrary")),
        name="inproj_main",
    )(x2, g, b, w_all)


def _inproj_qkv_kernel(h0_ref, h1_ref, h2_ref, w_ref, o0_ref, o1_ref, o2_ref):
    gw = ATT_GROUP_WIDTH
    o0_ref[0] = _dot(h0_ref[...], w_ref[:, 0:gw]).astype(BF16)
    d1 = ATT_GROUPS[1][1]
    o1_ref[0, 0] = _dot(h1_ref[...], w_ref[:, gw:2 * gw]).astype(BF16).reshape(d1, ROW_TILE // d1, gw)
    d2 = ATT_GROUPS[2][1]
    o2_ref[0, 0] = _dot(h2_ref[...], w_ref[:, 2 * gw:3 * gw]).astype(BF16).reshape(d2, ROW_TILE // d2, gw)


def _inproj_qkv(acts, w_all, batch, seq):
    n = acts[0].shape[0]
    gw = ATT_GROUP_WIDTH
    tiles_per_seq = seq // ROW_TILE
    d1, d2 = ATT_GROUPS[1][1], ATT_GROUPS[2][1]
    out_shape = (
        jax.ShapeDtypeStruct((3, n, gw), BF16),
        jax.ShapeDtypeStruct((3, batch, d1, seq // d1, gw), BF16),
        jax.ShapeDtypeStruct((3, batch, d2, seq // d2, gw), BF16),
    )
    act_spec = pl.BlockSpec((ROW_TILE, D_MODEL), lambda i, t: (i, 0))
    return pl.pallas_call(
        _inproj_qkv_kernel,
        out_shape=out_shape,
        grid=(n // ROW_TILE, 3),
        in_specs=[act_spec, act_spec, act_spec,
                  pl.BlockSpec((D_MODEL, ATT_WIDTH), lambda i, t: (0, t))],
        out_specs=(
            pl.BlockSpec((1, ROW_TILE, gw), lambda i, t: (t, i, 0)),
            pl.BlockSpec((1, 1, d1, ROW_TILE // d1, gw),
                         lambda i, t: (t, i // tiles_per_seq, 0, i % tiles_per_seq, 0)),
            pl.BlockSpec((1, 1, d2, ROW_TILE // d2, gw),
                         lambda i, t: (t, i // tiles_per_seq, 0, i % tiles_per_seq, 0)),
        ),
        compiler_params=_params(("parallel", "arbitrary")),
        name="inproj_qkv",
    )(*acts, w_all)


ATT_HEADS_PER_MATMUL = 8
ATT_BATCH_PER_STEP = 8


def _attn_block(q_ref, kc_ref, kp_ref, v_ref, o_ref, lse_ref, vt_ref, bias, has_prev):
    blk = ATT_BLOCK
    hd = ATT_HEAD_DIM
    gw = ATT_GROUP_WIDTH
    hpm = ATT_HEADS_PER_MATMUL
    width = hpm * hd

    vt_cur = jnp.concatenate(
        [v_ref[:, c * LANES:(c + 1) * LANES].astype(F32).T.astype(BF16)
         for c in range(gw // LANES)], axis=0)
    vt_prev = jnp.where(has_prev, vt_ref[...], jnp.zeros_like(vt_cur))
    vt_ref[...] = vt_cur

    lane_head = lax.broadcasted_iota(I32, (blk, width), 1) // hd
    per_tile = LANES // hd
    for g in range(ATT_HEADS // hpm):
        feat = slice(g * width, (g + 1) * width)
        q_g = q_ref[:, feat] * (hd ** -0.5)
        q_bd = jnp.concatenate([jnp.where(lane_head == i, q_g, jnp.zeros_like(q_g))
                                for i in range(hpm)], axis=0)
        k_g = jnp.concatenate([kp_ref[:, feat], kc_ref[:, feat]], axis=0)
        s_t = _dot_nt(k_g, q_bd) + bias
        m = jnp.max(s_t, axis=0, keepdims=True)
        p = jnp.exp(s_t - m)
        l = jnp.sum(p, axis=0, keepdims=True)
        v_t = jnp.concatenate([vt_prev[feat, :], vt_cur[feat, :]], axis=1)
        o_t = _dot(v_t, p.astype(BF16))
        lse = m + jnp.log(l)
        inv_l = 1.0 / l
        for c in range(width // LANES):
            tile = []
            for i in range(c * per_tile, (c + 1) * per_tile):
                cols = slice(i * blk, (i + 1) * blk)
                tile.append(o_t[i * hd:(i + 1) * hd, cols] * inv_l[:, cols])
                lse_ref[g * hpm + i:g * hpm + i + 1, :] = lse[:, cols]
            lanes = slice(g * width + c * LANES, g * width + (c + 1) * LANES)
            o_ref[:, lanes] = jnp.concatenate(tile, axis=0).T.astype(BF16)


def _attn_kernel(q_ref, kc_ref, kp_ref, v_ref, o_ref, lse_ref, vt_scr):
    blk = ATT_BLOCK
    has_prev = pl.program_id(2) > 0
    key_i = lax.broadcasted_iota(I32, (2 * blk, blk), 0)
    qry_i = lax.broadcasted_iota(I32, (2 * blk, blk), 1)
    live = (((key_i < blk) & (key_i >= qry_i) & has_prev)
            | ((key_i >= blk) & (key_i - blk <= qry_i)))
    bias = jnp.concatenate([jnp.where(live, 0.0, -jnp.inf)] * ATT_HEADS_PER_MATMUL, axis=1)
    for j in range(ATT_BATCH_PER_STEP):
        _attn_block(q_ref.at[j], kc_ref.at[j], kp_ref.at[j], v_ref.at[j], o_ref.at[j],
                    lse_ref.at[j], vt_scr.at[j], bias, has_prev)


def _attention_group(qkv):
    _, batch, dil, sub_len, gw = qkv.shape
    blk = ATT_BLOCK
    bps = ATT_BATCH_PER_STEP

    def spec(t, prev):
        def index(b, r, i):
            return (t, b, r, jnp.maximum(i - 1, 0) if prev else i, 0)
        return pl.BlockSpec((None, bps, None, blk, gw), index)

    return pl.pallas_call(
        _attn_kernel,
        out_shape=(jax.ShapeDtypeStruct((batch, dil, sub_len, gw), BF16),
                   jax.ShapeDtypeStruct((batch, dil, ATT_HEADS, sub_len), F32)),
        grid=(batch // bps, dil, sub_len // blk),
        in_specs=[spec(0, False), spec(1, False), spec(1, True), spec(2, False)],
        out_specs=(pl.BlockSpec((bps, None, blk, gw), lambda b, r, i: (b, r, i, 0)),
                   pl.BlockSpec((bps, None, ATT_HEADS, blk), lambda b, r, i: (b, r, 0, i))),
        scratch_shapes=[pltpu.VMEM((bps, gw, blk), BF16)],
        compiler_params=_params(("parallel", "parallel", "arbitrary")),
        name=f"dilated_attention_d{dil}",
    )(qkv, qkv, qkv, qkv)


HGRN_ROWS = 256
HGRN_HEADS_PER_STEP = 8
HGRN_MAX_FACTOR_EXPONENT = 60.0


def _hgrn_pair(q, key, bcum, v, gate, states, nw, same_chunk_causal, chunk_mask):
    c = HGRN_CHUNK
    dk = HGRN_HEAD_DIM
    rows = HGRN_ROWS
    nchunk = rows // c
    pair = 2 * dk

    b3 = bcum.reshape(nchunk, c, pair)
    b_mid = b3[:, c // 2:c // 2 + 1, :]
    b_last = b3[:, c - 1:c, :]
    q3 = q.astype(F32).reshape(nchunk, c, pair)
    q_t3 = q3 * jnp.exp(b3 - b_mid)
    k_t3 = key.reshape(nchunk, c, pair) * jnp.exp(b_mid - b3)
    q_t = q_t3.reshape(rows, pair).astype(BF16)
    k_t = k_t3.reshape(rows, pair).astype(BF16)
    q_in = (q_t3 * jnp.exp(b_mid)).reshape(rows, pair).astype(BF16)
    k_st = (k_t3 * jnp.exp(b_last - b_mid)).reshape(rows, pair).astype(BF16)
    decay = jnp.exp(b_last.reshape(nchunk, pair))

    def block_diagonal(x):
        zero = jnp.zeros((rows, dk), x.dtype)
        return jnp.concatenate([jnp.concatenate([x[:, :dk], zero], axis=1),
                                jnp.concatenate([zero, x[:, dk:]], axis=1)], axis=0)

    scores = _dot_nt(block_diagonal(q_t), k_t)
    att = jnp.concatenate([jnp.where(same_chunk_causal, scores[:rows], 0.0),
                           jnp.where(same_chunk_causal, scores[rows:], 0.0)], axis=1)
    o = _dot(att.astype(BF16), block_diagonal(v))

    new_states, outs = [], []
    for i, st in enumerate(states):
        hs = slice(i * dk, (i + 1) * dk)
        k_spread = jnp.concatenate([k_st[:, hs] * chunk_mask[j] for j in range(nchunk)], axis=1)
        incr = _dot_tn(v[:, hs], k_spread)
        before = []
        for j in range(nchunk):
            before.append(st.astype(BF16))
            st = st * decay[j:j + 1, hs] + incr[:, j * dk:(j + 1) * dk]
        q_spread = jnp.concatenate([q_in[:, hs] * chunk_mask[j] for j in range(nchunk)], axis=1)
        o_h = o[:, hs] + _dot_nt(q_spread, jnp.concatenate(before, axis=1))
        outs.append(o_h * lax.rsqrt(jnp.mean(o_h * o_h, -1, keepdims=True) + RMS_EPS * dk) * nw)
        new_states.append(st)
    gate = gate.astype(F32)
    return jnp.concatenate(outs, axis=1) * gate * _sigmoid(gate), new_states


def _hgrn_chunk_direct(q, key, b, v, gate, st, nw):
    c = HGRN_CHUNK
    dk = HGRN_HEAD_DIM
    v32 = v.astype(F32)
    row = lax.broadcasted_iota(I32, (c, dk), 0)
    o = _dot_nt((q * jnp.exp(b)).astype(BF16), st.astype(BF16))
    for s_ in range(c):
        dec = jnp.exp(jnp.where(row >= s_, b - b[s_:s_ + 1], -jnp.inf))
        a = jnp.sum(q * key[s_:s_ + 1] * dec, axis=-1, keepdims=True)
        o = o + a * v32[s_:s_ + 1]
    b_last = b[c - 1:c]
    st = st * jnp.exp(b_last) + _dot_tn(v, (key * jnp.exp(b_last - b)).astype(BF16))
    o = o * lax.rsqrt(jnp.mean(o * o, -1, keepdims=True) + RMS_EPS * dk) * nw
    gate = gate.astype(F32)
    return o * gate * _sigmoid(gate), st


def _hgrn_kernel(q_ref, f_ref, i_ref, g_ref, lb_ref, nw_ref, o_ref, state_scr, mask_scr,
                 key_scr, b_scr):
    seq = q_ref.shape[0]
    c = HGRN_CHUNK
    rows = HGRN_ROWS
    dk = HGRN_HEAD_DIM
    width = HGRN_HEADS_PER_STEP * dk
    nw = nw_ref[...]

    r_i = lax.broadcasted_iota(I32, (rows, rows), 0)
    c_i = lax.broadcasted_iota(I32, (rows, rows), 1)
    same_chunk_causal = (r_i // c == c_i // c) & (c_i <= r_i)
    tri = jnp.where(same_chunk_causal, 1.0, 0.0).astype(BF16)
    row_chunk = lax.broadcasted_iota(I32, (rows, dk), 0) // c
    for j in range(rows // c):
        mask_scr[j] = jnp.where(row_chunk == j, 1.0, 0.0).astype(BF16)

    state_scr[...] = jnp.zeros_like(state_scr)

    def body(gi, carry):
        rs = pl.ds(pl.multiple_of(gi * rows, rows), rows)
        lb = lb_ref[...]
        one_m_lb = 1.0 - lb
        z = f_ref[rs, :].astype(F32)
        log_f = jnp.log(lb + one_m_lb * _sigmoid(z))
        p0 = log_f.astype(BF16)
        p1 = (log_f - p0.astype(F32)).astype(BF16)
        both = _dot(tri, jnp.concatenate([p0, p1], axis=1))
        bcum = both[:, :width] + both[:, width:]

        b3 = bcum.reshape(rows // c, c, width)
        span = jnp.maximum(b3[:, 0, :] - b3[:, c // 2, :], b3[:, c // 2, :] - b3[:, c - 1, :])
        factorable = jnp.max(span) < HGRN_MAX_FACTOR_EXPONENT

        @pl.when(factorable)
        def _():
            key = one_m_lb * _sigmoid(-z)
            for h in range(0, HGRN_HEADS_PER_STEP, 2):
                cs = slice(h * dk, (h + 2) * dk)
                o, sts = _hgrn_pair(q_ref[rs, cs], key[:, cs], bcum[:, cs], i_ref[rs, cs],
                                    g_ref[rs, cs], [state_scr[h], state_scr[h + 1]], nw,
                                    same_chunk_causal, mask_scr)
                state_scr[h], state_scr[h + 1] = sts
                o_ref[rs, cs] = o.astype(BF16)

        @pl.when(jnp.logical_not(factorable))
        def _():
            key_scr[...] = one_m_lb * _sigmoid(-z)
            b_scr[...] = bcum
            for h in range(HGRN_HEADS_PER_STEP):
                cs = slice(h * dk, (h + 1) * dk)

                def chunk(j, st, cs=cs):
                    local = pl.ds(pl.multiple_of(j * c, c), c)
                    r = pl.ds(pl.multiple_of(gi * rows + j * c, c), c)
                    o, st = _hgrn_chunk_direct(q_ref[r, cs].astype(F32), key_scr[local, cs],
                                               b_scr[local, cs], i_ref[r, cs], g_ref[r, cs], st, nw)
                    o_ref[r, cs] = o.astype(BF16)
                    return st

                state_scr[h] = lax.fori_loop(0, rows // c, chunk, state_scr[h])

        return carry

    lax.fori_loop(0, seq // rows, body, 0)


def _hgrn(proj, lower_bound, norm_w, batch, seq):
    n = proj.shape[0]
    dk = HGRN_HEAD_DIM
    width = HGRN_HEADS_PER_STEP * dk
    steps = HGRN_HEADS // HGRN_HEADS_PER_STEP

    def seg(k):
        return pl.BlockSpec((seq, width), lambda b, h: (b, k * steps + h))

    return pl.pallas_call(
        _hgrn_kernel,
        out_shape=jax.ShapeDtypeStruct((n, HGRN_WIDTH), BF16),
        grid=(batch, steps),
        in_specs=[seg(0), seg(1), seg(2), seg(3),
                  pl.BlockSpec((1, width), lambda b, h: (0, h)),
                  pl.BlockSpec((1, dk), lambda b, h: (0, 0))],
        out_specs=pl.BlockSpec((seq, width), lambda b, h: (b, h)),
        scratch_shapes=[pltpu.VMEM((HGRN_HEADS_PER_STEP, dk, dk), F32),
                        pltpu.VMEM((HGRN_ROWS // HGRN_CHUNK, HGRN_ROWS, dk), BF16),
                        pltpu.VMEM((HGRN_ROWS, width), F32),
                        pltpu.VMEM((HGRN_ROWS, width), F32)],
        compiler_params=_params(("parallel", "parallel")),
        name="hgrn2",
    )(proj, proj, proj, proj, lower_bound, norm_w)


def _pick_first_max(vals, iota, axis, size):
    m = jnp.max(vals, axis=axis, keepdims=True)
    idx = jnp.min(jnp.where(vals == m, iota, size), axis=axis, keepdims=True)
    return m, idx


def _route_tile(logits, bias, base):
    e, tn = logits.shape
    groups = N_EXPERT_GROUPS
    gsz = e // groups
    neg = -jnp.inf

    scores = _sigmoid(logits)
    biased = scores + bias

    b3 = biased.reshape(groups, gsz, tn)
    io3 = lax.broadcasted_iota(I32, b3.shape, 1)
    m1, i1 = _pick_first_max(b3, io3, 1, gsz)
    m2 = jnp.max(jnp.where(io3 == i1, neg, b3), axis=1, keepdims=True)
    grp = m1 + m2
    iog = lax.broadcasted_iota(I32, grp.shape, 0)
    keep = jnp.zeros(grp.shape, F32)
    for _ in range(TOPK_GROUPS):
        _, gi = _pick_first_max(grp, iog, 0, groups)
        hit = iog == gi
        keep = jnp.where(hit, 1.0, keep)
        grp = jnp.where(hit, neg, grp)
    masked = jnp.where(keep > 0.0, b3, neg).reshape(e, tn)

    ioe = lax.broadcasted_iota(I32, (e, tn), 0)
    onehot = jnp.zeros((e, tn), F32)
    idxs, gates = [], []
    for _ in range(TOP_K):
        _, ei = _pick_first_max(masked, ioe, 0, e)
        hit = ioe == ei
        gates.append(jnp.sum(jnp.where(hit, scores, 0.0), axis=0, keepdims=True))
        onehot = jnp.where(hit, 1.0, onehot)
        masked = jnp.where(hit, neg, masked)
        idxs.append(ei)
    gsum = gates[0]
    for g in gates[1:]:
        gsum = gsum + g
    gate = jnp.concatenate(gates, axis=0) / gsum * ROUTED_SCALE

    t_r = lax.broadcasted_iota(I32, (tn, tn), 0)
    t_c = lax.broadcasted_iota(I32, (tn, tn), 1)
    earlier = jnp.where(t_r < t_c, 1.0, 0.0).astype(BF16)
    oh = onehot.astype(BF16)
    before = _dot(oh, earlier) + jnp.concatenate([base] * (tn // LANES), axis=1)
    ranks = [jnp.sum(jnp.where(ioe == ei, before, 0.0), axis=0, keepdims=True) for ei in idxs]
    total = base + _dot(oh, jnp.ones((tn, LANES), BF16))
    return (jnp.concatenate(idxs, axis=0), gate,
            jnp.concatenate(ranks, axis=0).astype(I32), total)


def _mix_kernel(x_ref, gin_ref, bin_ref, o0_ref, o1_ref, o2_ref, l0_ref, l1_ref, l2_ref,
                ob_ref, ga_ref, gb_ref, wa_ref, wb_ref, wo_ref, g1_ref, b1_ref,
                rwh_ref, rwl_ref, rbias_ref,
                h_ref, hp0_ref, hp1_ref, idx_ref, gate_ref, rank_ref, cnt_ref,
                so1_scr, so2_scr, lg_scr, carry_scr):
    tm = x_ref.shape[0]
    gw = ATT_GROUP_WIDTH
    step = pl.program_id(0)

    @pl.when(step == 0)
    def _():
        lg_scr[...] = jnp.zeros_like(lg_scr)
        carry_scr[...] = jnp.zeros_like(carry_scr)

    counts = carry_scr[...]
    idx, gate, rank, total = _route_tile(lg_scr[...], rbias_ref[...], counts)
    idx_ref[...] = idx
    gate_ref[...] = gate
    rank_ref[...] = rank
    counts = jnp.where(step > 0, total, counts)
    carry_scr[...] = counts
    cnt_ref[...] = counts

    for o_ref, so_scr, (_, dil) in ((o1_ref, so1_scr, ATT_GROUPS[1]),
                                    (o2_ref, so2_scr, ATT_GROUPS[2])):
        for r in range(dil):
            o_r = o_ref[r].astype(F32)
            for c in range(gw // LANES):
                so_scr[c, pl.ds(r, tm // dil, stride=dil), :] = o_r[:, c * LANES:(c + 1) * LANES]

    def natural(scr):
        return jnp.concatenate([scr[c] for c in range(gw // LANES)], axis=-1)

    l0, l1, l2 = l0_ref[...], l1_ref[...], l2_ref[...]
    m = jnp.maximum(jnp.maximum(l0, l1), l2)
    e0 = jnp.exp(l0 - m)
    e1 = jnp.exp(l1 - m)
    e2 = jnp.exp(l2 - m)
    inv = 1.0 / (e0 + e1 + e2)
    terms = []
    for w in (e0 * inv, e1 * inv, e2 * inv):
        hi = w.astype(BF16).astype(F32)
        terms += [hi, w - hi]
    terms = jnp.concatenate(terms, axis=0).astype(BF16)
    n_groups = len(ATT_GROUPS)
    t_i = lax.broadcasted_iota(I32, (n_groups * 2 * ATT_HEADS, n_groups * gw), 0)
    c_i = lax.broadcasted_iota(I32, (n_groups * 2 * ATT_HEADS, n_groups * gw), 1)
    spread = jnp.where((t_i // (2 * ATT_HEADS) == c_i // gw)
                       & (t_i % ATT_HEADS == c_i % gw // ATT_HEAD_DIM), 1.0, 0.0).astype(BF16)
    wide = _dot_tn(terms, spread)
    o_att = (wide[:, :gw] * o0_ref[...].astype(F32) + wide[:, gw:2 * gw] * natural(so1_scr)
             + wide[:, 2 * gw:] * natural(so2_scr))

    y_a = _dot(o_att.astype(BF16), wa_ref[...])
    y_b = _dot(ob_ref[...], wb_ref[...])
    merged = _sigmoid(ga_ref[...].astype(F32)) * y_a + _sigmoid(gb_ref[...].astype(F32)) * y_b
    mix = _dot(merged.astype(BF16), wo_ref[...])
    h_in = _layer_norm(x_ref[...], gin_ref[...], bin_ref[...])
    h1 = _layer_norm(DEEPNORM_ALPHA * h_in + mix, g1_ref[...], b1_ref[...])
    h_ref[...] = h1

    hp0_ref[...], hp1_ref[...] = _pack_rows(h1)

    h_hi = h1.astype(BF16)
    h_lo = (h1 - h_hi.astype(F32)).astype(BF16)
    rwh = rwh_ref[...]
    lg_scr[...] = _dot_nt(rwh, h_hi) + _dot_nt(rwh, h_lo) + _dot_nt(rwl_ref[...], h_hi)


def _mix(x2, gin, bin_, att, ob, proj, wa, wb, wo, g1, b1, rwh, rwl, rbias, batch, seq):
    n = x2.shape[0]
    tm = MIX_ROW_TILE
    gw = ATT_GROUP_WIDTH
    tiles_per_seq = seq // tm
    last = n // tm - 1
    (o0, l0), (o1, l1), (o2, l2) = att
    d1, d2 = ATT_GROUPS[1][1], ATT_GROUPS[2][1]

    def tile(i):
        return jnp.minimum(i, last)

    def routed(i):
        return jnp.maximum(i - 1, 0)

    lse_spec = pl.BlockSpec((None, ATT_HEADS, tm),
                            lambda i: (tile(i) // tiles_per_seq, 0, tile(i) % tiles_per_seq))

    def full(shape):
        return pl.BlockSpec(shape, lambda i: (0,) * len(shape))

    def dil_spec(dil):
        return pl.BlockSpec((None, dil, tm // dil, gw),
                            lambda i: (tile(i) // tiles_per_seq, 0, tile(i) % tiles_per_seq, 0))

    def rows(width, col=0):
        return pl.BlockSpec((tm, width), lambda i: (tile(i), col))

    tok_spec = pl.BlockSpec((TOP_K, tm), lambda i: (0, routed(i)))
    return pl.pallas_call(
        _mix_kernel,
        out_shape=(jax.ShapeDtypeStruct((n, D_MODEL), F32),
                   jax.ShapeDtypeStruct((n, PACK_WIDTH), I32),
                   jax.ShapeDtypeStruct((n, PACK_WIDTH), I32),
                   jax.ShapeDtypeStruct((TOP_K, n), I32),
                   jax.ShapeDtypeStruct((TOP_K, n), F32),
                   jax.ShapeDtypeStruct((TOP_K, n), I32),
                   jax.ShapeDtypeStruct((N_EXPERTS, LANES), F32)),
        grid=(n // tm + 1,),
        in_specs=[
            rows(D_MODEL),
            full((1, D_MODEL)), full((1, D_MODEL)),
            rows(gw), dil_spec(d1), dil_spec(d2),
            lse_spec, lse_spec, lse_spec,
            rows(HGRN_WIDTH), rows(D_MODEL, 4), rows(D_MODEL, 5),
            full((gw, D_MODEL)), full((HGRN_WIDTH, D_MODEL)), full((D_MODEL, D_MODEL)),
            full((1, D_MODEL)), full((1, D_MODEL)),
            full((N_EXPERTS, D_MODEL)), full((N_EXPERTS, D_MODEL)), full((N_EXPERTS, 1)),
        ],
        out_specs=(rows(D_MODEL), rows(PACK_WIDTH), rows(PACK_WIDTH),
                   tok_spec, tok_spec, tok_spec, full((N_EXPERTS, LANES))),
        scratch_shapes=[pltpu.VMEM((gw // LANES, tm, LANES), F32),
                        pltpu.VMEM((gw // LANES, tm, LANES), F32),
                        pltpu.VMEM((N_EXPERTS, tm), F32),
                        pltpu.VMEM((N_EXPERTS, LANES), F32)],
        compiler_params=_params(("arbitrary",)),
        name="branch_mix_route",
    )(x2, gin, bin_, o0, o1, o2, l0, l1, l2, ob, proj, proj, wa, wb, wo, g1, b1, rwh, rwl, rbias)


ROW_SLOTS = 16
ROW_GROUP = 4
ROW_AHEAD = ROW_SLOTS - ROW_GROUP
W_SLOTS = 3


def _expert_kernel(sblk_ref, nblk_ref, cnt_ref, xs0_hbm, xs1_hbm, wi_hbm, wo_hbm,
                   ys0_hbm, ys1_hbm, xbuf0, xbuf1, ybuf0, ybuf1, wibuf, wobuf,
                   in_sem, out_sem, w_sem, wi_scr, wo_scr):
    e = pl.program_id(0)
    n_exp = pl.num_programs(0)
    first_blk = sblk_ref[e]
    n_blk = nblk_ref[e]
    n_rows = cnt_ref[e]
    total = sblk_ref[n_exp - 1] + nblk_ref[n_exp - 1]
    xs_hbm, ys_hbm = (xs0_hbm, xs1_hbm), (ys0_hbm, ys1_hbm)
    xbuf, ybuf = (xbuf0, xbuf1), (ybuf0, ybuf1)

    def rows_of(b):
        return pl.ds(pl.multiple_of(b * MOE_BLOCK, MOE_BLOCK), MOE_BLOCK)

    def x_copy(b, part):
        slot = b % ROW_SLOTS
        return pltpu.make_async_copy(xs_hbm[part].at[rows_of(b)], xbuf[part].at[slot],
                                     in_sem.at[part, slot])

    def y_copy(b, part):
        slot = b % ROW_SLOTS
        return pltpu.make_async_copy(ybuf[part].at[slot], ys_hbm[part].at[rows_of(b)],
                                     out_sem.at[part, slot])

    def w_copies(ex):
        slot = ex % W_SLOTS
        return (pltpu.make_async_copy(wi_hbm.at[ex], wibuf.at[slot], w_sem.at[0, slot]),
                pltpu.make_async_copy(wo_hbm.at[ex], wobuf.at[slot], w_sem.at[1, slot]))

    def start_weights(ex):
        @pl.when((ex < n_exp) & (nblk_ref[jnp.minimum(ex, n_exp - 1)] > 0))
        def _():
            for cp in w_copies(ex):
                cp.start()

    @pl.when(e == 0)
    def _():
        for b in range(ROW_AHEAD):
            @pl.when(b < total)
            def _():
                for part in range(PACK_PARTS):
                    x_copy(b, part).start()
        for ex in range(W_SLOTS - 1):
            start_weights(ex)

    start_weights(e + (W_SLOTS - 1))

    def step(j, group):
        b0 = first_blk + j
        for i in range(group):
            for part in range(PACK_PARTS):
                x_copy(b0 + i, part).wait()
        for i in range(group):
            @pl.when(b0 + ROW_AHEAD + i < total)
            def _():
                for part in range(PACK_PARTS):
                    x_copy(b0 + ROW_AHEAD + i, part).start()
        for i in range(group):
            @pl.when(b0 + i >= ROW_SLOTS)
            def _():
                for part in range(PACK_PARTS):
                    y_copy(b0 + i - ROW_SLOTS, part).wait()

        rows = group * MOE_BLOCK
        live = (lax.broadcasted_iota(I32, (rows, PACK_WIDTH), 0)
                < n_rows - j * MOE_BLOCK)
        parts = [jnp.concatenate([xbuf[part][(b0 + i) % ROW_SLOTS] for i in range(group)], axis=0)
                 for part in range(PACK_PARTS)]
        x = _unpack_rows([jnp.where(live, p, 0) for p in parts])
        hu = _dot(x.astype(BF16), wi_scr[...])
        hg = hu[:, :EXPERT_DIM]
        hv = hu[:, EXPERT_DIM:]
        act = (hg * _sigmoid(hg) * hv).astype(BF16)
        packed = _pack_rows(_dot(act, wo_scr[...]))
        for i in range(group):
            for part in range(PACK_PARTS):
                ybuf[part][(b0 + i) % ROW_SLOTS] = packed[part][i * MOE_BLOCK:(i + 1) * MOE_BLOCK]
                y_copy(b0 + i, part).start()

    @pl.when(n_blk > 0)
    def _():
        for cp in w_copies(e):
            cp.wait()
        slot = e % W_SLOTS
        wi_scr[...] = wibuf[slot].astype(BF16)
        wo_scr[...] = wobuf[slot].astype(BF16)

        def full_group(g, carry):
            step(g * ROW_GROUP, ROW_GROUP)
            return carry

        lax.fori_loop(0, n_blk // ROW_GROUP, full_group, 0)
        done = n_blk // ROW_GROUP * ROW_GROUP
        group = ROW_GROUP // 2
        while group >= 1:
            @pl.when((n_blk & group) != 0)
            def _(group=group, done=done):
                step(done, group)
            done = done + (n_blk & group)
            group //= 2

    @pl.when(e == n_exp - 1)
    def _():
        for back in range(1, ROW_SLOTS + 1):
            @pl.when(total >= back)
            def _():
                for part in range(PACK_PARTS):
                    y_copy(total - back, part).wait()


def _experts(start_blk, n_blk, n_rows, xs, w_in_e, w_out_e):
    p = xs[0].shape[0]
    n_exp = w_in_e.shape[0]
    any_spec = pl.BlockSpec(memory_space=pl.ANY)
    row_buf = pltpu.VMEM((ROW_SLOTS, MOE_BLOCK, PACK_WIDTH), I32)
    grid_spec = pltpu.PrefetchScalarGridSpec(
        num_scalar_prefetch=3,
        grid=(n_exp,),
        in_specs=[any_spec, any_spec, any_spec, any_spec],
        out_specs=(any_spec, any_spec),
        scratch_shapes=[row_buf, row_buf, row_buf, row_buf,
                        pltpu.VMEM((W_SLOTS, D_MODEL, 2 * EXPERT_DIM), F32),
                        pltpu.VMEM((W_SLOTS, EXPERT_DIM, D_MODEL), F32),
                        pltpu.SemaphoreType.DMA((PACK_PARTS, ROW_SLOTS)),
                        pltpu.SemaphoreType.DMA((PACK_PARTS, ROW_SLOTS)),
                        pltpu.SemaphoreType.DMA((2, W_SLOTS)),
                        pltpu.VMEM((D_MODEL, 2 * EXPERT_DIM), BF16),
                        pltpu.VMEM((EXPERT_DIM, D_MODEL), BF16)],
    )
    return pl.pallas_call(
        _expert_kernel,
        out_shape=(jax.ShapeDtypeStruct((p, PACK_WIDTH), I32),) * PACK_PARTS,
        grid_spec=grid_spec,
        compiler_params=_params(("arbitrary",)),
        name="routed_experts",
    )(start_blk, n_blk, n_rows, xs[0], xs[1], w_in_e, w_out_e)


def _combine_kernel(h_ref, yg0_ref, yg1_ref, gate_ref, wsi_ref, wso_ref, g2_ref, b2_ref, o_ref):
    h1 = h_ref[...]
    gate = gate_ref[...]
    routed = None
    for k in range(TOP_K):
        y_k = _unpack_rows([yg0_ref[k], yg1_ref[k]]) * gate[:, k:k + 1]
        routed = y_k if routed is None else routed + y_k
    hs = _dot(h1.astype(BF16), wsi_ref[...])
    sg = hs[:, :SHARED_DIM]
    sv = hs[:, SHARED_DIM:]
    shared = _dot((sg * _sigmoid(sg) * sv).astype(BF16), wso_ref[...])
    o_ref[...] = _layer_norm(DEEPNORM_ALPHA * h1 + routed + shared, g2_ref[...], b2_ref[...])


def _combine(h1, yg, gate_nk, wsi, wso, g2, b2):
    n = h1.shape[0]
    tm = OUT_ROW_TILE
    yg_spec = pl.BlockSpec((TOP_K, tm, PACK_WIDTH), lambda i: (0, i, 0))

    def full(shape):
        return pl.BlockSpec(shape, lambda i: (0,) * len(shape))

    return pl.pallas_call(
        _combine_kernel,
        out_shape=jax.ShapeDtypeStruct((n, D_MODEL), F32),
        grid=(n // tm,),
        in_specs=[
            pl.BlockSpec((tm, D_MODEL), lambda i: (i, 0)),
            yg_spec, yg_spec,
            pl.BlockSpec((tm, TOP_K), lambda i: (i, 0)),
            full((D_MODEL, 2 * SHARED_DIM)), full((SHARED_DIM, D_MODEL)),
            full((1, D_MODEL)), full((1, D_MODEL)),
        ],
        out_specs=pl.BlockSpec((tm, D_MODEL), lambda i: (i, 0)),
        compiler_params=_params(("parallel",)),
        name="shared_combine",
    )(h1, yg[0], yg[1], gate_nk, wsi, wso, g2, b2)


def _plan_kernel(idx_ref, rank_ref, cnt_ref, dest_ref, sblk_ref):
    e = cnt_ref.shape[0]
    tn = idx_ref.shape[1]
    cnt = cnt_ref[...]
    nblk = jnp.floor((cnt + (MOE_BLOCK - 1)) * (1.0 / MOE_BLOCK))
    e_r = lax.broadcasted_iota(I32, (e, e), 0)
    e_c = lax.broadcasted_iota(I32, (e, e), 1)
    lower = jnp.where(e_c < e_r, 1.0, 0.0).astype(BF16)
    start_blk = _dot(lower, nblk.astype(BF16))

    ioe = lax.broadcasted_iota(I32, (e, tn), 0)
    start_row = jnp.concatenate([start_blk * MOE_BLOCK] * (tn // LANES), axis=1)
    idx = idx_ref[...]
    dests = [jnp.sum(jnp.where(ioe == idx[k:k + 1], start_row, 0.0), axis=0, keepdims=True)
             for k in range(TOP_K)]
    dest_ref[...] = jnp.concatenate(dests, axis=0).astype(I32) + rank_ref[...]
    sblk_ref[...] = start_blk.astype(I32)


def _plan(idx_t, rank_t, counts):
    k, n = idx_t.shape
    e = counts.shape[0]
    tn = MIX_ROW_TILE
    tok_spec = pl.BlockSpec((k, tn), lambda i: (0, i))
    exp_spec = pl.BlockSpec((e, LANES), lambda i: (0, 0))
    return pl.pallas_call(
        _plan_kernel,
        out_shape=(jax.ShapeDtypeStruct((k, n), I32), jax.ShapeDtypeStruct((e, LANES), I32)),
        grid=(n // tn,),
        in_specs=[tok_spec, tok_spec, exp_spec],
        out_specs=(tok_spec, exp_spec),
        compiler_params=_params(("arbitrary",)),
        name="dispatch_plan",
    )(idx_t, rank_t, counts)


SC_WINDOW = 128


def _sc_mesh():
    return plsc.VectorSubcoreMesh(core_axis_name="core", subcore_axis_name="subcore")


def _dispatch_rows(parts, dest, p_total):
    n, width = parts[0].shape
    top_k = dest.shape[0]
    out_type = (jax.ShapeDtypeStruct((p_total, width), parts[0].dtype),) * len(parts)

    @functools.partial(pl.kernel, mesh=_sc_mesh(), scratch_types=[], out_type=out_type,
                       name="dispatch_rows")
    def scatter(*refs):
        x_hbms = refs[:len(parts)]
        i_hbm = refs[len(parts)]
        o_hbms = refs[len(parts) + 1:]
        for x_hbm, o_hbm in zip(x_hbms, o_hbms):
            def body(x_vmem, i_vmem, o_hbm=o_hbm):
                for k in range(top_k):
                    pltpu.sync_copy(x_vmem, o_hbm.at[i_vmem.at[k]])

            pltpu.emit_pipeline(
                body,
                grid=(n // SC_WINDOW,),
                in_specs=[pl.BlockSpec((SC_WINDOW, width), lambda i: (i, 0)),
                          pl.BlockSpec((top_k, SC_WINDOW), lambda i: (0, i))],
                out_specs=[],
                core_axis_name=("core", "subcore"),
                dimension_semantics=(pltpu.PARALLEL,),
            )(x_hbm, i_hbm)

    return scatter(*parts, dest)


def _gather_rows(parts, idx_flat):
    count = idx_flat.shape[1]
    width = parts[0].shape[1]
    out_type = (jax.ShapeDtypeStruct((count, width), parts[0].dtype),) * len(parts)

    @functools.partial(pl.kernel, mesh=_sc_mesh(), scratch_types=[], out_type=out_type,
                       name="combine_rows")
    def gather(*refs):
        y_hbms = refs[:len(parts)]
        i_hbm = refs[len(parts)]
        o_hbms = refs[len(parts) + 1:]
        for y_hbm, o_hbm in zip(y_hbms, o_hbms):
            def body(i_vmem, o_vmem, y_hbm=y_hbm):
                pltpu.sync_copy(y_hbm.at[i_vmem.at[0]], o_vmem)

            pltpu.emit_pipeline(
                body,
                grid=(count // SC_WINDOW,),
                in_specs=[pl.BlockSpec((1, SC_WINDOW), lambda i: (0, i))],
                out_specs=[pl.BlockSpec((SC_WINDOW, width), lambda i: (i, 0))],
                core_axis_name=("core", "subcore"),
                dimension_semantics=(pltpu.PARALLEL,),
            )(i_hbm, o_hbm)

    return gather(*parts, idx_flat)


def kernel(x, ln_in_g, ln_in_b, w_in, hgrn_lb_logits, hgrn_norm_w, w_branch_att, w_branch_hgrn,
           w_out, ln1_g, ln1_b, router_w, router_bias, expert_w_in, expert_w_out, shared_w_in,
           shared_w_out, ln2_g, ln2_b):
    batch, seq, d = x.shape
    n = batch * seq
    x2 = x.reshape(n, d)
    row = lambda v: v.reshape(1, -1).astype(F32)

    lower_bounds = jnp.cumsum(jax.nn.softmax(hgrn_lb_logits.astype(F32), axis=0), axis=0)
    l = 0
    w_l = w_in[l].astype(BF16)
    gin, bin_ = row(ln_in_g), row(ln_in_b)

    proj, *acts = _inproj_main(x2, gin, bin_, w_l)
    qkv0, qkv1, qkv2 = _inproj_qkv(acts, w_l, batch, seq)
    qkv0 = qkv0.reshape(3, batch, 1, seq, ATT_GROUP_WIDTH)
    att = []
    for qkv in (qkv0, qkv1, qkv2):
        o_g, lse_g = _attention_group(qkv)
        att.append((o_g, lse_g.transpose(0, 2, 3, 1).reshape(batch, ATT_HEADS, seq)))
    att[0] = (att[0][0].reshape(n, ATT_GROUP_WIDTH), att[0][1])

    ob = _hgrn(proj, row(lower_bounds[l]), row(hgrn_norm_w[l]), batch, seq)

    rw_t = router_w[l].T.astype(F32)
    rwh = rw_t.astype(BF16)
    rwl = (rw_t - rwh.astype(F32)).astype(BF16)
    h1, hp0, hp1, idx_t, gate_t, rank_t, counts = _mix(
        x2, gin, bin_, att, ob, proj,
        w_branch_att[l].astype(BF16), w_branch_hgrn[l].astype(BF16), w_out[l].astype(BF16),
        row(ln1_g[l]), row(ln1_b[l]), rwh, rwl, router_bias[l].reshape(-1, 1).astype(F32),
        batch, seq)
    p_total = n * TOP_K + N_EXPERTS * MOE_BLOCK
    dest, start_blk = _plan(idx_t, rank_t, counts)
    n_rows = counts[:, 0].astype(I32)
    n_blk = (n_rows + (MOE_BLOCK - 1)) // MOE_BLOCK
    xs = _dispatch_rows((hp0, hp1), dest, p_total)
    ys = _experts(start_blk[:, 0], n_blk, n_rows, xs, expert_w_in[l], expert_w_out[l])
    yg = [g.reshape(TOP_K, n, PACK_WIDTH)
          for g in _gather_rows(ys, dest.reshape(1, TOP_K * n))]
    out = _combine(h1, yg, gate_t.T, shared_w_in[l].astype(BF16), shared_w_out[l].astype(BF16),
                   row(ln2_g[l]), row(ln2_b[l]))
    return out.reshape(batch, seq, d)
```

```python
import functools

import jax
import jax.numpy as jnp
from jax import lax
from jax.experimental import pallas as pl
from jax.experimental.pallas import tpu as pltpu
from jax.experimental.pallas import tpu_sc as plsc

F32 = jnp.float32
BF16 = jnp.bfloat16
U32 = jnp.uint32
I32 = jnp.int32

D_MODEL = 1024
ATT_GROUPS = ((128, 1), (512, 4), (2048, 16))
ATT_HEADS = 8
ATT_HEAD_DIM = 64
ATT_GROUP_WIDTH = ATT_HEADS * ATT_HEAD_DIM
ATT_WIDTH = len(ATT_GROUPS) * ATT_GROUP_WIDTH
ATT_BLOCK = 128
HGRN_HEAD_DIM = 128
HGRN_HEADS = D_MODEL // HGRN_HEAD_DIM
HGRN_WIDTH = HGRN_HEADS * HGRN_HEAD_DIM
HGRN_CHUNK = 32
N_EXPERTS = 256
TOP_K = 8
N_EXPERT_GROUPS = 8
TOPK_GROUPS = 4
EXPERT_DIM = 256
SHARED_DIM = 256
ROUTED_SCALE = 2.5
MOE_BLOCK = 128
LN_EPS = 1e-5
RMS_EPS = 1e-6
DEPTH = 1
DEEPNORM_ALPHA = (2 * DEPTH) ** 0.25

VMEM_LIMIT = 56 * 1024 * 1024
LANES = 128

ROW_TILE = 1024
MAIN_COL_TILE = 1536
MAIN_COL_STEPS = 4
MIX_ROW_TILE = 512
OUT_ROW_TILE = 512


def _params(sem, vmem=VMEM_LIMIT):
    return pltpu.CompilerParams(dimension_semantics=sem, vmem_limit_bytes=vmem)


def _layer_norm(x, g, b):
    mu = jnp.mean(x, -1, keepdims=True)
    xc = x - mu
    var = jnp.mean(xc * xc, -1, keepdims=True)
    return xc * lax.rsqrt(var + LN_EPS) * g + b


def _sigmoid(x):
    return 1.0 / (1.0 + jnp.exp(-x))


def _dot(a, b):
    return jnp.dot(a, b, preferred_element_type=F32)


def _dot_nt(a, b):
    return lax.dot_general(a, b, (((1,), (1,)), ((), ())), preferred_element_type=F32)


def _dot_tn(a, b):
    return lax.dot_general(a, b, (((0,), (0,)), ((), ())), preferred_element_type=F32)


def _pack_bf16_pair(lo, hi):
    lo_bits = pltpu.bitcast(lo.astype(BF16).astype(F32), U32) >> 16
    hi_bits = pltpu.bitcast(hi.astype(BF16).astype(F32), U32) & jnp.uint32(0xFFFF0000)
    return hi_bits | lo_bits


def _unpack_bf16_pair(w):
    lo = pltpu.bitcast(w << 16, F32)
    hi = pltpu.bitcast(w & jnp.uint32(0xFFFF0000), F32)
    return lo, hi


PACK_PARTS = 2
PACK_WIDTH = D_MODEL // 2 // PACK_PARTS


def _pack_rows(v):
    half = D_MODEL // 2
    parts = []
    for j in range(PACK_PARTS):
        lo = v[:, j * PACK_WIDTH:(j + 1) * PACK_WIDTH]
        hi = v[:, half + j * PACK_WIDTH:half + (j + 1) * PACK_WIDTH]
        parts.append(pltpu.bitcast(_pack_bf16_pair(lo, hi), I32))
    return parts


def _unpack_rows(parts):
    pairs = [_unpack_bf16_pair(pltpu.bitcast(p, U32)) for p in parts]
    return jnp.concatenate([lo for lo, _ in pairs] + [hi for _, hi in pairs], axis=-1)


def _inproj_main_kernel(x_ref, g_ref, b_ref, w_ref, o_ref, h0_ref, h1_ref, h2_ref, hf_scr):
    j = pl.program_id(1)

    @pl.when(j == 0)
    def _():
        hf = _layer_norm(x_ref[...], g_ref[...], b_ref[...])
        h0_ref[...] = hf.astype(BF16)
        for c in range(D_MODEL // LANES):
            hf_scr[c] = hf[:, c * LANES:(c + 1) * LANES]

    for h_ref, (_, dil) in ((h1_ref, ATT_GROUPS[1]), (h2_ref, ATT_GROUPS[2])):
        rows = ROW_TILE // dil
        share = dil // MAIN_COL_STEPS
        for q in range(share):
            r = j * share + q
            dst = pl.ds(pl.multiple_of(r * rows, rows), rows)
            for c in range(D_MODEL // LANES):
                h_ref[dst, c * LANES:(c + 1) * LANES] = (
                    hf_scr[c, pl.ds(r, rows, stride=dil), :].astype(BF16))

    o_ref[...] = _dot(h0_ref[...], w_ref[...]).astype(BF16)


def _inproj_main(x2, g, b, w_all):
    n = x2.shape[0]
    first = 3 * ATT_WIDTH // MAIN_COL_TILE
    width = w_all.shape[1] - 3 * ATT_WIDTH
    assert width == MAIN_COL_STEPS * MAIN_COL_TILE
    assert all(dil % MAIN_COL_STEPS == 0 for _, dil in ATT_GROUPS[1:])
    act = jax.ShapeDtypeStruct((n, D_MODEL), BF16)
    act_spec = pl.BlockSpec((ROW_TILE, D_MODEL), lambda i, j: (i, 0))
    return pl.pallas_call(
        _inproj_main_kernel,
        out_shape=(jax.ShapeDtypeStruct((n, width), BF16), act, act, act),
        grid=(n // ROW_TILE, MAIN_COL_STEPS),
        in_specs=[
            pl.BlockSpec((ROW_TILE, D_MODEL), lambda i, j: (i, 0)),
            pl.BlockSpec((1, D_MODEL), lambda i, j: (0, 0)),
            pl.BlockSpec((1, D_MODEL), lambda i, j: (0, 0)),
            pl.BlockSpec((D_MODEL, MAIN_COL_TILE), lambda i, j: (0, first + j)),
        ],
        out_specs=(pl.BlockSpec((ROW_TILE, MAIN_COL_TILE), lambda i, j: (i, j)),
                   act_spec, act_spec, act_spec),
        scratch_shapes=[pltpu.VMEM((D_MODEL // LANES, ROW_TILE, LANES), F32)],
        compiler_params=_params(("parallel", "arbitrary")),
        name="inproj_main",
    )(x2, g, b, w_all)


def _inproj_qkv_kernel(h0_ref, h1_ref, h2_ref, w_ref, o0_ref, o1_ref, o2_ref):
    gw = ATT_GROUP_WIDTH
    o0_ref[0] = _dot(h0_ref[...], w_ref[:, 0:gw]).astype(BF16)
    d1 = ATT_GROUPS[1][1]
    o1_ref[0, 0] = _dot(h1_ref[...], w_ref[:, gw:2 * gw]).astype(BF16).reshape(d1, ROW_TILE // d1, gw)
    d2 = ATT_GROUPS[2][1]
    o2_ref[0, 0] = _dot(h2_ref[...], w_ref[:, 2 * gw:3 * gw]).astype(BF16).reshape(d2, ROW_TILE // d2, gw)


def _inproj_qkv(acts, w_all, batch, seq):
    n = acts[0].shape[0]
    gw = ATT_GROUP_WIDTH
    tiles_per_seq = seq // ROW_TILE
    d1, d2 = ATT_GROUPS[1][1], ATT_GROUPS[2][1]
    out_shape = (
        jax.ShapeDtypeStruct((3, n, gw), BF16),
        jax.ShapeDtypeStruct((3, batch, d1, seq // d1, gw), BF16),
        jax.ShapeDtypeStruct((3, batch, d2, seq // d2, gw), BF16),
    )
    act_spec = pl.BlockSpec((ROW_TILE, D_MODEL), lambda i, t: (i, 0))
    return pl.pallas_call(
        _inproj_qkv_kernel,
        out_shape=out_shape,
        grid=(n // ROW_TILE, 3),
        in_specs=[act_spec, act_spec, act_spec,
                  pl.BlockSpec((D_MODEL, ATT_WIDTH), lambda i, t: (0, t))],
        out_specs=(
            pl.BlockSpec((1, ROW_TILE, gw), lambda i, t: (t, i, 0)),
            pl.BlockSpec((1, 1, d1, ROW_TILE // d1, gw),
                         lambda i, t: (t, i // tiles_per_seq, 0, i % tiles_per_seq, 0)),
            pl.BlockSpec((1, 1, d2, ROW_TILE // d2, gw),
                         lambda i, t: (t, i // tiles_per_seq, 0, i % tiles_per_seq, 0)),
        ),
        compiler_params=_params(("parallel", "arbitrary")),
        name="inproj_qkv",
    )(*acts, w_all)


ATT_HEADS_PER_MATMUL = 8
ATT_BATCH_PER_STEP = 8


def _attn_block(q_ref, kc_ref, kp_ref, v_ref, o_ref, lse_ref, vt_ref, bias, has_prev):
    blk = ATT_BLOCK
    hd = ATT_HEAD_DIM
    gw = ATT_GROUP_WIDTH
    hpm = ATT_HEADS_PER_MATMUL
    width = hpm * hd

    vt_cur = jnp.concatenate(
        [v_ref[:, c * LANES:(c + 1) * LANES].astype(F32).T.astype(BF16)
         for c in range(gw // LANES)], axis=0)
    vt_prev = jnp.where(has_prev, vt_ref[...], jnp.zeros_like(vt_cur))
    vt_ref[...] = vt_cur

    lane_head = lax.broadcasted_iota(I32, (blk, width), 1) // hd
    per_tile = LANES // hd
    for g in range(ATT_HEADS // hpm):
        feat = slice(g * width, (g + 1) * width)
        q_g = q_ref[:, feat] * (hd ** -0.5)
        q_bd = jnp.concatenate([jnp.where(lane_head == i, q_g, jnp.zeros_like(q_g))
                                for i in range(hpm)], axis=0)
        k_g = jnp.concatenate([kp_ref[:, feat], kc_ref[:, feat]], axis=0)
        s_t = _dot_nt(k_g, q_bd) + bias
        m = jnp.max(s_t, axis=0, keepdims=True)
        p = jnp.exp(s_t - m)
        l = jnp.sum(p, axis=0, keepdims=True)
        v_t = jnp.concatenate([vt_prev[feat, :], vt_cur[feat, :]], axis=1)
        o_t = _dot(v_t, p.astype(BF16))
        lse = m + jnp.log(l)
        inv_l = 1.0 / l
        for c in range(width // LANES):
            tile = []
            for i in range(c * per_tile, (c + 1) * per_tile):
                cols = slice(i * blk, (i + 1) * blk)
                tile.append(o_t[i * hd:(i + 1) * hd, cols] * inv_l[:, cols])
                lse_ref[g * hpm + i:g * hpm + i + 1, :] = lse[:, cols]
            lanes = slice(g * width + c * LANES, g * width + (c + 1) * LANES)
            o_ref[:, lanes] = jnp.concatenate(tile, axis=0).T.astype(BF16)


def _attn_kernel(q_ref, kc_ref, kp_ref, v_ref, o_ref, lse_ref, vt_scr):
    blk = ATT_BLOCK
    has_prev = pl.program_id(2) > 0
    key_i = lax.broadcasted_iota(I32, (2 * blk, blk), 0)
    qry_i = lax.broadcasted_iota(I32, (2 * blk, blk), 1)
    live = (((key_i < blk) & (key_i >= qry_i) & has_prev)
            | ((key_i >= blk) & (key_i - blk <= qry_i)))
    bias = jnp.concatenate([jnp.where(live, 0.0, -jnp.inf)] * ATT_HEADS_PER_MATMUL, axis=1)
    for j in range(ATT_BATCH_PER_STEP):
        _attn_block(q_ref.at[j], kc_ref.at[j], kp_ref.at[j], v_ref.at[j], o_ref.at[j],
                    lse_ref.at[j], vt_scr.at[j], bias, has_prev)


def _attention_group(qkv):
    _, batch, dil, sub_len, gw = qkv.shape
    blk = ATT_BLOCK
    bps = ATT_BATCH_PER_STEP

    def spec(t, prev):
        def index(b, r, i):
            return (t, b, r, jnp.maximum(i - 1, 0) if prev else i, 0)
        return pl.BlockSpec((None, bps, None, blk, gw), index)

    return pl.pallas_call(
        _attn_kernel,
        out_shape=(jax.ShapeDtypeStruct((batch, dil, sub_len, gw), BF16),
                   jax.ShapeDtypeStruct((batch, dil, ATT_HEADS, sub_len), F32)),
        grid=(batch // bps, dil, sub_len // blk),
        in_specs=[spec(0, False), spec(1, False), spec(1, True), spec(2, False)],
        out_specs=(pl.BlockSpec((bps, None, blk, gw), lambda b, r, i: (b, r, i, 0)),
                   pl.BlockSpec((bps, None, ATT_HEADS, blk), lambda b, r, i: (b, r, 0, i))),
        scratch_shapes=[pltpu.VMEM((bps, gw, blk), BF16)],
        compiler_params=_params(("parallel", "parallel", "arbitrary")),
        name=f"dilated_attention_d{dil}",
    )(qkv, qkv, qkv, qkv)


HGRN_ROWS = 256
HGRN_HEADS_PER_STEP = 8
HGRN_MAX_FACTOR_EXPONENT = 60.0


def _hgrn_pair(q, key, bcum, v, gate, states, nw, same_chunk_causal, chunk_mask):
    c = HGRN_CHUNK
    dk = HGRN_HEAD_DIM
    rows = HGRN_ROWS
    nchunk = rows // c
    pair = 2 * dk

    b3 = bcum.reshape(nchunk, c, pair)
    b_mid = b3[:, c // 2:c // 2 + 1, :]
    b_last = b3[:, c - 1:c, :]
    q3 = q.astype(F32).reshape(nchunk, c, pair)
    q_t3 = q3 * jnp.exp(b3 - b_mid)
    k_t3 = key.reshape(nchunk, c, pair) * jnp.exp(b_mid - b3)
    q_t = q_t3.reshape(rows, pair).astype(BF16)
    k_t = k_t3.reshape(rows, pair).astype(BF16)
    q_in = (q_t3 * jnp.exp(b_mid)).reshape(rows, pair).astype(BF16)
    k_st = (k_t3 * jnp.exp(b_last - b_mid)).reshape(rows, pair).astype(BF16)
    decay = jnp.exp(b_last.reshape(nchunk, pair))

    def block_diagonal(x):
        zero = jnp.zeros((rows, dk), x.dtype)
        return jnp.concatenate([jnp.concatenate([x[:, :dk], zero], axis=1),
                                jnp.concatenate([zero, x[:, dk:]], axis=1)], axis=0)

    scores = _dot_nt(block_diagonal(q_t), k_t)
    att = jnp.concatenate([jnp.where(same_chunk_causal, scores[:rows], 0.0),
                           jnp.where(same_chunk_causal, scores[rows:], 0.0)], axis=1)
    o = _dot(att.astype(BF16), block_diagonal(v))

    new_states, outs = [], []
    for i, st in enumerate(states):
        hs = slice(i * dk, (i + 1) * dk)
        k_spread = jnp.concatenate([k_st[:, hs] * chunk_mask[j] for j in range(nchunk)], axis=1)
        incr = _dot_tn(v[:, hs], k_spread)
        before = []
        for j in range(nchunk):
            before.append(st.astype(BF16))
            st = st * decay[j:j + 1, hs] + incr[:, j * dk:(j + 1) * dk]
        q_spread = jnp.concatenate([q_in[:, hs] * chunk_mask[j] for j in range(nchunk)], axis=1)
        o_h = o[:, hs] + _dot_nt(q_spread, jnp.concatenate(before, axis=1))
        outs.append(o_h * lax.rsqrt(jnp.mean(o_h * o_h, -1, keepdims=True) + RMS_EPS * dk) * nw)
        new_states.append(st)
    gate = gate.astype(F32)
    return jnp.concatenate(outs, axis=1) * gate * _sigmoid(gate), new_states


def _hgrn_chunk_direct(q, key, b, v, gate, st, nw):
    c = HGRN_CHUNK
    dk = HGRN_HEAD_DIM
    v32 = v.astype(F32)
    row = lax.broadcasted_iota(I32, (c, dk), 0)
    o = _dot_nt((q * jnp.exp(b)).astype(BF16), st.astype(BF16))
    for s_ in range(c):
        dec = jnp.exp(jnp.where(row >= s_, b - b[s_:s_ + 1], -jnp.inf))
        a = jnp.sum(q * key[s_:s_ + 1] * dec, axis=-1, keepdims=True)
        o = o + a * v32[s_:s_ + 1]
    b_last = b[c - 1:c]
    st = st * jnp.exp(b_last) + _dot_tn(v, (key * jnp.exp(b_last - b)).astype(BF16))
    o = o * lax.rsqrt(jnp.mean(o * o, -1, keepdims=True) + RMS_EPS * dk) * nw
    gate = gate.astype(F32)
    return o * gate * _sigmoid(gate), st


def _hgrn_kernel(q_ref, f_ref, i_ref, g_ref, lb_ref, nw_ref, o_ref, state_scr, mask_scr,
                 key_scr, b_scr):
    seq = q_ref.shape[0]
    c = HGRN_CHUNK
    rows = HGRN_ROWS
    dk = HGRN_HEAD_DIM
    width = HGRN_HEADS_PER_STEP * dk
    nw = nw_ref[...]

    r_i = lax.broadcasted_iota(I32, (rows, rows), 0)
    c_i = lax.broadcasted_iota(I32, (rows, rows), 1)
    same_chunk_causal = (r_i // c == c_i // c) & (c_i <= r_i)
    tri = jnp.where(same_chunk_causal, 1.0, 0.0).astype(BF16)
    row_chunk = lax.broadcasted_iota(I32, (rows, dk), 0) // c
    for j in range(rows // c):
        mask_scr[j] = jnp.where(row_chunk == j, 1.0, 0.0).astype(BF16)

    state_scr[...] = jnp.zeros_like(state_scr)

    def body(gi, carry):
        rs = pl.ds(pl.multiple_of(gi * rows, rows), rows)
        lb = lb_ref[...]
        one_m_lb = 1.0 - lb
        z = f_ref[rs, :].astype(F32)
        log_f = jnp.log(lb + one_m_lb * _sigmoid(z))
        p0 = log_f.astype(BF16)
        p1 = (log_f - p0.astype(F32)).astype(BF16)
        both = _dot(tri, jnp.concatenate([p0, p1], axis=1))
        bcum = both[:, :width] + both[:, width:]

        b3 = bcum.reshape(rows // c, c, width)
        span = jnp.maximum(b3[:, 0, :] - b3[:, c // 2, :], b3[:, c // 2, :] - b3[:, c - 1, :])
        factorable = jnp.max(span) < HGRN_MAX_FACTOR_EXPONENT

        @pl.when(factorable)
        def _():
            key = one_m_lb * _sigmoid(-z)
            for h in range(0, HGRN_HEADS_PER_STEP, 2):
                cs = slice(h * dk, (h + 2) * dk)
                o, sts = _hgrn_pair(q_ref[rs, cs], key[:, cs], bcum[:, cs], i_ref[rs, cs],
                                    g_ref[rs, cs], [state_scr[h], state_scr[h + 1]], nw,
                                    same_chunk_causal, mask_scr)
                state_scr[h], state_scr[h + 1] = sts
                o_ref[rs, cs] = o.astype(BF16)

        @pl.when(jnp.logical_not(factorable))
        def _():
            key_scr[...] = one_m_lb * _sigmoid(-z)
            b_scr[...] = bcum
            for h in range(HGRN_HEADS_PER_STEP):
                cs = slice(h * dk, (h + 1) * dk)

                def chunk(j, st, cs=cs):
                    local = pl.ds(pl.multiple_of(j * c, c), c)
                    r = pl.ds(pl.multiple_of(gi * rows + j * c, c), c)
                    o, st = _hgrn_chunk_direct(q_ref[r, cs].astype(F32), key_scr[local, cs],
                                               b_scr[local, cs], i_ref[r, cs], g_ref[r, cs], st, nw)
                    o_ref[r, cs] = o.astype(BF16)
                    return st

                state_scr[h] = lax.fori_loop(0, rows // c, chunk, state_scr[h])

        return carry

    lax.fori_loop(0, seq // rows, body, 0)


def _hgrn(proj, lower_bound, norm_w, batch, seq):
    n = proj.shape[0]
    dk = HGRN_HEAD_DIM
    width = HGRN_HEADS_PER_STEP * dk
    steps = HGRN_HEADS // HGRN_HEADS_PER_STEP

    def seg(k):
        return pl.BlockSpec((seq, width), lambda b, h: (b, k * steps + h))

    return pl.pallas_call(
        _hgrn_kernel,
        out_shape=jax.ShapeDtypeStruct((n, HGRN_WIDTH), BF16),
        grid=(batch, steps),
        in_specs=[seg(0), seg(1), seg(2), seg(3),
                  pl.BlockSpec((1, width), lambda b, h: (0, h)),
                  pl.BlockSpec((1, dk), lambda b, h: (0, 0))],
        out_specs=pl.BlockSpec((seq, width), lambda b, h: (b, h)),
        scratch_shapes=[pltpu.VMEM((HGRN_HEADS_PER_STEP, dk, dk), F32),
                        pltpu.VMEM((HGRN_ROWS // HGRN_CHUNK, HGRN_ROWS, dk), BF16),
                        pltpu.VMEM((HGRN_ROWS, width), F32),
                        pltpu.VMEM((HGRN_ROWS, width), F32)],
        compiler_params=_params(("parallel", "parallel")),
        name="hgrn2",
    )(proj, proj, proj, proj, lower_bound, norm_w)


def _pick_first_max(vals, iota, axis, size):
    m = jnp.max(vals, axis=axis, keepdims=True)
    idx = jnp.min(jnp.where(vals == m, iota, size), axis=axis, keepdims=True)
    return m, idx


def _route_tile(logits, bias, base):
    e, tn = logits.shape
    groups = N_EXPERT_GROUPS
    gsz = e // groups
    neg = -jnp.inf

    scores = _sigmoid(logits)
    biased = scores + bias

    b3 = biased.reshape(groups, gsz, tn)
    io3 = lax.broadcasted_iota(I32, b3.shape, 1)
    m1, i1 = _pick_first_max(b3, io3, 1, gsz)
    m2 = jnp.max(jnp.where(io3 == i1, neg, b3), axis=1, keepdims=True)
    grp = m1 + m2
    iog = lax.broadcasted_iota(I32, grp.shape, 0)
    keep = jnp.zeros(grp.shape, F32)
    for _ in range(TOPK_GROUPS):
        _, gi = _pick_first_max(grp, iog, 0, groups)
        hit = iog == gi
        keep = jnp.where(hit, 1.0, keep)
        grp = jnp.where(hit, neg, grp)
    masked = jnp.where(keep > 0.0, b3, neg).reshape(e, tn)

    ioe = lax.broadcasted_iota(I32, (e, tn), 0)
    onehot = jnp.zeros((e, tn), F32)
    idxs, gates = [], []
    for _ in range(TOP_K):
        _, ei = _pick_first_max(masked, ioe, 0, e)
        hit = ioe == ei
        gates.append(jnp.sum(jnp.where(hit, scores, 0.0), axis=0, keepdims=True))
        onehot = jnp.where(hit, 1.0, onehot)
        masked = jnp.where(hit, neg, masked)
        idxs.append(ei)
    gsum = gates[0]
    for g in gates[1:]:
        gsum = gsum + g
    gate = jnp.concatenate(gates, axis=0) / gsum * ROUTED_SCALE

    t_r = lax.broadcasted_iota(I32, (tn, tn), 0)
    t_c = lax.broadcasted_iota(I32, (tn, tn), 1)
    earlier = jnp.where(t_r < t_c, 1.0, 0.0).astype(BF16)
    oh = onehot.astype(BF16)
    before = _dot(oh, earlier) + jnp.concatenate([base] * (tn // LANES), axis=1)
    ranks = [jnp.sum(jnp.where(ioe == ei, before, 0.0), axis=0, keepdims=True) for ei in idxs]
    total = base + _dot(oh, jnp.ones((tn, LANES), BF16))
    return (jnp.concatenate(idxs, axis=0), gate,
            jnp.concatenate(ranks, axis=0).astype(I32), total)


def _mix_kernel(x_ref, gin_ref, bin_ref, o0_ref, o1_ref, o2_ref, l0_ref, l1_ref, l2_ref,
                ob_ref, ga_ref, gb_ref, wa_ref, wb_ref, wo_ref, g1_ref, b1_ref,
                rwh_ref, rwl_ref, rbias_ref,
                h_ref, hp0_ref, hp1_ref, idx_ref, gate_ref, rank_ref, cnt_ref,
                so1_scr, so2_scr, lg_scr, carry_scr):
    tm = x_ref.shape[0]
    gw = ATT_GROUP_WIDTH
    step = pl.program_id(0)

    @pl.when(step == 0)
    def _():
        lg_scr[...] = jnp.zeros_like(lg_scr)
        carry_scr[...] = jnp.zeros_like(carry_scr)

    counts = carry_scr[...]
    idx, gate, rank, total = _route_tile(lg_scr[...], rbias_ref[...], counts)
    idx_ref[...] = idx
    gate_ref[...] = gate
    rank_ref[...] = rank
    counts = jnp.where(step > 0, total, counts)
    carry_scr[...] = counts
    cnt_ref[...] = counts

    for o_ref, so_scr, (_, dil) in ((o1_ref, so1_scr, ATT_GROUPS[1]),
                                    (o2_ref, so2_scr, ATT_GROUPS[2])):
        for r in range(dil):
            o_r = o_ref[r].astype(F32)
            for c in range(gw // LANES):
                so_scr[c, pl.ds(r, tm // dil, stride=dil), :] = o_r[:, c * LANES:(c + 1) * LANES]

    def natural(scr):
        return jnp.concatenate([scr[c] for c in range(gw // LANES)], axis=-1)

    l0, l1, l2 = l0_ref[...], l1_ref[...], l2_ref[...]
    m = jnp.maximum(jnp.maximum(l0, l1), l2)
    e0 = jnp.exp(l0 - m)
    e1 = jnp.exp(l1 - m)
    e2 = jnp.exp(l2 - m)
    inv = 1.0 / (e0 + e1 + e2)
    terms = []
    for w in (e0 * inv, e1 * inv, e2 * inv):
        hi = w.astype(BF16).astype(F32)
        terms += [hi, w - hi]
    terms = jnp.concatenate(terms, axis=0).astype(BF16)
    n_groups = len(ATT_GROUPS)
    t_i = lax.broadcasted_iota(I32, (n_groups * 2 * ATT_HEADS, n_groups * gw), 0)
    c_i = lax.broadcasted_iota(I32, (n_groups * 2 * ATT_HEADS, n_groups * gw), 1)
    spread = jnp.where((t_i // (2 * ATT_HEADS) == c_i // gw)
                       & (t_i % ATT_HEADS == c_i % gw // ATT_HEAD_DIM), 1.0, 0.0).astype(BF16)
    wide = _dot_tn(terms, spread)
    o_att = (wide[:, :gw] * o0_ref[...].astype(F32) + wide[:, gw:2 * gw] * natural(so1_scr)
             + wide[:, 2 * gw:] * natural(so2_scr))

    y_a = _dot(o_att.astype(BF16), wa_ref[...])
    y_b = _dot(ob_ref[...], wb_ref[...])
    merged = _sigmoid(ga_ref[...].astype(F32)) * y_a + _sigmoid(gb_ref[...].astype(F32)) * y_b
    mix = _dot(merged.astype(BF16), wo_ref[...])
    h_in = _layer_norm(x_ref[...], gin_ref[...], bin_ref[...])
    h1 = _layer_norm(DEEPNORM_ALPHA * h_in + mix, g1_ref[...], b1_ref[...])
    h_ref[...] = h1

    hp0_ref[...], hp1_ref[...] = _pack_rows(h1)

    h_hi = h1.astype(BF16)
    h_lo = (h1 - h_hi.astype(F32)).astype(BF16)
    rwh = rwh_ref[...]
    lg_scr[...] = _dot_nt(rwh, h_hi) + _dot_nt(rwh, h_lo) + _dot_nt(rwl_ref[...], h_hi)


def _mix(x2, gin, bin_, att, ob, proj, wa, wb, wo, g1, b1, rwh, rwl, rbias, batch, seq):
    n = x2.shape[0]
    tm = MIX_ROW_TILE
    gw = ATT_GROUP_WIDTH
    tiles_per_seq = seq // tm
    last = n // tm - 1
    (o0, l0), (o1, l1), (o2, l2) = att
    d1, d2 = ATT_GROUPS[1][1], ATT_GROUPS[2][1]

    def tile(i):
        return jnp.minimum(i, last)

    def routed(i):
        return jnp.maximum(i - 1, 0)

    lse_spec = pl.BlockSpec((None, ATT_HEADS, tm),
                            lambda i: (tile(i) // tiles_per_seq, 0, tile(i) % tiles_per_seq))

    def full(shape):
        return pl.BlockSpec(shape, lambda i: (0,) * len(shape))

    def dil_spec(dil):
        return pl.BlockSpec((None, dil, tm // dil, gw),
                            lambda i: (tile(i) // tiles_per_seq, 0, tile(i) % tiles_per_seq, 0))

    def rows(width, col=0):
        return pl.BlockSpec((tm, width), lambda i: (tile(i), col))

    tok_spec = pl.BlockSpec((TOP_K, tm), lambda i: (0, routed(i)))
    return pl.pallas_call(
        _mix_kernel,
        out_shape=(jax.ShapeDtypeStruct((n, D_MODEL), F32),
                   jax.ShapeDtypeStruct((n, PACK_WIDTH), I32),
                   jax.ShapeDtypeStruct((n, PACK_WIDTH), I32),
                   jax.ShapeDtypeStruct((TOP_K, n), I32),
                   jax.ShapeDtypeStruct((TOP_K, n), F32),
                   jax.ShapeDtypeStruct((TOP_K, n), I32),
                   jax.ShapeDtypeStruct((N_EXPERTS, LANES), F32)),
        grid=(n // tm + 1,),
        in_specs=[
            rows(D_MODEL),
            full((1, D_MODEL)), full((1, D_MODEL)),
            rows(gw), dil_spec(d1), dil_spec(d2),
            lse_spec, lse_spec, lse_spec,
            rows(HGRN_WIDTH), rows(D_MODEL, 4), rows(D_MODEL, 5),
            full((gw, D_MODEL)), full((HGRN_WIDTH, D_MODEL)), full((D_MODEL, D_MODEL)),
            full((1, D_MODEL)), full((1, D_MODEL)),
            full((N_EXPERTS, D_MODEL)), full((N_EXPERTS, D_MODEL)), full((N_EXPERTS, 1)),
        ],
        out_specs=(rows(D_MODEL), rows(PACK_WIDTH), rows(PACK_WIDTH),
                   tok_spec, tok_spec, tok_spec, full((N_EXPERTS, LANES))),
        scratch_shapes=[pltpu.VMEM((gw // LANES, tm, LANES), F32),
                        pltpu.VMEM((gw // LANES, tm, LANES), F32),
                        pltpu.VMEM((N_EXPERTS, tm), F32),
                        pltpu.VMEM((N_EXPERTS, LANES), F32)],
        compiler_params=_params(("arbitrary",)),
        name="branch_mix_route",
    )(x2, gin, bin_, o0, o1, o2, l0, l1, l2, ob, proj, proj, wa, wb, wo, g1, b1, rwh, rwl, rbias)


ROW_SLOTS = 16
ROW_GROUP = 4
ROW_AHEAD = ROW_SLOTS - ROW_GROUP
W_SLOTS = 3


def _expert_kernel(sblk_ref, nblk_ref, cnt_ref, xs0_hbm, xs1_hbm, wi_hbm, wo_hbm,
                   ys0_hbm, ys1_hbm, xbuf0, xbuf1, ybuf0, ybuf1, wibuf, wobuf,
                   in_sem, out_sem, w_sem, wi_scr, wo_scr):
    e = pl.program_id(0)
    n_exp = pl.num_programs(0)
    first_blk = sblk_ref[e]
    n_blk = nblk_ref[e]
    n_rows = cnt_ref[e]
    total = sblk_ref[n_exp - 1] + nblk_ref[n_exp - 1]
    xs_hbm, ys_hbm = (xs0_hbm, xs1_hbm), (ys0_hbm, ys1_hbm)
    xbuf, ybuf = (xbuf0, xbuf1), (ybuf0, ybuf1)

    def rows_of(b):
        return pl.ds(pl.multiple_of(b * MOE_BLOCK, MOE_BLOCK), MOE_BLOCK)

    def x_copy(b, part):
        slot = b % ROW_SLOTS
        return pltpu.make_async_copy(xs_hbm[part].at[rows_of(b)], xbuf[part].at[slot],
                                     in_sem.at[part, slot])

    def y_copy(b, part):
        slot = b % ROW_SLOTS
        return pltpu.make_async_copy(ybuf[part].at[slot], ys_hbm[part].at[rows_of(b)],
                                     out_sem.at[part, slot])

    def w_copies(ex):
        slot = ex % W_SLOTS
        return (pltpu.make_async_copy(wi_hbm.at[ex], wibuf.at[slot], w_sem.at[0, slot]),
                pltpu.make_async_copy(wo_hbm.at[ex], wobuf.at[slot], w_sem.at[1, slot]))

    def start_weights(ex):
        @pl.when((ex < n_exp) & (nblk_ref[jnp.minimum(ex, n_exp - 1)] > 0))
        def _():
            for cp in w_copies(ex):
                cp.start(priority=1)

    @pl.when(e == 0)
    def _():
        for b in range(ROW_AHEAD):
            @pl.when(b < total)
            def _():
                for part in range(PACK_PARTS):
                    x_copy(b, part).start()
        for ex in range(W_SLOTS - 1):
            start_weights(ex)

    start_weights(e + (W_SLOTS - 1))

    def step(j, group):
        b0 = first_blk + j
        for i in range(group):
            for part in range(PACK_PARTS):
                x_copy(b0 + i, part).wait()
        for i in range(group):
            @pl.when(b0 + ROW_AHEAD + i < total)
            def _():
                for part in range(PACK_PARTS):
                    x_copy(b0 + ROW_AHEAD + i, part).start()
        for i in range(group):
            @pl.when(b0 + i >= ROW_SLOTS)
            def _():
                for part in range(PACK_PARTS):
                    y_copy(b0 + i - ROW_SLOTS, part).wait()

        rows = group * MOE_BLOCK
        live = (lax.broadcasted_iota(I32, (rows, PACK_WIDTH), 0)
                < n_rows - j * MOE_BLOCK)
        parts = [jnp.concatenate([xbuf[part][(b0 + i) % ROW_SLOTS] for i in range(group)], axis=0)
                 for part in range(PACK_PARTS)]
        x = _unpack_rows([jnp.where(live, p, 0) for p in parts])
        hu = _dot(x.astype(BF16), wi_scr[...])
        hg = hu[:, :EXPERT_DIM]
        hv = hu[:, EXPERT_DIM:]
        act = (hg * _sigmoid(hg) * hv).astype(BF16)
        packed = _pack_rows(_dot(act, wo_scr[...]))
        for i in range(group):
            for part in range(PACK_PARTS):
                ybuf[part][(b0 + i) % ROW_SLOTS] = packed[part][i * MOE_BLOCK:(i + 1) * MOE_BLOCK]
                y_copy(b0 + i, part).start()

    @pl.when(n_blk > 0)
    def _():
        for cp in w_copies(e):
            cp.wait()
        slot = e % W_SLOTS
        wi_scr[...] = wibuf[slot].astype(BF16)
        wo_scr[...] = wobuf[slot].astype(BF16)

        def full_group(g, carry):
            step(g * ROW_GROUP, ROW_GROUP)
            return carry

        lax.fori_loop(0, n_blk // ROW_GROUP, full_group, 0)
        done = n_blk // ROW_GROUP * ROW_GROUP
        group = ROW_GROUP // 2
        while group >= 1:
            @pl.when((n_blk & group) != 0)
            def _(group=group, done=done):
                step(done, group)
            done = done + (n_blk & group)
            group //= 2

    @pl.when(e == n_exp - 1)
    def _():
        for back in range(1, ROW_SLOTS + 1):
            @pl.when(total >= back)
            def _():
                for part in range(PACK_PARTS):
                    y_copy(total - back, part).wait()


def _experts(start_blk, n_blk, n_rows, xs, w_in_e, w_out_e):
    p = xs[0].shape[0]
    n_exp = w_in_e.shape[0]
    any_spec = pl.BlockSpec(memory_space=pl.ANY)
    row_buf = pltpu.VMEM((ROW_SLOTS, MOE_BLOCK, PACK_WIDTH), I32)
    grid_spec = pltpu.PrefetchScalarGridSpec(
        num_scalar_prefetch=3,
        grid=(n_exp,),
        in_specs=[any_spec, any_spec, any_spec, any_spec],
        out_specs=(any_spec, any_spec),
        scratch_shapes=[row_buf, row_buf, row_buf, row_buf,
                        pltpu.VMEM((W_SLOTS, D_MODEL, 2 * EXPERT_DIM), F32),
                        pltpu.VMEM((W_SLOTS, EXPERT_DIM, D_MODEL), F32),
                        pltpu.SemaphoreType.DMA((PACK_PARTS, ROW_SLOTS)),
                        pltpu.SemaphoreType.DMA((PACK_PARTS, ROW_SLOTS)),
                        pltpu.SemaphoreType.DMA((2, W_SLOTS)),
                        pltpu.VMEM((D_MODEL, 2 * EXPERT_DIM), BF16),
                        pltpu.VMEM((EXPERT_DIM, D_MODEL), BF16)],
    )
    return pl.pallas_call(
        _expert_kernel,
        out_shape=(jax.ShapeDtypeStruct((p, PACK_WIDTH), I32),) * PACK_PARTS,
        grid_spec=grid_spec,
        compiler_params=_params(("arbitrary",)),
        name="routed_experts",
    )(start_blk, n_blk, n_rows, xs[0], xs[1], w_in_e, w_out_e)


def _combine_kernel(h_ref, yg0_ref, yg1_ref, gate_ref, wsi_ref, wso_ref, g2_ref, b2_ref, o_ref):
    h1 = h_ref[...]
    gate = gate_ref[...]
    routed = None
    for k in range(TOP_K):
        y_k = _unpack_rows([yg0_ref[k], yg1_ref[k]]) * gate[:, k:k + 1]
        routed = y_k if routed is None else routed + y_k
    hs = _dot(h1.astype(BF16), wsi_ref[...])
    sg = hs[:, :SHARED_DIM]
    sv = hs[:, SHARED_DIM:]
    shared = _dot((sg * _sigmoid(sg) * sv).astype(BF16), wso_ref[...])
    o_ref[...] = _layer_norm(DEEPNORM_ALPHA * h1 + routed + shared, g2_ref[...], b2_ref[...])


def _combine(h1, yg, gate_nk, wsi, wso, g2, b2):
    n = h1.shape[0]
    tm = OUT_ROW_TILE
    yg_spec = pl.BlockSpec((TOP_K, tm, PACK_WIDTH), lambda i: (0, i, 0))

    def full(shape):
        return pl.BlockSpec(shape, lambda i: (0,) * len(shape))

    return pl.pallas_call(
        _combine_kernel,
        out_shape=jax.ShapeDtypeStruct((n, D_MODEL), F32),
        grid=(n // tm,),
        in_specs=[
            pl.BlockSpec((tm, D_MODEL), lambda i: (i, 0)),
            yg_spec, yg_spec,
            pl.BlockSpec((tm, TOP_K), lambda i: (i, 0)),
            full((D_MODEL, 2 * SHARED_DIM)), full((SHARED_DIM, D_MODEL)),
            full((1, D_MODEL)), full((1, D_MODEL)),
        ],
        out_specs=pl.BlockSpec((tm, D_MODEL), lambda i: (i, 0)),
        compiler_params=_params(("parallel",)),
        name="shared_combine",
    )(h1, yg[0], yg[1], gate_nk, wsi, wso, g2, b2)


def _plan_kernel(idx_ref, rank_ref, cnt_ref, dest_ref, sblk_ref):
    e = cnt_ref.shape[0]
    tn = idx_ref.shape[1]
    cnt = cnt_ref[...]
    nblk = jnp.floor((cnt + (MOE_BLOCK - 1)) * (1.0 / MOE_BLOCK))
    e_r = lax.broadcasted_iota(I32, (e, e), 0)
    e_c = lax.broadcasted_iota(I32, (e, e), 1)
    lower = jnp.where(e_c < e_r, 1.0, 0.0).astype(BF16)
    start_blk = _dot(lower, nblk.astype(BF16))

    ioe = lax.broadcasted_iota(I32, (e, tn), 0)
    start_row = jnp.concatenate([start_blk * MOE_BLOCK] * (tn // LANES), axis=1)
    idx = idx_ref[...]
    dests = [jnp.sum(jnp.where(ioe == idx[k:k + 1], start_row, 0.0), axis=0, keepdims=True)
             for k in range(TOP_K)]
    dest_ref[...] = jnp.concatenate(dests, axis=0).astype(I32) + rank_ref[...]
    sblk_ref[...] = start_blk.astype(I32)


def _plan(idx_t, rank_t, counts):
    k, n = idx_t.shape
    e = counts.shape[0]
    tn = MIX_ROW_TILE
    tok_spec = pl.BlockSpec((k, tn), lambda i: (0, i))
    exp_spec = pl.BlockSpec((e, LANES), lambda i: (0, 0))
    return pl.pallas_call(
        _plan_kernel,
        out_shape=(jax.ShapeDtypeStruct((k, n), I32), jax.ShapeDtypeStruct((e, LANES), I32)),
        grid=(n // tn,),
        in_specs=[tok_spec, tok_spec, exp_spec],
        out_specs=(tok_spec, exp_spec),
        compiler_params=_params(("arbitrary",)),
        name="dispatch_plan",
    )(idx_t, rank_t, counts)


SC_WINDOW = 128


def _sc_mesh():
    return plsc.VectorSubcoreMesh(core_axis_name="core", subcore_axis_name="subcore")


def _dispatch_rows(parts, dest, p_total):
    n, width = parts[0].shape
    top_k = dest.shape[0]
    out_type = (jax.ShapeDtypeStruct((p_total, width), parts[0].dtype),) * len(parts)

    @functools.partial(pl.kernel, mesh=_sc_mesh(), scratch_types=[], out_type=out_type,
                       name="dispatch_rows")
    def scatter(*refs):
        x_hbms = refs[:len(parts)]
        i_hbm = refs[len(parts)]
        o_hbms = refs[len(parts) + 1:]
        for x_hbm, o_hbm in zip(x_hbms, o_hbms):
            def body(x_vmem, i_vmem, o_hbm=o_hbm):
                for k in range(top_k):
                    pltpu.sync_copy(x_vmem, o_hbm.at[i_vmem.at[k]])

            pltpu.emit_pipeline(
                body,
                grid=(n // SC_WINDOW,),
                in_specs=[pl.BlockSpec((SC_WINDOW, width), lambda i: (i, 0)),
                          pl.BlockSpec((top_k, SC_WINDOW), lambda i: (0, i))],
                out_specs=[],
                core_axis_name=("core", "subcore"),
                dimension_semantics=(pltpu.PARALLEL,),
            )(x_hbm, i_hbm)

    return scatter(*parts, dest)


def _gather_rows(parts, idx_flat):
    count = idx_flat.shape[1]
    width = parts[0].shape[1]
    out_type = (jax.ShapeDtypeStruct((count, width), parts[0].dtype),) * len(parts)

    @functools.partial(pl.kernel, mesh=_sc_mesh(), scratch_types=[], out_type=out_type,
                       name="combine_rows")
    def gather(*refs):
        y_hbms = refs[:len(parts)]
        i_hbm = refs[len(parts)]
        o_hbms = refs[len(parts) + 1:]
        for y_hbm, o_hbm in zip(y_hbms, o_hbms):
            def body(i_vmem, o_vmem, y_hbm=y_hbm):
                pltpu.sync_copy(y_hbm.at[i_vmem.at[0]], o_vmem)

            pltpu.emit_pipeline(
                body,
                grid=(count // SC_WINDOW,),
                in_specs=[pl.BlockSpec((1, SC_WINDOW), lambda i: (0, i))],
                out_specs=[pl.BlockSpec((SC_WINDOW, width), lambda i: (i, 0))],
                core_axis_name=("core", "subcore"),
                dimension_semantics=(pltpu.PARALLEL,),
            )(i_hbm, o_hbm)

    return gather(*parts, idx_flat)


def kernel(x, ln_in_g, ln_in_b, w_in, hgrn_lb_logits, hgrn_norm_w, w_branch_att, w_branch_hgrn,
           w_out, ln1_g, ln1_b, router_w, router_bias, expert_w_in, expert_w_out, shared_w_in,
           shared_w_out, ln2_g, ln2_b):
    batch, seq, d = x.shape
    n = batch * seq
    x2 = x.reshape(n, d)
    row = lambda v: v.reshape(1, -1).astype(F32)

    lower_bounds = jnp.cumsum(jax.nn.softmax(hgrn_lb_logits.astype(F32), axis=0), axis=0)
    l = 0
    w_l = w_in[l].astype(BF16)
    gin, bin_ = row(ln_in_g), row(ln_in_b)

    proj, *acts = _inproj_main(x2, gin, bin_, w_l)
    qkv0, qkv1, qkv2 = _inproj_qkv(acts, w_l, batch, seq)
    qkv0 = qkv0.reshape(3, batch, 1, seq, ATT_GROUP_WIDTH)
    att = []
    for qkv in (qkv0, qkv1, qkv2):
        o_g, lse_g = _attention_group(qkv)
        att.append((o_g, lse_g.transpose(0, 2, 3, 1).reshape(batch, ATT_HEADS, seq)))
    att[0] = (att[0][0].reshape(n, ATT_GROUP_WIDTH), att[0][1])

    ob = _hgrn(proj, row(lower_bounds[l]), row(hgrn_norm_w[l]), batch, seq)

    rw_t = router_w[l].T.astype(F32)
    rwh = rw_t.astype(BF16)
    rwl = (rw_t - rwh.astype(F32)).astype(BF16)
    h1, hp0, hp1, idx_t, gate_t, rank_t, counts = _mix(
        x2, gin, bin_, att, ob, proj,
        w_branch_att[l].astype(BF16), w_branch_hgrn[l].astype(BF16), w_out[l].astype(BF16),
        row(ln1_g[l]), row(ln1_b[l]), rwh, rwl, router_bias[l].reshape(-1, 1).astype(F32),
        batch, seq)
    p_total = n * TOP_K + N_EXPERTS * MOE_BLOCK
    dest, start_blk = _plan(idx_t, rank_t, counts)
    n_rows = counts[:, 0].astype(I32)
    n_blk = (n_rows + (MOE_BLOCK - 1)) // MOE_BLOCK
    xs = _dispatch_rows((hp0, hp1), dest, p_total)
    ys = _experts(start_blk[:, 0], n_blk, n_rows, xs, expert_w_in[l], expert_w_out[l])
    yg = [g.reshape(TOP_K, n, PACK_WIDTH)
          for g in _gather_rows(ys, dest.reshape(1, TOP_K * n))]
    out = _combine(h1, yg, gate_t.T, shared_w_in[l].astype(BF16), shared_w_out[l].astype(BF16),
                   row(ln2_g[l]), row(ln2_b[l]))
    return out.reshape(batch, seq, d)
```
